```python
import math
import jax, jax.numpy as jnp
from jax import lax
import numpy as np

D_MODEL = 1024
BATCH = 8
SEQ = 8192
DEPTH = 2

MEM_LEN = 256
D_MIX = D_MODEL
HEAD_DIM = 64
SSM_WIDTH = D_MIX // 4
SSM_GROUP = 16
SSM_GROUPS = SSM_WIDTH // SSM_GROUP
SSM_STATE = 64
CONV_WIDTH = D_MIX // 4
CONV_K = 31
NSA_WIDTH = D_MIX // 2
NSA_HEADS = NSA_WIDTH // HEAD_DIM
NSA_KV_HEADS = 2
NSA_GROUP = NSA_HEADS // NSA_KV_HEADS
N_BRANCH = 3
L_CMP = 32
CMP_STRIDE = 16
L_SEL = 64
N_SELECT = 16
N_LOCAL = 2
WINDOW = 512
NSA_Q_BLOCK = 64
SEL_FORCE = 1e6
T5_BUCKETS = 32
T5_MAX_DIST = 128
X_HEADS = 4
X_WIDTH = X_HEADS * HEAD_DIM
D_FF = 2816
N_EXPERTS = 8
TOP_K = 2
MOE_BLOCK = 256
N_DENSE = (DEPTH + 1) // 2
N_MOE = DEPTH // 2
PROJ_WIDTH = SSM_WIDTH + 2 * CONV_WIDTH + NSA_WIDTH + 6 * NSA_KV_HEADS * HEAD_DIM + N_BRANCH * NSA_HEADS
EPS = 1e-6

kernel_name = "hybrid_s5_conformer_nsa_moe_trunk"


def _rms(x):
    xf = x.astype(jnp.float32)
    return (xf * lax.rsqrt(jnp.mean(xf * xf, axis=-1, keepdims=True) + EPS)).astype(x.dtype)


def rms_norm(x, g):
    return _rms(x) * g


def layer_norm(x, g, b):
    xf = x.astype(jnp.float32)
    mu = jnp.mean(xf, axis=-1, keepdims=True)
    var = jnp.mean(jnp.square(xf - mu), axis=-1, keepdims=True)
    return ((xf - mu) * lax.rsqrt(var + EPS)).astype(x.dtype) * g + b


def masked_softmax(s, mask):
    s = jnp.where(mask, s.astype(jnp.float32), -jnp.inf)
    m = jnp.max(s, axis=-1, keepdims=True)
    m = jnp.where(jnp.isfinite(m), m, 0.0)
    p = jnp.exp(s - m)
    return p / jnp.maximum(jnp.sum(p, axis=-1, keepdims=True), 1e-30)


def t5_bucket(dist):
    n = jnp.maximum(dist, 0)
    max_exact = T5_BUCKETS // 2
    nf = jnp.maximum(n, 1).astype(jnp.float32)
    large = max_exact + (jnp.log(nf / max_exact) / math.log(T5_MAX_DIST / max_exact)
                         * (T5_BUCKETS - max_exact)).astype(jnp.int32)
    large = jnp.minimum(large, T5_BUCKETS - 1)
    return jnp.where(n < max_exact, n, large)


def _complex_linear_combine(e1, e2):
    a1r, a1i, b1r, b1i = e1
    a2r, a2i, b2r, b2i = e2
    return (a1r * a2r - a1i * a2i,
            a1r * a2i + a1i * a2r,
            a2r * b1r - a2i * b1i + b2r,
            a2r * b1i + a2i * b1r + b2i)


def s5_mixer(u, lam_re, lam_im, log_dt, b_re, b_im, c_re, c_im, d_skip, w_glu):
    bsz, seq, _ = u.shape
    dt = jnp.exp(log_dt.astype(jnp.float32))[:, None]
    lr, li = lam_re.astype(jnp.float32), lam_im.astype(jnp.float32)
    mag = jnp.exp(lr * dt)
    ar, ai = mag * jnp.cos(li * dt), mag * jnp.sin(li * dt)
    den = lr * lr + li * li
    fr = ((ar - 1.0) * lr + ai * li) / den
    fi = (ai * lr - (ar - 1.0) * li) / den
    bbr = fr[..., None] * b_re - fi[..., None] * b_im
    bbi = fr[..., None] * b_im + fi[..., None] * b_re
    ug = u.reshape(bsz, seq, SSM_GROUPS, SSM_GROUP).astype(jnp.float32)
    bur = jnp.einsum('bsgh,gph->bsgp', ug, bbr)
    bui = jnp.einsum('bsgh,gph->bsgp', ug, bbi)
    a_r = jnp.broadcast_to(ar, (1, seq, SSM_GROUPS, SSM_STATE))
    a_i = jnp.broadcast_to(ai, (1, seq, SSM_GROUPS, SSM_STATE))
    _, _, xr, xi = lax.associative_scan(_complex_linear_combine, (a_r, a_i, bur, bui), axis=1)
    y = (jnp.einsum('bsgp,ghp->bsgh', xr, c_re) - jnp.einsum('bsgp,ghp->bsgh', xi, c_im)
         + d_skip * ug)
    y = y.reshape(bsz, seq, SSM_WIDTH).astype(u.dtype)
    a, g = jnp.split(jax.nn.gelu(y) @ w_glu, 2, axis=-1)
    return a * jax.nn.sigmoid(g)


def conv_mixer(z, w_dw, b_dw, ln_g, ln_b, w_pw):
    a, g = jnp.split(z, 2, axis=-1)
    v = a * jax.nn.sigmoid(g)
    y = lax.conv_general_dilated(v, w_dw[:, None, :], window_strides=(1,),
                                 padding=[(CONV_K - 1, 0)],
                                 dimension_numbers=('NWC', 'WIO', 'NWC'),
                                 feature_group_count=CONV_WIDTH) + b_dw
    y = jax.nn.silu(layer_norm(y, ln_g, ln_b))
    return y @ w_pw


def nsa_compress(kv, pe, w1, w2):
    seq = kv.shape[2]
    n_cmp = (seq - L_CMP) // CMP_STRIDE + 1
    idx = jnp.arange(n_cmp)[:, None] * CMP_STRIDE + jnp.arange(L_CMP)[None, :]
    blocks = kv[:, :, idx] + pe
    flat = blocks.reshape(blocks.shape[0], blocks.shape[1], n_cmp, L_CMP * HEAD_DIM)
    return jax.nn.gelu(flat @ w1) @ w2


def cmp_to_sel_overlap(n_cmp, n_sel):
    cs = jnp.arange(n_cmp) * CMP_STRIDE
    ss = jnp.arange(n_sel) * L_SEL
    ov = jnp.minimum(cs[:, None] + L_CMP, ss[None, :] + L_SEL) - jnp.maximum(cs[:, None], ss[None, :])
    return jnp.maximum(ov, 0).astype(jnp.float32) / L_CMP


def nsa_mixer(q, k_c, v_c, k_s, v_s, k_w, v_w, gate_logits, q_gain, k_gain, cmp_pe, cmp_w1, cmp_w2, t5_table):
    bsz, seq, _ = q.shape
    qb = NSA_Q_BLOCK
    q = rms_norm(q.reshape(bsz, seq, NSA_HEADS, HEAD_DIM), q_gain)
    q = q.reshape(bsz, seq, NSA_KV_HEADS, NSA_GROUP, HEAD_DIM).transpose(0, 2, 3, 1, 4)

    def kv_heads(t):
        return t.reshape(bsz, seq, NSA_KV_HEADS, HEAD_DIM).transpose(0, 2, 1, 3)

    k_cmp = rms_norm(nsa_compress(kv_heads(k_c), cmp_pe[0], cmp_w1[0], cmp_w2[0]), k_gain[0])
    v_cmp = nsa_compress(kv_heads(v_c), cmp_pe[1], cmp_w1[1], cmp_w2[1])
    n_cmp = k_cmp.shape[2]
    n_sel = seq // L_SEL
    k_sel = min(N_SELECT, n_sel)
    k_slc = rms_norm(kv_heads(k_s), k_gain[1]).reshape(bsz, NSA_KV_HEADS, n_sel, L_SEL, HEAD_DIM)
    v_slc = kv_heads(v_s).reshape(bsz, NSA_KV_HEADS, n_sel, L_SEL, HEAD_DIM)
    pad = ((0, 0), (0, 0), (WINDOW, 0), (0, 0))
    k_win = jnp.pad(rms_norm(kv_heads(k_w), k_gain[2]), pad)
    v_win = jnp.pad(kv_heads(v_w), pad)
    gates = jax.nn.sigmoid(gate_logits.reshape(bsz, seq, NSA_KV_HEADS, NSA_GROUP, N_BRANCH)
                           .astype(jnp.float32)).astype(q.dtype).transpose(0, 2, 3, 1, 4)
    overlap = cmp_to_sel_overlap(n_cmp, n_sel)
    table_kg = t5_table.T.reshape(NSA_KV_HEADS, NSA_GROUP, T5_BUCKETS)
    cmp_end = jnp.arange(n_cmp) * CMP_STRIDE + (L_CMP - 1)
    sel_blk = jnp.arange(n_sel)
    b_ix = jnp.arange(bsz)[:, None, None, None]
    h_ix = jnp.arange(NSA_KV_HEADS)[None, :, None, None]
    kv_ix = jnp.arange(NSA_KV_HEADS)[None, :, None, None, None]
    g_ix = jnp.arange(NSA_GROUP)[None, None, :, None, None]
    scale = HEAD_DIM ** -0.5

    def shared_bias(dist):
        return jnp.moveaxis(t5_table[t5_bucket(dist)], -1, 0).reshape(NSA_KV_HEADS, NSA_GROUP, *dist.shape)

    def block(c):
        start = c * qb
        t = start + jnp.arange(qb)
        qc = lax.dynamic_slice_in_dim(q, start, qb, axis=3)
        gc = lax.dynamic_slice_in_dim(gates, start, qb, axis=3)
        dist_c = t[:, None] - cmp_end[None, :]
        s_c = jnp.einsum('bkgqd,bknd->bkgqn', qc, k_cmp).astype(jnp.float32) * scale + shared_bias(dist_c)
        p_c = masked_softmax(s_c, dist_c >= 0)
        o_c = jnp.einsum('bkgqn,bknd->bkgqd', p_c.astype(v_cmp.dtype), v_cmp)
        imp = jnp.einsum('bkgqn,nj->bkqj', p_c, overlap)
        blk_t = (t // L_SEL)[:, None]
        forced = (sel_blk[None, :] == 0) | (sel_blk[None, :] > blk_t - N_LOCAL)
        imp = jnp.where(sel_blk[None, :] > blk_t, -jnp.inf, jnp.where(forced, SEL_FORCE, imp))
        top_val, top_idx = lax.top_k(imp, k_sel)
        k_g = k_slc[b_ix, h_ix, top_idx].reshape(bsz, NSA_KV_HEADS, qb, k_sel * L_SEL, HEAD_DIM)
        v_g = v_slc[b_ix, h_ix, top_idx].reshape(bsz, NSA_KV_HEADS, qb, k_sel * L_SEL, HEAD_DIM)
        pos = (top_idx[..., None] * L_SEL + jnp.arange(L_SEL)).reshape(bsz, NSA_KV_HEADS, qb, k_sel * L_SEL)
        dist_s = t[:, None] - pos
        mask_s = (dist_s >= 0) & jnp.repeat(jnp.isfinite(top_val), L_SEL, axis=-1)
        bias_s = table_kg[kv_ix, g_ix, t5_bucket(dist_s)[:, :, None]]
        s_s = jnp.einsum('bkgqd,bkqsd->bkgqs', qc, k_g).astype(jnp.float32) * scale + bias_s
        p_s = masked_softmax(s_s, mask_s[:, :, None])
        o_s = jnp.einsum('bkgqs,bkqsd->bkgqd', p_s.astype(v_g.dtype), v_g)
        kw = lax.dynamic_slice_in_dim(k_win, start, WINDOW + qb, axis=2)
        vw = lax.dynamic_slice_in_dim(v_win, start, WINDOW + qb, axis=2)
        s_pos = start - WINDOW + jnp.arange(WINDOW + qb)
        dist_w = t[:, None] - s_pos[None, :]
        mask_w = (dist_w >= 0) & (dist_w < WINDOW) & (s_pos[None, :] >= 0)
        s_w = jnp.einsum('bkgqd,bknd->bkgqn', qc, kw).astype(jnp.float32) * scale + shared_bias(dist_w)
        p_w = masked_softmax(s_w, mask_w)
        o_w = jnp.einsum('bkgqn,bknd->bkgqd', p_w.astype(vw.dtype), vw)
        return gc[..., 0:1] * o_c + gc[..., 1:2] * o_s + gc[..., 2:3] * o_w

    out = lax.map(block, jnp.arange(seq // qb))
    return out.transpose(1, 0, 4, 2, 3, 5).reshape(bsz, seq, NSA_WIDTH)


def cross_attention(hn, memn, w_q, w_kv, q_gain, k_gain, w_o):
    bsz, seq, _ = hn.shape
    q = rms_norm((hn @ w_q).reshape(bsz, seq, X_HEADS, HEAD_DIM), q_gain)
    k, v = jnp.split(memn @ w_kv, 2, axis=-1)
    k = rms_norm(k.reshape(bsz, -1, X_HEADS, HEAD_DIM), k_gain)
    v = v.reshape(bsz, -1, X_HEADS, HEAD_DIM)
    s = jnp.einsum('bshd,bmhd->bhsm', q, k).astype(jnp.float32) * HEAD_DIM ** -0.5
    p = jax.nn.softmax(s, axis=-1).astype(v.dtype)
    o = jnp.einsum('bhsm,bmhd->bshd', p, v).reshape(bsz, seq, X_WIDTH)
    return o @ w_o


def swiglu(x, w_up, w_down):
    g, v = jnp.split(x @ w_up, 2, axis=-1)
    return (jax.nn.silu(g) * v) @ w_down


def moe_swiglu(x2, w_router, w_up, w_down):
    n_tok, d = x2.shape
    logits = (x2 @ w_router).astype(jnp.float32)
    top_val, top_idx = lax.top_k(logits, TOP_K)
    gate = jax.nn.softmax(top_val, axis=-1)
    e_flat = top_idx.reshape(-1)
    tok_flat = jnp.repeat(jnp.arange(n_tok), TOP_K)
    g_flat = gate.reshape(-1)
    n_asg = e_flat.shape[0]
    order = jnp.argsort(e_flat)
    e_s, tok_s, g_s = e_flat[order], tok_flat[order], g_flat[order]
    counts = jnp.bincount(e_flat, length=N_EXPERTS)
    padded = (counts + MOE_BLOCK - 1) // MOE_BLOCK * MOE_BLOCK
    start_s = jnp.cumsum(counts) - counts
    pad_end = jnp.cumsum(padded)
    start_p = pad_end - padded
    dest = start_p[e_s] + (jnp.arange(n_asg) - start_s[e_s])
    n_rows = (n_asg + MOE_BLOCK - 1) // MOE_BLOCK * MOE_BLOCK + N_EXPERTS * MOE_BLOCK
    row_tok = jnp.full((n_rows,), n_tok, jnp.int32).at[dest].set(tok_s)
    row_gate = jnp.zeros((n_rows,), jnp.float32).at[dest].set(g_s)
    n_blk = n_rows // MOE_BLOCK
    blk_exp = jnp.minimum(jnp.searchsorted(pad_end, jnp.arange(n_blk) * MOE_BLOCK, side='right'), N_EXPERTS - 1)
    x_pad = jnp.concatenate([x2, jnp.zeros((1, d), x2.dtype)], axis=0)
    xb = x_pad[row_tok].reshape(n_blk, MOE_BLOCK, d)

    def expert_block(args):
        xblk, e = args
        return swiglu(xblk, w_up[e], w_down[e])

    yb = lax.map(expert_block, (xb, blk_exp)).reshape(n_rows, d)
    y = yb * row_gate[:, None].astype(yb.dtype)
    return jax.ops.segment_sum(y, row_tok, num_segments=n_tok + 1)[:n_tok].astype(x2.dtype)


def setup_inputs(seed: int = 0) -> dict:
    key = jax.random.key(seed)
    ks = iter(jax.random.split(key, 40))
    f32 = jnp.float32
    L = DEPTH

    def nrm(shape, scale):
        return jax.random.normal(next(ks), shape, f32) * scale

    def gain(shape):
        return 1.0 + nrm(shape, 0.02)

    inp = {}
    inp['x'] = nrm((BATCH, SEQ, D_MODEL), 1.0)
    inp['mem'] = nrm((BATCH, MEM_LEN, D_MODEL), 1.0)
    inp['norm_mix'] = gain((L, D_MODEL))
    inp['w_in'] = nrm((L, D_MODEL, PROJ_WIDTH), D_MODEL ** -0.5)
    inp['ssm_lambda_re'] = -0.5 + nrm((L, SSM_GROUPS, SSM_STATE), 0.01)
    inp['ssm_lambda_im'] = math.pi * jnp.arange(SSM_STATE, dtype=f32) + nrm((L, SSM_GROUPS, SSM_STATE), 0.01)
    inp['ssm_log_dt'] = jax.random.uniform(next(ks), (L, SSM_GROUPS), f32, math.log(1e-3), math.log(1e-1))
    inp['ssm_b_re'] = nrm((L, SSM_GROUPS, SSM_STATE, SSM_GROUP), (2 * SSM_GROUP) ** -0.5)
    inp['ssm_b_im'] = nrm((L, SSM_GROUPS, SSM_STATE, SSM_GROUP), (2 * SSM_GROUP) ** -0.5)
    inp['ssm_c_re'] = nrm((L, SSM_GROUPS, SSM_GROUP, SSM_STATE), SSM_STATE ** -0.5)
    inp['ssm_c_im'] = nrm((L, SSM_GROUPS, SSM_GROUP, SSM_STATE), SSM_STATE ** -0.5)
    inp['ssm_d'] = nrm((L, SSM_GROUPS, SSM_GROUP), 1.0)
    inp['ssm_w_glu'] = nrm((L, SSM_WIDTH, 2 * SSM_WIDTH), SSM_WIDTH ** -0.5)
    inp['conv_w_dw'] = nrm((L, CONV_K, CONV_WIDTH), CONV_K ** -0.5)
    inp['conv_b_dw'] = nrm((L, CONV_WIDTH), 0.02)
    inp['conv_ln_g'] = gain((L, CONV_WIDTH))
    inp['conv_ln_b'] = nrm((L, CONV_WIDTH), 0.02)
    inp['conv_w_pw'] = nrm((L, CONV_WIDTH, CONV_WIDTH), CONV_WIDTH ** -0.5)
    inp['nsa_q_norm'] = gain((L, HEAD_DIM))
    inp['nsa_k_norm'] = gain((L, N_BRANCH, HEAD_DIM))
    inp['nsa_cmp_pe'] = nrm((L, 2, L_CMP, HEAD_DIM), 0.1)
    inp['nsa_cmp_w1'] = nrm((L, 2, L_CMP * HEAD_DIM, HEAD_DIM), (L_CMP * HEAD_DIM) ** -0.5)
    inp['nsa_cmp_w2'] = nrm((L, 2, HEAD_DIM, HEAD_DIM), HEAD_DIM ** -0.5)
    inp['mix_out_norm'] = gain((L, D_MIX))
    inp['w_out'] = nrm((L, D_MIX, D_MODEL), D_MIX ** -0.5)
    inp['t5_table'] = nrm((T5_BUCKETS, NSA_HEADS), 0.5)
    inp['norm_cross'] = gain((L, D_MODEL))
    inp['norm_mem'] = gain((L, D_MODEL))
    inp['x_w_q'] = nrm((L, D_MODEL, X_WIDTH), D_MODEL ** -0.5)
    inp['x_w_kv'] = nrm((L, D_MODEL, 2 * X_WIDTH), D_MODEL ** -0.5)
    inp['x_q_norm'] = gain((L, HEAD_DIM))
    inp['x_k_norm'] = gain((L, HEAD_DIM))
    inp['x_w_o'] = nrm((L, X_WIDTH, D_MODEL), X_WIDTH ** -0.5)
    inp['norm_ffn'] = gain((L, D_MODEL))
    inp['ffn_w_up'] = nrm((N_DENSE, D_MODEL, 2 * D_FF), D_MODEL ** -0.5)
    inp['ffn_w_down'] = nrm((N_DENSE, D_FF, D_MODEL), D_FF ** -0.5)
    inp['moe_router'] = nrm((N_MOE, D_MODEL, N_EXPERTS), D_MODEL ** -0.5)
    inp['moe_w_up'] = nrm((N_MOE, N_EXPERTS, D_MODEL, 2 * D_FF), D_MODEL ** -0.5)
    inp['moe_w_down'] = nrm((N_MOE, N_EXPERTS, D_FF, D_MODEL), D_FF ** -0.5)
    return inp


def reference(x, mem, norm_mix, w_in, ssm_lambda_re, ssm_lambda_im, ssm_log_dt, ssm_b_re, ssm_b_im,
              ssm_c_re, ssm_c_im, ssm_d, ssm_w_glu, conv_w_dw, conv_b_dw, conv_ln_g, conv_ln_b, conv_w_pw,
              nsa_q_norm, nsa_k_norm, nsa_cmp_pe, nsa_cmp_w1, nsa_cmp_w2, mix_out_norm, w_out, t5_table,
              norm_cross, norm_mem, x_w_q, x_w_kv, x_q_norm, x_k_norm, x_w_o, norm_ffn, ffn_w_up, ffn_w_down,
              moe_router, moe_w_up, moe_w_down):
    bsz, seq, d = x.shape
    kv_w = NSA_KV_HEADS * HEAD_DIM
    cuts = [int(c) for c in np.cumsum([SSM_WIDTH, 2 * CONV_WIDTH, NSA_WIDTH] + [kv_w] * 6)]
    h = x
    for layer in range(DEPTH):
        z = rms_norm(h, norm_mix[layer]) @ w_in[layer]
        u_ssm, z_conv, q, k_c, v_c, k_s, v_s, k_w, v_w, g_nsa = jnp.split(z, cuts, axis=-1)
        y_ssm = s5_mixer(u_ssm, ssm_lambda_re[layer], ssm_lambda_im[layer], ssm_log_dt[layer],
                         ssm_b_re[layer], ssm_b_im[layer], ssm_c_re[layer], ssm_c_im[layer],
                         ssm_d[layer], ssm_w_glu[layer])
        y_conv = conv_mixer(z_conv, conv_w_dw[layer], conv_b_dw[layer], conv_ln_g[layer],
                            conv_ln_b[layer], conv_w_pw[layer])
        y_nsa = nsa_mixer(q, k_c, v_c, k_s, v_s, k_w, v_w, g_nsa, nsa_q_norm[layer], nsa_k_norm[layer],
                          nsa_cmp_pe[layer], nsa_cmp_w1[layer], nsa_cmp_w2[layer], t5_table)
        y = jnp.concatenate([_rms(y_ssm), _rms(y_conv), _rms(y_nsa)], axis=-1) * mix_out_norm[layer]
        h = h + y @ w_out[layer]
        h = h + cross_attention(rms_norm(h, norm_cross[layer]), rms_norm(mem, norm_mem[layer]),
                                x_w_q[layer], x_w_kv[layer], x_q_norm[layer], x_k_norm[layer], x_w_o[layer])
        hn = rms_norm(h, norm_ffn[layer])
        if layer % 2 == 0:
            h = h + swiglu(hn, ffn_w_up[layer // 2], ffn_w_down[layer // 2])
        else:
            h = h + moe_swiglu(hn.reshape(bsz * seq, d), moe_router[layer // 2], moe_w_up[layer // 2],
                               moe_w_down[layer // 2]).reshape(bsz, seq, d)
    return h
```

```python
import functools
import math

import jax
import jax.numpy as jnp
import numpy as np
from jax import lax
from jax.experimental import pallas as pl
from jax.experimental.pallas import tpu as pltpu

F32 = jnp.float32
BF16 = jnp.bfloat16

D_MODEL = 1024
HEAD_DIM = 64
SSM_WIDTH = 256
SSM_GROUP = 16
SSM_GROUPS = 16
SSM_STATE = 64
SSM_CHUNK = 16
CONV_WIDTH = 256
CONV_K = 31
CONV_HALO = 32
NSA_WIDTH = 512
NSA_HEADS = 8
NSA_KV_HEADS = 2
NSA_GROUP = 4
N_BRANCH = 3
L_CMP = 32
CMP_STRIDE = 16
L_SEL = 64
N_SELECT = 16
N_LOCAL = 2
WINDOW = 512
SEL_FORCE = 1e6
T5_BUCKETS = 32
T5_MAX_DIST = 128
X_HEADS = 4
X_WIDTH = 256
D_FF = 2816
N_EXPERTS = 8
TOP_K = 2
EPS = 1e-6
NEG = -1e30
LOG2E = math.log2(math.e)

TQ = 256
SEL_PAD = 256
FF_CHUNK = 256
MOE_TB = 1024
MOE_SUB = 128

_VMEM_LIMIT = 56 * 1024 * 1024


def _cparams(sem, vmem=None):
    return pltpu.CompilerParams(dimension_semantics=sem, vmem_limit_bytes=vmem)


def _rms_rows(x):
    return x * lax.rsqrt(jnp.mean(x * x, axis=-1, keepdims=True) + EPS)


def _proj_kernel(x_ref, g_ref, wtok_ref, wt_ref, ztok_ref, zt_ref):
    xn = (_rms_rows(x_ref[0]) * g_ref[...]).astype(BF16)
    ztok_ref[0] = jnp.dot(xn, wtok_ref[...], preferred_element_type=F32)
    zt_ref[0] = lax.dot_general(wt_ref[...], xn, (((1,), (1,)), ((), ())), preferred_element_type=F32)


def _proj(h, gain, w_tok, w_t, tm=512):
    b, s, d = h.shape
    ntok, nt = w_tok.shape[1], w_t.shape[0]
    return pl.pallas_call(
        _proj_kernel,
        grid=(b, s // tm),
        in_specs=[pl.BlockSpec((1, tm, d), lambda i, j: (i, j, 0)),
                  pl.BlockSpec((1, d), lambda i, j: (0, 0)),
                  pl.BlockSpec((d, ntok), lambda i, j: (0, 0)),
                  pl.BlockSpec((nt, d), lambda i, j: (0, 0))],
        out_specs=[pl.BlockSpec((1, tm, ntok), lambda i, j: (i, j, 0)),
                   pl.BlockSpec((1, nt, tm), lambda i, j: (i, 0, j))],
        out_shape=[jax.ShapeDtypeStruct((b, s, ntok), F32), jax.ShapeDtypeStruct((b, nt, s), F32)],
        compiler_params=_cparams(("parallel", "parallel"), _VMEM_LIMIT),
        name="proj",
    )(h, gain.reshape(1, d), w_tok, w_t)


def _s5_tables(lam_re, lam_im, log_dt, b_re, b_im, c_re, c_im, d_skip):
    L, H, P = SSM_CHUNK, SSM_GROUP, SSM_STATE
    dt = jnp.exp(log_dt.astype(F32))[:, None]
    lr, li = lam_re.astype(F32), lam_im.astype(F32)
    mag = jnp.exp(lr * dt)
    ar, ai = mag * jnp.cos(li * dt), mag * jnp.sin(li * dt)
    den = lr * lr + li * li
    fr = ((ar - 1.0) * lr + ai * li) / den
    fi = (ai * lr - (ar - 1.0) * li) / den
    bbr = fr[..., None] * b_re - fi[..., None] * b_im
    bbi = fr[..., None] * b_im + fi[..., None] * b_re
    j = jnp.arange(L + 1, dtype=F32)[:, None, None]
    pmag = jnp.exp(lr[None] * dt[None] * j)
    pr, pi = pmag * jnp.cos(li[None] * dt[None] * j), pmag * jnp.sin(li[None] * dt[None] * j)
    cbr = c_re[:, :, :, None] * bbr[:, None, :, :] - c_im[:, :, :, None] * bbi[:, None, :, :]
    cbi = c_re[:, :, :, None] * bbi[:, None, :, :] + c_im[:, :, :, None] * bbr[:, None, :, :]
    kj = jnp.einsum('jgp,ghpk->jghk', pr[:L], cbr) - jnp.einsum('jgp,ghpk->jghk', pi[:L], cbi)
    lag = jnp.arange(L)[None, :] - jnp.arange(L)[:, None]
    kt = jnp.where((lag >= 0)[:, :, None, None, None], kj[jnp.clip(lag, 0, L - 1)], 0.0)
    kt = kt + (jnp.eye(L)[:, :, None, None, None] * (jnp.eye(H)[None, None, None] * d_skip[None, None, :, :, None]))
    tmat = kt.transpose(2, 0, 4, 1, 3).reshape(SSM_GROUPS, L * H, L * H)
    qr, qi = pr[L - 1 - jnp.arange(L)], pi[L - 1 - jnp.arange(L)]
    wre = qr[..., None] * bbr[None] - qi[..., None] * bbi[None]
    wim = qr[..., None] * bbi[None] + qi[..., None] * bbr[None]
    wre = wre.transpose(1, 0, 3, 2).reshape(SSM_GROUPS, L * H, P)
    wim = wim.transpose(1, 0, 3, 2).reshape(SSM_GROUPS, L * H, P)
    w1 = jnp.concatenate([wre, wim], axis=-1)
    w2 = jnp.concatenate([wim, wre], axis=-1)
    sr, si = pr[1:], pi[1:]
    vr = c_re[None] * sr[:, :, None, :] - c_im[None] * si[:, :, None, :]
    vi = c_re[None] * si[:, :, None, :] + c_im[None] * sr[:, :, None, :]
    vmat = jnp.concatenate([vr, -vi], axis=-1).transpose(1, 3, 0, 2).reshape(SSM_GROUPS, 2 * P, L * H)
    a_r, a_i = pr[L], pi[L]
    am = jnp.stack([jnp.concatenate([a_r, a_r], -1), jnp.concatenate([-a_i, a_i], -1),
                    jnp.concatenate([a_i, -a_i], -1)], axis=1)
    am = jnp.concatenate([am, jnp.zeros((SSM_GROUPS, 5, 2 * P), F32)], axis=1)
    return tmat.astype(BF16), w1.astype(BF16), w2.astype(BF16), vmat.astype(BF16), am


def _s5_kernel(x_ref, t_ref, w1_ref, w2_ref, v_ref, a_ref, o_ref, s1_ref, s2_ref, xin_ref, *, bsz, n_chunks):
    x = x_ref[0]
    s1_ref[...] = jnp.dot(x, w1_ref[0], preferred_element_type=F32)
    s2_ref[...] = jnp.dot(x, w2_ref[0], preferred_element_type=F32)
    a1, a2, a3 = a_ref[0, 0:1, :], a_ref[0, 1:2, :], a_ref[0, 2:3, :]

    def step(c, carry):
        p, q = carry
        rows = pl.ds(pl.multiple_of(c * bsz, bsz), bsz)
        xin_ref[rows, :] = p
        return (p * a1 + q * a2 + s1_ref[rows, :], q * a1 + p * a3 + s2_ref[rows, :])

    zero = jnp.zeros((bsz, 2 * SSM_STATE), F32)
    lax.fori_loop(0, n_chunks, step, (zero, zero))
    y = (jnp.dot(x, t_ref[0], preferred_element_type=F32)
         + jnp.dot(xin_ref[...].astype(BF16), v_ref[0], preferred_element_type=F32))
    o_ref[0] = jax.nn.gelu(y).astype(o_ref.dtype)


def _s5(xg, tables, bsz):
    tmat, w1, w2, vmat, am = tables
    g, r, lh = xg.shape
    p2 = 2 * SSM_STATE
    kern = functools.partial(_s5_kernel, bsz=bsz, n_chunks=r // bsz)
    return pl.pallas_call(
        kern,
        grid=(g,),
        in_specs=[pl.BlockSpec((1, r, lh), lambda i: (i, 0, 0)),
                  pl.BlockSpec((1, lh, lh), lambda i: (i, 0, 0)),
                  pl.BlockSpec((1, lh, p2), lambda i: (i, 0, 0)),
                  pl.BlockSpec((1, lh, p2), lambda i: (i, 0, 0)),
                  pl.BlockSpec((1, p2, lh), lambda i: (i, 0, 0)),
                  pl.BlockSpec((1, 8, p2), lambda i: (i, 0, 0))],
        out_specs=pl.BlockSpec((1, r, lh), lambda i: (i, 0, 0)),
        out_shape=jax.ShapeDtypeStruct((g, r, lh), BF16),
        scratch_shapes=[pltpu.VMEM((r, p2), F32), pltpu.VMEM((r, p2), F32), pltpu.VMEM((r, p2), F32)],
        compiler_params=_cparams(("parallel",), _VMEM_LIMIT),
        name="s5_scan",
    )(xg, tmat, w1, w2, vmat, am)


def _conv_kernel(z_ref, halo_ref, wdw_ref, bdw_ref, lng_ref, lnb_ref, wpw_ref, go_ref, o_ref, buf_ref, *, tt):
    first = pl.program_id(1) == 0
    zc = z_ref[0]
    zh = halo_ref[0]
    vh = zh[:, :CONV_WIDTH] * jax.nn.sigmoid(zh[:, CONV_WIDTH:])
    buf_ref[0:CONV_HALO, :] = vh * jnp.where(first, 0.0, 1.0)
    buf_ref[CONV_HALO:CONV_HALO + tt, :] = zc[:, :CONV_WIDTH] * jax.nn.sigmoid(zc[:, CONV_WIDTH:])
    acc = jnp.zeros((tt, CONV_WIDTH), F32) + bdw_ref[...]
    for k in range(CONV_K):
        acc = acc + wdw_ref[k:k + 1, :] * buf_ref[pl.ds(CONV_HALO - (CONV_K - 1) + k, tt), :]
    mu = jnp.mean(acc, axis=-1, keepdims=True)
    var = jnp.mean(jnp.square(acc - mu), axis=-1, keepdims=True)
    y = (acc - mu) * lax.rsqrt(var + EPS) * lng_ref[...] + lnb_ref[...]
    y = jax.nn.silu(y)
    y = jnp.dot(y.astype(BF16), wpw_ref[...], preferred_element_type=F32)
    o_ref[0] = (_rms_rows(y) * go_ref[...]).astype(o_ref.dtype)


def _conv(ztok, w_dw, b_dw, ln_g, ln_b, w_pw, g_out, tt=512):
    b, s, _ = ztok.shape
    cw = CONV_WIDTH
    hb = tt // CONV_HALO
    kern = functools.partial(_conv_kernel, tt=tt)
    row = lambda v: v.reshape(1, cw)
    return pl.pallas_call(
        kern,
        grid=(b, s // tt),
        in_specs=[pl.BlockSpec((1, tt, 2 * cw), lambda i, j: (i, j, 0)),
                  pl.BlockSpec((1, CONV_HALO, 2 * cw), lambda i, j: (i, jnp.maximum(j * hb - 1, 0), 0)),
                  pl.BlockSpec((CONV_K + 1, cw), lambda i, j: (0, 0)),
                  pl.BlockSpec((1, cw), lambda i, j: (0, 0)),
                  pl.BlockSpec((1, cw), lambda i, j: (0, 0)),
                  pl.BlockSpec((1, cw), lambda i, j: (0, 0)),
                  pl.BlockSpec((cw, cw), lambda i, j: (0, 0)),
                  pl.BlockSpec((1, cw), lambda i, j: (0, 0))],
        out_specs=pl.BlockSpec((1, tt, cw), lambda i, j: (i, j, 0)),
        out_shape=jax.ShapeDtypeStruct((b, s, cw), BF16),
        scratch_shapes=[pltpu.VMEM((CONV_HALO + tt, cw), F32)],
        compiler_params=_cparams(("parallel", "arbitrary")),
        name="conv_mixer",
    )(ztok, ztok, jnp.concatenate([w_dw, jnp.zeros((1, cw), F32)], 0), row(b_dw), row(ln_g), row(ln_b),
      w_pw.astype(BF16), row(g_out))


def _knorm_kernel(ks_ref, kw_ref, gs_ref, gw_ref, os_ref, ow_ref):
    for src, g_ref, dst in ((ks_ref, gs_ref, os_ref), (kw_ref, gw_ref, ow_ref)):
        x = src[0]
        for h in range(NSA_KV_HEADS):
            xh = x[:, h * HEAD_DIM:(h + 1) * HEAD_DIM]
            dst[0, h] = (_rms_rows(xh) * g_ref[...]).astype(dst.dtype)


def _knorm(ztok, col_s, col_w, gain_s, gain_w, tt=512):
    b, s, _ = ztok.shape
    kw = NSA_KV_HEADS * HEAD_DIM
    out = jax.ShapeDtypeStruct((b, NSA_KV_HEADS, s, HEAD_DIM), BF16)
    ospec = pl.BlockSpec((1, NSA_KV_HEADS, tt, HEAD_DIM), lambda i, j: (i, 0, j, 0))
    return pl.pallas_call(
        _knorm_kernel,
        grid=(b, s // tt),
        in_specs=[pl.BlockSpec((1, tt, kw), lambda i, j: (i, j, col_s // kw)),
                  pl.BlockSpec((1, tt, kw), lambda i, j: (i, j, col_w // kw)),
                  pl.BlockSpec((1, HEAD_DIM), lambda i, j: (0, 0)),
                  pl.BlockSpec((1, HEAD_DIM), lambda i, j: (0, 0))],
        out_specs=[ospec, ospec],
        out_shape=[out, out],
        compiler_params=_cparams(("parallel", "parallel")),
        name="nsa_key_norm",
    )(ztok, ztok, gain_s.reshape(1, HEAD_DIM), gain_w.reshape(1, HEAD_DIM))


def _compress_kernel(k_ref, v_ref, wka_ref, wkb_ref, ck_ref, w2k_ref, gk_ref,
                     wva_ref, wvb_ref, cv_ref, w2v_ref, ko_ref, vo_ref):
    hi = lax.Precision.HIGHEST
    n = k_ref.shape[1]
    kseg = k_ref[0]
    a = jnp.dot(kseg, wka_ref[...], precision=hi, preferred_element_type=F32)
    bm = jnp.dot(kseg, wkb_ref[...], precision=hi, preferred_element_type=F32)
    pre = a + pltpu.roll(bm, n - 1, 0) + ck_ref[...]
    kc = jnp.dot(jax.nn.gelu(pre), w2k_ref[...], precision=hi, preferred_element_type=F32)
    for h in range(NSA_KV_HEADS):
        kh = kc[:, h * HEAD_DIM:(h + 1) * HEAD_DIM]
        ko_ref[0, h] = (_rms_rows(kh) * gk_ref[...]).astype(ko_ref.dtype)
    vseg = v_ref[0]
    nt = (((1,), (1,)), ((), ()))
    at = lax.dot_general(wva_ref[...], vseg, nt, precision=hi, preferred_element_type=F32)
    bt = lax.dot_general(wvb_ref[...], vseg, nt, precision=hi, preferred_element_type=F32)
    pre_t = at + pltpu.roll(bt, n - 1, 1) + cv_ref[...]
    vt = jnp.dot(w2v_ref[...], jax.nn.gelu(pre_t), precision=hi, preferred_element_type=F32)
    for h in range(NSA_KV_HEADS):
        vo_ref[0, h] = vt[h * HEAD_DIM:(h + 1) * HEAD_DIM, :].astype(vo_ref.dtype)


def _blockdiag2(w):
    z = jnp.zeros_like(w)
    return jnp.concatenate([jnp.concatenate([w, z], 1), jnp.concatenate([z, w], 1)], 0)


def _compress(kseg, vseg, pe, w1, w2, k_gain):
    b, n, width = kseg.shape
    hd, kvw = HEAD_DIM, NSA_KV_HEADS * HEAD_DIM

    def expand(w):
        wl = w.reshape(L_CMP, hd, hd)
        halves = []
        for part in (wl[:CMP_STRIDE], wl[CMP_STRIDE:]):
            e = jnp.einsum('ldj,kq->lkdqj', part, jnp.eye(NSA_KV_HEADS, dtype=F32))
            halves.append(e.reshape(CMP_STRIDE * kvw, kvw))
        return halves

    wka, wkb = expand(w1[0])
    wva, wvb = expand(w1[1])
    ck = jnp.tile((pe[0].reshape(1, L_CMP * hd) @ w1[0]), (1, NSA_KV_HEADS))
    cv = jnp.tile((pe[1].reshape(1, L_CMP * hd) @ w1[1]), (1, NSA_KV_HEADS)).T
    full = lambda shape: pl.BlockSpec(shape, lambda i: tuple(0 for _ in shape))
    return pl.pallas_call(
        _compress_kernel,
        grid=(b,),
        in_specs=[pl.BlockSpec((1, n, width), lambda i: (i, 0, 0)),
                  pl.BlockSpec((1, n, width), lambda i: (i, 0, 0)),
                  full((width, kvw)), full((width, kvw)), full((1, kvw)), full((kvw, kvw)), full((1, hd)),
                  full((kvw, width)), full((kvw, width)), full((kvw, 1)), full((kvw, kvw))],
        out_specs=[pl.BlockSpec((1, NSA_KV_HEADS, n, hd), lambda i: (i, 0, 0, 0)),
                   pl.BlockSpec((1, NSA_KV_HEADS, hd, n), lambda i: (i, 0, 0, 0))],
        out_shape=[jax.ShapeDtypeStruct((b, NSA_KV_HEADS, n, hd), BF16),
                   jax.ShapeDtypeStruct((b, NSA_KV_HEADS, hd, n), BF16)],
        compiler_params=_cparams(("parallel",), _VMEM_LIMIT),
        name="nsa_compress",
    )(kseg, vseg, wka, wkb, ck, _blockdiag2(w2[0]), k_gain.reshape(1, hd),
      wva.T, wvb.T, cv, _blockdiag2(w2[1]).T)


def _t5_bias_by_dist(t5_table):
    n = np.arange(T5_MAX_DIST + 1)
    max_exact = T5_BUCKETS // 2
    nf = np.maximum(n, 1).astype(np.float32)
    large = max_exact + (np.log(nf / np.float32(max_exact)) / np.float32(math.log(T5_MAX_DIST / max_exact))
                         * np.float32(T5_BUCKETS - max_exact)).astype(np.int32)
    large = np.minimum(large, T5_BUCKETS - 1)
    bucket = np.where(n < max_exact, n, large)
    return t5_table[bucket]


def _bias_tile(fd, dist, valid):
    d = np.clip(dist, 0, T5_MAX_DIST)
    tile = jnp.moveaxis(fd[d], -1, 0)
    return jnp.where(jnp.asarray(valid)[None], tile, NEG).astype(F32)


def _nsa_bias_tiles(t5_table, seq):
    fd = _t5_bias_by_dist(t5_table).astype(F32)
    fd = (fd - fd[T5_MAX_DIST:]) * LOG2E
    i = np.arange(TQ)[None, :]
    n_cmp = seq // CMP_STRIDE
    r0 = n_cmp - TQ // CMP_STRIDE
    r = np.arange(2 * n_cmp - TQ // CMP_STRIDE)[:, None]
    dc = i - CMP_STRIDE * (r - r0) - (L_CMP - 1)
    cmp_t = _bias_tile(fd, dc, dc >= 0)
    ds = i + SEL_PAD - np.arange(SEL_PAD + TQ)[:, None]
    sel_t = _bias_tile(fd, ds, ds >= 0)
    rw = np.arange(WINDOW + TQ)[:, None]
    dw = i + WINDOW - rw
    win_t = jnp.stack([_bias_tile(fd, dw, (dw >= 0) & (dw < WINDOW) & (rw >= WINDOW - q0))
                       for q0 in (0, TQ, 2 * TQ)])
    split = lambda t: t.reshape(*t.shape[:-3], NSA_KV_HEADS, NSA_GROUP, *t.shape[-2:])
    return split(cmp_t), split(sel_t), split(win_t)


def _q_head(qt_ref, g, qg_ref):
    q = qt_ref[0, g * HEAD_DIM:(g + 1) * HEAD_DIM, :]
    inv = lax.rsqrt(jnp.mean(q * q, axis=0, keepdims=True) + EPS)
    return (q * inv * qg_ref[...]).astype(BF16)


def _nsa_cmp_kernel(qt_ref, qg_ref, kc_ref, vct_ref, bias_ref, ov_ref, oc_ref, sel_ref, *, n_cmp, n_sel):
    qi = pl.program_id(2)
    kc = kc_ref[0, 0]
    vct = vct_ref[0, 0]
    row0 = pl.multiple_of((n_cmp - TQ // CMP_STRIDE) - qi * (TQ // CMP_STRIDE), TQ // CMP_STRIDE)
    psum = jnp.zeros((n_cmp, TQ), F32)
    for g in range(NSA_GROUP):
        qb = _q_head(qt_ref, g, qg_ref)
        s = jnp.dot(kc, qb, preferred_element_type=F32) + bias_ref[0, g, pl.ds(row0, n_cmp), :]
        m = jnp.max(s, axis=0, keepdims=True)
        m = jnp.where(m < 0.5 * NEG, 0.0, m)
        p = jnp.exp2(s - m)
        p = p * (1.0 / jnp.maximum(jnp.sum(p, axis=0, keepdims=True), 1e-30))
        oc_ref[0, g * HEAD_DIM:(g + 1) * HEAD_DIM, :] = jnp.dot(vct, p.astype(BF16), preferred_element_type=F32)
        psum = psum + p
    hi = psum.astype(BF16)
    lo = (psum - hi.astype(F32)).astype(BF16)
    imp = (jnp.dot(ov_ref[...], hi, preferred_element_type=F32)
           + jnp.dot(ov_ref[...], lo, preferred_element_type=F32))
    blk = lax.broadcasted_iota(jnp.int32, (n_sel, TQ), 0)
    blk_t = lax.shift_right_logical(qi * TQ + lax.broadcasted_iota(jnp.int32, (n_sel, TQ), 1), L_SEL.bit_length() - 1)
    forced = (blk == 0) | (blk > blk_t - N_LOCAL)
    v0 = jnp.where(blk > blk_t, -jnp.inf, jnp.where(forced, SEL_FORCE, imp))

    def pick(_, v):
        m = jnp.max(v, axis=0, keepdims=True)
        cand = (v == m) & (m > -jnp.inf)
        first = jnp.min(jnp.where(cand, blk, n_sel), axis=0, keepdims=True)
        return jnp.where(blk == first, -jnp.inf, v)

    v = lax.fori_loop(0, min(N_SELECT, n_sel), pick, v0)
    sel_ref[0, 0] = jnp.where((v == -jnp.inf) & (v0 > -jnp.inf), 0.0, NEG)


def _nsa_cmp(zt, qg, kc, vct, bias_c, ovt):
    b, _, s = zt.shape
    n_cmp, n_sel = s // CMP_STRIDE, s // L_SEL
    gw = NSA_GROUP * HEAD_DIM
    kern = functools.partial(_nsa_cmp_kernel, n_cmp=n_cmp, n_sel=n_sel)
    return pl.pallas_call(
        kern,
        grid=(b, NSA_KV_HEADS, s // TQ),
        in_specs=[pl.BlockSpec((1, gw, TQ), lambda i, k, j: (i, k, j)),
                  pl.BlockSpec((HEAD_DIM, TQ), lambda i, k, j: (0, 0)),
                  pl.BlockSpec((1, 1, n_cmp, HEAD_DIM), lambda i, k, j: (i, k, 0, 0)),
                  pl.BlockSpec((1, 1, HEAD_DIM, n_cmp), lambda i, k, j: (i, k, 0, 0)),
                  pl.BlockSpec((1, NSA_GROUP, bias_c.shape[2], TQ), lambda i, k, j: (k, 0, 0, 0)),
                  pl.BlockSpec((n_sel, n_cmp), lambda i, k, j: (0, 0))],
        out_specs=[pl.BlockSpec((1, gw, TQ), lambda i, k, j: (i, k, j)),
                   pl.BlockSpec((1, 1, n_sel, TQ), lambda i, k, j: (i, k, 0, j))],
        out_shape=[jax.ShapeDtypeStruct((b, NSA_WIDTH, s), F32),
                   jax.ShapeDtypeStruct((b, NSA_KV_HEADS, n_sel, s), F32)],
        compiler_params=_cparams(("parallel", "parallel", "parallel"), _VMEM_LIMIT),
        name="nsa_compressed_select",
    )(zt, qg, kc, vct, bias_c, ovt)


def _nsa_main_kernel(qt_ref, qg_ref, ks_ref, vst_ref, kw_ref, vwt_ref, sel_ref, bs_ref, bw_ref, gate_ref,
                     oc_ref, o_ref, m_ref, l_ref, acc_ref):
    qi = pl.program_id(2)
    q0 = pl.multiple_of(qi * TQ, TQ)
    near = SEL_PAD + TQ

    def expand_sel(first_blk, n_blk):
        rows = [jnp.broadcast_to(sel_ref[0, 0, pl.ds(first_blk + r, 1), :], (L_SEL, TQ)) for r in range(n_blk)]
        return jnp.concatenate(rows, axis=0)

    qbs = [_q_head(qt_ref, g, qg_ref) for g in range(NSA_GROUP)]
    k_near = ks_ref[0, 0, pl.ds(q0, near), :]
    v_near = vst_ref[0, 0, :, pl.ds(q0, near)]
    mask_near = expand_sel(qi * (TQ // L_SEL), near // L_SEL)
    for g in range(NSA_GROUP):
        s = jnp.dot(k_near, qbs[g], preferred_element_type=F32) + bs_ref[0, g] + mask_near
        m = jnp.max(s, axis=0, keepdims=True)
        p = jnp.exp2(s - m)
        m_ref[g] = m
        l_ref[g] = jnp.sum(p, axis=0, keepdims=True)
        acc_ref[g] = jnp.dot(v_near, p.astype(BF16), preferred_element_type=F32)

    def far(c, carry):
        r0 = pl.multiple_of(c * TQ, TQ)
        kc = ks_ref[0, 0, pl.ds(r0, TQ), :]
        vc = vst_ref[0, 0, :, pl.ds(r0, TQ)]
        mask = expand_sel(c * (TQ // L_SEL), TQ // L_SEL)
        for g in range(NSA_GROUP):
            s = jnp.dot(kc, qbs[g], preferred_element_type=F32) + mask
            m_old = m_ref[g]
            m_new = jnp.maximum(m_old, jnp.max(s, axis=0, keepdims=True))
            alpha = jnp.exp2(m_old - m_new)
            p = jnp.exp2(s - m_new)
            m_ref[g] = m_new
            l_ref[g] = alpha * l_ref[g] + jnp.sum(p, axis=0, keepdims=True)
            acc_ref[g] = alpha * acc_ref[g] + jnp.dot(vc, p.astype(BF16), preferred_element_type=F32)
        return carry

    lax.fori_loop(SEL_PAD // TQ, qi, far, 0)

    k_win = kw_ref[0, 0, pl.ds(q0, WINDOW + TQ), :]
    v_win = vwt_ref[0, 0, :, pl.ds(q0, WINDOW + TQ)]
    for g in range(NSA_GROUP):
        s = jnp.dot(k_win, qbs[g], preferred_element_type=F32) + bw_ref[0, 0, g]
        m = jnp.max(s, axis=0, keepdims=True)
        p = jnp.exp2(s - m)
        lw = jnp.maximum(jnp.sum(p, axis=0, keepdims=True), 1e-30)
        ow = jnp.dot(v_win, p.astype(BF16), preferred_element_type=F32) * (1.0 / lw)
        os = acc_ref[g] * (1.0 / jnp.maximum(l_ref[g], 1e-30))
        gates = jax.nn.sigmoid(gate_ref[0, 0, g * N_BRANCH:(g + 1) * N_BRANCH, :])
        rows = slice(g * HEAD_DIM, (g + 1) * HEAD_DIM)
        o_ref[0, rows, :] = gates[0:1] * oc_ref[0, rows, :] + gates[1:2] * os + gates[2:3] * ow


def _nsa_main(zt, qg, ks_p, vst_p, kw_p, vwt_p, sel_p, bias_s, bias_w, gates_t, oc_t):
    b, _, s = zt.shape
    gw = NSA_GROUP * HEAD_DIM
    sp, wp = ks_p.shape[2], kw_p.shape[2]
    nb = sel_p.shape[2]
    return pl.pallas_call(
        _nsa_main_kernel,
        grid=(b, NSA_KV_HEADS, s // TQ),
        in_specs=[pl.BlockSpec((1, gw, TQ), lambda i, k, j: (i, k, j)),
                  pl.BlockSpec((HEAD_DIM, TQ), lambda i, k, j: (0, 0)),
                  pl.BlockSpec((1, 1, sp, HEAD_DIM), lambda i, k, j: (i, k, 0, 0)),
                  pl.BlockSpec((1, 1, HEAD_DIM, sp), lambda i, k, j: (i, k, 0, 0)),
                  pl.BlockSpec((1, 1, wp, HEAD_DIM), lambda i, k, j: (i, k, 0, 0)),
                  pl.BlockSpec((1, 1, HEAD_DIM, wp), lambda i, k, j: (i, k, 0, 0)),
                  pl.BlockSpec((1, 1, nb, TQ), lambda i, k, j: (i, k, 0, j)),
                  pl.BlockSpec((1, NSA_GROUP, SEL_PAD + TQ, TQ), lambda i, k, j: (k, 0, 0, 0)),
                  pl.BlockSpec((1, 1, NSA_GROUP, WINDOW + TQ, TQ), lambda i, k, j: (jnp.minimum(j, 2), k, 0, 0, 0)),
                  pl.BlockSpec((1, 1, 16, TQ), lambda i, k, j: (i, k, 0, j)),
                  pl.BlockSpec((1, gw, TQ), lambda i, k, j: (i, k, j))],
        out_specs=pl.BlockSpec((1, gw, TQ), lambda i, k, j: (i, k, j)),
        out_shape=jax.ShapeDtypeStruct((b, NSA_WIDTH, s), F32),
        scratch_shapes=[pltpu.VMEM((NSA_GROUP, 1, TQ), F32), pltpu.VMEM((NSA_GROUP, 1, TQ), F32),
                        pltpu.VMEM((NSA_GROUP, HEAD_DIM, TQ), F32)],
        compiler_params=_cparams(("parallel", "parallel", "arbitrary"), _VMEM_LIMIT),
        name="nsa_selected_window",
    )(zt, qg, ks_p, vst_p, kw_p, vwt_p, sel_p, bias_s, bias_w, gates_t, oc_t)


def _mixout_kernel(gy_ref, yc_ref, yn_ref, h_ref, wglu_ref, go_ref, wo_ref, o_ref):
    sw = SSM_WIDTH
    ag = jnp.dot(gy_ref[0], wglu_ref[...], preferred_element_type=F32)
    ys = ag[:, :sw] * jax.nn.sigmoid(ag[:, sw:])
    ys = (_rms_rows(ys) * go_ref[:, 0:sw]).astype(BF16)
    yn = (_rms_rows(yn_ref[0]) * go_ref[:, 2 * sw:]).astype(BF16)
    out = (jnp.dot(ys, wo_ref[0:sw, :], preferred_element_type=F32)
           + jnp.dot(yc_ref[0], wo_ref[sw:2 * sw, :], preferred_element_type=F32)
           + jnp.dot(yn, wo_ref[2 * sw:, :], preferred_element_type=F32))
    o_ref[0] = h_ref[0] + out


def _mixout(gy, yc, yn, h, w_glu, g_out, w_out, tm=512):
    b, s, d = h.shape
    tok = lambda w: pl.BlockSpec((1, tm, w), lambda i, j: (i, j, 0))
    return pl.pallas_call(
        _mixout_kernel,
        grid=(b, s // tm),
        in_specs=[tok(SSM_WIDTH), tok(CONV_WIDTH), tok(NSA_WIDTH), tok(d),
                  pl.BlockSpec((SSM_WIDTH, 2 * SSM_WIDTH), lambda i, j: (0, 0)),
                  pl.BlockSpec((1, d), lambda i, j: (0, 0)),
                  pl.BlockSpec((d, d), lambda i, j: (0, 0))],
        out_specs=tok(d),
        out_shape=jax.ShapeDtypeStruct((b, s, d), F32),
        compiler_params=_cparams(("parallel", "parallel")),
        name="mix_out",
    )(gy, yc, yn, h, w_glu.astype(BF16), g_out.reshape(1, d), w_out.astype(BF16))


def _memkv_kernel(mem_ref, g_ref, w_ref, kg_ref, k_ref, v_ref):
    mn = (_rms_rows(mem_ref[0]) * g_ref[...]).astype(BF16)
    kv = jnp.dot(mn, w_ref[...], preferred_element_type=F32)
    for h in range(X_HEADS):
        cols = slice(h * HEAD_DIM, (h + 1) * HEAD_DIM)
        k_ref[0, :, cols] = (_rms_rows(kv[:, cols]) * kg_ref[...]).astype(k_ref.dtype)
    v_ref[0] = kv[:, X_WIDTH:].astype(v_ref.dtype)


def _memkv(mem, gain, w_kv, k_gain):
    b, m, d = mem.shape
    out = jax.ShapeDtypeStruct((b, m, X_WIDTH), BF16)
    return pl.pallas_call(
        _memkv_kernel,
        grid=(b,),
        in_specs=[pl.BlockSpec((1, m, d), lambda i: (i, 0, 0)),
                  pl.BlockSpec((1, d), lambda i: (0, 0)),
                  pl.BlockSpec((d, 2 * X_WIDTH), lambda i: (0, 0)),
                  pl.BlockSpec((1, HEAD_DIM), lambda i: (0, 0))],
        out_specs=[pl.BlockSpec((1, m, X_WIDTH), lambda i: (i, 0, 0))] * 2,
        out_shape=[out, out],
        compiler_params=_cparams(("parallel",)),
        name="cross_mem_kv",
    )(mem, gain.reshape(1, d), w_kv.astype(BF16), k_gain.reshape(1, HEAD_DIM))


def _cross_kernel(h_ref, g_ref, wq_ref, qg_ref, k_ref, v_ref, wo_ref, o_ref):
    h = h_ref[0]
    hn = (_rms_rows(h) * g_ref[...]).astype(BF16)
    q = jnp.dot(hn, wq_ref[...], preferred_element_type=F32)
    out = h
    for hd in range(X_HEADS):
        cols = slice(hd * HEAD_DIM, (hd + 1) * HEAD_DIM)
        qh = (_rms_rows(q[:, cols]) * qg_ref[...]).astype(BF16)
        s = lax.dot_general(qh, k_ref[0, :, cols], (((1,), (1,)), ((), ())), preferred_element_type=F32)
        p = jnp.exp(s - jnp.max(s, axis=-1, keepdims=True))
        p = p * (1.0 / jnp.sum(p, axis=-1, keepdims=True))
        o = jnp.dot(p.astype(BF16), v_ref[0, :, cols], preferred_element_type=F32)
        out = out + jnp.dot(o.astype(BF16), wo_ref[cols, :], preferred_element_type=F32)
    o_ref[0] = out


def _cross(h, gain, w_q, q_gain, k, v, w_o, tm=512):
    b, s, d = h.shape
    m = k.shape[1]
    return pl.pallas_call(
        _cross_kernel,
        grid=(b, s // tm),
        in_specs=[pl.BlockSpec((1, tm, d), lambda i, j: (i, j, 0)),
                  pl.BlockSpec((1, d), lambda i, j: (0, 0)),
                  pl.BlockSpec((d, X_WIDTH), lambda i, j: (0, 0)),
                  pl.BlockSpec((1, HEAD_DIM), lambda i, j: (0, 0)),
                  pl.BlockSpec((1, m, X_WIDTH), lambda i, j: (i, 0, 0)),
                  pl.BlockSpec((1, m, X_WIDTH), lambda i, j: (i, 0, 0)),
                  pl.BlockSpec((X_WIDTH, d), lambda i, j: (0, 0))],
        out_specs=pl.BlockSpec((1, tm, d), lambda i, j: (i, j, 0)),
        out_shape=jax.ShapeDtypeStruct((b, s, d), F32),
        compiler_params=_cparams(("parallel", "parallel")),
        name="cross_attention",
    )(h, gain.reshape(1, d), w_q.astype(BF16), (q_gain * HEAD_DIM ** -0.5).reshape(1, HEAD_DIM), k, v,
      w_o.astype(BF16))


def _ffn_kernel(h_ref, g_ref, wg_ref, wv_ref, wd_ref, o_ref, xn_ref, acc_ref):
    f = pl.program_id(1)

    @pl.when(f == 0)
    def _():
        xn_ref[...] = (_rms_rows(h_ref[...]) * g_ref[...]).astype(BF16)
        acc_ref[...] = jnp.zeros_like(acc_ref)

    x = xn_ref[...]
    gate = jnp.dot(x, wg_ref[...], preferred_element_type=F32)
    val = jnp.dot(x, wv_ref[...], preferred_element_type=F32)
    act = (jax.nn.silu(gate) * val).astype(BF16)
    acc_ref[...] += jnp.dot(act, wd_ref[...], preferred_element_type=F32)

    @pl.when(f == pl.num_programs(1) - 1)
    def _():
        o_ref[...] = h_ref[...] + acc_ref[...]


def _ffn(h2d, gain, w_up, w_down, tm=1024):
    t, d = h2d.shape
    nf = D_FF // FF_CHUNK
    wb = w_up.astype(BF16)
    return pl.pallas_call(
        _ffn_kernel,
        grid=(t // tm, nf),
        in_specs=[pl.BlockSpec((tm, d), lambda i, f: (i, 0)),
                  pl.BlockSpec((1, d), lambda i, f: (0, 0)),
                  pl.BlockSpec((d, FF_CHUNK), lambda i, f: (0, f)),
                  pl.BlockSpec((d, FF_CHUNK), lambda i, f: (0, f + nf)),
                  pl.BlockSpec((FF_CHUNK, d), lambda i, f: (f, 0))],
        out_specs=pl.BlockSpec((tm, d), lambda i, f: (i, 0)),
        out_shape=jax.ShapeDtypeStruct((t, d), F32),
        scratch_shapes=[pltpu.VMEM((tm, d), BF16), pltpu.VMEM((tm, d), F32)],
        compiler_params=_cparams(("parallel", "arbitrary"), _VMEM_LIMIT),
        name="ffn_swiglu",
    )(h2d, gain.reshape(1, d), wb, wb, w_down.astype(BF16))


def _router_kernel(h_ref, g_ref, wr_ref, xn_ref, gate_ref, asg_ref):
    xn = _rms_rows(h_ref[...]) * g_ref[...]
    xn_ref[...] = xn.astype(BF16)
    logits = jnp.dot(xn, wr_ref[...], precision=lax.Precision.HIGHEST, preferred_element_type=F32)
    lane = lax.broadcasted_iota(jnp.int32, logits.shape, 1)
    lg = jnp.where(lane < N_EXPERTS, logits, -jnp.inf)
    m1 = jnp.max(lg, axis=-1, keepdims=True)
    i1 = jnp.min(jnp.where(lg == m1, lane, 128), axis=-1, keepdims=True)
    lg2 = jnp.where(lane == i1, -jnp.inf, lg)
    m2 = jnp.max(lg2, axis=-1, keepdims=True)
    i2 = jnp.min(jnp.where(lg2 == m2, lane, 128), axis=-1, keepdims=True)
    e = jnp.exp(m2 - m1)
    den = 1.0 + e
    gate_ref[...] = jnp.where(lane == i1, 1.0 / den, jnp.where(lane == i2, e / den, 0.0))
    asg_ref[...] = ((lane == i1) | (lane == i2)).astype(jnp.int32)


def _router(h2d, gain, w_router, tm=512):
    t, d = h2d.shape
    wr = jnp.concatenate([w_router, jnp.zeros((d, 128 - N_EXPERTS), F32)], axis=1)
    return pl.pallas_call(
        _router_kernel,
        grid=(t // tm,),
        in_specs=[pl.BlockSpec((tm, d), lambda i: (i, 0)),
                  pl.BlockSpec((1, d), lambda i: (0, 0)),
                  pl.BlockSpec((d, 128), lambda i: (0, 0))],
        out_specs=[pl.BlockSpec((tm, d), lambda i: (i, 0)),
                   pl.BlockSpec((tm, 128), lambda i: (i, 0)),
                   pl.BlockSpec((tm, 128), lambda i: (i, 0))],
        out_shape=[jax.ShapeDtypeStruct((t, d), BF16), jax.ShapeDtypeStruct((t, 128), F32),
                   jax.ShapeDtypeStruct((t, 128), jnp.int32)],
        compiler_params=_cparams(("parallel",)),
        name="moe_router",
    )(h2d, gain.reshape(1, d), wr)


def _moe_gather_kernel(rb_ref, lo_ref, hi_ref, first_ref, tgt_ref, x_ref, o_ref):
    e, j, slot = pl.program_id(0), pl.program_id(1), pl.program_id(2)
    rb = rb_ref[e, j, slot]
    lo, hi = lo_ref[e, j], hi_ref[e, j]

    @pl.when(first_ref[e, j, slot] == 1)
    def _():
        o_ref[...] = jnp.zeros_like(o_ref)

    active = (slot == 0) | (rb != rb_ref[e, j, 0])
    tgt = tgt_ref[0]
    for sub in range(MOE_TB // MOE_SUB):
        base = rb * MOE_TB + sub * MOE_SUB

        @pl.when(active & (lo < base + MOE_SUB) & (hi > base))
        def _():
            rows = base + lax.broadcasted_iota(jnp.int32, (MOE_SUB, MOE_TB), 0)
            onehot = jnp.where(tgt == rows, 1.0, 0.0).astype(BF16)
            part = jnp.dot(onehot, x_ref[...], preferred_element_type=F32)
            sl = slice(sub * MOE_SUB, (sub + 1) * MOE_SUB)
            o_ref[sl, :] = o_ref[sl, :] + part.astype(o_ref.dtype)


def _moe_ffn_kernel(exp_ref, nused_ref, x_ref, wg_ref, wv_ref, wd_ref, o_ref, acc_ref):
    r, f = pl.program_id(0), pl.program_id(1)
    used = r < nused_ref[0]

    @pl.when(f == 0)
    def _():
        acc_ref[...] = jnp.zeros_like(acc_ref)

    @pl.when(used)
    def _():
        x = x_ref[...]
        gate = jnp.dot(x, wg_ref[0], preferred_element_type=F32)
        val = jnp.dot(x, wv_ref[0], preferred_element_type=F32)
        act = (jax.nn.silu(gate) * val).astype(BF16)
        acc_ref[...] += jnp.dot(act, wd_ref[0], preferred_element_type=F32)

    @pl.when(f == pl.num_programs(1) - 1)
    def _():
        o_ref[...] = acc_ref[...].astype(o_ref.dtype)
    del exp_ref


def _moe_scatter_kernel(rb_ref, lo_ref, hi_ref, tgt_ref, gate_ref, y_ref, h_ref, o_ref):
    j, e, slot = pl.program_id(0), pl.program_id(1), pl.program_id(2)
    rb = rb_ref[e, j, slot]
    lo, hi = lo_ref[e, j], hi_ref[e, j]

    @pl.when((e == 0) & (slot == 0))
    def _():
        o_ref[...] = h_ref[...]

    active = (slot == 0) | (rb != rb_ref[e, j, 0])
    tgt = tgt_ref[0]
    gate = gate_ref[0]
    for sub in range(MOE_TB // MOE_SUB):
        base = rb * MOE_TB + sub * MOE_SUB

        @pl.when(active & (lo < base + MOE_SUB) & (hi > base))
        def _():
            rows = base + lax.broadcasted_iota(jnp.int32, (MOE_TB, MOE_SUB), 1)
            onehot = jnp.where(tgt == rows, 1.0, 0.0).astype(BF16)
            part = jnp.dot(onehot, y_ref[sub * MOE_SUB:(sub + 1) * MOE_SUB, :], preferred_element_type=F32)
            o_ref[...] = o_ref[...] + gate * part


def _moe(h2d, gain, w_router, w_up, w_down):
    t, d = h2d.shape
    tb = MOE_TB
    nj = t // tb
    n_rb = (t * TOP_K) // tb + N_EXPERTS
    xn, gates, asg = _router(h2d, gain, w_router)
    asg = asg[:, :N_EXPERTS]
    gates = gates[:, :N_EXPERTS]
    cs = jnp.cumsum(asg, axis=0)
    rank = cs - asg
    counts = cs[-1]
    padded = (counts + tb - 1) // tb * tb
    pad_end = jnp.cumsum(padded)
    start_p = pad_end - padded
    tgt = jnp.where(asg == 1, start_p[None, :] + rank, -1).astype(jnp.int32)
    cb = jnp.concatenate([jnp.zeros((1, N_EXPERTS), jnp.int32), cs[tb - 1::tb]], axis=0)
    lo = (start_p[None, :] + cb[:-1]).T.astype(jnp.int32)
    hi = (start_p[None, :] + cb[1:]).T.astype(jnp.int32)
    rb0 = lo // tb
    rb1 = jnp.maximum(rb0, (hi - 1) // tb)
    rb = jnp.stack([rb0, rb1], axis=-1).astype(jnp.int32)
    flat = rb.reshape(-1)
    first = jnp.concatenate([jnp.ones((1,), jnp.int32), (flat[1:] != flat[:-1]).astype(jnp.int32)])
    first = first.reshape(N_EXPERTS, nj, 2)
    n_used = (pad_end[-1] // tb).astype(jnp.int32).reshape(1)
    blk_exp = jnp.minimum(jnp.searchsorted(pad_end, jnp.arange(n_rb) * tb, side='right'),
                          N_EXPERTS - 1).astype(jnp.int32)

    xs = pl.pallas_call(
        _moe_gather_kernel,
        grid_spec=pltpu.PrefetchScalarGridSpec(
            num_scalar_prefetch=4,
            grid=(N_EXPERTS, nj, 2),
            in_specs=[pl.BlockSpec((1, 1, tb), lambda e, j, s, *_: (e, 0, j)),
                      pl.BlockSpec((tb, d), lambda e, j, s, *_: (j, 0))],
            out_specs=pl.BlockSpec((tb, d), lambda e, j, s, rb_ref, *_: (rb_ref[e, j, s], 0))),
        out_shape=jax.ShapeDtypeStruct((n_rb * tb, d), BF16),
        compiler_params=_cparams(("arbitrary", "arbitrary", "arbitrary"), _VMEM_LIMIT),
        name="moe_gather",
    )(rb, lo, hi, first, tgt.T.reshape(N_EXPERTS, 1, t), xn)

    nf = D_FF // FF_CHUNK
    wub = w_up.astype(BF16)
    ys = pl.pallas_call(
        _moe_ffn_kernel,
        grid_spec=pltpu.PrefetchScalarGridSpec(
            num_scalar_prefetch=2,
            grid=(n_rb, nf),
            in_specs=[pl.BlockSpec((tb, d), lambda r, f, *_: (r, 0)),
                      pl.BlockSpec((1, d, FF_CHUNK), lambda r, f, ex, nu: (ex[r], 0, f)),
                      pl.BlockSpec((1, d, FF_CHUNK), lambda r, f, ex, nu: (ex[r], 0, f + nf)),
                      pl.BlockSpec((1, FF_CHUNK, d), lambda r, f, ex, nu: (ex[r], f, 0))],
            out_specs=pl.BlockSpec((tb, d), lambda r, f, *_: (r, 0)),
            scratch_shapes=[pltpu.VMEM((tb, d), F32)]),
        out_shape=jax.ShapeDtypeStruct((n_rb * tb, d), BF16),
        compiler_params=_cparams(("arbitrary", "arbitrary"), _VMEM_LIMIT),
        name="moe_expert_ffn",
    )(blk_exp, n_used, xs, wub, wub, w_down.astype(BF16))

    return pl.pallas_call(
        _moe_scatter_kernel,
        grid_spec=pltpu.PrefetchScalarGridSpec(
            num_scalar_prefetch=3,
            grid=(nj, N_EXPERTS, 2),
            in_specs=[pl.BlockSpec((1, tb, 1), lambda j, e, s, *_: (e, j, 0)),
                      pl.BlockSpec((1, tb, 1), lambda j, e, s, *_: (e, j, 0)),
                      pl.BlockSpec((tb, d), lambda j, e, s, rb_ref, *_: (rb_ref[e, j, s], 0)),
                      pl.BlockSpec((tb, d), lambda j, e, s, *_: (j, 0))],
            out_specs=pl.BlockSpec((tb, d), lambda j, e, s, *_: (j, 0))),
        out_shape=jax.ShapeDtypeStruct((t, d), F32),
        compiler_params=_cparams(("arbitrary", "arbitrary", "arbitrary"), _VMEM_LIMIT),
        name="moe_scatter",
    )(rb, lo, hi, tgt.T.reshape(N_EXPERTS, t, 1), gates.T.reshape(N_EXPERTS, t, 1), ys, h2d)


_COL_CONV, _COL_SSM, _COL_KC, _COL_VC, _COL_KS, _COL_KW = 0, 512, 768, 896, 1024, 1152
_ROW_Q, _ROW_VS, _ROW_VW, _ROW_G = 0, 512, 640, 768


def _split_w_in(w_in):
    kvw = NSA_KV_HEADS * HEAD_DIM
    cuts = np.cumsum([0, SSM_WIDTH, 2 * CONV_WIDTH, NSA_WIDTH] + [kvw] * 6 + [N_BRANCH * NSA_HEADS])
    seg = lambda i: w_in[:, cuts[i]:cuts[i + 1]]
    ssm, conv, q, k_c, v_c, k_s, v_s, k_w, v_w, gate = (seg(i) for i in range(10))
    w_tok = jnp.concatenate([conv, ssm, k_c, v_c, k_s, k_w], axis=1).astype(BF16)
    gate = jnp.concatenate([gate, jnp.zeros((w_in.shape[0], 8), F32)], axis=1)
    w_t = jnp.concatenate([q, v_s, v_w, gate], axis=1).T.astype(BF16)
    return w_tok, w_t


def _layer_mixers(h, p, t5_tiles):
    b, s, d = h.shape
    w_tok, w_t = _split_w_in(p['w_in'])
    ztok, zt = _proj(h, p['norm_mix'], w_tok, w_t)

    n_chunks = s // SSM_CHUNK
    u = ztok[:, :, _COL_SSM:_COL_SSM + SSM_WIDTH].astype(BF16)
    xg = u.reshape(b, n_chunks, SSM_CHUNK, SSM_GROUPS, SSM_GROUP).transpose(3, 1, 0, 2, 4)
    xg = xg.reshape(SSM_GROUPS, n_chunks * b, SSM_CHUNK * SSM_GROUP)
    tables = _s5_tables(p['ssm_lambda_re'], p['ssm_lambda_im'], p['ssm_log_dt'], p['ssm_b_re'], p['ssm_b_im'],
                        p['ssm_c_re'], p['ssm_c_im'], p['ssm_d'])
    gy = _s5(xg, tables, b)
    gy = gy.reshape(SSM_GROUPS, n_chunks, b, SSM_CHUNK, SSM_GROUP).transpose(2, 1, 3, 0, 4).reshape(b, s, SSM_WIDTH)

    g_out = p['mix_out_norm']
    yc = _conv(ztok, p['conv_w_dw'], p['conv_b_dw'], p['conv_ln_g'], p['conv_ln_b'], p['conv_w_pw'],
               g_out[SSM_WIDTH:SSM_WIDTH + CONV_WIDTH])

    kvw = NSA_KV_HEADS * HEAD_DIM
    n_cmp = s // CMP_STRIDE
    k_norm = p['nsa_k_norm']
    kseg = ztok[:, :, _COL_KC:_COL_KC + kvw].reshape(b, n_cmp, CMP_STRIDE * kvw)
    vseg = ztok[:, :, _COL_VC:_COL_VC + kvw].reshape(b, n_cmp, CMP_STRIDE * kvw)
    kc, vct = _compress(kseg, vseg, p['nsa_cmp_pe'], p['nsa_cmp_w1'], p['nsa_cmp_w2'], k_norm[0])
    ks, kw = _knorm(ztok, _COL_KS, _COL_KW, k_norm[1], k_norm[2])
    qg = jnp.broadcast_to((p['nsa_q_norm'] * (HEAD_DIM ** -0.5 * LOG2E))[:, None], (HEAD_DIM, TQ))
    bias_c, bias_s, bias_w = t5_tiles
    n_sel = s // L_SEL
    cs_ = np.arange(n_cmp) * CMP_STRIDE
    ss_ = np.arange(n_sel) * L_SEL
    ov = np.maximum(np.minimum(cs_[:, None] + L_CMP, ss_[None, :] + L_SEL) - np.maximum(cs_[:, None], ss_[None, :]), 0)
    ovt = jnp.asarray((ov.astype(np.float32) / L_CMP).T, BF16)
    oc_t, sel = _nsa_cmp(zt, qg, kc, vct, bias_c, ovt)

    front = lambda x, n, axis: jnp.pad(x, [(n, 0) if a == axis else (0, 0) for a in range(x.ndim)])
    heads_t = lambda rows: zt[:, rows:rows + kvw, :].astype(BF16).reshape(b, NSA_KV_HEADS, HEAD_DIM, s)
    ks_p = front(ks, SEL_PAD, 2)
    kw_p = front(kw, WINDOW, 2)
    vst_p = front(heads_t(_ROW_VS), SEL_PAD, 3)
    vwt_p = front(heads_t(_ROW_VW), WINDOW, 3)
    sel_p = jnp.pad(sel, ((0, 0), (0, 0), (SEL_PAD // L_SEL, 0), (0, 0)), constant_values=NEG)
    gl = zt[:, _ROW_G:_ROW_G + N_BRANCH * NSA_HEADS, :].reshape(b, NSA_KV_HEADS, NSA_GROUP * N_BRANCH, s)
    gates_t = jnp.pad(gl, ((0, 0), (0, 0), (0, 16 - NSA_GROUP * N_BRANCH), (0, 0)))
    yn_t = _nsa_main(zt, qg, ks_p, vst_p, kw_p, vwt_p, sel_p, bias_s, bias_w, gates_t, oc_t)
    yn = yn_t.transpose(0, 2, 1)

    return _mixout(gy, yc, yn, h, p['ssm_w_glu'], g_out, p['w_out'])


def kernel(x, mem, norm_mix, w_in, ssm_lambda_re, ssm_lambda_im, ssm_log_dt, ssm_b_re, ssm_b_im, ssm_c_re, ssm_c_im, ssm_d, ssm_w_glu, conv_w_dw, conv_b_dw, conv_ln_g, conv_ln_b, conv_w_pw, nsa_q_norm, nsa_k_norm, nsa_cmp_pe, nsa_cmp_w1, nsa_cmp_w2, mix_out_norm, w_out, t5_table, norm_cross, norm_mem, x_w_q, x_w_kv, x_q_norm, x_k_norm, x_w_o, norm_ffn, ffn_w_up, ffn_w_down, moe_router, moe_w_up, moe_w_down):
    b, s, d = x.shape
    depth = w_in.shape[0]
    per_layer = dict(norm_mix=norm_mix, w_in=w_in, ssm_lambda_re=ssm_lambda_re, ssm_lambda_im=ssm_lambda_im,
                     ssm_log_dt=ssm_log_dt, ssm_b_re=ssm_b_re, ssm_b_im=ssm_b_im, ssm_c_re=ssm_c_re,
                     ssm_c_im=ssm_c_im, ssm_d=ssm_d, ssm_w_glu=ssm_w_glu, conv_w_dw=conv_w_dw,
                     conv_b_dw=conv_b_dw, conv_ln_g=conv_ln_g, conv_ln_b=conv_ln_b, conv_w_pw=conv_w_pw,
                     nsa_q_norm=nsa_q_norm, nsa_k_norm=nsa_k_norm, nsa_cmp_pe=nsa_cmp_pe, nsa_cmp_w1=nsa_cmp_w1,
                     nsa_cmp_w2=nsa_cmp_w2, mix_out_norm=mix_out_norm, w_out=w_out)
    t5_tiles = _nsa_bias_tiles(t5_table, s)
    h = x
    for layer in range(depth):
        p = {k: v[layer] for k, v in per_layer.items()}
        h = _layer_mixers(h, p, t5_tiles)
        mk, mv = _memkv(mem, norm_mem[layer], x_w_kv[layer], x_k_norm[layer])
        h = _cross(h, norm_cross[layer], x_w_q[layer], x_q_norm[layer], mk, mv, x_w_o[layer])
        h2d = h.reshape(b * s, d)
        if layer % 2 == 0:
            h2d = _ffn(h2d, norm_ffn[layer], ffn_w_up[layer // 2], ffn_w_down[layer // 2])
        else:
            h2d = _moe(h2d, norm_ffn[layer], moe_router[layer // 2], moe_w_up[layer // 2], moe_w_down[layer // 2])
        h = h2d.reshape(b, s, d)
    return h
```

```python
import functools
import math

import jax
import jax.numpy as jnp
import numpy as np
from jax import lax
from jax.experimental import pallas as pl
from jax.experimental.pallas import tpu as pltpu

F32 = jnp.float32
BF16 = jnp.bfloat16

D_MODEL = 1024
HEAD_DIM = 64
SSM_WIDTH = 256
SSM_GROUP = 16
SSM_GROUPS = 16
SSM_STATE = 64
SSM_CHUNK = 16
CONV_WIDTH = 256
CONV_K = 31
CONV_HALO = 32
NSA_WIDTH = 512
NSA_HEADS = 8
NSA_KV_HEADS = 2
NSA_GROUP = 4
N_BRANCH = 3
L_CMP = 32
CMP_STRIDE = 16
L_SEL = 64
N_SELECT = 16
N_LOCAL = 2
WINDOW = 512
SEL_FORCE = 1e6
T5_BUCKETS = 32
T5_MAX_DIST = 128
X_HEADS = 4
X_WIDTH = 256
D_FF = 2816
N_EXPERTS = 8
TOP_K = 2
EPS = 1e-6
NEG = -1e30
LOG2E = math.log2(math.e)

TQ = 256
SEL_PAD = 256
FF_CHUNK = 256
MOE_TB = 1024
MOE_SUB = 128

_VMEM_LIMIT = 56 * 1024 * 1024


def _cparams(sem, vmem=None):
    return pltpu.CompilerParams(dimension_semantics=sem, vmem_limit_bytes=vmem)


def _rms_rows(x):
    return x * lax.rsqrt(jnp.mean(x * x, axis=-1, keepdims=True) + EPS)


def _proj_kernel(x_ref, g_ref, wtok_ref, wt_ref, ztok_ref, zt_ref):
    xn = (_rms_rows(x_ref[0]) * g_ref[...]).astype(BF16)
    ztok_ref[0] = jnp.dot(xn, wtok_ref[...], preferred_element_type=F32)
    zt_ref[0] = lax.dot_general(wt_ref[...], xn, (((1,), (1,)), ((), ())), preferred_element_type=F32)


def _proj(h, gain, w_tok, w_t, tm=512):
    b, s, d = h.shape
    ntok, nt = w_tok.shape[1], w_t.shape[0]
    return pl.pallas_call(
        _proj_kernel,
        grid=(b, s // tm),
        in_specs=[pl.BlockSpec((1, tm, d), lambda i, j: (i, j, 0)),
                  pl.BlockSpec((1, d), lambda i, j: (0, 0)),
                  pl.BlockSpec((d, ntok), lambda i, j: (0, 0)),
                  pl.BlockSpec((nt, d), lambda i, j: (0, 0))],
        out_specs=[pl.BlockSpec((1, tm, ntok), lambda i, j: (i, j, 0)),
                   pl.BlockSpec((1, nt, tm), lambda i, j: (i, 0, j))],
        out_shape=[jax.ShapeDtypeStruct((b, s, ntok), F32), jax.ShapeDtypeStruct((b, nt, s), F32)],
        compiler_params=_cparams(("parallel", "parallel"), _VMEM_LIMIT),
        name="proj",
    )(h, gain.reshape(1, d), w_tok, w_t)


def _s5_tables(lam_re, lam_im, log_dt, b_re, b_im, c_re, c_im, d_skip):
    L, H, P = SSM_CHUNK, SSM_GROUP, SSM_STATE
    dt = jnp.exp(log_dt.astype(F32))[:, None]
    lr, li = lam_re.astype(F32), lam_im.astype(F32)
    mag = jnp.exp(lr * dt)
    ar, ai = mag * jnp.cos(li * dt), mag * jnp.sin(li * dt)
    den = lr * lr + li * li
    fr = ((ar - 1.0) * lr + ai * li) / den
    fi = (ai * lr - (ar - 1.0) * li) / den
    bbr = fr[..., None] * b_re - fi[..., None] * b_im
    bbi = fr[..., None] * b_im + fi[..., None] * b_re
    j = jnp.arange(L + 1, dtype=F32)[:, None, None]
    pmag = jnp.exp(lr[None] * dt[None] * j)
    pr, pi = pmag * jnp.cos(li[None] * dt[None] * j), pmag * jnp.sin(li[None] * dt[None] * j)
    cbr = c_re[:, :, :, None] * bbr[:, None, :, :] - c_im[:, :, :, None] * bbi[:, None, :, :]
    cbi = c_re[:, :, :, None] * bbi[:, None, :, :] + c_im[:, :, :, None] * bbr[:, None, :, :]
    hp = lax.Precision.HIGHEST
    kj = (jnp.einsum('jgp,ghpk->jghk', pr[:L], cbr, precision=hp)
          - jnp.einsum('jgp,ghpk->jghk', pi[:L], cbi, precision=hp))
    lag = np.arange(L)[None, :] - np.arange(L)[:, None]
    place = (lag[None] == np.arange(L)[:, None, None]).astype(np.float32)
    kt = jnp.einsum('jab,jghk->abghk', place, kj, precision=hp)
    kt = kt + (jnp.eye(L)[:, :, None, None, None] * (jnp.eye(H)[None, None, None] * d_skip[None, None, :, :, None]))
    tmat = kt.transpose(2, 0, 4, 1, 3).reshape(SSM_GROUPS, L * H, L * H)
    qr, qi = pr[:L][::-1], pi[:L][::-1]
    wre = qr[..., None] * bbr[None] - qi[..., None] * bbi[None]
    wim = qr[..., None] * bbi[None] + qi[..., None] * bbr[None]
    wre = wre.transpose(1, 0, 3, 2).reshape(SSM_GROUPS, L * H, P)
    wim = wim.transpose(1, 0, 3, 2).reshape(SSM_GROUPS, L * H, P)
    w1 = jnp.concatenate([wre, wim], axis=-1)
    w2 = jnp.concatenate([wim, wre], axis=-1)
    sr, si = pr[1:], pi[1:]
    vr = c_re[None] * sr[:, :, None, :] - c_im[None] * si[:, :, None, :]
    vi = c_re[None] * si[:, :, None, :] + c_im[None] * sr[:, :, None, :]
    vmat = jnp.concatenate([vr, -vi], axis=-1).transpose(1, 3, 0, 2).reshape(SSM_GROUPS, 2 * P, L * H)
    a_r, a_i = pr[L], pi[L]
    am = jnp.stack([jnp.concatenate([a_r, a_r], -1), jnp.concatenate([-a_i, a_i], -1),
                    jnp.concatenate([a_i, -a_i], -1)], axis=1)
    am = jnp.concatenate([am, jnp.zeros((SSM_GROUPS, 5, 2 * P), F32)], axis=1)
    return tmat.astype(BF16), w1.astype(BF16), w2.astype(BF16), vmat.astype(BF16), am


def _s5_kernel(x_ref, t_ref, w1_ref, w2_ref, v_ref, a_ref, o_ref, s1_ref, s2_ref, xin_ref, *, bsz, n_chunks):
    x = x_ref[0]
    s1_ref[...] = jnp.dot(x, w1_ref[0], preferred_element_type=F32)
    s2_ref[...] = jnp.dot(x, w2_ref[0], preferred_element_type=F32)
    a1, a2, a3 = a_ref[0, 0:1, :], a_ref[0, 1:2, :], a_ref[0, 2:3, :]

    def step(c, carry):
        p, q = carry
        rows = pl.ds(pl.multiple_of(c * bsz, bsz), bsz)
        xin_ref[rows, :] = p
        return (p * a1 + q * a2 + s1_ref[rows, :], q * a1 + p * a3 + s2_ref[rows, :])

    zero = jnp.zeros((bsz, 2 * SSM_STATE), F32)
    lax.fori_loop(0, n_chunks, step, (zero, zero))
    y = (jnp.dot(x, t_ref[0], preferred_element_type=F32)
         + jnp.dot(xin_ref[...].astype(BF16), v_ref[0], preferred_element_type=F32))
    o_ref[0] = jax.nn.gelu(y).astype(o_ref.dtype)


def _s5(xg, tables, bsz):
    tmat, w1, w2, vmat, am = tables
    g, r, lh = xg.shape
    p2 = 2 * SSM_STATE
    kern = functools.partial(_s5_kernel, bsz=bsz, n_chunks=r // bsz)
    return pl.pallas_call(
        kern,
        grid=(g,),
        in_specs=[pl.BlockSpec((1, r, lh), lambda i: (i, 0, 0)),
                  pl.BlockSpec((1, lh, lh), lambda i: (i, 0, 0)),
                  pl.BlockSpec((1, lh, p2), lambda i: (i, 0, 0)),
                  pl.BlockSpec((1, lh, p2), lambda i: (i, 0, 0)),
                  pl.BlockSpec((1, p2, lh), lambda i: (i, 0, 0)),
                  pl.BlockSpec((1, 8, p2), lambda i: (i, 0, 0))],
        out_specs=pl.BlockSpec((1, r, lh), lambda i: (i, 0, 0)),
        out_shape=jax.ShapeDtypeStruct((g, r, lh), BF16),
        scratch_shapes=[pltpu.VMEM((r, p2), F32), pltpu.VMEM((r, p2), F32), pltpu.VMEM((r, p2), F32)],
        compiler_params=_cparams(("parallel",), _VMEM_LIMIT),
        name="s5_scan",
    )(xg, tmat, w1, w2, vmat, am)


def _conv_kernel(z_ref, halo_ref, wdw_ref, bdw_ref, lng_ref, lnb_ref, wpw_ref, go_ref, o_ref, buf_ref, *, tt):
    first = pl.program_id(1) == 0
    zc = z_ref[0]
    zh = halo_ref[0]
    vh = zh[:, :CONV_WIDTH] * jax.nn.sigmoid(zh[:, CONV_WIDTH:])
    buf_ref[0:CONV_HALO, :] = vh * jnp.where(first, 0.0, 1.0)
    buf_ref[CONV_HALO:CONV_HALO + tt, :] = zc[:, :CONV_WIDTH] * jax.nn.sigmoid(zc[:, CONV_WIDTH:])
    acc = jnp.zeros((tt, CONV_WIDTH), F32) + bdw_ref[...]
    for k in range(CONV_K):
        acc = acc + wdw_ref[k:k + 1, :] * buf_ref[pl.ds(CONV_HALO - (CONV_K - 1) + k, tt), :]
    mu = jnp.mean(acc, axis=-1, keepdims=True)
    var = jnp.mean(jnp.square(acc - mu), axis=-1, keepdims=True)
    y = (acc - mu) * lax.rsqrt(var + EPS) * lng_ref[...] + lnb_ref[...]
    y = jax.nn.silu(y)
    y = jnp.dot(y.astype(BF16), wpw_ref[...], preferred_element_type=F32)
    o_ref[0] = (_rms_rows(y) * go_ref[...]).astype(o_ref.dtype)


def _conv(ztok, w_dw, b_dw, ln_g, ln_b, w_pw, g_out, tt=512):
    b, s, _ = ztok.shape
    cw = CONV_WIDTH
    hb = tt // CONV_HALO
    kern = functools.partial(_conv_kernel, tt=tt)
    row = lambda v: v.reshape(1, cw)
    return pl.pallas_call(
        kern,
        grid=(b, s // tt),
        in_specs=[pl.BlockSpec((1, tt, 2 * cw), lambda i, j: (i, j, 0)),
                  pl.BlockSpec((1, CONV_HALO, 2 * cw), lambda i, j: (i, jnp.maximum(j * hb - 1, 0), 0)),
                  pl.BlockSpec((CONV_K + 1, cw), lambda i, j: (0, 0)),
                  pl.BlockSpec((1, cw), lambda i, j: (0, 0)),
                  pl.BlockSpec((1, cw), lambda i, j: (0, 0)),
                  pl.BlockSpec((1, cw), lambda i, j: (0, 0)),
                  pl.BlockSpec((cw, cw), lambda i, j: (0, 0)),
                  pl.BlockSpec((1, cw), lambda i, j: (0, 0))],
        out_specs=pl.BlockSpec((1, tt, cw), lambda i, j: (i, j, 0)),
        out_shape=jax.ShapeDtypeStruct((b, s, cw), BF16),
        scratch_shapes=[pltpu.VMEM((CONV_HALO + tt, cw), F32)],
        compiler_params=_cparams(("parallel", "arbitrary")),
        name="conv_mixer",
    )(ztok, ztok, jnp.concatenate([w_dw, jnp.zeros((1, cw), F32)], 0), row(b_dw), row(ln_g), row(ln_b),
      w_pw.astype(BF16), row(g_out))


def _knorm_kernel(ks_ref, kw_ref, gs_ref, gw_ref, os_ref, ow_ref):
    for src, g_ref, dst in ((ks_ref, gs_ref, os_ref), (kw_ref, gw_ref, ow_ref)):
        x = src[0]
        for h in range(NSA_KV_HEADS):
            xh = x[:, h * HEAD_DIM:(h + 1) * HEAD_DIM]
            dst[0, h] = (_rms_rows(xh) * g_ref[...]).astype(dst.dtype)


def _knorm(ztok, col_s, col_w, gain_s, gain_w, tt=512):
    b, s, _ = ztok.shape
    kw = NSA_KV_HEADS * HEAD_DIM
    out = jax.ShapeDtypeStruct((b, NSA_KV_HEADS, s, HEAD_DIM), BF16)
    ospec = pl.BlockSpec((1, NSA_KV_HEADS, tt, HEAD_DIM), lambda i, j: (i, 0, j, 0))
    return pl.pallas_call(
        _knorm_kernel,
        grid=(b, s // tt),
        in_specs=[pl.BlockSpec((1, tt, kw), lambda i, j: (i, j, col_s // kw)),
                  pl.BlockSpec((1, tt, kw), lambda i, j: (i, j, col_w // kw)),
                  pl.BlockSpec((1, HEAD_DIM), lambda i, j: (0, 0)),
                  pl.BlockSpec((1, HEAD_DIM), lambda i, j: (0, 0))],
        out_specs=[ospec, ospec],
        out_shape=[out, out],
        compiler_params=_cparams(("parallel", "parallel")),
        name="nsa_key_norm",
    )(ztok, ztok, gain_s.reshape(1, HEAD_DIM), gain_w.reshape(1, HEAD_DIM))


def _compress_kernel(k_ref, v_ref, wka_ref, wkb_ref, ck_ref, w2k_ref, gk_ref,
                     wva_ref, wvb_ref, cv_ref, w2v_ref, ko_ref, vo_ref):
    hi = lax.Precision.HIGHEST
    n = k_ref.shape[1] // CMP_STRIDE
    kvw = k_ref.shape[2]
    nt = (((1,), (1,)), ((), ()))
    a, bm = jnp.zeros((n, kvw), F32), jnp.zeros((n, kvw), F32)
    at, bt = jnp.zeros((kvw, n), F32), jnp.zeros((kvw, n), F32)
    for l in range(CMP_STRIDE):
        kl = k_ref[0, pl.ds(l, n, stride=CMP_STRIDE), :]
        vl = v_ref[0, pl.ds(l, n, stride=CMP_STRIDE), :]
        a = a + jnp.dot(kl, wka_ref[l], precision=hi, preferred_element_type=F32)
        bm = bm + jnp.dot(kl, wkb_ref[l], precision=hi, preferred_element_type=F32)
        at = at + lax.dot_general(wva_ref[l], vl, nt, precision=hi, preferred_element_type=F32)
        bt = bt + lax.dot_general(wvb_ref[l], vl, nt, precision=hi, preferred_element_type=F32)
    pre = a + pltpu.roll(bm, n - 1, 0) + ck_ref[...]
    kc = jnp.dot(jax.nn.gelu(pre), w2k_ref[...], precision=hi, preferred_element_type=F32)
    for h in range(NSA_KV_HEADS):
        kh = kc[:, h * HEAD_DIM:(h + 1) * HEAD_DIM]
        ko_ref[0, h] = (_rms_rows(kh) * gk_ref[...]).astype(ko_ref.dtype)
    pre_t = at + pltpu.roll(bt, n - 1, 1) + cv_ref[...]
    vt = jnp.dot(w2v_ref[...], jax.nn.gelu(pre_t), precision=hi, preferred_element_type=F32)
    for h in range(NSA_KV_HEADS):
        vo_ref[0, h] = vt[h * HEAD_DIM:(h + 1) * HEAD_DIM, :].astype(vo_ref.dtype)


def _blockdiag2(w):
    z = jnp.zeros_like(w)
    return jnp.concatenate([jnp.concatenate([w, z], 1), jnp.concatenate([z, w], 1)], 0)


def _compress(ztok, col_k, col_v, pe, w1, w2, k_gain):
    b, s, _ = ztok.shape
    n = s // CMP_STRIDE
    hd, kvw = HEAD_DIM, NSA_KV_HEADS * HEAD_DIM
    hp = lax.Precision.HIGHEST

    def expand(w):
        wl = w.reshape(L_CMP, hd, hd)
        e = wl[:, None, :, None, :] * jnp.eye(NSA_KV_HEADS, dtype=F32)[None, :, None, :, None]
        e = e.reshape(L_CMP, kvw, kvw)
        return e[:CMP_STRIDE], e[CMP_STRIDE:]

    wka, wkb = expand(w1[0])
    wva, wvb = expand(w1[1])
    ck = jnp.tile(jnp.dot(pe[0].reshape(1, L_CMP * hd), w1[0], precision=hp), (1, NSA_KV_HEADS))
    cv = jnp.tile(jnp.dot(pe[1].reshape(1, L_CMP * hd), w1[1], precision=hp), (1, NSA_KV_HEADS)).T
    full = lambda shape: pl.BlockSpec(shape, lambda i: tuple(0 for _ in shape))
    return pl.pallas_call(
        _compress_kernel,
        grid=(b,),
        in_specs=[pl.BlockSpec((1, s, kvw), lambda i: (i, 0, col_k // kvw)),
                  pl.BlockSpec((1, s, kvw), lambda i: (i, 0, col_v // kvw)),
                  full((CMP_STRIDE, kvw, kvw)), full((CMP_STRIDE, kvw, kvw)), full((1, kvw)), full((kvw, kvw)),
                  full((1, hd)),
                  full((CMP_STRIDE, kvw, kvw)), full((CMP_STRIDE, kvw, kvw)), full((kvw, 1)), full((kvw, kvw))],
        out_specs=[pl.BlockSpec((1, NSA_KV_HEADS, n, hd), lambda i: (i, 0, 0, 0)),
                   pl.BlockSpec((1, NSA_KV_HEADS, hd, n), lambda i: (i, 0, 0, 0))],
        out_shape=[jax.ShapeDtypeStruct((b, NSA_KV_HEADS, n, hd), BF16),
                   jax.ShapeDtypeStruct((b, NSA_KV_HEADS, hd, n), BF16)],
        compiler_params=_cparams(("parallel",), _VMEM_LIMIT),
        name="nsa_compress",
    )(ztok, ztok, wka, wkb, ck, _blockdiag2(w2[0]), k_gain.reshape(1, hd),
      wva.transpose(0, 2, 1), wvb.transpose(0, 2, 1), cv, _blockdiag2(w2[1]).T)


def _t5_bias_by_dist(t5_table):
    n = np.arange(T5_MAX_DIST + 1)
    max_exact = T5_BUCKETS // 2
    nf = np.maximum(n, 1).astype(np.float32)
    large = max_exact + (np.log(nf / np.float32(max_exact)) / np.float32(math.log(T5_MAX_DIST / max_exact))
                         * np.float32(T5_BUCKETS - max_exact)).astype(np.int32)
    large = np.minimum(large, T5_BUCKETS - 1)
    bucket = np.where(n < max_exact, n, large)
    onehot = (bucket[:, None] == np.arange(T5_BUCKETS)[None, :]).astype(np.float32)
    return jnp.dot(onehot, t5_table, precision=lax.Precision.HIGHEST)


def _bias_tile(fdt, rows, stride, dist00, d_max=None):
    heads = fdt.shape[0]
    a0 = stride * (rows - 1)
    d_lo = dist00 - a0
    length = a0 + TQ + stride
    d_hi = d_lo + length
    d_max = d_hi if d_max is None else d_max
    pieces = []
    for lo, hi, kind in ((d_lo, min(d_hi, 0), 'neg'), (max(d_lo, 0), min(d_hi, T5_MAX_DIST), 'tab'),
                         (max(d_lo, T5_MAX_DIST), min(d_hi, d_max), 'far'), (max(d_lo, d_max), d_hi, 'neg')):
        if hi > lo:
            pieces.append(fdt[:, lo:hi] if kind == 'tab'
                          else jnp.full((heads, hi - lo), NEG if kind == 'neg' else 0.0, F32))
    vec = jnp.concatenate(pieces, axis=1)
    skew = jnp.tile(vec, (1, rows))[:, :rows * (length - stride)].reshape(heads, rows, length - stride)
    return skew[:, :, a0:a0 + TQ]


def _nsa_bias_tiles(t5_table, seq):
    fd = _t5_bias_by_dist(t5_table).astype(F32)
    fdt = ((fd - fd[T5_MAX_DIST:]) * LOG2E).T
    n_cmp = seq // CMP_STRIDE
    r0 = n_cmp - TQ // CMP_STRIDE
    cmp_t = _bias_tile(fdt, 2 * n_cmp - TQ // CMP_STRIDE, CMP_STRIDE, CMP_STRIDE * r0 - (L_CMP - 1))
    sel_t = _bias_tile(fdt, SEL_PAD + TQ, 1, SEL_PAD)
    win = _bias_tile(fdt, WINDOW + TQ, 1, WINDOW, d_max=WINDOW)
    rw = np.arange(WINDOW + TQ)[None, :, None]
    win_t = jnp.stack([jnp.where(rw >= WINDOW - q0, win, NEG) for q0 in (0, TQ, 2 * TQ)])
    split = lambda t: t.reshape(*t.shape[:-3], NSA_KV_HEADS, NSA_GROUP, *t.shape[-2:])

    def wide(t):
        t = jnp.swapaxes(split(t), -3, -2)
        return t.reshape(*t.shape[:-2], NSA_GROUP * TQ)

    return wide(cmp_t), wide(sel_t), wide(win_t)


def _q_head(qt_ref, g, qg_ref):
    q = qt_ref[0, g * HEAD_DIM:(g + 1) * HEAD_DIM, :]
    inv = lax.rsqrt(jnp.mean(q * q, axis=0, keepdims=True) + EPS)
    return (q * inv * qg_ref[...]).astype(BF16)


def _nsa_cmp_kernel(qt_ref, qg_ref, kc_ref, vct_ref, bias_ref, ov_ref, oc_ref, sel_ref, *, n_cmp, n_sel):
    qi = pl.program_id(2)
    kc = kc_ref[0, 0]
    vct = vct_ref[0, 0]
    row0 = pl.multiple_of((n_cmp - TQ // CMP_STRIDE) - qi * (TQ // CMP_STRIDE), TQ // CMP_STRIDE)
    qw = jnp.concatenate([_q_head(qt_ref, g, qg_ref) for g in range(NSA_GROUP)], axis=1)
    s = jnp.dot(kc, qw, preferred_element_type=F32) + bias_ref[0, pl.ds(row0, n_cmp), :]
    m = jnp.max(s, axis=0, keepdims=True)
    m = jnp.where(m < 0.5 * NEG, 0.0, m)
    p = jnp.exp2(s - m)
    p = p * (1.0 / jnp.maximum(jnp.sum(p, axis=0, keepdims=True), 1e-30))
    oc = jnp.dot(vct, p.astype(BF16), preferred_element_type=F32)
    psum = jnp.zeros((n_cmp, TQ), F32)
    for g in range(NSA_GROUP):
        oc_ref[0, g * HEAD_DIM:(g + 1) * HEAD_DIM, :] = oc[:, g * TQ:(g + 1) * TQ]
        psum = psum + p[:, g * TQ:(g + 1) * TQ]
    hi = psum.astype(BF16)
    lo = (psum - hi.astype(F32)).astype(BF16)
    imp = (jnp.dot(ov_ref[...], hi, preferred_element_type=F32)
           + jnp.dot(ov_ref[...], lo, preferred_element_type=F32))
    blk = lax.broadcasted_iota(jnp.int32, (n_sel, TQ), 0)
    blk_t = lax.shift_right_logical(qi * TQ + lax.broadcasted_iota(jnp.int32, (n_sel, TQ), 1), L_SEL.bit_length() - 1)
    forced = (blk == 0) | (blk > blk_t - N_LOCAL)
    v0 = jnp.where(blk > blk_t, -jnp.inf, jnp.where(forced, SEL_FORCE, imp))

    def pick(_, v):
        m = jnp.max(v, axis=0, keepdims=True)
        cand = (v == m) & (m > -jnp.inf)
        first = jnp.min(jnp.where(cand, blk, n_sel), axis=0, keepdims=True)
        return jnp.where(blk == first, -jnp.inf, v)

    v = lax.fori_loop(0, min(N_SELECT, n_sel), pick, v0)
    sel_ref[0, 0] = jnp.where((v == -jnp.inf) & (v0 > -jnp.inf), 0.0, NEG)


def _nsa_cmp(zt, qg, kc, vct, bias_c, ovt):
    b, _, s = zt.shape
    n_cmp, n_sel = s // CMP_STRIDE, s // L_SEL
    gw = NSA_GROUP * HEAD_DIM
    kern = functools.partial(_nsa_cmp_kernel, n_cmp=n_cmp, n_sel=n_sel)
    return pl.pallas_call(
        kern,
        grid=(b, NSA_KV_HEADS, s // TQ),
        in_specs=[pl.BlockSpec((1, gw, TQ), lambda i, k, j: (i, k, j)),
                  pl.BlockSpec((HEAD_DIM, TQ), lambda i, k, j: (0, 0)),
                  pl.BlockSpec((1, 1, n_cmp, HEAD_DIM), lambda i, k, j: (i, k, 0, 0)),
                  pl.BlockSpec((1, 1, HEAD_DIM, n_cmp), lambda i, k, j: (i, k, 0, 0)),
                  pl.BlockSpec((1, bias_c.shape[1], NSA_GROUP * TQ), lambda i, k, j: (k, 0, 0)),
                  pl.BlockSpec((n_sel, n_cmp), lambda i, k, j: (0, 0))],
        out_specs=[pl.BlockSpec((1, gw, TQ), lambda i, k, j: (i, k, j)),
                   pl.BlockSpec((1, 1, n_sel, TQ), lambda i, k, j: (i, k, 0, j))],
        out_shape=[jax.ShapeDtypeStruct((b, NSA_WIDTH, s), F32),
                   jax.ShapeDtypeStruct((b, NSA_KV_HEADS, n_sel, s), F32)],
        compiler_params=_cparams(("parallel", "parallel", "parallel"), _VMEM_LIMIT),
        name="nsa_compressed_select",
    )(zt, qg, kc, vct, bias_c, ovt)


def _nsa_main_kernel(qt_ref, qg_ref, ks_ref, vst_ref, kw_ref, vwt_ref, sel_ref, bs_ref, bw_ref, gate_ref,
                     oc_ref, o_ref, acc_ref):
    qi = pl.program_id(1)
    q0 = pl.multiple_of(qi * TQ, TQ)
    near = SEL_PAD + TQ
    gw = NSA_GROUP * HEAD_DIM
    kvs = range(NSA_KV_HEADS)

    def expand_sel(kv, first_blk, n_blk):
        rows = [jnp.broadcast_to(sel_ref[0, kv, pl.ds(first_blk + r, 1), :], (L_SEL, TQ)) for r in range(n_blk)]
        rows = jnp.concatenate(rows, axis=0)
        return jnp.concatenate([rows] * NSA_GROUP, axis=1)

    qw = [jnp.concatenate([_q_head(qt_ref, kv * NSA_GROUP + g, qg_ref) for g in range(NSA_GROUP)], axis=1)
          for kv in kvs]
    m0 = []
    for kv in kvs:
        s = (jnp.dot(ks_ref[0, kv, pl.ds(q0, near), :], qw[kv], preferred_element_type=F32) + bs_ref[kv]
             + expand_sel(kv, qi * (TQ // L_SEL), near // L_SEL))
        m = jnp.max(s, axis=0, keepdims=True)
        p = jnp.exp2(s - m).astype(BF16)
        acc_ref[kv] = jnp.dot(vst_ref[0, kv, :, pl.ds(q0, near)], p, preferred_element_type=F32)
        m0.append(m)

    def far(c, m_old):
        r0 = pl.multiple_of(c * TQ, TQ)
        m_out = []
        for kv in kvs:
            s = (jnp.dot(ks_ref[0, kv, pl.ds(r0, TQ), :], qw[kv], preferred_element_type=F32)
                 + expand_sel(kv, c * (TQ // L_SEL), TQ // L_SEL))
            m_new = jnp.maximum(m_old[kv], jnp.max(s, axis=0, keepdims=True))
            alpha = jnp.exp2(m_old[kv] - m_new)
            p = jnp.exp2(s - m_new).astype(BF16)
            acc_ref[kv] = alpha * acc_ref[kv] + jnp.dot(vst_ref[0, kv, :, pl.ds(r0, TQ)], p,
                                                        preferred_element_type=F32)
            m_out.append(m_new)
        return tuple(m_out)

    lax.fori_loop(SEL_PAD // TQ, qi, far, tuple(m0))

    for kv in kvs:
        s = jnp.dot(kw_ref[0, kv, pl.ds(q0, WINDOW + TQ), :], qw[kv], preferred_element_type=F32) + bw_ref[0, kv]
        p = jnp.exp2(s - jnp.max(s, axis=0, keepdims=True)).astype(BF16)
        ow = jnp.dot(vwt_ref[0, kv, :, pl.ds(q0, WINDOW + TQ)], p, preferred_element_type=F32)
        ow = ow[:HEAD_DIM] * (1.0 / jnp.maximum(ow[HEAD_DIM:HEAD_DIM + 1], 1e-30))
        os = acc_ref[kv]
        os = os[:HEAD_DIM] * (1.0 / jnp.maximum(os[HEAD_DIM:HEAD_DIM + 1], 1e-30))
        for g in range(NSA_GROUP):
            gates = jax.nn.sigmoid(gate_ref[0, kv, g * N_BRANCH:(g + 1) * N_BRANCH, :])
            rows = slice(kv * gw + g * HEAD_DIM, kv * gw + (g + 1) * HEAD_DIM)
            cols = slice(g * TQ, (g + 1) * TQ)
            o_ref[0, rows, :] = (gates[0:1] * oc_ref[0, rows, :] + gates[1:2] * os[:, cols]
                                 + gates[2:3] * ow[:, cols])


def _nsa_main(zt, qg, ks_p, vst_p, kw_p, vwt_p, sel_p, bias_s, bias_w, gates_t, oc_t):
    b, _, s = zt.shape
    kvh = NSA_KV_HEADS
    sp, wp = ks_p.shape[2], kw_p.shape[2]
    nb, vr = sel_p.shape[2], vst_p.shape[2]
    once = pl.Buffered(1)
    return pl.pallas_call(
        _nsa_main_kernel,
        grid=(b, s // TQ),
        in_specs=[pl.BlockSpec((1, NSA_WIDTH, TQ), lambda i, j: (i, 0, j)),
                  pl.BlockSpec((HEAD_DIM, TQ), lambda i, j: (0, 0)),
                  pl.BlockSpec((1, kvh, sp, HEAD_DIM), lambda i, j: (i, 0, 0, 0), pipeline_mode=once),
                  pl.BlockSpec((1, kvh, vr, sp), lambda i, j: (i, 0, 0, 0), pipeline_mode=once),
                  pl.BlockSpec((1, kvh, wp, HEAD_DIM), lambda i, j: (i, 0, 0, 0), pipeline_mode=once),
                  pl.BlockSpec((1, kvh, vr, wp), lambda i, j: (i, 0, 0, 0), pipeline_mode=once),
                  pl.BlockSpec((1, kvh, nb, TQ), lambda i, j: (i, 0, 0, j)),
                  pl.BlockSpec((kvh, SEL_PAD + TQ, NSA_GROUP * TQ), lambda i, j: (0, 0, 0), pipeline_mode=once),
                  pl.BlockSpec((1, kvh, WINDOW + TQ, NSA_GROUP * TQ), lambda i, j: (jnp.minimum(j, 2), 0, 0, 0)),
                  pl.BlockSpec((1, kvh, 16, TQ), lambda i, j: (i, 0, 0, j)),
                  pl.BlockSpec((1, NSA_WIDTH, TQ), lambda i, j: (i, 0, j))],
        out_specs=pl.BlockSpec((1, NSA_WIDTH, TQ), lambda i, j: (i, 0, j)),
        out_shape=jax.ShapeDtypeStruct((b, NSA_WIDTH, s), F32),
        scratch_shapes=[pltpu.VMEM((kvh, vr, NSA_GROUP * TQ), F32)],
        compiler_params=_cparams(("parallel", "arbitrary"), _VMEM_LIMIT),
        name="nsa_selected_window",
    )(zt, qg, ks_p, vst_p, kw_p, vwt_p, sel_p, bias_s, bias_w, gates_t, oc_t)


def _mixout_kernel(gy_ref, yc_ref, yn_ref, h_ref, wglu_ref, go_ref, wo_ref, o_ref):
    sw = SSM_WIDTH
    ag = jnp.dot(gy_ref[0], wglu_ref[...], preferred_element_type=F32)
    ys = ag[:, :sw] * jax.nn.sigmoid(ag[:, sw:])
    ys = (_rms_rows(ys) * go_ref[:, 0:sw]).astype(BF16)
    yn = (_rms_rows(yn_ref[0]) * go_ref[:, 2 * sw:]).astype(BF16)
    out = (jnp.dot(ys, wo_ref[0:sw, :], preferred_element_type=F32)
           + jnp.dot(yc_ref[0], wo_ref[sw:2 * sw, :], preferred_element_type=F32)
           + jnp.dot(yn, wo_ref[2 * sw:, :], preferred_element_type=F32))
    o_ref[0] = h_ref[0] + out


def _mixout(gy, yc, yn, h, w_glu, g_out, w_out, tm=512):
    b, s, d = h.shape
    tok = lambda w: pl.BlockSpec((1, tm, w), lambda i, j: (i, j, 0))
    return pl.pallas_call(
        _mixout_kernel,
        grid=(b, s // tm),
        in_specs=[tok(SSM_WIDTH), tok(CONV_WIDTH), tok(NSA_WIDTH), tok(d),
                  pl.BlockSpec((SSM_WIDTH, 2 * SSM_WIDTH), lambda i, j: (0, 0)),
                  pl.BlockSpec((1, d), lambda i, j: (0, 0)),
                  pl.BlockSpec((d, d), lambda i, j: (0, 0))],
        out_specs=tok(d),
        out_shape=jax.ShapeDtypeStruct((b, s, d), F32),
        compiler_params=_cparams(("parallel", "parallel")),
        name="mix_out",
    )(gy, yc, yn, h, w_glu.astype(BF16), g_out.reshape(1, d), w_out.astype(BF16))


def _memkv_kernel(mem_ref, g_ref, w_ref, kg_ref, k_ref, v_ref):
    mn = (_rms_rows(mem_ref[0]) * g_ref[...]).astype(BF16)
    kv = jnp.dot(mn, w_ref[...], preferred_element_type=F32)
    for h in range(X_HEADS):
        cols = slice(h * HEAD_DIM, (h + 1) * HEAD_DIM)
        k_ref[0, :, cols] = (_rms_rows(kv[:, cols]) * kg_ref[...]).astype(k_ref.dtype)
    v_ref[0] = kv[:, X_WIDTH:].astype(v_ref.dtype)


def _memkv(mem, gain, w_kv, k_gain):
    b, m, d = mem.shape
    out = jax.ShapeDtypeStruct((b, m, X_WIDTH), BF16)
    return pl.pallas_call(
        _memkv_kernel,
        grid=(b,),
        in_specs=[pl.BlockSpec((1, m, d), lambda i: (i, 0, 0)),
                  pl.BlockSpec((1, d), lambda i: (0, 0)),
                  pl.BlockSpec((d, 2 * X_WIDTH), lambda i: (0, 0)),
                  pl.BlockSpec((1, HEAD_DIM), lambda i: (0, 0))],
        out_specs=[pl.BlockSpec((1, m, X_WIDTH), lambda i: (i, 0, 0))] * 2,
        out_shape=[out, out],
        compiler_params=_cparams(("parallel",)),
        name="cross_mem_kv",
    )(mem, gain.reshape(1, d), w_kv.astype(BF16), k_gain.reshape(1, HEAD_DIM))


def _cross_kernel(h_ref, g_ref, wq_ref, qg_ref, k_ref, v_ref, wo_ref, o_ref):
    h = h_ref[0]
    hn = (_rms_rows(h) * g_ref[...]).astype(BF16)
    q = jnp.dot(hn, wq_ref[...], preferred_element_type=F32)
    out = h
    for hd in range(X_HEADS):
        cols = slice(hd * HEAD_DIM, (hd + 1) * HEAD_DIM)
        qh = (_rms_rows(q[:, cols]) * qg_ref[...]).astype(BF16)
        s = lax.dot_general(qh, k_ref[0, :, cols], (((1,), (1,)), ((), ())), preferred_element_type=F32)
        p = jnp.exp(s - jnp.max(s, axis=-1, keepdims=True))
        p = p * (1.0 / jnp.sum(p, axis=-1, keepdims=True))
        o = jnp.dot(p.astype(BF16), v_ref[0, :, cols], preferred_element_type=F32)
        out = out + jnp.dot(o.astype(BF16), wo_ref[cols, :], preferred_element_type=F32)
    o_ref[0] = out


def _cross(h, gain, w_q, q_gain, k, v, w_o, tm=512):
    b, s, d = h.shape
    m = k.shape[1]
    return pl.pallas_call(
        _cross_kernel,
        grid=(b, s // tm),
        in_specs=[pl.BlockSpec((1, tm, d), lambda i, j: (i, j, 0)),
                  pl.BlockSpec((1, d), lambda i, j: (0, 0)),
                  pl.BlockSpec((d, X_WIDTH), lambda i, j: (0, 0)),
                  pl.BlockSpec((1, HEAD_DIM), lambda i, j: (0, 0)),
                  pl.BlockSpec((1, m, X_WIDTH), lambda i, j: (i, 0, 0)),
                  pl.BlockSpec((1, m, X_WIDTH), lambda i, j: (i, 0, 0)),
                  pl.BlockSpec((X_WIDTH, d), lambda i, j: (0, 0))],
        out_specs=pl.BlockSpec((1, tm, d), lambda i, j: (i, j, 0)),
        out_shape=jax.ShapeDtypeStruct((b, s, d), F32),
        compiler_params=_cparams(("parallel", "parallel")),
        name="cross_attention",
    )(h, gain.reshape(1, d), w_q.astype(BF16), (q_gain * HEAD_DIM ** -0.5).reshape(1, HEAD_DIM), k, v,
      w_o.astype(BF16))


def _ffn_kernel(h_ref, g_ref, wg_ref, wv_ref, wd_ref, o_ref, xn_ref, acc_ref):
    f = pl.program_id(1)

    @pl.when(f == 0)
    def _():
        xn_ref[...] = (_rms_rows(h_ref[...]) * g_ref[...]).astype(BF16)
        acc_ref[...] = jnp.zeros_like(acc_ref)

    x = xn_ref[...]
    gate = jnp.dot(x, wg_ref[...], preferred_element_type=F32)
    val = jnp.dot(x, wv_ref[...], preferred_element_type=F32)
    act = (jax.nn.silu(gate) * val).astype(BF16)
    acc_ref[...] += jnp.dot(act, wd_ref[...], preferred_element_type=F32)

    @pl.when(f == pl.num_programs(1) - 1)
    def _():
        o_ref[...] = h_ref[...] + acc_ref[...]


def _ffn(h2d, gain, w_up, w_down, tm=1024):
    t, d = h2d.shape
    nf = D_FF // FF_CHUNK
    wb = w_up.astype(BF16)
    return pl.pallas_call(
        _ffn_kernel,
        grid=(t // tm, nf),
        in_specs=[pl.BlockSpec((tm, d), lambda i, f: (i, 0)),
                  pl.BlockSpec((1, d), lambda i, f: (0, 0)),
                  pl.BlockSpec((d, FF_CHUNK), lambda i, f: (0, f)),
                  pl.BlockSpec((d, FF_CHUNK), lambda i, f: (0, f + nf)),
                  pl.BlockSpec((FF_CHUNK, d), lambda i, f: (f, 0))],
        out_specs=pl.BlockSpec((tm, d), lambda i, f: (i, 0)),
        out_shape=jax.ShapeDtypeStruct((t, d), F32),
        scratch_shapes=[pltpu.VMEM((tm, d), BF16), pltpu.VMEM((tm, d), F32)],
        compiler_params=_cparams(("parallel", "arbitrary"), _VMEM_LIMIT),
        name="ffn_swiglu",
    )(h2d, gain.reshape(1, d), wb, wb, w_down.astype(BF16))


def _router_kernel(h_ref, g_ref, wr_ref, xn_ref, gate_ref, asg_ref):
    xn = _rms_rows(h_ref[...]) * g_ref[...]
    xn_ref[...] = xn.astype(BF16)
    logits = jnp.dot(xn, wr_ref[...], precision=lax.Precision.HIGHEST, preferred_element_type=F32)
    lane = lax.broadcasted_iota(jnp.int32, logits.shape, 1)
    lg = jnp.where(lane < N_EXPERTS, logits, -jnp.inf)
    m1 = jnp.max(lg, axis=-1, keepdims=True)
    i1 = jnp.min(jnp.where(lg == m1, lane, 128), axis=-1, keepdims=True)
    lg2 = jnp.where(lane == i1, -jnp.inf, lg)
    m2 = jnp.max(lg2, axis=-1, keepdims=True)
    i2 = jnp.min(jnp.where(lg2 == m2, lane, 128), axis=-1, keepdims=True)
    e = jnp.exp(m2 - m1)
    den = 1.0 + e
    gate_ref[...] = jnp.where(lane == i1, 1.0 / den, jnp.where(lane == i2, e / den, 0.0))
    asg_ref[...] = ((lane == i1) | (lane == i2)).astype(jnp.int32)


def _router(h2d, gain, w_router, tm=512):
    t, d = h2d.shape
    wr = jnp.concatenate([w_router, jnp.zeros((d, 128 - N_EXPERTS), F32)], axis=1)
    return pl.pallas_call(
        _router_kernel,
        grid=(t // tm,),
        in_specs=[pl.BlockSpec((tm, d), lambda i: (i, 0)),
                  pl.BlockSpec((1, d), lambda i: (0, 0)),
                  pl.BlockSpec((d, 128), lambda i: (0, 0))],
        out_specs=[pl.BlockSpec((tm, d), lambda i: (i, 0)),
                   pl.BlockSpec((tm, 128), lambda i: (i, 0)),
                   pl.BlockSpec((tm, 128), lambda i: (i, 0))],
        out_shape=[jax.ShapeDtypeStruct((t, d), BF16), jax.ShapeDtypeStruct((t, 128), F32),
                   jax.ShapeDtypeStruct((t, 128), jnp.int32)],
        compiler_params=_cparams(("parallel",)),
        name="moe_router",
    )(h2d, gain.reshape(1, d), wr)


def _moe_gather_kernel(rb_ref, lo_ref, hi_ref, first_ref, tgt_ref, x_ref, o_ref):
    e, j, slot = pl.program_id(0), pl.program_id(1), pl.program_id(2)
    rb = rb_ref[e, j, slot]
    lo, hi = lo_ref[e, j], hi_ref[e, j]

    @pl.when(first_ref[e, j, slot] == 1)
    def _():
        o_ref[...] = jnp.zeros_like(o_ref)

    active = (slot == 0) | (rb != rb_ref[e, j, 0])
    tgt = tgt_ref[0]
    for sub in range(MOE_TB // MOE_SUB):
        base = rb * MOE_TB + sub * MOE_SUB

        @pl.when(active & (lo < base + MOE_SUB) & (hi > base))
        def _():
            rows = base + lax.broadcasted_iota(jnp.int32, (MOE_SUB, MOE_TB), 0)
            onehot = jnp.where(tgt == rows, 1.0, 0.0).astype(BF16)
            part = jnp.dot(onehot, x_ref[...], preferred_element_type=F32)
            sl = slice(sub * MOE_SUB, (sub + 1) * MOE_SUB)
            o_ref[sl, :] = o_ref[sl, :] + part.astype(o_ref.dtype)


def _moe_ffn_kernel(exp_ref, nused_ref, x_ref, wg_ref, wv_ref, wd_ref, o_ref, acc_ref):
    r, f = pl.program_id(0), pl.program_id(1)
    used = r < nused_ref[0]

    @pl.when(f == 0)
    def _():
        acc_ref[...] = jnp.zeros_like(acc_ref)

    @pl.when(used)
    def _():
        x = x_ref[...]
        gate = jnp.dot(x, wg_ref[0], preferred_element_type=F32)
        val = jnp.dot(x, wv_ref[0], preferred_element_type=F32)
        act = (jax.nn.silu(gate) * val).astype(BF16)
        acc_ref[...] += jnp.dot(act, wd_ref[0], preferred_element_type=F32)

    @pl.when(f == pl.num_programs(1) - 1)
    def _():
        o_ref[...] = acc_ref[...].astype(o_ref.dtype)
    del exp_ref


def _moe_scatter_kernel(rb_ref, lo_ref, hi_ref, tgt_ref, gate_ref, y_ref, h_ref, o_ref):
    j, e, slot = pl.program_id(0), pl.program_id(1), pl.program_id(2)
    rb = rb_ref[e, j, slot]
    lo, hi = lo_ref[e, j], hi_ref[e, j]

    @pl.when((e == 0) & (slot == 0))
    def _():
        o_ref[...] = h_ref[...]

    active = (slot == 0) | (rb != rb_ref[e, j, 0])
    tgt = tgt_ref[0]
    gate = gate_ref[0]
    for sub in range(MOE_TB // MOE_SUB):
        base = rb * MOE_TB + sub * MOE_SUB

        @pl.when(active & (lo < base + MOE_SUB) & (hi > base))
        def _():
            rows = base + lax.broadcasted_iota(jnp.int32, (MOE_TB, MOE_SUB), 1)
            onehot = jnp.where(tgt == rows, 1.0, 0.0).astype(BF16)
            part = jnp.dot(onehot, y_ref[sub * MOE_SUB:(sub + 1) * MOE_SUB, :], preferred_element_type=F32)
            o_ref[...] = o_ref[...] + gate * part


def _moe(h2d, gain, w_router, w_up, w_down):
    t, d = h2d.shape
    tb = MOE_TB
    nj = t // tb
    n_rb = (t * TOP_K) // tb + N_EXPERTS
    xn, gates, asg = _router(h2d, gain, w_router)
    asg = asg[:, :N_EXPERTS]
    gates = gates[:, :N_EXPERTS]
    cs = jnp.cumsum(asg, axis=0)
    rank = cs - asg
    counts = cs[-1]
    padded = (counts + tb - 1) // tb * tb
    pad_end = jnp.cumsum(padded)
    start_p = pad_end - padded
    tgt = jnp.where(asg == 1, start_p[None, :] + rank, -1).astype(jnp.int32)
    cb = jnp.concatenate([jnp.zeros((1, N_EXPERTS), jnp.int32), cs[tb - 1::tb]], axis=0)
    lo = (start_p[None, :] + cb[:-1]).T.astype(jnp.int32)
    hi = (start_p[None, :] + cb[1:]).T.astype(jnp.int32)
    rb0 = lo // tb
    rb1 = jnp.maximum(rb0, (hi - 1) // tb)
    rb = jnp.stack([rb0, rb1], axis=-1).astype(jnp.int32)
    flat = rb.reshape(-1)
    first = jnp.concatenate([jnp.ones((1,), jnp.int32), (flat[1:] != flat[:-1]).astype(jnp.int32)])
    first = first.reshape(N_EXPERTS, nj, 2)
    n_used = (pad_end[-1] // tb).astype(jnp.int32).reshape(1)
    blk_exp = jnp.minimum(jnp.searchsorted(pad_end, jnp.arange(n_rb) * tb, side='right'),
                          N_EXPERTS - 1).astype(jnp.int32)

    xs = pl.pallas_call(
        _moe_gather_kernel,
        grid_spec=pltpu.PrefetchScalarGridSpec(
            num_scalar_prefetch=4,
            grid=(N_EXPERTS, nj, 2),
            in_specs=[pl.BlockSpec((1, 1, tb), lambda e, j, s, *_: (e, 0, j)),
                      pl.BlockSpec((tb, d), lambda e, j, s, *_: (j, 0))],
            out_specs=pl.BlockSpec((tb, d), lambda e, j, s, rb_ref, *_: (rb_ref[e, j, s], 0))),
        out_shape=jax.ShapeDtypeStruct((n_rb * tb, d), BF16),
        compiler_params=_cparams(("arbitrary", "arbitrary", "arbitrary"), _VMEM_LIMIT),
        name="moe_gather",
    )(rb, lo, hi, first, tgt.T.reshape(N_EXPERTS, 1, t), xn)

    nf = D_FF // FF_CHUNK
    wub = w_up.astype(BF16)
    ys = pl.pallas_call(
        _moe_ffn_kernel,
        grid_spec=pltpu.PrefetchScalarGridSpec(
            num_scalar_prefetch=2,
            grid=(n_rb, nf),
            in_specs=[pl.BlockSpec((tb, d), lambda r, f, *_: (r, 0)),
                      pl.BlockSpec((1, d, FF_CHUNK), lambda r, f, ex, nu: (ex[r], 0, f)),
                      pl.BlockSpec((1, d, FF_CHUNK), lambda r, f, ex, nu: (ex[r], 0, f + nf)),
                      pl.BlockSpec((1, FF_CHUNK, d), lambda r, f, ex, nu: (ex[r], f, 0))],
            out_specs=pl.BlockSpec((tb, d), lambda r, f, *_: (r, 0)),
            scratch_shapes=[pltpu.VMEM((tb, d), F32)]),
        out_shape=jax.ShapeDtypeStruct((n_rb * tb, d), BF16),
        compiler_params=_cparams(("arbitrary", "arbitrary"), _VMEM_LIMIT),
        name="moe_expert_ffn",
    )(blk_exp, n_used, xs, wub, wub, w_down.astype(BF16))

    return pl.pallas_call(
        _moe_scatter_kernel,
        grid_spec=pltpu.PrefetchScalarGridSpec(
            num_scalar_prefetch=3,
            grid=(nj, N_EXPERTS, 2),
            in_specs=[pl.BlockSpec((1, tb, 1), lambda j, e, s, *_: (e, j, 0)),
                      pl.BlockSpec((1, tb, 1), lambda j, e, s, *_: (e, j, 0)),
                      pl.BlockSpec((tb, d), lambda j, e, s, rb_ref, *_: (rb_ref[e, j, s], 0)),
                      pl.BlockSpec((tb, d), lambda j, e, s, *_: (j, 0))],
            out_specs=pl.BlockSpec((tb, d), lambda j, e, s, *_: (j, 0))),
        out_shape=jax.ShapeDtypeStruct((t, d), F32),
        compiler_params=_cparams(("arbitrary", "arbitrary", "arbitrary"), _VMEM_LIMIT),
        name="moe_scatter",
    )(rb, lo, hi, tgt.T.reshape(N_EXPERTS, t, 1), gates.T.reshape(N_EXPERTS, t, 1), ys, h2d)


_COL_CONV, _COL_SSM, _COL_KC, _COL_VC, _COL_KS, _COL_KW = 0, 512, 768, 896, 1024, 1152
_ROW_Q, _ROW_VS, _ROW_VW, _ROW_G = 0, 512, 640, 768


def _split_w_in(w_in):
    kvw = NSA_KV_HEADS * HEAD_DIM
    cuts = np.cumsum([0, SSM_WIDTH, 2 * CONV_WIDTH, NSA_WIDTH] + [kvw] * 6 + [N_BRANCH * NSA_HEADS])
    seg = lambda i: w_in[:, cuts[i]:cuts[i + 1]]
    ssm, conv, q, k_c, v_c, k_s, v_s, k_w, v_w, gate = (seg(i) for i in range(10))
    w_tok = jnp.concatenate([conv, ssm, k_c, v_c, k_s, k_w], axis=1).astype(BF16)
    gate = jnp.concatenate([gate, jnp.zeros((w_in.shape[0], 8), F32)], axis=1)
    w_t = jnp.concatenate([q, v_s, v_w, gate], axis=1).T.astype(BF16)
    return w_tok, w_t


def _layer_mixers(h, p, t5_tiles):
    b, s, d = h.shape
    w_tok, w_t = _split_w_in(p['w_in'])
    ztok, zt = _proj(h, p['norm_mix'], w_tok, w_t)

    n_chunks = s // SSM_CHUNK
    u = ztok[:, :, _COL_SSM:_COL_SSM + SSM_WIDTH].astype(BF16)
    xg = u.reshape(b, n_chunks, SSM_CHUNK, SSM_GROUPS, SSM_GROUP).transpose(3, 1, 0, 2, 4)
    xg = xg.reshape(SSM_GROUPS, n_chunks * b, SSM_CHUNK * SSM_GROUP)
    tables = _s5_tables(p['ssm_lambda_re'], p['ssm_lambda_im'], p['ssm_log_dt'], p['ssm_b_re'], p['ssm_b_im'],
                        p['ssm_c_re'], p['ssm_c_im'], p['ssm_d'])
    gy = _s5(xg, tables, b)
    gy = gy.reshape(SSM_GROUPS, n_chunks, b, SSM_CHUNK, SSM_GROUP).transpose(2, 1, 3, 0, 4).reshape(b, s, SSM_WIDTH)

    g_out = p['mix_out_norm']
    yc = _conv(ztok, p['conv_w_dw'], p['conv_b_dw'], p['conv_ln_g'], p['conv_ln_b'], p['conv_w_pw'],
               g_out[SSM_WIDTH:SSM_WIDTH + CONV_WIDTH])

    kvw = NSA_KV_HEADS * HEAD_DIM
    n_cmp = s // CMP_STRIDE
    k_norm = p['nsa_k_norm']
    kc, vct = _compress(ztok, _COL_KC, _COL_VC, p['nsa_cmp_pe'], p['nsa_cmp_w1'], p['nsa_cmp_w2'], k_norm[0])
    ks, kw = _knorm(ztok, _COL_KS, _COL_KW, k_norm[1], k_norm[2])
    qg = jnp.broadcast_to((p['nsa_q_norm'] * (HEAD_DIM ** -0.5 * LOG2E))[:, None], (HEAD_DIM, TQ))
    bias_c, bias_s, bias_w = t5_tiles
    n_sel = s // L_SEL
    cs_ = np.arange(n_cmp) * CMP_STRIDE
    ss_ = np.arange(n_sel) * L_SEL
    ov = np.maximum(np.minimum(cs_[:, None] + L_CMP, ss_[None, :] + L_SEL) - np.maximum(cs_[:, None], ss_[None, :]), 0)
    ovt = jnp.asarray((ov.astype(np.float32) / L_CMP).T, BF16)
    oc_t, sel = _nsa_cmp(zt, qg, kc, vct, bias_c, ovt)

    front = lambda x, n, axis: jnp.pad(x, [(n, 0) if a == axis else (0, 0) for a in range(x.ndim)])
    ones_rows = jnp.concatenate([jnp.ones((b, NSA_KV_HEADS, 1, s), BF16),
                                 jnp.zeros((b, NSA_KV_HEADS, 15, s), BF16)], axis=2)
    heads_t = lambda rows: jnp.concatenate(
        [zt[:, rows:rows + kvw, :].astype(BF16).reshape(b, NSA_KV_HEADS, HEAD_DIM, s), ones_rows], axis=2)
    ks_p = front(ks, SEL_PAD, 2)
    kw_p = front(kw, WINDOW, 2)
    vst_p = front(heads_t(_ROW_VS), SEL_PAD, 3)
    vwt_p = front(heads_t(_ROW_VW), WINDOW, 3)
    sel_p = jnp.pad(sel, ((0, 0), (0, 0), (SEL_PAD // L_SEL, 0), (0, 0)), constant_values=NEG)
    gl = zt[:, _ROW_G:_ROW_G + N_BRANCH * NSA_HEADS, :].reshape(b, NSA_KV_HEADS, NSA_GROUP * N_BRANCH, s)
    gates_t = jnp.pad(gl, ((0, 0), (0, 0), (0, 16 - NSA_GROUP * N_BRANCH), (0, 0)))
    yn_t = _nsa_main(zt, qg, ks_p, vst_p, kw_p, vwt_p, sel_p, bias_s, bias_w, gates_t, oc_t)
    yn = yn_t.transpose(0, 2, 1)

    return _mixout(gy, yc, yn, h, p['ssm_w_glu'], g_out, p['w_out'])


def kernel(x, mem, norm_mix, w_in, ssm_lambda_re, ssm_lambda_im, ssm_log_dt, ssm_b_re, ssm_b_im, ssm_c_re, ssm_c_im, ssm_d, ssm_w_glu, conv_w_dw, conv_b_dw, conv_ln_g, conv_ln_b, conv_w_pw, nsa_q_norm, nsa_k_norm, nsa_cmp_pe, nsa_cmp_w1, nsa_cmp_w2, mix_out_norm, w_out, t5_table, norm_cross, norm_mem, x_w_q, x_w_kv, x_q_norm, x_k_norm, x_w_o, norm_ffn, ffn_w_up, ffn_w_down, moe_router, moe_w_up, moe_w_down):
    b, s, d = x.shape
    depth = w_in.shape[0]
    per_layer = dict(norm_mix=norm_mix, w_in=w_in, ssm_lambda_re=ssm_lambda_re, ssm_lambda_im=ssm_lambda_im,
                     ssm_log_dt=ssm_log_dt, ssm_b_re=ssm_b_re, ssm_b_im=ssm_b_im, ssm_c_re=ssm_c_re,
                     ssm_c_im=ssm_c_im, ssm_d=ssm_d, ssm_w_glu=ssm_w_glu, conv_w_dw=conv_w_dw,
                     conv_b_dw=conv_b_dw, conv_ln_g=conv_ln_g, conv_ln_b=conv_ln_b, conv_w_pw=conv_w_pw,
                     nsa_q_norm=nsa_q_norm, nsa_k_norm=nsa_k_norm, nsa_cmp_pe=nsa_cmp_pe, nsa_cmp_w1=nsa_cmp_w1,
                     nsa_cmp_w2=nsa_cmp_w2, mix_out_norm=mix_out_norm, w_out=w_out)
    t5_tiles = _nsa_bias_tiles(t5_table, s)
    h = x
    for layer in range(depth):
        p = {k: v[layer] for k, v in per_layer.items()}
        h = _layer_mixers(h, p, t5_tiles)
        mk, mv = _memkv(mem, norm_mem[layer], x_w_kv[layer], x_k_norm[layer])
        h = _cross(h, norm_cross[layer], x_w_q[layer], x_q_norm[layer], mk, mv, x_w_o[layer])
        h2d = h.reshape(b * s, d)
        if layer % 2 == 0:
            h2d = _ffn(h2d, norm_ffn[layer], ffn_w_up[layer // 2], ffn_w_down[layer // 2])
        else:
            h2d = _moe(h2d, norm_ffn[layer], moe_router[layer // 2], moe_w_up[layer // 2], moe_w_down[layer // 2])
        h = h2d.reshape(b, s, d)
    return h
```

```python
import functools
import math

import jax
import jax.numpy as jnp
import numpy as np
from jax import lax
from jax.experimental import pallas as pl
from jax.experimental.pallas import tpu as pltpu

F32 = jnp.float32
BF16 = jnp.bfloat16

D_MODEL = 1024
HEAD_DIM = 64
SSM_WIDTH = 256
SSM_GROUP = 16
SSM_GROUPS = 16
SSM_STATE = 64
SSM_CHUNK = 16
CONV_WIDTH = 256
CONV_K = 31
CONV_HALO = 32
NSA_WIDTH = 512
NSA_HEADS = 8
NSA_KV_HEADS = 2
NSA_GROUP = 4
N_BRANCH = 3
L_CMP = 32
CMP_STRIDE = 16
L_SEL = 64
N_SELECT = 16
N_LOCAL = 2
WINDOW = 512
SEL_FORCE = 1e6
T5_BUCKETS = 32
T5_MAX_DIST = 128
X_HEADS = 4
X_WIDTH = 256
D_FF = 2816
N_EXPERTS = 8
TOP_K = 2
EPS = 1e-6
NEG = -1e30
LOG2E = math.log2(math.e)

TQ = 256
SEL_PAD = 256
FF_CHUNK = 256
MOE_TB = 1024
MOE_SUB = 128

_VMEM_LIMIT = 56 * 1024 * 1024


def _cparams(sem, vmem=None):
    return pltpu.CompilerParams(dimension_semantics=sem, vmem_limit_bytes=vmem)


def _rms_rows(x):
    return x * lax.rsqrt(jnp.mean(x * x, axis=-1, keepdims=True) + EPS)


def _proj_kernel(x_ref, g_ref, wtok_ref, wt_ref, ztok_ref, zt_ref):
    xn = (_rms_rows(x_ref[0]) * g_ref[...]).astype(BF16)
    ztok_ref[0] = jnp.dot(xn, wtok_ref[...], preferred_element_type=F32)
    zt_ref[0] = lax.dot_general(wt_ref[...], xn, (((1,), (1,)), ((), ())), preferred_element_type=F32)


def _proj(h, gain, w_tok, w_t, tm=512):
    b, s, d = h.shape
    ntok, nt = w_tok.shape[1], w_t.shape[0]
    return pl.pallas_call(
        _proj_kernel,
        grid=(b, s // tm),
        in_specs=[pl.BlockSpec((1, tm, d), lambda i, j: (i, j, 0)),
                  pl.BlockSpec((1, d), lambda i, j: (0, 0)),
                  pl.BlockSpec((d, ntok), lambda i, j: (0, 0)),
                  pl.BlockSpec((nt, d), lambda i, j: (0, 0))],
        out_specs=[pl.BlockSpec((1, tm, ntok), lambda i, j: (i, j, 0)),
                   pl.BlockSpec((1, nt, tm), lambda i, j: (i, 0, j))],
        out_shape=[jax.ShapeDtypeStruct((b, s, ntok), F32), jax.ShapeDtypeStruct((b, nt, s), F32)],
        compiler_params=_cparams(("parallel", "parallel"), _VMEM_LIMIT),
        name="proj",
    )(h, gain.reshape(1, d), w_tok, w_t)


def _s5_tables(lam_re, lam_im, log_dt, b_re, b_im, c_re, c_im, d_skip):
    L, H, P = SSM_CHUNK, SSM_GROUP, SSM_STATE
    dt = jnp.exp(log_dt.astype(F32))[:, None]
    lr, li = lam_re.astype(F32), lam_im.astype(F32)
    mag = jnp.exp(lr * dt)
    ar, ai = mag * jnp.cos(li * dt), mag * jnp.sin(li * dt)
    den = lr * lr + li * li
    fr = ((ar - 1.0) * lr + ai * li) / den
    fi = (ai * lr - (ar - 1.0) * li) / den
    bbr = fr[..., None] * b_re - fi[..., None] * b_im
    bbi = fr[..., None] * b_im + fi[..., None] * b_re
    j = jnp.arange(L + 1, dtype=F32)[:, None, None]
    pmag = jnp.exp(lr[None] * dt[None] * j)
    pr, pi = pmag * jnp.cos(li[None] * dt[None] * j), pmag * jnp.sin(li[None] * dt[None] * j)
    cbr = c_re[:, :, :, None] * bbr[:, None, :, :] - c_im[:, :, :, None] * bbi[:, None, :, :]
    cbi = c_re[:, :, :, None] * bbi[:, None, :, :] + c_im[:, :, :, None] * bbr[:, None, :, :]
    hp = lax.Precision.HIGHEST
    kj = (jnp.einsum('jgp,ghpk->jghk', pr[:L], cbr, precision=hp)
          - jnp.einsum('jgp,ghpk->jghk', pi[:L], cbi, precision=hp))
    lag = np.arange(L)[None, :] - np.arange(L)[:, None]
    place = (lag[None] == np.arange(L)[:, None, None]).astype(np.float32)
    kt = jnp.einsum('jab,jghk->abghk', place, kj, precision=hp)
    kt = kt + (jnp.eye(L)[:, :, None, None, None] * (jnp.eye(H)[None, None, None] * d_skip[None, None, :, :, None]))
    tmat = kt.transpose(2, 0, 4, 1, 3).reshape(SSM_GROUPS, L * H, L * H)
    qr, qi = pr[:L][::-1], pi[:L][::-1]
    wre = qr[..., None] * bbr[None] - qi[..., None] * bbi[None]
    wim = qr[..., None] * bbi[None] + qi[..., None] * bbr[None]
    wre = wre.transpose(1, 0, 3, 2).reshape(SSM_GROUPS, L * H, P)
    wim = wim.transpose(1, 0, 3, 2).reshape(SSM_GROUPS, L * H, P)
    w1 = jnp.concatenate([wre, wim], axis=-1)
    w2 = jnp.concatenate([wim, wre], axis=-1)
    sr, si = pr[1:], pi[1:]
    vr = c_re[None] * sr[:, :, None, :] - c_im[None] * si[:, :, None, :]
    vi = c_re[None] * si[:, :, None, :] + c_im[None] * sr[:, :, None, :]
    vmat = jnp.concatenate([vr, -vi], axis=-1).transpose(1, 3, 0, 2).reshape(SSM_GROUPS, 2 * P, L * H)
    a_r, a_i = pr[L], pi[L]
    am = jnp.stack([jnp.concatenate([a_r, a_r], -1), jnp.concatenate([-a_i, a_i], -1),
                    jnp.concatenate([a_i, -a_i], -1)], axis=1)
    am = jnp.concatenate([am, jnp.zeros((SSM_GROUPS, 5, 2 * P), F32)], axis=1)
    return tmat.astype(BF16), w1.astype(BF16), w2.astype(BF16), vmat.astype(BF16), am


def _s5_kernel(x_ref, t_ref, w1_ref, w2_ref, v_ref, a_ref, o_ref, s1_ref, s2_ref, xin_ref, *, bsz, n_chunks):
    x = x_ref[0]
    s1_ref[...] = jnp.dot(x, w1_ref[0], preferred_element_type=F32)
    s2_ref[...] = jnp.dot(x, w2_ref[0], preferred_element_type=F32)
    a1, a2, a3 = a_ref[0, 0:1, :], a_ref[0, 1:2, :], a_ref[0, 2:3, :]

    def step(c, carry):
        p, q = carry
        rows = pl.ds(pl.multiple_of(c * bsz, bsz), bsz)
        xin_ref[rows, :] = p
        return (p * a1 + q * a2 + s1_ref[rows, :], q * a1 + p * a3 + s2_ref[rows, :])

    zero = jnp.zeros((bsz, 2 * SSM_STATE), F32)
    lax.fori_loop(0, n_chunks, step, (zero, zero))
    y = (jnp.dot(x, t_ref[0], preferred_element_type=F32)
         + jnp.dot(xin_ref[...].astype(BF16), v_ref[0], preferred_element_type=F32))
    o_ref[0] = jax.nn.gelu(y).astype(o_ref.dtype)


def _s5(xg, tables, bsz):
    tmat, w1, w2, vmat, am = tables
    g, r, lh = xg.shape
    p2 = 2 * SSM_STATE
    kern = functools.partial(_s5_kernel, bsz=bsz, n_chunks=r // bsz)
    return pl.pallas_call(
        kern,
        grid=(g,),
        in_specs=[pl.BlockSpec((1, r, lh), lambda i: (i, 0, 0)),
                  pl.BlockSpec((1, lh, lh), lambda i: (i, 0, 0)),
                  pl.BlockSpec((1, lh, p2), lambda i: (i, 0, 0)),
                  pl.BlockSpec((1, lh, p2), lambda i: (i, 0, 0)),
                  pl.BlockSpec((1, p2, lh), lambda i: (i, 0, 0)),
                  pl.BlockSpec((1, 8, p2), lambda i: (i, 0, 0))],
        out_specs=pl.BlockSpec((1, r, lh), lambda i: (i, 0, 0)),
        out_shape=jax.ShapeDtypeStruct((g, r, lh), BF16),
        scratch_shapes=[pltpu.VMEM((r, p2), F32), pltpu.VMEM((r, p2), F32), pltpu.VMEM((r, p2), F32)],
        compiler_params=_cparams(("parallel",), _VMEM_LIMIT),
        name="s5_scan",
    )(xg, tmat, w1, w2, vmat, am)


def _conv_kernel(z_ref, halo_ref, wdw_ref, bdw_ref, lng_ref, lnb_ref, wpw_ref, go_ref, o_ref, buf_ref, *, tt):
    first = pl.program_id(1) == 0
    zc = z_ref[0]
    zh = halo_ref[0]
    vh = zh[:, :CONV_WIDTH] * jax.nn.sigmoid(zh[:, CONV_WIDTH:])
    buf_ref[0:CONV_HALO, :] = vh * jnp.where(first, 0.0, 1.0)
    buf_ref[CONV_HALO:CONV_HALO + tt, :] = zc[:, :CONV_WIDTH] * jax.nn.sigmoid(zc[:, CONV_WIDTH:])
    acc = jnp.zeros((tt, CONV_WIDTH), F32) + bdw_ref[...]
    for k in range(CONV_K):
        acc = acc + wdw_ref[k:k + 1, :] * buf_ref[pl.ds(CONV_HALO - (CONV_K - 1) + k, tt), :]
    mu = jnp.mean(acc, axis=-1, keepdims=True)
    var = jnp.mean(jnp.square(acc - mu), axis=-1, keepdims=True)
    y = (acc - mu) * lax.rsqrt(var + EPS) * lng_ref[...] + lnb_ref[...]
    y = jax.nn.silu(y)
    y = jnp.dot(y.astype(BF16), wpw_ref[...], preferred_element_type=F32)
    o_ref[0] = (_rms_rows(y) * go_ref[...]).astype(o_ref.dtype)


def _conv(ztok, w_dw, b_dw, ln_g, ln_b, w_pw, g_out, tt=512):
    b, s, _ = ztok.shape
    cw = CONV_WIDTH
    hb = tt // CONV_HALO
    kern = functools.partial(_conv_kernel, tt=tt)
    row = lambda v: v.reshape(1, cw)
    return pl.pallas_call(
        kern,
        grid=(b, s // tt),
        in_specs=[pl.BlockSpec((1, tt, 2 * cw), lambda i, j: (i, j, 0)),
                  pl.BlockSpec((1, CONV_HALO, 2 * cw), lambda i, j: (i, jnp.maximum(j * hb - 1, 0), 0)),
                  pl.BlockSpec((CONV_K + 1, cw), lambda i, j: (0, 0)),
                  pl.BlockSpec((1, cw), lambda i, j: (0, 0)),
                  pl.BlockSpec((1, cw), lambda i, j: (0, 0)),
                  pl.BlockSpec((1, cw), lambda i, j: (0, 0)),
                  pl.BlockSpec((cw, cw), lambda i, j: (0, 0)),
                  pl.BlockSpec((1, cw), lambda i, j: (0, 0))],
        out_specs=pl.BlockSpec((1, tt, cw), lambda i, j: (i, j, 0)),
        out_shape=jax.ShapeDtypeStruct((b, s, cw), BF16),
        scratch_shapes=[pltpu.VMEM((CONV_HALO + tt, cw), F32)],
        compiler_params=_cparams(("parallel", "arbitrary")),
        name="conv_mixer",
    )(ztok, ztok, jnp.concatenate([w_dw, jnp.zeros((1, cw), F32)], 0), row(b_dw), row(ln_g), row(ln_b),
      w_pw.astype(BF16), row(g_out))


def _knorm_kernel(ks_ref, kw_ref, gs_ref, gw_ref, os_ref, ow_ref):
    for src, g_ref, dst in ((ks_ref, gs_ref, os_ref), (kw_ref, gw_ref, ow_ref)):
        x = src[0]
        for h in range(NSA_KV_HEADS):
            xh = x[:, h * HEAD_DIM:(h + 1) * HEAD_DIM]
            dst[0, h] = (_rms_rows(xh) * g_ref[...]).astype(dst.dtype)


def _knorm(ztok, col_s, col_w, gain_s, gain_w, tt=512):
    b, s, _ = ztok.shape
    kw = NSA_KV_HEADS * HEAD_DIM
    out = jax.ShapeDtypeStruct((b, NSA_KV_HEADS, s, HEAD_DIM), BF16)
    ospec = pl.BlockSpec((1, NSA_KV_HEADS, tt, HEAD_DIM), lambda i, j: (i, 0, j, 0))
    return pl.pallas_call(
        _knorm_kernel,
        grid=(b, s // tt),
        in_specs=[pl.BlockSpec((1, tt, kw), lambda i, j: (i, j, col_s // kw)),
                  pl.BlockSpec((1, tt, kw), lambda i, j: (i, j, col_w // kw)),
                  pl.BlockSpec((1, HEAD_DIM), lambda i, j: (0, 0)),
                  pl.BlockSpec((1, HEAD_DIM), lambda i, j: (0, 0))],
        out_specs=[ospec, ospec],
        out_shape=[out, out],
        compiler_params=_cparams(("parallel", "parallel")),
        name="nsa_key_norm",
    )(ztok, ztok, gain_s.reshape(1, HEAD_DIM), gain_w.reshape(1, HEAD_DIM))


def _compress_kernel(k_ref, v_ref, wka_ref, wkb_ref, ck_ref, w2k_ref, gk_ref,
                     wva_ref, wvb_ref, cv_ref, w2v_ref, ko_ref, vo_ref):
    hi = lax.Precision.HIGHEST
    n = k_ref.shape[1] // CMP_STRIDE
    kvw = k_ref.shape[2]
    nt = (((1,), (1,)), ((), ()))
    a, bm = jnp.zeros((n, kvw), F32), jnp.zeros((n, kvw), F32)
    at, bt = jnp.zeros((kvw, n), F32), jnp.zeros((kvw, n), F32)
    for l in range(CMP_STRIDE):
        kl = k_ref[0, pl.ds(l, n, stride=CMP_STRIDE), :]
        vl = v_ref[0, pl.ds(l, n, stride=CMP_STRIDE), :]
        a = a + jnp.dot(kl, wka_ref[l], precision=hi, preferred_element_type=F32)
        bm = bm + jnp.dot(kl, wkb_ref[l], precision=hi, preferred_element_type=F32)
        at = at + lax.dot_general(wva_ref[l], vl, nt, precision=hi, preferred_element_type=F32)
        bt = bt + lax.dot_general(wvb_ref[l], vl, nt, precision=hi, preferred_element_type=F32)
    pre = a + pltpu.roll(bm, n - 1, 0) + ck_ref[...]
    kc = jnp.dot(jax.nn.gelu(pre), w2k_ref[...], precision=hi, preferred_element_type=F32)
    for h in range(NSA_KV_HEADS):
        kh = kc[:, h * HEAD_DIM:(h + 1) * HEAD_DIM]
        ko_ref[0, h] = (_rms_rows(kh) * gk_ref[...]).astype(ko_ref.dtype)
    pre_t = at + pltpu.roll(bt, n - 1, 1) + cv_ref[...]
    vt = jnp.dot(w2v_ref[...], jax.nn.gelu(pre_t), precision=hi, preferred_element_type=F32)
    for h in range(NSA_KV_HEADS):
        vo_ref[0, h] = vt[h * HEAD_DIM:(h + 1) * HEAD_DIM, :].astype(vo_ref.dtype)


def _blockdiag2(w):
    z = jnp.zeros_like(w)
    return jnp.concatenate([jnp.concatenate([w, z], 1), jnp.concatenate([z, w], 1)], 0)


def _compress(ztok, col_k, col_v, pe, w1, w2, k_gain):
    b, s, _ = ztok.shape
    n = s // CMP_STRIDE
    hd, kvw = HEAD_DIM, NSA_KV_HEADS * HEAD_DIM
    hp = lax.Precision.HIGHEST

    def expand(w):
        wl = w.reshape(L_CMP, hd, hd)
        e = wl[:, None, :, None, :] * jnp.eye(NSA_KV_HEADS, dtype=F32)[None, :, None, :, None]
        e = e.reshape(L_CMP, kvw, kvw)
        return e[:CMP_STRIDE], e[CMP_STRIDE:]

    wka, wkb = expand(w1[0])
    wva, wvb = expand(w1[1])
    ck = jnp.tile(jnp.dot(pe[0].reshape(1, L_CMP * hd), w1[0], precision=hp), (1, NSA_KV_HEADS))
    cv = jnp.tile(jnp.dot(pe[1].reshape(1, L_CMP * hd), w1[1], precision=hp), (1, NSA_KV_HEADS)).T
    full = lambda shape: pl.BlockSpec(shape, lambda i: tuple(0 for _ in shape))
    return pl.pallas_call(
        _compress_kernel,
        grid=(b,),
        in_specs=[pl.BlockSpec((1, s, kvw), lambda i: (i, 0, col_k // kvw)),
                  pl.BlockSpec((1, s, kvw), lambda i: (i, 0, col_v // kvw)),
                  full((CMP_STRIDE, kvw, kvw)), full((CMP_STRIDE, kvw, kvw)), full((1, kvw)), full((kvw, kvw)),
                  full((1, hd)),
                  full((CMP_STRIDE, kvw, kvw)), full((CMP_STRIDE, kvw, kvw)), full((kvw, 1)), full((kvw, kvw))],
        out_specs=[pl.BlockSpec((1, NSA_KV_HEADS, n, hd), lambda i: (i, 0, 0, 0)),
                   pl.BlockSpec((1, NSA_KV_HEADS, hd, n), lambda i: (i, 0, 0, 0))],
        out_shape=[jax.ShapeDtypeStruct((b, NSA_KV_HEADS, n, hd), BF16),
                   jax.ShapeDtypeStruct((b, NSA_KV_HEADS, hd, n), BF16)],
        compiler_params=_cparams(("parallel",), _VMEM_LIMIT),
        name="nsa_compress",
    )(ztok, ztok, wka, wkb, ck, _blockdiag2(w2[0]), k_gain.reshape(1, hd),
      wva.transpose(0, 2, 1), wvb.transpose(0, 2, 1), cv, _blockdiag2(w2[1]).T)


def _t5_bias_by_dist(t5_table):
    n = np.arange(T5_MAX_DIST + 1)
    max_exact = T5_BUCKETS // 2
    nf = np.maximum(n, 1).astype(np.float32)
    large = max_exact + (np.log(nf / np.float32(max_exact)) / np.float32(math.log(T5_MAX_DIST / max_exact))
                         * np.float32(T5_BUCKETS - max_exact)).astype(np.int32)
    large = np.minimum(large, T5_BUCKETS - 1)
    bucket = np.where(n < max_exact, n, large)
    onehot = (bucket[:, None] == np.arange(T5_BUCKETS)[None, :]).astype(np.float32)
    return jnp.dot(onehot, t5_table, precision=lax.Precision.HIGHEST)


def _bias_tile(fdt, rows, stride, dist00, d_max=None):
    heads = fdt.shape[0]
    a0 = stride * (rows - 1)
    d_lo = dist00 - a0
    length = a0 + TQ + stride
    d_hi = d_lo + length
    d_max = d_hi if d_max is None else d_max
    pieces = []
    for lo, hi, kind in ((d_lo, min(d_hi, 0), 'neg'), (max(d_lo, 0), min(d_hi, T5_MAX_DIST), 'tab'),
                         (max(d_lo, T5_MAX_DIST), min(d_hi, d_max), 'far'), (max(d_lo, d_max), d_hi, 'neg')):
        if hi > lo:
            pieces.append(fdt[:, lo:hi] if kind == 'tab'
                          else jnp.full((heads, hi - lo), NEG if kind == 'neg' else 0.0, F32))
    vec = jnp.concatenate(pieces, axis=1)
    skew = jnp.tile(vec, (1, rows))[:, :rows * (length - stride)].reshape(heads, rows, length - stride)
    return skew[:, :, a0:a0 + TQ]


def _nsa_bias_tiles(t5_table, seq):
    fd = _t5_bias_by_dist(t5_table).astype(F32)
    fdt = ((fd - fd[T5_MAX_DIST:]) * LOG2E).T
    n_cmp = seq // CMP_STRIDE
    qt = TQ // CMP_STRIDE
    r0 = n_cmp - qt
    band = _bias_tile(fdt, 2 * qt, CMP_STRIDE, CMP_STRIDE * qt - (L_CMP - 1))
    heads = fdt.shape[0]
    cmp_t = jnp.concatenate([jnp.zeros((heads, r0 - qt, TQ), F32), band,
                             jnp.full((heads, n_cmp - qt, TQ), NEG, F32)], axis=1)
    sel_t = _bias_tile(fdt, SEL_PAD + TQ, 1, SEL_PAD)
    win = _bias_tile(fdt, WINDOW + TQ, 1, WINDOW, d_max=WINDOW)
    rw = np.arange(WINDOW + TQ)[None, :, None]
    win_t = jnp.stack([jnp.where(rw >= WINDOW - q0, win, NEG) for q0 in (0, TQ, 2 * TQ)])
    split = lambda t: t.reshape(*t.shape[:-3], NSA_KV_HEADS, NSA_GROUP, *t.shape[-2:])

    def wide(t):
        t = jnp.swapaxes(split(t), -3, -2)
        return t.reshape(*t.shape[:-2], NSA_GROUP * TQ)

    return wide(cmp_t), wide(sel_t), wide(win_t)


def _q_head(qt_ref, g, qg_ref):
    q = qt_ref[0, g * HEAD_DIM:(g + 1) * HEAD_DIM, :]
    inv = lax.rsqrt(jnp.mean(q * q, axis=0, keepdims=True) + EPS)
    return (q * inv * qg_ref[...]).astype(BF16)


def _nsa_cmp_kernel(qt_ref, qg_ref, kc_ref, vct_ref, bias_ref, ov_ref, oc_ref, sel_ref, *, n_cmp, n_sel):
    qi = pl.program_id(2)
    kc = kc_ref[0, 0]
    vct = vct_ref[0, 0]
    row0 = pl.multiple_of((n_cmp - TQ // CMP_STRIDE) - qi * (TQ // CMP_STRIDE), TQ // CMP_STRIDE)
    qw = jnp.concatenate([_q_head(qt_ref, g, qg_ref) for g in range(NSA_GROUP)], axis=1)
    s = jnp.dot(kc, qw, preferred_element_type=F32) + bias_ref[0, pl.ds(row0, n_cmp), :]
    m = jnp.max(s, axis=0, keepdims=True)
    m = jnp.where(m < 0.5 * NEG, 0.0, m)
    p = jnp.exp2(s - m)
    p = p * (1.0 / jnp.maximum(jnp.sum(p, axis=0, keepdims=True), 1e-30))
    oc = jnp.dot(vct, p.astype(BF16), preferred_element_type=F32)
    psum = jnp.zeros((n_cmp, TQ), F32)
    for g in range(NSA_GROUP):
        oc_ref[0, g * HEAD_DIM:(g + 1) * HEAD_DIM, :] = oc[:, g * TQ:(g + 1) * TQ]
        psum = psum + p[:, g * TQ:(g + 1) * TQ]
    hi = psum.astype(BF16)
    lo = (psum - hi.astype(F32)).astype(BF16)
    imp = (jnp.dot(ov_ref[...], hi, preferred_element_type=F32)
           + jnp.dot(ov_ref[...], lo, preferred_element_type=F32))
    blk = lax.broadcasted_iota(jnp.int32, (n_sel, TQ), 0)
    blk_t = lax.shift_right_logical(qi * TQ + lax.broadcasted_iota(jnp.int32, (n_sel, TQ), 1), L_SEL.bit_length() - 1)
    forced = (blk == 0) | (blk > blk_t - N_LOCAL)
    v0 = jnp.where(blk > blk_t, -jnp.inf, jnp.where(forced, SEL_FORCE, imp))

    def pick(_, v):
        m = jnp.max(v, axis=0, keepdims=True)
        cand = (v == m) & (m > -jnp.inf)
        first = jnp.min(jnp.where(cand, blk, n_sel), axis=0, keepdims=True)
        return jnp.where(blk == first, -jnp.inf, v)

    v = lax.fori_loop(0, min(N_SELECT, n_sel), pick, v0)
    sel_ref[0, 0] = jnp.where((v == -jnp.inf) & (v0 > -jnp.inf), 0.0, NEG)


def _nsa_cmp(zt, qg, kc, vct, bias_c, ovt):
    b, _, s = zt.shape
    n_cmp, n_sel = s // CMP_STRIDE, s // L_SEL
    gw = NSA_GROUP * HEAD_DIM
    kern = functools.partial(_nsa_cmp_kernel, n_cmp=n_cmp, n_sel=n_sel)
    return pl.pallas_call(
        kern,
        grid=(b, NSA_KV_HEADS, s // TQ),
        in_specs=[pl.BlockSpec((1, gw, TQ), lambda i, k, j: (i, k, j)),
                  pl.BlockSpec((HEAD_DIM, TQ), lambda i, k, j: (0, 0)),
                  pl.BlockSpec((1, 1, n_cmp, HEAD_DIM), lambda i, k, j: (i, k, 0, 0)),
                  pl.BlockSpec((1, 1, HEAD_DIM, n_cmp), lambda i, k, j: (i, k, 0, 0)),
                  pl.BlockSpec((1, bias_c.shape[1], NSA_GROUP * TQ), lambda i, k, j: (k, 0, 0)),
                  pl.BlockSpec((n_sel, n_cmp), lambda i, k, j: (0, 0))],
        out_specs=[pl.BlockSpec((1, gw, TQ), lambda i, k, j: (i, k, j)),
                   pl.BlockSpec((1, 1, n_sel, TQ), lambda i, k, j: (i, k, 0, j))],
        out_shape=[jax.ShapeDtypeStruct((b, NSA_WIDTH, s), F32),
                   jax.ShapeDtypeStruct((b, NSA_KV_HEADS, n_sel, s), F32)],
        compiler_params=_cparams(("parallel", "parallel", "parallel"), _VMEM_LIMIT),
        name="nsa_compressed_select",
    )(zt, qg, kc, vct, bias_c, ovt)


def _nsa_main_kernel(qt_ref, qg_ref, ks_ref, vst_ref, kw_ref, vwt_ref, sel_ref, bs_ref, bw_ref, gate_ref,
                     oc_ref, o_ref, acc_ref, s_ref):
    qi = pl.program_id(1)
    q0 = pl.multiple_of(qi * TQ, TQ)
    near = SEL_PAD + TQ
    gw = NSA_GROUP * HEAD_DIM
    kvs = range(NSA_KV_HEADS)

    def expand_sel(kv, first_blk, n_blk):
        rows = [jnp.broadcast_to(sel_ref[0, kv, pl.ds(first_blk + r, 1), :], (L_SEL, TQ)) for r in range(n_blk)]
        rows = jnp.concatenate(rows, axis=0)
        return jnp.concatenate([rows] * NSA_GROUP, axis=1)

    qw = [jnp.concatenate([_q_head(qt_ref, kv * NSA_GROUP + g, qg_ref) for g in range(NSA_GROUP)], axis=1)
          for kv in kvs]
    m0 = []
    for kv in kvs:
        s = (jnp.dot(ks_ref[0, kv, pl.ds(q0, near), :], qw[kv], preferred_element_type=F32) + bs_ref[kv]
             + expand_sel(kv, qi * (TQ // L_SEL), near // L_SEL))
        m = jnp.max(s, axis=0, keepdims=True)
        p = jnp.exp2(s - m).astype(BF16)
        acc_ref[kv] = jnp.dot(vst_ref[0, kv, :, pl.ds(q0, near)], p, preferred_element_type=F32)
        m0.append(m)

    def scores(c, slot):
        r0 = pl.multiple_of(c * TQ, TQ)
        mc = []
        for kv in kvs:
            s = (jnp.dot(ks_ref[0, kv, pl.ds(r0, TQ), :], qw[kv], preferred_element_type=F32)
                 + expand_sel(kv, c * (TQ // L_SEL), TQ // L_SEL))
            s_ref[slot, kv] = s
            mc.append(jnp.max(s, axis=0, keepdims=True))
        return tuple(mc)

    def consume(c, slot, m_old, mc):
        r0 = pl.multiple_of(c * TQ, TQ)
        m_out = []
        for kv in kvs:
            m_new = jnp.maximum(m_old[kv], mc[kv])
            alpha = jnp.exp2(m_old[kv] - m_new)
            p = jnp.exp2(s_ref[slot, kv] - m_new).astype(BF16)
            acc_ref[kv] = alpha * acc_ref[kv] + jnp.dot(vst_ref[0, kv, :, pl.ds(r0, TQ)], p,
                                                        preferred_element_type=F32)
            m_out.append(m_new)
        return tuple(m_out)

    first = SEL_PAD // TQ
    n_far = qi - first

    def pair(i, carry):
        m, mc = carry
        c = first + 2 * i
        mc1 = scores(c + 1, 1)
        m = consume(c, 0, m, mc)
        mc2 = scores(c + 2, 0)
        m = consume(c + 1, 1, m, mc1)
        return m, mc2

    m_far, mc_far = lax.fori_loop(0, n_far // 2, pair, (tuple(m0), scores(first, 0)))

    @pl.when((n_far > 0) & (n_far % 2 == 1))
    def _():
        consume(first + n_far - 1, 0, m_far, mc_far)

    for kv in kvs:
        s = jnp.dot(kw_ref[0, kv, pl.ds(q0, WINDOW + TQ), :], qw[kv], preferred_element_type=F32) + bw_ref[0, kv]
        p = jnp.exp2(s - jnp.max(s, axis=0, keepdims=True)).astype(BF16)
        ow = jnp.dot(vwt_ref[0, kv, :, pl.ds(q0, WINDOW + TQ)], p, preferred_element_type=F32)
        ow = ow[:HEAD_DIM] * (1.0 / jnp.maximum(ow[HEAD_DIM:HEAD_DIM + 1], 1e-30))
        os = acc_ref[kv]
        os = os[:HEAD_DIM] * (1.0 / jnp.maximum(os[HEAD_DIM:HEAD_DIM + 1], 1e-30))
        for g in range(NSA_GROUP):
            gates = jax.nn.sigmoid(gate_ref[0, kv, g * N_BRANCH:(g + 1) * N_BRANCH, :])
            rows = slice(kv * gw + g * HEAD_DIM, kv * gw + (g + 1) * HEAD_DIM)
            cols = slice(g * TQ, (g + 1) * TQ)
            o_ref[0, rows, :] = (gates[0:1] * oc_ref[0, rows, :] + gates[1:2] * os[:, cols]
                                 + gates[2:3] * ow[:, cols])


def _nsa_main(zt, qg, ks_p, vst_p, kw_p, vwt_p, sel_p, bias_s, bias_w, gates_t, oc_t):
    b, _, s = zt.shape
    kvh = NSA_KV_HEADS
    sp, wp = ks_p.shape[2], kw_p.shape[2]
    nb, vr = sel_p.shape[2], vst_p.shape[2]
    once = pl.Buffered(1)
    return pl.pallas_call(
        _nsa_main_kernel,
        grid=(b, s // TQ),
        in_specs=[pl.BlockSpec((1, NSA_WIDTH, TQ), lambda i, j: (i, 0, j)),
                  pl.BlockSpec((HEAD_DIM, TQ), lambda i, j: (0, 0)),
                  pl.BlockSpec((1, kvh, sp, HEAD_DIM), lambda i, j: (i, 0, 0, 0), pipeline_mode=once),
                  pl.BlockSpec((1, kvh, vr, sp), lambda i, j: (i, 0, 0, 0), pipeline_mode=once),
                  pl.BlockSpec((1, kvh, wp, HEAD_DIM), lambda i, j: (i, 0, 0, 0), pipeline_mode=once),
                  pl.BlockSpec((1, kvh, vr, wp), lambda i, j: (i, 0, 0, 0), pipeline_mode=once),
                  pl.BlockSpec((1, kvh, nb, TQ), lambda i, j: (i, 0, 0, j)),
                  pl.BlockSpec((kvh, SEL_PAD + TQ, NSA_GROUP * TQ), lambda i, j: (0, 0, 0), pipeline_mode=once),
                  pl.BlockSpec((1, kvh, WINDOW + TQ, NSA_GROUP * TQ), lambda i, j: (jnp.minimum(j, 2), 0, 0, 0)),
                  pl.BlockSpec((1, kvh, 16, TQ), lambda i, j: (i, 0, 0, j)),
                  pl.BlockSpec((1, NSA_WIDTH, TQ), lambda i, j: (i, 0, j))],
        out_specs=pl.BlockSpec((1, NSA_WIDTH, TQ), lambda i, j: (i, 0, j)),
        out_shape=jax.ShapeDtypeStruct((b, NSA_WIDTH, s), F32),
        scratch_shapes=[pltpu.VMEM((kvh, vr, NSA_GROUP * TQ), F32),
                        pltpu.VMEM((2, kvh, TQ, NSA_GROUP * TQ), F32)],
        compiler_params=_cparams(("parallel", "arbitrary"), _VMEM_LIMIT),
        name="nsa_selected_window",
    )(zt, qg, ks_p, vst_p, kw_p, vwt_p, sel_p, bias_s, bias_w, gates_t, oc_t)


def _mixout_kernel(gy_ref, yc_ref, yn_ref, h_ref, wglu_ref, go_ref, wo_ref, o_ref):
    sw = SSM_WIDTH
    ag = jnp.dot(gy_ref[0], wglu_ref[...], preferred_element_type=F32)
    ys = ag[:, :sw] * jax.nn.sigmoid(ag[:, sw:])
    ys = (_rms_rows(ys) * go_ref[:, 0:sw]).astype(BF16)
    yn = (_rms_rows(yn_ref[0]) * go_ref[:, 2 * sw:]).astype(BF16)
    out = (jnp.dot(ys, wo_ref[0:sw, :], preferred_element_type=F32)
           + jnp.dot(yc_ref[0], wo_ref[sw:2 * sw, :], preferred_element_type=F32)
           + jnp.dot(yn, wo_ref[2 * sw:, :], preferred_element_type=F32))
    o_ref[0] = h_ref[0] + out


def _mixout(gy, yc, yn, h, w_glu, g_out, w_out, tm=512):
    b, s, d = h.shape
    tok = lambda w: pl.BlockSpec((1, tm, w), lambda i, j: (i, j, 0))
    return pl.pallas_call(
        _mixout_kernel,
        grid=(b, s // tm),
        in_specs=[tok(SSM_WIDTH), tok(CONV_WIDTH), tok(NSA_WIDTH), tok(d),
                  pl.BlockSpec((SSM_WIDTH, 2 * SSM_WIDTH), lambda i, j: (0, 0)),
                  pl.BlockSpec((1, d), lambda i, j: (0, 0)),
                  pl.BlockSpec((d, d), lambda i, j: (0, 0))],
        out_specs=tok(d),
        out_shape=jax.ShapeDtypeStruct((b, s, d), F32),
        compiler_params=_cparams(("parallel", "parallel")),
        name="mix_out",
    )(gy, yc, yn, h, w_glu.astype(BF16), g_out.reshape(1, d), w_out.astype(BF16))


def _memkv_kernel(mem_ref, g_ref, w_ref, kg_ref, k_ref, v_ref):
    mn = (_rms_rows(mem_ref[0]) * g_ref[...]).astype(BF16)
    kv = jnp.dot(mn, w_ref[...], preferred_element_type=F32)
    for h in range(X_HEADS):
        cols = slice(h * HEAD_DIM, (h + 1) * HEAD_DIM)
        k_ref[0, :, cols] = (_rms_rows(kv[:, cols]) * kg_ref[...]).astype(k_ref.dtype)
    v_ref[0] = kv[:, X_WIDTH:].astype(v_ref.dtype)


def _memkv(mem, gain, w_kv, k_gain):
    b, m, d = mem.shape
    out = jax.ShapeDtypeStruct((b, m, X_WIDTH), BF16)
    return pl.pallas_call(
        _memkv_kernel,
        grid=(b,),
        in_specs=[pl.BlockSpec((1, m, d), lambda i: (i, 0, 0)),
                  pl.BlockSpec((1, d), lambda i: (0, 0)),
                  pl.BlockSpec((d, 2 * X_WIDTH), lambda i: (0, 0)),
                  pl.BlockSpec((1, HEAD_DIM), lambda i: (0, 0))],
        out_specs=[pl.BlockSpec((1, m, X_WIDTH), lambda i: (i, 0, 0))] * 2,
        out_shape=[out, out],
        compiler_params=_cparams(("parallel",)),
        name="cross_mem_kv",
    )(mem, gain.reshape(1, d), w_kv.astype(BF16), k_gain.reshape(1, HEAD_DIM))


def _cross_kernel(h_ref, g_ref, wq_ref, qg_ref, k_ref, v_ref, wo_ref, o_ref):
    h = h_ref[0]
    hn = (_rms_rows(h) * g_ref[...]).astype(BF16)
    q = jnp.dot(hn, wq_ref[...], preferred_element_type=F32)
    out = h
    for hd in range(X_HEADS):
        cols = slice(hd * HEAD_DIM, (hd + 1) * HEAD_DIM)
        qh = (_rms_rows(q[:, cols]) * qg_ref[...]).astype(BF16)
        s = lax.dot_general(qh, k_ref[0, :, cols], (((1,), (1,)), ((), ())), preferred_element_type=F32)
        p = jnp.exp(s - jnp.max(s, axis=-1, keepdims=True))
        p = p * (1.0 / jnp.sum(p, axis=-1, keepdims=True))
        o = jnp.dot(p.astype(BF16), v_ref[0, :, cols], preferred_element_type=F32)
        out = out + jnp.dot(o.astype(BF16), wo_ref[cols, :], preferred_element_type=F32)
    o_ref[0] = out


def _cross(h, gain, w_q, q_gain, k, v, w_o, tm=512):
    b, s, d = h.shape
    m = k.shape[1]
    return pl.pallas_call(
        _cross_kernel,
        grid=(b, s // tm),
        in_specs=[pl.BlockSpec((1, tm, d), lambda i, j: (i, j, 0)),
                  pl.BlockSpec((1, d), lambda i, j: (0, 0)),
                  pl.BlockSpec((d, X_WIDTH), lambda i, j: (0, 0)),
                  pl.BlockSpec((1, HEAD_DIM), lambda i, j: (0, 0)),
                  pl.BlockSpec((1, m, X_WIDTH), lambda i, j: (i, 0, 0)),
                  pl.BlockSpec((1, m, X_WIDTH), lambda i, j: (i, 0, 0)),
                  pl.BlockSpec((X_WIDTH, d), lambda i, j: (0, 0))],
        out_specs=pl.BlockSpec((1, tm, d), lambda i, j: (i, j, 0)),
        out_shape=jax.ShapeDtypeStruct((b, s, d), F32),
        compiler_params=_cparams(("parallel", "parallel")),
        name="cross_attention",
    )(h, gain.reshape(1, d), w_q.astype(BF16), (q_gain * HEAD_DIM ** -0.5).reshape(1, HEAD_DIM), k, v,
      w_o.astype(BF16))


def _ffn_kernel(h_ref, g_ref, wg_ref, wv_ref, wd_ref, o_ref, xn_ref, acc_ref):
    f = pl.program_id(1)

    @pl.when(f == 0)
    def _():
        xn_ref[...] = (_rms_rows(h_ref[...]) * g_ref[...]).astype(BF16)
        acc_ref[...] = jnp.zeros_like(acc_ref)

    x = xn_ref[...]
    gate = jnp.dot(x, wg_ref[...], preferred_element_type=F32)
    val = jnp.dot(x, wv_ref[...], preferred_element_type=F32)
    act = (jax.nn.silu(gate) * val).astype(BF16)
    acc_ref[...] += jnp.dot(act, wd_ref[...], preferred_element_type=F32)

    @pl.when(f == pl.num_programs(1) - 1)
    def _():
        o_ref[...] = h_ref[...] + acc_ref[...]


def _ffn(h2d, gain, w_up, w_down, tm=1024):
    t, d = h2d.shape
    nf = D_FF // FF_CHUNK
    wb = w_up.astype(BF16)
    return pl.pallas_call(
        _ffn_kernel,
        grid=(t // tm, nf),
        in_specs=[pl.BlockSpec((tm, d), lambda i, f: (i, 0)),
                  pl.BlockSpec((1, d), lambda i, f: (0, 0)),
                  pl.BlockSpec((d, FF_CHUNK), lambda i, f: (0, f)),
                  pl.BlockSpec((d, FF_CHUNK), lambda i, f: (0, f + nf)),
                  pl.BlockSpec((FF_CHUNK, d), lambda i, f: (f, 0))],
        out_specs=pl.BlockSpec((tm, d), lambda i, f: (i, 0)),
        out_shape=jax.ShapeDtypeStruct((t, d), F32),
        scratch_shapes=[pltpu.VMEM((tm, d), BF16), pltpu.VMEM((tm, d), F32)],
        compiler_params=_cparams(("parallel", "arbitrary"), _VMEM_LIMIT),
        name="ffn_swiglu",
    )(h2d, gain.reshape(1, d), wb, wb, w_down.astype(BF16))


def _router_kernel(h_ref, g_ref, wr_ref, xn_ref, gate_ref, asg_ref):
    xn = _rms_rows(h_ref[...]) * g_ref[...]
    xn_ref[...] = xn.astype(BF16)
    logits = jnp.dot(xn, wr_ref[...], precision=lax.Precision.HIGHEST, preferred_element_type=F32)
    lane = lax.broadcasted_iota(jnp.int32, logits.shape, 1)
    lg = jnp.where(lane < N_EXPERTS, logits, -jnp.inf)
    m1 = jnp.max(lg, axis=-1, keepdims=True)
    i1 = jnp.min(jnp.where(lg == m1, lane, 128), axis=-1, keepdims=True)
    lg2 = jnp.where(lane == i1, -jnp.inf, lg)
    m2 = jnp.max(lg2, axis=-1, keepdims=True)
    i2 = jnp.min(jnp.where(lg2 == m2, lane, 128), axis=-1, keepdims=True)
    e = jnp.exp(m2 - m1)
    den = 1.0 + e
    gate_ref[...] = jnp.where(lane == i1, 1.0 / den, jnp.where(lane == i2, e / den, 0.0))
    asg_ref[...] = ((lane == i1) | (lane == i2)).astype(jnp.int32)


def _router(h2d, gain, w_router, tm=512):
    t, d = h2d.shape
    wr = jnp.concatenate([w_router, jnp.zeros((d, 128 - N_EXPERTS), F32)], axis=1)
    return pl.pallas_call(
        _router_kernel,
        grid=(t // tm,),
        in_specs=[pl.BlockSpec((tm, d), lambda i: (i, 0)),
                  pl.BlockSpec((1, d), lambda i: (0, 0)),
                  pl.BlockSpec((d, 128), lambda i: (0, 0))],
        out_specs=[pl.BlockSpec((tm, d), lambda i: (i, 0)),
                   pl.BlockSpec((tm, 128), lambda i: (i, 0)),
                   pl.BlockSpec((tm, 128), lambda i: (i, 0))],
        out_shape=[jax.ShapeDtypeStruct((t, d), BF16), jax.ShapeDtypeStruct((t, 128), F32),
                   jax.ShapeDtypeStruct((t, 128), jnp.int32)],
        compiler_params=_cparams(("parallel",)),
        name="moe_router",
    )(h2d, gain.reshape(1, d), wr)


def _moe_gather_kernel(rb_ref, lo_ref, hi_ref, first_ref, tgt_ref, x_ref, o_ref):
    e, j, slot = pl.program_id(0), pl.program_id(1), pl.program_id(2)
    rb = rb_ref[e, j, slot]
    lo, hi = lo_ref[e, j], hi_ref[e, j]

    @pl.when(first_ref[e, j, slot] == 1)
    def _():
        o_ref[...] = jnp.zeros_like(o_ref)

    active = (slot == 0) | (rb != rb_ref[e, j, 0])
    tgt = tgt_ref[0]
    for sub in range(MOE_TB // MOE_SUB):
        base = rb * MOE_TB + sub * MOE_SUB

        @pl.when(active & (lo < base + MOE_SUB) & (hi > base))
        def _():
            rows = base + lax.broadcasted_iota(jnp.int32, (MOE_SUB, MOE_TB), 0)
            onehot = jnp.where(tgt == rows, 1.0, 0.0).astype(BF16)
            part = jnp.dot(onehot, x_ref[...], preferred_element_type=F32)
            sl = slice(sub * MOE_SUB, (sub + 1) * MOE_SUB)
            o_ref[sl, :] = o_ref[sl, :] + part.astype(o_ref.dtype)


def _moe_ffn_kernel(exp_ref, nused_ref, x_ref, wg_ref, wv_ref, wd_ref, o_ref, acc_ref):
    r, f = pl.program_id(0), pl.program_id(1)
    used = r < nused_ref[0]

    @pl.when(f == 0)
    def _():
        acc_ref[...] = jnp.zeros_like(acc_ref)

    @pl.when(used)
    def _():
        x = x_ref[...]
        gate = jnp.dot(x, wg_ref[0], preferred_element_type=F32)
        val = jnp.dot(x, wv_ref[0], preferred_element_type=F32)
        act = (jax.nn.silu(gate) * val).astype(BF16)
        acc_ref[...] += jnp.dot(act, wd_ref[0], preferred_element_type=F32)

    @pl.when(f == pl.num_programs(1) - 1)
    def _():
        o_ref[...] = acc_ref[...].astype(o_ref.dtype)
    del exp_ref


def _moe_scatter_kernel(rb_ref, lo_ref, hi_ref, tgt_ref, gate_ref, y_ref, h_ref, o_ref):
    j, e, slot = pl.program_id(0), pl.program_id(1), pl.program_id(2)
    rb = rb_ref[e, j, slot]
    lo, hi = lo_ref[e, j], hi_ref[e, j]

    @pl.when((e == 0) & (slot == 0))
    def _():
        o_ref[...] = h_ref[...]

    active = (slot == 0) | (rb != rb_ref[e, j, 0])
    mine = lax.broadcasted_iota(jnp.int32, tgt_ref.shape, 1) == e
    tgt = jnp.sum(jnp.where(mine, tgt_ref[...], 0), axis=1, keepdims=True)
    gate = jnp.sum(jnp.where(mine, gate_ref[...], 0.0), axis=1, keepdims=True)
    for sub in range(MOE_TB // MOE_SUB):
        base = rb * MOE_TB + sub * MOE_SUB

        @pl.when(active & (lo < base + MOE_SUB) & (hi > base))
        def _():
            rows = base + lax.broadcasted_iota(jnp.int32, (MOE_TB, MOE_SUB), 1)
            onehot = jnp.where(tgt == rows, 1.0, 0.0).astype(BF16)
            part = jnp.dot(onehot, y_ref[sub * MOE_SUB:(sub + 1) * MOE_SUB, :], preferred_element_type=F32)
            o_ref[...] = o_ref[...] + gate * part


def _moe(h2d, gain, w_router, w_up, w_down):
    t, d = h2d.shape
    tb = MOE_TB
    nj = t // tb
    n_rb = (t * TOP_K) // tb + N_EXPERTS
    xn, gates, asg = _router(h2d, gain, w_router)
    asg = asg[:, :N_EXPERTS]
    gates = gates[:, :N_EXPERTS]
    cs = jnp.cumsum(asg, axis=0)
    rank = cs - asg
    counts = cs[-1]
    padded = (counts + tb - 1) // tb * tb
    pad_end = jnp.cumsum(padded)
    start_p = pad_end - padded
    tgt = jnp.where(asg == 1, start_p[None, :] + rank, -1).astype(jnp.int32)
    cb = jnp.concatenate([jnp.zeros((1, N_EXPERTS), jnp.int32), cs[tb - 1::tb]], axis=0)
    lo = (start_p[None, :] + cb[:-1]).T.astype(jnp.int32)
    hi = (start_p[None, :] + cb[1:]).T.astype(jnp.int32)
    rb0 = lo // tb
    rb1 = jnp.maximum(rb0, (hi - 1) // tb)
    rb = jnp.stack([rb0, rb1], axis=-1).astype(jnp.int32)
    flat = rb.reshape(-1)
    first = jnp.concatenate([jnp.ones((1,), jnp.int32), (flat[1:] != flat[:-1]).astype(jnp.int32)])
    first = first.reshape(N_EXPERTS, nj, 2)
    n_used = (pad_end[-1] // tb).astype(jnp.int32).reshape(1)
    blk_exp = jnp.minimum(jnp.searchsorted(pad_end, jnp.arange(n_rb) * tb, side='right'),
                          N_EXPERTS - 1).astype(jnp.int32)

    xs = pl.pallas_call(
        _moe_gather_kernel,
        grid_spec=pltpu.PrefetchScalarGridSpec(
            num_scalar_prefetch=4,
            grid=(N_EXPERTS, nj, 2),
            in_specs=[pl.BlockSpec((1, 1, tb), lambda e, j, s, *_: (e, 0, j)),
                      pl.BlockSpec((tb, d), lambda e, j, s, *_: (j, 0))],
            out_specs=pl.BlockSpec((tb, d), lambda e, j, s, rb_ref, *_: (rb_ref[e, j, s], 0))),
        out_shape=jax.ShapeDtypeStruct((n_rb * tb, d), BF16),
        compiler_params=_cparams(("arbitrary", "arbitrary", "arbitrary"), _VMEM_LIMIT),
        name="moe_gather",
    )(rb, lo, hi, first, tgt.T.reshape(N_EXPERTS, 1, t), xn)

    nf = D_FF // FF_CHUNK
    wub = w_up.astype(BF16)
    ys = pl.pallas_call(
        _moe_ffn_kernel,
        grid_spec=pltpu.PrefetchScalarGridSpec(
            num_scalar_prefetch=2,
            grid=(n_rb, nf),
            in_specs=[pl.BlockSpec((tb, d), lambda r, f, *_: (r, 0)),
                      pl.BlockSpec((1, d, FF_CHUNK), lambda r, f, ex, nu: (ex[r], 0, f)),
                      pl.BlockSpec((1, d, FF_CHUNK), lambda r, f, ex, nu: (ex[r], 0, f + nf)),
                      pl.BlockSpec((1, FF_CHUNK, d), lambda r, f, ex, nu: (ex[r], f, 0))],
            out_specs=pl.BlockSpec((tb, d), lambda r, f, *_: (r, 0)),
            scratch_shapes=[pltpu.VMEM((tb, d), F32)]),
        out_shape=jax.ShapeDtypeStruct((n_rb * tb, d), BF16),
        compiler_params=_cparams(("arbitrary", "arbitrary"), _VMEM_LIMIT),
        name="moe_expert_ffn",
    )(blk_exp, n_used, xs, wub, wub, w_down.astype(BF16))

    return pl.pallas_call(
        _moe_scatter_kernel,
        grid_spec=pltpu.PrefetchScalarGridSpec(
            num_scalar_prefetch=3,
            grid=(nj, N_EXPERTS, 2),
            in_specs=[pl.BlockSpec((tb, N_EXPERTS), lambda j, e, s, *_: (j, 0)),
                      pl.BlockSpec((tb, N_EXPERTS), lambda j, e, s, *_: (j, 0)),
                      pl.BlockSpec((tb, d), lambda j, e, s, rb_ref, *_: (rb_ref[e, j, s], 0)),
                      pl.BlockSpec((tb, d), lambda j, e, s, *_: (j, 0))],
            out_specs=pl.BlockSpec((tb, d), lambda j, e, s, *_: (j, 0))),
        out_shape=jax.ShapeDtypeStruct((t, d), F32),
        compiler_params=_cparams(("arbitrary", "arbitrary", "arbitrary"), _VMEM_LIMIT),
        name="moe_scatter",
    )(rb, lo, hi, tgt, gates, ys, h2d)


_COL_CONV, _COL_SSM, _COL_KC, _COL_VC, _COL_KS, _COL_KW = 0, 512, 768, 896, 1024, 1152
_ROW_Q, _ROW_VS, _ROW_VW, _ROW_G = 0, 512, 640, 768


def _split_w_in(w_in):
    kvw = NSA_KV_HEADS * HEAD_DIM
    cuts = np.cumsum([0, SSM_WIDTH, 2 * CONV_WIDTH, NSA_WIDTH] + [kvw] * 6 + [N_BRANCH * NSA_HEADS])
    seg = lambda i: w_in[:, cuts[i]:cuts[i + 1]]
    ssm, conv, q, k_c, v_c, k_s, v_s, k_w, v_w, gate = (seg(i) for i in range(10))
    w_tok = jnp.concatenate([conv, ssm, k_c, v_c, k_s, k_w], axis=1).astype(BF16)
    gate = jnp.concatenate([gate, jnp.zeros((w_in.shape[0], 8), F32)], axis=1)
    w_t = jnp.concatenate([q, v_s, v_w, gate], axis=1).T.astype(BF16)
    return w_tok, w_t


def _layer_mixers(h, p, t5_tiles):
    b, s, d = h.shape
    w_tok, w_t = _split_w_in(p['w_in'])
    ztok, zt = _proj(h, p['norm_mix'], w_tok, w_t)

    n_chunks = s // SSM_CHUNK
    u = ztok[:, :, _COL_SSM:_COL_SSM + SSM_WIDTH].astype(BF16)
    xg = u.reshape(b, n_chunks, SSM_CHUNK, SSM_GROUPS, SSM_GROUP).transpose(3, 1, 0, 2, 4)
    xg = xg.reshape(SSM_GROUPS, n_chunks * b, SSM_CHUNK * SSM_GROUP)
    tables = _s5_tables(p['ssm_lambda_re'], p['ssm_lambda_im'], p['ssm_log_dt'], p['ssm_b_re'], p['ssm_b_im'],
                        p['ssm_c_re'], p['ssm_c_im'], p['ssm_d'])
    gy = _s5(xg, tables, b)
    gy = gy.reshape(SSM_GROUPS, n_chunks, b, SSM_CHUNK, SSM_GROUP).transpose(2, 1, 3, 0, 4).reshape(b, s, SSM_WIDTH)

    g_out = p['mix_out_norm']
    yc = _conv(ztok, p['conv_w_dw'], p['conv_b_dw'], p['conv_ln_g'], p['conv_ln_b'], p['conv_w_pw'],
               g_out[SSM_WIDTH:SSM_WIDTH + CONV_WIDTH])

    kvw = NSA_KV_HEADS * HEAD_DIM
    n_cmp = s // CMP_STRIDE
    k_norm = p['nsa_k_norm']
    kc, vct = _compress(ztok, _COL_KC, _COL_VC, p['nsa_cmp_pe'], p['nsa_cmp_w1'], p['nsa_cmp_w2'], k_norm[0])
    ks, kw = _knorm(ztok, _COL_KS, _COL_KW, k_norm[1], k_norm[2])
    qg = jnp.broadcast_to((p['nsa_q_norm'] * (HEAD_DIM ** -0.5 * LOG2E))[:, None], (HEAD_DIM, TQ))
    bias_c, bias_s, bias_w = t5_tiles
    n_sel = s // L_SEL
    cs_ = np.arange(n_cmp) * CMP_STRIDE
    ss_ = np.arange(n_sel) * L_SEL
    ov = np.maximum(np.minimum(cs_[:, None] + L_CMP, ss_[None, :] + L_SEL) - np.maximum(cs_[:, None], ss_[None, :]), 0)
    ovt = jnp.asarray((ov.astype(np.float32) / L_CMP).T, BF16)
    oc_t, sel = _nsa_cmp(zt, qg, kc, vct, bias_c, ovt)

    front = lambda x, n, axis: jnp.pad(x, [(n, 0) if a == axis else (0, 0) for a in range(x.ndim)])
    ones_rows = jnp.concatenate([jnp.ones((b, NSA_KV_HEADS, 1, s), BF16),
                                 jnp.zeros((b, NSA_KV_HEADS, 15, s), BF16)], axis=2)
    heads_t = lambda rows: jnp.concatenate(
        [zt[:, rows:rows + kvw, :].astype(BF16).reshape(b, NSA_KV_HEADS, HEAD_DIM, s), ones_rows], axis=2)
    ks_p = front(ks, SEL_PAD, 2)
    kw_p = front(kw, WINDOW, 2)
    vst_p = front(heads_t(_ROW_VS), SEL_PAD, 3)
    vwt_p = front(heads_t(_ROW_VW), WINDOW, 3)
    sel_p = jnp.pad(sel, ((0, 0), (0, 0), (SEL_PAD // L_SEL, 0), (0, 0)), constant_values=NEG)
    gl = zt[:, _ROW_G:_ROW_G + N_BRANCH * NSA_HEADS, :].reshape(b, NSA_KV_HEADS, NSA_GROUP * N_BRANCH, s)
    gates_t = jnp.pad(gl, ((0, 0), (0, 0), (0, 16 - NSA_GROUP * N_BRANCH), (0, 0)))
    yn_t = _nsa_main(zt, qg, ks_p, vst_p, kw_p, vwt_p, sel_p, bias_s, bias_w, gates_t, oc_t)
    yn = yn_t.transpose(0, 2, 1)

    return _mixout(gy, yc, yn, h, p['ssm_w_glu'], g_out, p['w_out'])


def kernel(x, mem, norm_mix, w_in, ssm_lambda_re, ssm_lambda_im, ssm_log_dt, ssm_b_re, ssm_b_im, ssm_c_re, ssm_c_im, ssm_d, ssm_w_glu, conv_w_dw, conv_b_dw, conv_ln_g, conv_ln_b, conv_w_pw, nsa_q_norm, nsa_k_norm, nsa_cmp_pe, nsa_cmp_w1, nsa_cmp_w2, mix_out_norm, w_out, t5_table, norm_cross, norm_mem, x_w_q, x_w_kv, x_q_norm, x_k_norm, x_w_o, norm_ffn, ffn_w_up, ffn_w_down, moe_router, moe_w_up, moe_w_down):
    b, s, d = x.shape
    depth = w_in.shape[0]
    per_layer = dict(norm_mix=norm_mix, w_in=w_in, ssm_lambda_re=ssm_lambda_re, ssm_lambda_im=ssm_lambda_im,
                     ssm_log_dt=ssm_log_dt, ssm_b_re=ssm_b_re, ssm_b_im=ssm_b_im, ssm_c_re=ssm_c_re,
                     ssm_c_im=ssm_c_im, ssm_d=ssm_d, ssm_w_glu=ssm_w_glu, conv_w_dw=conv_w_dw,
                     conv_b_dw=conv_b_dw, conv_ln_g=conv_ln_g, conv_ln_b=conv_ln_b, conv_w_pw=conv_w_pw,
                     nsa_q_norm=nsa_q_norm, nsa_k_norm=nsa_k_norm, nsa_cmp_pe=nsa_cmp_pe, nsa_cmp_w1=nsa_cmp_w1,
                     nsa_cmp_w2=nsa_cmp_w2, mix_out_norm=mix_out_norm, w_out=w_out)
    t5_tiles = _nsa_bias_tiles(t5_table, s)
    h = x
    for layer in range(depth):
        p = {k: v[layer] for k, v in per_layer.items()}
        h = _layer_mixers(h, p, t5_tiles)
        mk, mv = _memkv(mem, norm_mem[layer], x_w_kv[layer], x_k_norm[layer])
        h = _cross(h, norm_cross[layer], x_w_q[layer], x_q_norm[layer], mk, mv, x_w_o[layer])
        h2d = h.reshape(b * s, d)
        if layer % 2 == 0:
            h2d = _ffn(h2d, norm_ffn[layer], ffn_w_up[layer // 2], ffn_w_down[layer // 2])
        else:
            h2d = _moe(h2d, norm_ffn[layer], moe_router[layer // 2], moe_w_up[layer // 2], moe_w_down[layer // 2])
        h = h2d.reshape(b, s, d)
    return h
```

```python
import functools
import math

import jax
import jax.numpy as jnp
import numpy as np
from jax import lax
from jax.experimental import pallas as pl
from jax.experimental.pallas import tpu as pltpu

F32 = jnp.float32
BF16 = jnp.bfloat16

D_MODEL = 1024
HEAD_DIM = 64
SSM_WIDTH = 256
SSM_GROUP = 16
SSM_GROUPS = 16
SSM_STATE = 64
SSM_CHUNK = 16
CONV_WIDTH = 256
CONV_K = 31
CONV_HALO = 32
NSA_WIDTH = 512
NSA_HEADS = 8
NSA_KV_HEADS = 2
NSA_GROUP = 4
N_BRANCH = 3
L_CMP = 32
CMP_STRIDE = 16
L_SEL = 64
N_SELECT = 16
N_LOCAL = 2
WINDOW = 512
SEL_FORCE = 1e6
T5_BUCKETS = 32
T5_MAX_DIST = 128
X_HEADS = 4
X_WIDTH = 256
D_FF = 2816
N_EXPERTS = 8
TOP_K = 2
EPS = 1e-6
NEG = -1e30
LOG2E = math.log2(math.e)

TQ = 256
SEL_PAD = 256
FF_CHUNK = 256
MOE_TB = 1024
MOE_SUB = 128
MOE_WIN = 512

_VMEM_LIMIT = 56 * 1024 * 1024


def _cparams(sem, vmem=None):
    return pltpu.CompilerParams(dimension_semantics=sem, vmem_limit_bytes=vmem)


def _rms_rows(x):
    return x * lax.rsqrt(jnp.mean(x * x, axis=-1, keepdims=True) + EPS)


def _proj_kernel(x_ref, g_ref, wtok_ref, wt_ref, ztok_ref, zt_ref):
    xn = (_rms_rows(x_ref[0]) * g_ref[...]).astype(BF16)
    ztok_ref[0] = jnp.dot(xn, wtok_ref[...], preferred_element_type=F32)
    zt_ref[0] = lax.dot_general(wt_ref[...], xn, (((1,), (1,)), ((), ())), preferred_element_type=F32)


def _proj(h, gain, w_tok, w_t, tm=512):
    b, s, d = h.shape
    ntok, nt = w_tok.shape[1], w_t.shape[0]
    return pl.pallas_call(
        _proj_kernel,
        grid=(b, s // tm),
        in_specs=[pl.BlockSpec((1, tm, d), lambda i, j: (i, j, 0)),
                  pl.BlockSpec((1, d), lambda i, j: (0, 0)),
                  pl.BlockSpec((d, ntok), lambda i, j: (0, 0)),
                  pl.BlockSpec((nt, d), lambda i, j: (0, 0))],
        out_specs=[pl.BlockSpec((1, tm, ntok), lambda i, j: (i, j, 0)),
                   pl.BlockSpec((1, nt, tm), lambda i, j: (i, 0, j))],
        out_shape=[jax.ShapeDtypeStruct((b, s, ntok), F32), jax.ShapeDtypeStruct((b, nt, s), F32)],
        compiler_params=_cparams(("parallel", "parallel"), _VMEM_LIMIT),
        name="proj",
    )(h, gain.reshape(1, d), w_tok, w_t)


def _s5_tables(lam_re, lam_im, log_dt, b_re, b_im, c_re, c_im, d_skip):
    L, H, P = SSM_CHUNK, SSM_GROUP, SSM_STATE
    dt = jnp.exp(log_dt.astype(F32))[:, None]
    lr, li = lam_re.astype(F32), lam_im.astype(F32)
    mag = jnp.exp(lr * dt)
    ar, ai = mag * jnp.cos(li * dt), mag * jnp.sin(li * dt)
    den = lr * lr + li * li
    fr = ((ar - 1.0) * lr + ai * li) / den
    fi = (ai * lr - (ar - 1.0) * li) / den
    bbr = fr[..., None] * b_re - fi[..., None] * b_im
    bbi = fr[..., None] * b_im + fi[..., None] * b_re
    j = jnp.arange(L + 1, dtype=F32)[:, None, None]
    pmag = jnp.exp(lr[None] * dt[None] * j)
    pr, pi = pmag * jnp.cos(li[None] * dt[None] * j), pmag * jnp.sin(li[None] * dt[None] * j)
    cbr = c_re[:, :, :, None] * bbr[:, None, :, :] - c_im[:, :, :, None] * bbi[:, None, :, :]
    cbi = c_re[:, :, :, None] * bbi[:, None, :, :] + c_im[:, :, :, None] * bbr[:, None, :, :]
    hp = lax.Precision.HIGHEST
    kj = (jnp.einsum('jgp,ghpk->jghk', pr[:L], cbr, precision=hp)
          - jnp.einsum('jgp,ghpk->jghk', pi[:L], cbi, precision=hp))
    lag = np.arange(L)[None, :] - np.arange(L)[:, None]
    place = (lag[None] == np.arange(L)[:, None, None]).astype(np.float32)
    kt = jnp.einsum('jab,jghk->abghk', place, kj, precision=hp)
    kt = kt + (jnp.eye(L)[:, :, None, None, None] * (jnp.eye(H)[None, None, None] * d_skip[None, None, :, :, None]))
    tmat = kt.transpose(2, 0, 4, 1, 3).reshape(SSM_GROUPS, L * H, L * H)
    qr, qi = pr[:L][::-1], pi[:L][::-1]
    wre = qr[..., None] * bbr[None] - qi[..., None] * bbi[None]
    wim = qr[..., None] * bbi[None] + qi[..., None] * bbr[None]
    wre = wre.transpose(1, 0, 3, 2).reshape(SSM_GROUPS, L * H, P)
    wim = wim.transpose(1, 0, 3, 2).reshape(SSM_GROUPS, L * H, P)
    w1 = jnp.concatenate([wre, wim], axis=-1)
    w2 = jnp.concatenate([wim, wre], axis=-1)
    sr, si = pr[1:], pi[1:]
    vr = c_re[None] * sr[:, :, None, :] - c_im[None] * si[:, :, None, :]
    vi = c_re[None] * si[:, :, None, :] + c_im[None] * sr[:, :, None, :]
    vmat = jnp.concatenate([vr, -vi], axis=-1).transpose(1, 3, 0, 2).reshape(SSM_GROUPS, 2 * P, L * H)
    a_r, a_i = pr[L], pi[L]
    am = jnp.stack([jnp.concatenate([a_r, a_r], -1), jnp.concatenate([-a_i, a_i], -1),
                    jnp.concatenate([a_i, -a_i], -1)], axis=1)
    am = jnp.concatenate([am, jnp.zeros((SSM_GROUPS, 5, 2 * P), F32)], axis=1)
    return tmat.astype(BF16), w1.astype(BF16), w2.astype(BF16), vmat.astype(BF16), am


def _s5_kernel(x_ref, t_ref, w1_ref, w2_ref, v_ref, a_ref, o_ref, s1_ref, s2_ref, xin_ref, *, bsz, n_chunks):
    x = x_ref[0]
    s1_ref[...] = jnp.dot(x, w1_ref[0], preferred_element_type=F32)
    s2_ref[...] = jnp.dot(x, w2_ref[0], preferred_element_type=F32)
    a1, a2, a3 = a_ref[0, 0:1, :], a_ref[0, 1:2, :], a_ref[0, 2:3, :]

    def step(c, carry):
        p, q = carry
        rows = pl.ds(pl.multiple_of(c * bsz, bsz), bsz)
        xin_ref[rows, :] = p
        return (p * a1 + q * a2 + s1_ref[rows, :], q * a1 + p * a3 + s2_ref[rows, :])

    zero = jnp.zeros((bsz, 2 * SSM_STATE), F32)
    lax.fori_loop(0, n_chunks, step, (zero, zero))
    y = (jnp.dot(x, t_ref[0], preferred_element_type=F32)
         + jnp.dot(xin_ref[...].astype(BF16), v_ref[0], preferred_element_type=F32))
    o_ref[0] = jax.nn.gelu(y).astype(o_ref.dtype)


def _s5(xg, tables, bsz):
    tmat, w1, w2, vmat, am = tables
    g, r, lh = xg.shape
    p2 = 2 * SSM_STATE
    kern = functools.partial(_s5_kernel, bsz=bsz, n_chunks=r // bsz)
    return pl.pallas_call(
        kern,
        grid=(g,),
        in_specs=[pl.BlockSpec((1, r, lh), lambda i: (i, 0, 0)),
                  pl.BlockSpec((1, lh, lh), lambda i: (i, 0, 0)),
                  pl.BlockSpec((1, lh, p2), lambda i: (i, 0, 0)),
                  pl.BlockSpec((1, lh, p2), lambda i: (i, 0, 0)),
                  pl.BlockSpec((1, p2, lh), lambda i: (i, 0, 0)),
                  pl.BlockSpec((1, 8, p2), lambda i: (i, 0, 0))],
        out_specs=pl.BlockSpec((1, r, lh), lambda i: (i, 0, 0)),
        out_shape=jax.ShapeDtypeStruct((g, r, lh), BF16),
        scratch_shapes=[pltpu.VMEM((r, p2), F32), pltpu.VMEM((r, p2), F32), pltpu.VMEM((r, p2), F32)],
        compiler_params=_cparams(("parallel",), _VMEM_LIMIT),
        name="s5_scan",
    )(xg, tmat, w1, w2, vmat, am)


def _conv_kernel(z_ref, halo_ref, wdw_ref, bdw_ref, lng_ref, lnb_ref, wpw_ref, go_ref, o_ref, buf_ref, *, tt):
    first = pl.program_id(1) == 0
    zc = z_ref[0]
    zh = halo_ref[0]
    vh = zh[:, :CONV_WIDTH] * jax.nn.sigmoid(zh[:, CONV_WIDTH:])
    buf_ref[0:CONV_HALO, :] = vh * jnp.where(first, 0.0, 1.0)
    buf_ref[CONV_HALO:CONV_HALO + tt, :] = zc[:, :CONV_WIDTH] * jax.nn.sigmoid(zc[:, CONV_WIDTH:])
    acc = jnp.zeros((tt, CONV_WIDTH), F32) + bdw_ref[...]
    for k in range(CONV_K):
        acc = acc + wdw_ref[k:k + 1, :] * buf_ref[pl.ds(CONV_HALO - (CONV_K - 1) + k, tt), :]
    mu = jnp.mean(acc, axis=-1, keepdims=True)
    var = jnp.mean(jnp.square(acc - mu), axis=-1, keepdims=True)
    y = (acc - mu) * lax.rsqrt(var + EPS) * lng_ref[...] + lnb_ref[...]
    y = jax.nn.silu(y)
    y = jnp.dot(y.astype(BF16), wpw_ref[...], preferred_element_type=F32)
    o_ref[0] = (_rms_rows(y) * go_ref[...]).astype(o_ref.dtype)


def _conv(ztok, w_dw, b_dw, ln_g, ln_b, w_pw, g_out, tt=512):
    b, s, _ = ztok.shape
    cw = CONV_WIDTH
    hb = tt // CONV_HALO
    kern = functools.partial(_conv_kernel, tt=tt)
    row = lambda v: v.reshape(1, cw)
    return pl.pallas_call(
        kern,
        grid=(b, s // tt),
        in_specs=[pl.BlockSpec((1, tt, 2 * cw), lambda i, j: (i, j, 0)),
                  pl.BlockSpec((1, CONV_HALO, 2 * cw), lambda i, j: (i, jnp.maximum(j * hb - 1, 0), 0)),
                  pl.BlockSpec((CONV_K + 1, cw), lambda i, j: (0, 0)),
                  pl.BlockSpec((1, cw), lambda i, j: (0, 0)),
                  pl.BlockSpec((1, cw), lambda i, j: (0, 0)),
                  pl.BlockSpec((1, cw), lambda i, j: (0, 0)),
                  pl.BlockSpec((cw, cw), lambda i, j: (0, 0)),
                  pl.BlockSpec((1, cw), lambda i, j: (0, 0))],
        out_specs=pl.BlockSpec((1, tt, cw), lambda i, j: (i, j, 0)),
        out_shape=jax.ShapeDtypeStruct((b, s, cw), BF16),
        scratch_shapes=[pltpu.VMEM((CONV_HALO + tt, cw), F32)],
        compiler_params=_cparams(("parallel", "arbitrary")),
        name="conv_mixer",
    )(ztok, ztok, jnp.concatenate([w_dw, jnp.zeros((1, cw), F32)], 0), row(b_dw), row(ln_g), row(ln_b),
      w_pw.astype(BF16), row(g_out))


def _knorm_kernel(ks_ref, kw_ref, gs_ref, gw_ref, os_ref, ow_ref):
    for src, g_ref, dst in ((ks_ref, gs_ref, os_ref), (kw_ref, gw_ref, ow_ref)):
        x = src[0]
        for h in range(NSA_KV_HEADS):
            xh = x[:, h * HEAD_DIM:(h + 1) * HEAD_DIM]
            dst[0, h] = (_rms_rows(xh) * g_ref[...]).astype(dst.dtype)


def _knorm(ztok, col_s, col_w, gain_s, gain_w, tt=512):
    b, s, _ = ztok.shape
    kw = NSA_KV_HEADS * HEAD_DIM
    out = jax.ShapeDtypeStruct((b, NSA_KV_HEADS, s, HEAD_DIM), BF16)
    ospec = pl.BlockSpec((1, NSA_KV_HEADS, tt, HEAD_DIM), lambda i, j: (i, 0, j, 0))
    return pl.pallas_call(
        _knorm_kernel,
        grid=(b, s // tt),
        in_specs=[pl.BlockSpec((1, tt, kw), lambda i, j: (i, j, col_s // kw)),
                  pl.BlockSpec((1, tt, kw), lambda i, j: (i, j, col_w // kw)),
                  pl.BlockSpec((1, HEAD_DIM), lambda i, j: (0, 0)),
                  pl.BlockSpec((1, HEAD_DIM), lambda i, j: (0, 0))],
        out_specs=[ospec, ospec],
        out_shape=[out, out],
        compiler_params=_cparams(("parallel", "parallel")),
        name="nsa_key_norm",
    )(ztok, ztok, gain_s.reshape(1, HEAD_DIM), gain_w.reshape(1, HEAD_DIM))


def _compress_kernel(k_ref, v_ref, wka_ref, wkb_ref, ck_ref, w2k_ref, gk_ref,
                     wva_ref, wvb_ref, cv_ref, w2v_ref, ko_ref, vo_ref):
    hi = lax.Precision.HIGHEST
    n = k_ref.shape[1] // CMP_STRIDE
    kvw = k_ref.shape[2]
    nt = (((1,), (1,)), ((), ()))
    a, bm = jnp.zeros((n, kvw), F32), jnp.zeros((n, kvw), F32)
    at, bt = jnp.zeros((kvw, n), F32), jnp.zeros((kvw, n), F32)
    for l in range(CMP_STRIDE):
        kl = k_ref[0, pl.ds(l, n, stride=CMP_STRIDE), :]
        vl = v_ref[0, pl.ds(l, n, stride=CMP_STRIDE), :]
        a = a + jnp.dot(kl, wka_ref[l], precision=hi, preferred_element_type=F32)
        bm = bm + jnp.dot(kl, wkb_ref[l], precision=hi, preferred_element_type=F32)
        at = at + lax.dot_general(wva_ref[l], vl, nt, precision=hi, preferred_element_type=F32)
        bt = bt + lax.dot_general(wvb_ref[l], vl, nt, precision=hi, preferred_element_type=F32)
    pre = a + pltpu.roll(bm, n - 1, 0) + ck_ref[...]
    kc = jnp.dot(jax.nn.gelu(pre), w2k_ref[...], precision=hi, preferred_element_type=F32)
    for h in range(NSA_KV_HEADS):
        kh = kc[:, h * HEAD_DIM:(h + 1) * HEAD_DIM]
        ko_ref[0, h] = (_rms_rows(kh) * gk_ref[...]).astype(ko_ref.dtype)
    pre_t = at + pltpu.roll(bt, n - 1, 1) + cv_ref[...]
    vt = jnp.dot(w2v_ref[...], jax.nn.gelu(pre_t), precision=hi, preferred_element_type=F32)
    for h in range(NSA_KV_HEADS):
        vo_ref[0, h] = vt[h * HEAD_DIM:(h + 1) * HEAD_DIM, :].astype(vo_ref.dtype)


def _blockdiag2(w):
    z = jnp.zeros_like(w)
    return jnp.concatenate([jnp.concatenate([w, z], 1), jnp.concatenate([z, w], 1)], 0)


def _compress(ztok, col_k, col_v, pe, w1, w2, k_gain):
    b, s, _ = ztok.shape
    n = s // CMP_STRIDE
    hd, kvw = HEAD_DIM, NSA_KV_HEADS * HEAD_DIM
    hp = lax.Precision.HIGHEST

    def expand(w):
        wl = w.reshape(L_CMP, hd, hd)
        e = wl[:, None, :, None, :] * jnp.eye(NSA_KV_HEADS, dtype=F32)[None, :, None, :, None]
        e = e.reshape(L_CMP, kvw, kvw)
        return e[:CMP_STRIDE], e[CMP_STRIDE:]

    wka, wkb = expand(w1[0])
    wva, wvb = expand(w1[1])
    ck = jnp.tile(jnp.dot(pe[0].reshape(1, L_CMP * hd), w1[0], precision=hp), (1, NSA_KV_HEADS))
    cv = jnp.tile(jnp.dot(pe[1].reshape(1, L_CMP * hd), w1[1], precision=hp), (1, NSA_KV_HEADS)).T
    full = lambda shape: pl.BlockSpec(shape, lambda i: tuple(0 for _ in shape))
    return pl.pallas_call(
        _compress_kernel,
        grid=(b,),
        in_specs=[pl.BlockSpec((1, s, kvw), lambda i: (i, 0, col_k // kvw)),
                  pl.BlockSpec((1, s, kvw), lambda i: (i, 0, col_v // kvw)),
                  full((CMP_STRIDE, kvw, kvw)), full((CMP_STRIDE, kvw, kvw)), full((1, kvw)), full((kvw, kvw)),
                  full((1, hd)),
                  full((CMP_STRIDE, kvw, kvw)), full((CMP_STRIDE, kvw, kvw)), full((kvw, 1)), full((kvw, kvw))],
        out_specs=[pl.BlockSpec((1, NSA_KV_HEADS, n, hd), lambda i: (i, 0, 0, 0)),
                   pl.BlockSpec((1, NSA_KV_HEADS, hd, n), lambda i: (i, 0, 0, 0))],
        out_shape=[jax.ShapeDtypeStruct((b, NSA_KV_HEADS, n, hd), BF16),
                   jax.ShapeDtypeStruct((b, NSA_KV_HEADS, hd, n), BF16)],
        compiler_params=_cparams(("parallel",), _VMEM_LIMIT),
        name="nsa_compress",
    )(ztok, ztok, wka, wkb, ck, _blockdiag2(w2[0]), k_gain.reshape(1, hd),
      wva.transpose(0, 2, 1), wvb.transpose(0, 2, 1), cv, _blockdiag2(w2[1]).T)


def _t5_bias_by_dist(t5_table):
    n = np.arange(T5_MAX_DIST + 1)
    max_exact = T5_BUCKETS // 2
    nf = np.maximum(n, 1).astype(np.float32)
    large = max_exact + (np.log(nf / np.float32(max_exact)) / np.float32(math.log(T5_MAX_DIST / max_exact))
                         * np.float32(T5_BUCKETS - max_exact)).astype(np.int32)
    large = np.minimum(large, T5_BUCKETS - 1)
    bucket = np.where(n < max_exact, n, large)
    onehot = (bucket[:, None] == np.arange(T5_BUCKETS)[None, :]).astype(np.float32)
    return jnp.dot(onehot, t5_table, precision=lax.Precision.HIGHEST)


def _bias_tile(fdt, rows, stride, dist00, d_max=None):
    heads = fdt.shape[0]
    a0 = stride * (rows - 1)
    d_lo = dist00 - a0
    length = a0 + TQ
    d_hi = d_lo + length
    d_max = d_hi if d_max is None else d_max
    pieces = []
    for lo, hi, kind in ((d_lo, min(d_hi, 0), 'neg'), (max(d_lo, 0), min(d_hi, T5_MAX_DIST), 'tab'),
                         (max(d_lo, T5_MAX_DIST), min(d_hi, d_max), 'far'), (max(d_lo, d_max), d_hi, 'neg')):
        if hi > lo:
            pieces.append(fdt[:, lo:hi] if kind == 'tab'
                          else jnp.full((heads, hi - lo), NEG if kind == 'neg' else 0.0, F32))
    vec = jnp.concatenate(pieces, axis=1)
    if stride != 1 or rows % 128:
        return jnp.stack([vec[:, a0 - stride * r:a0 - stride * r + TQ] for r in range(rows)], axis=1)
    n_a = rows // 128
    width = 128 * (n_a - 1) + TQ
    base = jnp.stack([vec[:, 127 - b:127 - b + width] for b in range(128)], axis=1)
    return jnp.concatenate([base[:, :, 128 * (n_a - 1 - a):128 * (n_a - 1 - a) + TQ] for a in range(n_a)], axis=1)


def _nsa_bias_tiles(t5_table, seq):
    fd = _t5_bias_by_dist(t5_table).astype(F32)
    fdt = ((fd - fd[T5_MAX_DIST:]) * LOG2E).T
    n_cmp = seq // CMP_STRIDE
    qt = TQ // CMP_STRIDE
    r0 = n_cmp - qt
    band = _bias_tile(fdt, 2 * qt, CMP_STRIDE, CMP_STRIDE * qt - (L_CMP - 1))
    heads = fdt.shape[0]
    cmp_t = jnp.concatenate([jnp.zeros((heads, r0 - qt, TQ), F32), band,
                             jnp.full((heads, n_cmp - qt, TQ), NEG, F32)], axis=1)
    sel_t = _bias_tile(fdt, SEL_PAD + TQ, 1, SEL_PAD)
    win = _bias_tile(fdt, WINDOW + TQ, 1, WINDOW, d_max=WINDOW)
    rw = np.arange(WINDOW + TQ)[None, :, None]
    win_t = jnp.stack([jnp.where(rw >= WINDOW - q0, win, NEG) for q0 in (0, TQ, 2 * TQ)])
    split = lambda t: t.reshape(*t.shape[:-3], NSA_KV_HEADS, NSA_GROUP, *t.shape[-2:])

    def wide(t):
        t = jnp.swapaxes(split(t), -3, -2)
        return t.reshape(*t.shape[:-2], NSA_GROUP * TQ)

    return wide(cmp_t), wide(sel_t), wide(win_t)


def _q_head(qt_ref, g, qg_ref):
    q = qt_ref[0, g * HEAD_DIM:(g + 1) * HEAD_DIM, :]
    inv = lax.rsqrt(jnp.mean(q * q, axis=0, keepdims=True) + EPS)
    return (q * inv * qg_ref[...]).astype(BF16)


def _nsa_cmp_kernel(qt_ref, qg_ref, kc_ref, vct_ref, bias_ref, ov_ref, oc_ref, sel_ref, *, n_cmp, n_sel):
    qi = pl.program_id(2)
    kc = kc_ref[0, 0]
    vct = vct_ref[0, 0]
    row0 = pl.multiple_of((n_cmp - TQ // CMP_STRIDE) - qi * (TQ // CMP_STRIDE), TQ // CMP_STRIDE)
    qw = jnp.concatenate([_q_head(qt_ref, g, qg_ref) for g in range(NSA_GROUP)], axis=1)
    s = jnp.dot(kc, qw, preferred_element_type=F32) + bias_ref[0, pl.ds(row0, n_cmp), :]
    m = jnp.max(s, axis=0, keepdims=True)
    m = jnp.where(m < 0.5 * NEG, 0.0, m)
    p = jnp.exp2(s - m)
    p = p * (1.0 / jnp.maximum(jnp.sum(p, axis=0, keepdims=True), 1e-30))
    oc = jnp.dot(vct, p.astype(BF16), preferred_element_type=F32)
    psum = jnp.zeros((n_cmp, TQ), F32)
    for g in range(NSA_GROUP):
        oc_ref[0, g * HEAD_DIM:(g + 1) * HEAD_DIM, :] = oc[:, g * TQ:(g + 1) * TQ]
        psum = psum + p[:, g * TQ:(g + 1) * TQ]
    hi = psum.astype(BF16)
    lo = (psum - hi.astype(F32)).astype(BF16)
    imp = (jnp.dot(ov_ref[...], hi, preferred_element_type=F32)
           + jnp.dot(ov_ref[...], lo, preferred_element_type=F32))
    blk = lax.broadcasted_iota(jnp.int32, (n_sel, TQ), 0)
    blk_t = lax.shift_right_logical(qi * TQ + lax.broadcasted_iota(jnp.int32, (n_sel, TQ), 1), L_SEL.bit_length() - 1)
    forced = (blk == 0) | (blk > blk_t - N_LOCAL)
    v0 = jnp.where(blk > blk_t, -jnp.inf, jnp.where(forced, SEL_FORCE, imp))

    def pick(_, v):
        m = jnp.max(v, axis=0, keepdims=True)
        cand = (v == m) & (m > -jnp.inf)
        first = jnp.min(jnp.where(cand, blk, n_sel), axis=0, keepdims=True)
        return jnp.where(blk == first, -jnp.inf, v)

    v = lax.fori_loop(0, min(N_SELECT, n_sel), pick, v0)
    sel_ref[0, 0] = jnp.where((v == -jnp.inf) & (v0 > -jnp.inf), 0.0, NEG)


def _nsa_cmp(zt, qg, kc, vct, bias_c, ovt):
    b, _, s = zt.shape
    n_cmp, n_sel = s // CMP_STRIDE, s // L_SEL
    gw = NSA_GROUP * HEAD_DIM
    kern = functools.partial(_nsa_cmp_kernel, n_cmp=n_cmp, n_sel=n_sel)
    return pl.pallas_call(
        kern,
        grid=(b, NSA_KV_HEADS, s // TQ),
        in_specs=[pl.BlockSpec((1, gw, TQ), lambda i, k, j: (i, k, j)),
                  pl.BlockSpec((HEAD_DIM, TQ), lambda i, k, j: (0, 0)),
                  pl.BlockSpec((1, 1, n_cmp, HEAD_DIM), lambda i, k, j: (i, k, 0, 0)),
                  pl.BlockSpec((1, 1, HEAD_DIM, n_cmp), lambda i, k, j: (i, k, 0, 0)),
                  pl.BlockSpec((1, bias_c.shape[1], NSA_GROUP * TQ), lambda i, k, j: (k, 0, 0)),
                  pl.BlockSpec((n_sel, n_cmp), lambda i, k, j: (0, 0))],
        out_specs=[pl.BlockSpec((1, gw, TQ), lambda i, k, j: (i, k, j)),
                   pl.BlockSpec((1, 1, n_sel, TQ), lambda i, k, j: (i, k, 0, j))],
        out_shape=[jax.ShapeDtypeStruct((b, NSA_WIDTH, s), F32),
                   jax.ShapeDtypeStruct((b, NSA_KV_HEADS, n_sel, s), F32)],
        compiler_params=_cparams(("parallel", "parallel", "parallel"), _VMEM_LIMIT),
        name="nsa_compressed_select",
    )(zt, qg, kc, vct, bias_c, ovt)


def _nsa_main_kernel(qt_ref, qg_ref, ks_ref, vst_ref, kw_ref, vwt_ref, sel_ref, bs_ref, bw_ref, gate_ref,
                     oc_ref, o_ref, acc_ref, s_ref):
    qi = pl.program_id(1)
    q0 = pl.multiple_of(qi * TQ, TQ)
    near = SEL_PAD + TQ
    gw = NSA_GROUP * HEAD_DIM
    kvs = range(NSA_KV_HEADS)

    def expand_sel(kv, first_blk, n_blk):
        rows = [jnp.broadcast_to(sel_ref[0, kv, pl.ds(first_blk + r, 1), :], (L_SEL, TQ)) for r in range(n_blk)]
        rows = jnp.concatenate(rows, axis=0)
        return jnp.concatenate([rows] * NSA_GROUP, axis=1)

    qw = [jnp.concatenate([_q_head(qt_ref, kv * NSA_GROUP + g, qg_ref) for g in range(NSA_GROUP)], axis=1)
          for kv in kvs]
    m0 = []
    for kv in kvs:
        s = (jnp.dot(ks_ref[0, kv, pl.ds(q0, near), :], qw[kv], preferred_element_type=F32) + bs_ref[kv]
             + expand_sel(kv, qi * (TQ // L_SEL), near // L_SEL))
        m = jnp.max(s, axis=0, keepdims=True)
        p = jnp.exp2(s - m).astype(BF16)
        acc_ref[kv] = jnp.dot(vst_ref[0, kv, :, pl.ds(q0, near)], p, preferred_element_type=F32)
        m0.append(m)

    def scores(c, slot):
        r0 = pl.multiple_of(c * TQ, TQ)
        mc = []
        for kv in kvs:
            s = (jnp.dot(ks_ref[0, kv, pl.ds(r0, TQ), :], qw[kv], preferred_element_type=F32)
                 + expand_sel(kv, c * (TQ // L_SEL), TQ // L_SEL))
            s_ref[slot, kv] = s
            mc.append(jnp.max(s, axis=0, keepdims=True))
        return tuple(mc)

    def consume(c, slot, m_old, mc):
        r0 = pl.multiple_of(c * TQ, TQ)
        m_out = []
        for kv in kvs:
            m_new = jnp.maximum(m_old[kv], mc[kv])
            alpha = jnp.exp2(m_old[kv] - m_new)
            p = jnp.exp2((s_ref[slot, kv] - m_new).astype(BF16))
            acc_ref[kv] = alpha * acc_ref[kv] + jnp.dot(vst_ref[0, kv, :, pl.ds(r0, TQ)], p,
                                                        preferred_element_type=F32)
            m_out.append(m_new)
        return tuple(m_out)

    first = SEL_PAD // TQ
    n_far = qi - first

    def pair(i, carry):
        m, mc = carry
        c = first + 2 * i
        mc1 = scores(c + 1, 1)
        m = consume(c, 0, m, mc)
        mc2 = scores(c + 2, 0)
        m = consume(c + 1, 1, m, mc1)
        return m, mc2

    m_far, mc_far = lax.fori_loop(0, n_far // 2, pair, (tuple(m0), scores(first, 0)))

    @pl.when((n_far > 0) & (n_far % 2 == 1))
    def _():
        consume(first + n_far - 1, 0, m_far, mc_far)

    for kv in kvs:
        s = jnp.dot(kw_ref[0, kv, pl.ds(q0, WINDOW + TQ), :], qw[kv], preferred_element_type=F32) + bw_ref[0, kv]
        p = jnp.exp2(s - jnp.max(s, axis=0, keepdims=True)).astype(BF16)
        ow = jnp.dot(vwt_ref[0, kv, :, pl.ds(q0, WINDOW + TQ)], p, preferred_element_type=F32)
        ow = ow[:HEAD_DIM] * (1.0 / jnp.maximum(ow[HEAD_DIM:HEAD_DIM + 1], 1e-30))
        os = acc_ref[kv]
        os = os[:HEAD_DIM] * (1.0 / jnp.maximum(os[HEAD_DIM:HEAD_DIM + 1], 1e-30))
        for g in range(NSA_GROUP):
            gates = jax.nn.sigmoid(gate_ref[0, kv, g * N_BRANCH:(g + 1) * N_BRANCH, :])
            rows = slice(kv * gw + g * HEAD_DIM, kv * gw + (g + 1) * HEAD_DIM)
            cols = slice(g * TQ, (g + 1) * TQ)
            o_ref[0, rows, :] = (gates[0:1] * oc_ref[0, rows, :] + gates[1:2] * os[:, cols]
                                 + gates[2:3] * ow[:, cols])


def _nsa_main(zt, qg, ks_p, vst_p, kw_p, vwt_p, sel_p, bias_s, bias_w, gates_t, oc_t):
    b, _, s = zt.shape
    kvh = NSA_KV_HEADS
    sp, wp = ks_p.shape[2], kw_p.shape[2]
    nb, vr = sel_p.shape[2], vst_p.shape[2]
    once = pl.Buffered(1)
    return pl.pallas_call(
        _nsa_main_kernel,
        grid=(b, s // TQ),
        in_specs=[pl.BlockSpec((1, NSA_WIDTH, TQ), lambda i, j: (i, 0, j)),
                  pl.BlockSpec((HEAD_DIM, TQ), lambda i, j: (0, 0)),
                  pl.BlockSpec((1, kvh, sp, HEAD_DIM), lambda i, j: (i, 0, 0, 0), pipeline_mode=once),
                  pl.BlockSpec((1, kvh, vr, sp), lambda i, j: (i, 0, 0, 0), pipeline_mode=once),
                  pl.BlockSpec((1, kvh, wp, HEAD_DIM), lambda i, j: (i, 0, 0, 0), pipeline_mode=once),
                  pl.BlockSpec((1, kvh, vr, wp), lambda i, j: (i, 0, 0, 0), pipeline_mode=once),
                  pl.BlockSpec((1, kvh, nb, TQ), lambda i, j: (i, 0, 0, j)),
                  pl.BlockSpec((kvh, SEL_PAD + TQ, NSA_GROUP * TQ), lambda i, j: (0, 0, 0), pipeline_mode=once),
                  pl.BlockSpec((1, kvh, WINDOW + TQ, NSA_GROUP * TQ), lambda i, j: (jnp.minimum(j, 2), 0, 0, 0)),
                  pl.BlockSpec((1, kvh, 16, TQ), lambda i, j: (i, 0, 0, j)),
                  pl.BlockSpec((1, NSA_WIDTH, TQ), lambda i, j: (i, 0, j))],
        out_specs=pl.BlockSpec((1, NSA_WIDTH, TQ), lambda i, j: (i, 0, j)),
        out_shape=jax.ShapeDtypeStruct((b, NSA_WIDTH, s), F32),
        scratch_shapes=[pltpu.VMEM((kvh, vr, NSA_GROUP * TQ), F32),
                        pltpu.VMEM((2, kvh, TQ, NSA_GROUP * TQ), F32)],
        compiler_params=_cparams(("parallel", "arbitrary"), _VMEM_LIMIT),
        name="nsa_selected_window",
    )(zt, qg, ks_p, vst_p, kw_p, vwt_p, sel_p, bias_s, bias_w, gates_t, oc_t)


def _mixout_kernel(gy_ref, yc_ref, yn_ref, h_ref, wglu_ref, go_ref, wo_ref, o_ref):
    sw = SSM_WIDTH
    ag = jnp.dot(gy_ref[0], wglu_ref[...], preferred_element_type=F32)
    ys = ag[:, :sw] * jax.nn.sigmoid(ag[:, sw:])
    ys = (_rms_rows(ys) * go_ref[:, 0:sw]).astype(BF16)
    yn = (_rms_rows(yn_ref[0]) * go_ref[:, 2 * sw:]).astype(BF16)
    out = (jnp.dot(ys, wo_ref[0:sw, :], preferred_element_type=F32)
           + jnp.dot(yc_ref[0], wo_ref[sw:2 * sw, :], preferred_element_type=F32)
           + jnp.dot(yn, wo_ref[2 * sw:, :], preferred_element_type=F32))
    o_ref[0] = h_ref[0] + out


def _mixout(gy, yc, yn, h, w_glu, g_out, w_out, tm=512):
    b, s, d = h.shape
    tok = lambda w: pl.BlockSpec((1, tm, w), lambda i, j: (i, j, 0))
    return pl.pallas_call(
        _mixout_kernel,
        grid=(b, s // tm),
        in_specs=[tok(SSM_WIDTH), tok(CONV_WIDTH), tok(NSA_WIDTH), tok(d),
                  pl.BlockSpec((SSM_WIDTH, 2 * SSM_WIDTH), lambda i, j: (0, 0)),
                  pl.BlockSpec((1, d), lambda i, j: (0, 0)),
                  pl.BlockSpec((d, d), lambda i, j: (0, 0))],
        out_specs=tok(d),
        out_shape=jax.ShapeDtypeStruct((b, s, d), F32),
        compiler_params=_cparams(("parallel", "parallel")),
        name="mix_out",
    )(gy, yc, yn, h, w_glu.astype(BF16), g_out.reshape(1, d), w_out.astype(BF16))


def _memkv_kernel(mem_ref, g_ref, w_ref, kg_ref, k_ref, v_ref):
    mn = (_rms_rows(mem_ref[0]) * g_ref[...]).astype(BF16)
    kv = jnp.dot(mn, w_ref[...], preferred_element_type=F32)
    for h in range(X_HEADS):
        cols = slice(h * HEAD_DIM, (h + 1) * HEAD_DIM)
        k_ref[0, :, cols] = (_rms_rows(kv[:, cols]) * kg_ref[...]).astype(k_ref.dtype)
    v_ref[0] = kv[:, X_WIDTH:].astype(v_ref.dtype)


def _memkv(mem, gain, w_kv, k_gain):
    b, m, d = mem.shape
    out = jax.ShapeDtypeStruct((b, m, X_WIDTH), BF16)
    return pl.pallas_call(
        _memkv_kernel,
        grid=(b,),
        in_specs=[pl.BlockSpec((1, m, d), lambda i: (i, 0, 0)),
                  pl.BlockSpec((1, d), lambda i: (0, 0)),
                  pl.BlockSpec((d, 2 * X_WIDTH), lambda i: (0, 0)),
                  pl.BlockSpec((1, HEAD_DIM), lambda i: (0, 0))],
        out_specs=[pl.BlockSpec((1, m, X_WIDTH), lambda i: (i, 0, 0))] * 2,
        out_shape=[out, out],
        compiler_params=_cparams(("parallel",)),
        name="cross_mem_kv",
    )(mem, gain.reshape(1, d), w_kv.astype(BF16), k_gain.reshape(1, HEAD_DIM))


def _cross_kernel(h_ref, g_ref, wq_ref, qg_ref, k_ref, v_ref, wo_ref, o_ref):
    h = h_ref[0]
    hn = (_rms_rows(h) * g_ref[...]).astype(BF16)
    q = jnp.dot(hn, wq_ref[...], preferred_element_type=F32)
    out = h
    for hd in range(X_HEADS):
        cols = slice(hd * HEAD_DIM, (hd + 1) * HEAD_DIM)
        qh = (_rms_rows(q[:, cols]) * qg_ref[...]).astype(BF16)
        s = lax.dot_general(qh, k_ref[0, :, cols], (((1,), (1,)), ((), ())), preferred_element_type=F32)
        p = jnp.exp(s - jnp.max(s, axis=-1, keepdims=True))
        p = p * (1.0 / jnp.sum(p, axis=-1, keepdims=True))
        o = jnp.dot(p.astype(BF16), v_ref[0, :, cols], preferred_element_type=F32)
        out = out + jnp.dot(o.astype(BF16), wo_ref[cols, :], preferred_element_type=F32)
    o_ref[0] = out


def _cross(h, gain, w_q, q_gain, k, v, w_o, tm=512):
    b, s, d = h.shape
    m = k.shape[1]
    return pl.pallas_call(
        _cross_kernel,
        grid=(b, s // tm),
        in_specs=[pl.BlockSpec((1, tm, d), lambda i, j: (i, j, 0)),
                  pl.BlockSpec((1, d), lambda i, j: (0, 0)),
                  pl.BlockSpec((d, X_WIDTH), lambda i, j: (0, 0)),
                  pl.BlockSpec((1, HEAD_DIM), lambda i, j: (0, 0)),
                  pl.BlockSpec((1, m, X_WIDTH), lambda i, j: (i, 0, 0)),
                  pl.BlockSpec((1, m, X_WIDTH), lambda i, j: (i, 0, 0)),
                  pl.BlockSpec((X_WIDTH, d), lambda i, j: (0, 0))],
        out_specs=pl.BlockSpec((1, tm, d), lambda i, j: (i, j, 0)),
        out_shape=jax.ShapeDtypeStruct((b, s, d), F32),
        compiler_params=_cparams(("parallel", "parallel")),
        name="cross_attention",
    )(h, gain.reshape(1, d), w_q.astype(BF16), (q_gain * HEAD_DIM ** -0.5).reshape(1, HEAD_DIM), k, v,
      w_o.astype(BF16))


def _ffn_kernel(h_ref, g_ref, wg_ref, wv_ref, wd_ref, o_ref, xn_ref, acc_ref):
    f = pl.program_id(1)

    @pl.when(f == 0)
    def _():
        xn_ref[...] = (_rms_rows(h_ref[...]) * g_ref[...]).astype(BF16)
        acc_ref[...] = jnp.zeros_like(acc_ref)

    x = xn_ref[...]
    gate = jnp.dot(x, wg_ref[...], preferred_element_type=F32)
    val = jnp.dot(x, wv_ref[...], preferred_element_type=F32)
    act = (jax.nn.silu(gate) * val).astype(BF16)
    acc_ref[...] += jnp.dot(act, wd_ref[...], preferred_element_type=F32)

    @pl.when(f == pl.num_programs(1) - 1)
    def _():
        o_ref[...] = h_ref[...] + acc_ref[...]


def _ffn(h2d, gain, w_up, w_down, tm=1024):
    t, d = h2d.shape
    nf = D_FF // FF_CHUNK
    wb = w_up.astype(BF16)
    return pl.pallas_call(
        _ffn_kernel,
        grid=(t // tm, nf),
        in_specs=[pl.BlockSpec((tm, d), lambda i, f: (i, 0)),
                  pl.BlockSpec((1, d), lambda i, f: (0, 0)),
                  pl.BlockSpec((d, FF_CHUNK), lambda i, f: (0, f)),
                  pl.BlockSpec((d, FF_CHUNK), lambda i, f: (0, f + nf)),
                  pl.BlockSpec((FF_CHUNK, d), lambda i, f: (f, 0))],
        out_specs=pl.BlockSpec((tm, d), lambda i, f: (i, 0)),
        out_shape=jax.ShapeDtypeStruct((t, d), F32),
        scratch_shapes=[pltpu.VMEM((tm, d), BF16), pltpu.VMEM((tm, d), F32)],
        compiler_params=_cparams(("parallel", "arbitrary"), _VMEM_LIMIT),
        name="ffn_swiglu",
    )(h2d, gain.reshape(1, d), wb, wb, w_down.astype(BF16))


def _router_kernel(h_ref, g_ref, wr_ref, xn_ref, gate_ref, asg_ref):
    xn = _rms_rows(h_ref[...]) * g_ref[...]
    xn_ref[...] = xn.astype(BF16)
    logits = jnp.dot(xn, wr_ref[...], precision=lax.Precision.HIGHEST, preferred_element_type=F32)
    lane = lax.broadcasted_iota(jnp.int32, logits.shape, 1)
    lg = jnp.where(lane < N_EXPERTS, logits, -jnp.inf)
    m1 = jnp.max(lg, axis=-1, keepdims=True)
    i1 = jnp.min(jnp.where(lg == m1, lane, 128), axis=-1, keepdims=True)
    lg2 = jnp.where(lane == i1, -jnp.inf, lg)
    m2 = jnp.max(lg2, axis=-1, keepdims=True)
    i2 = jnp.min(jnp.where(lg2 == m2, lane, 128), axis=-1, keepdims=True)
    e = jnp.exp(m2 - m1)
    den = 1.0 + e
    gate_ref[...] = jnp.where(lane == i1, 1.0 / den, jnp.where(lane == i2, e / den, 0.0))
    asg_ref[...] = ((lane == i1) | (lane == i2)).astype(jnp.int32)


def _router(h2d, gain, w_router, tm=512):
    t, d = h2d.shape
    wr = jnp.concatenate([w_router, jnp.zeros((d, 128 - N_EXPERTS), F32)], axis=1)
    return pl.pallas_call(
        _router_kernel,
        grid=(t // tm,),
        in_specs=[pl.BlockSpec((tm, d), lambda i: (i, 0)),
                  pl.BlockSpec((1, d), lambda i: (0, 0)),
                  pl.BlockSpec((d, 128), lambda i: (0, 0))],
        out_specs=[pl.BlockSpec((tm, d), lambda i: (i, 0)),
                   pl.BlockSpec((tm, 128), lambda i: (i, 0)),
                   pl.BlockSpec((tm, 128), lambda i: (i, 0))],
        out_shape=[jax.ShapeDtypeStruct((t, d), BF16), jax.ShapeDtypeStruct((t, 128), F32),
                   jax.ShapeDtypeStruct((t, 128), jnp.int32)],
        compiler_params=_cparams(("parallel",)),
        name="moe_router",
    )(h2d, gain.reshape(1, d), wr)


def _moe_windows(rb, lo, hi, active):
    lo_l = jnp.clip(lo - rb * MOE_TB, 0, MOE_TB)
    hi_l = jnp.clip(hi - rb * MOE_TB, 0, MOE_TB)
    shift = MOE_SUB.bit_length() - 1
    w0 = jnp.minimum(lax.shift_left(lax.shift_right_logical(lo_l, shift), shift), MOE_TB - MOE_WIN)
    has = active & (hi_l > lo_l)
    return ((w0, 0, has), (MOE_TB - MOE_WIN, w0 + MOE_WIN, has & (hi_l > w0 + MOE_WIN)))


def _moe_gather_kernel(rb_ref, lo_ref, hi_ref, first_ref, tgt_ref, x_ref, o_ref):
    e, j, slot = pl.program_id(0), pl.program_id(1), pl.program_id(2)
    rb = rb_ref[e, j, slot]
    lo, hi = lo_ref[e, j], hi_ref[e, j]

    @pl.when(first_ref[e, j, slot] == 1)
    def _():
        o_ref[...] = jnp.zeros_like(o_ref)

    active = (slot == 0) | (rb != rb_ref[e, j, 0])
    tgt = tgt_ref[0]
    for start, cutoff, needed in _moe_windows(rb, lo, hi, active):

        @pl.when(needed)
        def _():
            local = start + lax.broadcasted_iota(jnp.int32, (MOE_WIN, MOE_TB), 0)
            rows = jnp.where(local >= cutoff, rb * MOE_TB + local, -2)
            onehot = jnp.where(tgt == rows, 1.0, 0.0).astype(BF16)
            part = jnp.dot(onehot, x_ref[...], preferred_element_type=F32)
            sl = pl.ds(pl.multiple_of(start, MOE_SUB), MOE_WIN)
            o_ref[sl, :] = o_ref[sl, :] + part.astype(o_ref.dtype)


def _moe_ffn_kernel(exp_ref, nused_ref, x_ref, wg_ref, wv_ref, wd_ref, o_ref, acc_ref):
    r, f = pl.program_id(0), pl.program_id(1)
    used = r < nused_ref[0]

    @pl.when(f == 0)
    def _():
        acc_ref[...] = jnp.zeros_like(acc_ref)

    @pl.when(used)
    def _():
        x = x_ref[...]
        gate = jnp.dot(x, wg_ref[0], preferred_element_type=F32)
        val = jnp.dot(x, wv_ref[0], preferred_element_type=F32)
        act = (jax.nn.silu(gate) * val).astype(BF16)
        acc_ref[...] += jnp.dot(act, wd_ref[0], preferred_element_type=F32)

    @pl.when(f == pl.num_programs(1) - 1)
    def _():
        o_ref[...] = acc_ref[...].astype(o_ref.dtype)
    del exp_ref


def _moe_scatter_kernel(rb_ref, lo_ref, hi_ref, tgt_ref, gate_ref, y_ref, h_ref, o_ref):
    j, e, slot = pl.program_id(0), pl.program_id(1), pl.program_id(2)
    rb = rb_ref[e, j, slot]
    lo, hi = lo_ref[e, j], hi_ref[e, j]

    @pl.when((e == 0) & (slot == 0))
    def _():
        o_ref[...] = h_ref[...]

    active = (slot == 0) | (rb != rb_ref[e, j, 0])
    mine = lax.broadcasted_iota(jnp.int32, tgt_ref.shape, 1) == e
    tgt = jnp.sum(jnp.where(mine, tgt_ref[...], 0), axis=1, keepdims=True)
    gate = jnp.sum(jnp.where(mine, gate_ref[...], 0.0), axis=1, keepdims=True)
    for start, cutoff, needed in _moe_windows(rb, lo, hi, active):

        @pl.when(needed)
        def _():
            local = start + lax.broadcasted_iota(jnp.int32, (MOE_TB, MOE_WIN), 1)
            rows = jnp.where(local >= cutoff, rb * MOE_TB + local, -2)
            onehot = jnp.where(tgt == rows, 1.0, 0.0).astype(BF16)
            y = y_ref[pl.ds(pl.multiple_of(start, MOE_SUB), MOE_WIN), :]
            o_ref[...] = o_ref[...] + gate * jnp.dot(onehot, y, preferred_element_type=F32)


def _moe(h2d, gain, w_router, w_up, w_down):
    t, d = h2d.shape
    tb = MOE_TB
    nj = t // tb
    n_rb = (t * TOP_K) // tb + N_EXPERTS
    xn, gates, asg = _router(h2d, gain, w_router)
    asg = asg[:, :N_EXPERTS]
    gates = gates[:, :N_EXPERTS]
    cs = jnp.cumsum(asg, axis=0)
    rank = cs - asg
    counts = cs[-1]
    padded = (counts + tb - 1) // tb * tb
    pad_end = jnp.cumsum(padded)
    start_p = pad_end - padded
    tgt = jnp.where(asg == 1, start_p[None, :] + rank, -1).astype(jnp.int32)
    cb = jnp.concatenate([jnp.zeros((1, N_EXPERTS), jnp.int32), cs[tb - 1::tb]], axis=0)
    lo = (start_p[None, :] + cb[:-1]).T.astype(jnp.int32)
    hi = (start_p[None, :] + cb[1:]).T.astype(jnp.int32)
    rb0 = lo // tb
    rb1 = jnp.maximum(rb0, (hi - 1) // tb)
    rb = jnp.stack([rb0, rb1], axis=-1).astype(jnp.int32)
    flat = rb.reshape(-1)
    first = jnp.concatenate([jnp.ones((1,), jnp.int32), (flat[1:] != flat[:-1]).astype(jnp.int32)])
    first = first.reshape(N_EXPERTS, nj, 2)
    n_used = (pad_end[-1] // tb).astype(jnp.int32).reshape(1)
    blk_exp = jnp.minimum(jnp.searchsorted(pad_end, jnp.arange(n_rb) * tb, side='right'),
                          N_EXPERTS - 1).astype(jnp.int32)

    xs = pl.pallas_call(
        _moe_gather_kernel,
        grid_spec=pltpu.PrefetchScalarGridSpec(
            num_scalar_prefetch=4,
            grid=(N_EXPERTS, nj, 2),
            in_specs=[pl.BlockSpec((1, 1, tb), lambda e, j, s, *_: (e, 0, j)),
                      pl.BlockSpec((tb, d), lambda e, j, s, *_: (j, 0))],
            out_specs=pl.BlockSpec((tb, d), lambda e, j, s, rb_ref, *_: (rb_ref[e, j, s], 0))),
        out_shape=jax.ShapeDtypeStruct((n_rb * tb, d), BF16),
        compiler_params=_cparams(("arbitrary", "arbitrary", "arbitrary"), _VMEM_LIMIT),
        name="moe_gather",
    )(rb, lo, hi, first, tgt.T.reshape(N_EXPERTS, 1, t), xn)

    nf = D_FF // FF_CHUNK
    wub = w_up.astype(BF16)
    ys = pl.pallas_call(
        _moe_ffn_kernel,
        grid_spec=pltpu.PrefetchScalarGridSpec(
            num_scalar_prefetch=2,
            grid=(n_rb, nf),
            in_specs=[pl.BlockSpec((tb, d), lambda r, f, *_: (r, 0)),
                      pl.BlockSpec((1, d, FF_CHUNK), lambda r, f, ex, nu: (ex[r], 0, f)),
                      pl.BlockSpec((1, d, FF_CHUNK), lambda r, f, ex, nu: (ex[r], 0, f + nf)),
                      pl.BlockSpec((1, FF_CHUNK, d), lambda r, f, ex, nu: (ex[r], f, 0))],
            out_specs=pl.BlockSpec((tb, d), lambda r, f, *_: (r, 0)),
            scratch_shapes=[pltpu.VMEM((tb, d), F32)]),
        out_shape=jax.ShapeDtypeStruct((n_rb * tb, d), BF16),
        compiler_params=_cparams(("arbitrary", "arbitrary"), _VMEM_LIMIT),
        name="moe_expert_ffn",
    )(blk_exp, n_used, xs, wub, wub, w_down.astype(BF16))

    return pl.pallas_call(
        _moe_scatter_kernel,
        grid_spec=pltpu.PrefetchScalarGridSpec(
            num_scalar_prefetch=3,
            grid=(nj, N_EXPERTS, 2),
            in_specs=[pl.BlockSpec((tb, N_EXPERTS), lambda j, e, s, *_: (j, 0)),
                      pl.BlockSpec((tb, N_EXPERTS), lambda j, e, s, *_: (j, 0)),
                      pl.BlockSpec((tb, d), lambda j, e, s, rb_ref, *_: (rb_ref[e, j, s], 0)),
                      pl.BlockSpec((tb, d), lambda j, e, s, *_: (j, 0))],
            out_specs=pl.BlockSpec((tb, d), lambda j, e, s, *_: (j, 0))),
        out_shape=jax.ShapeDtypeStruct((t, d), F32),
        compiler_params=_cparams(("arbitrary", "arbitrary", "arbitrary"), _VMEM_LIMIT),
        name="moe_scatter",
    )(rb, lo, hi, tgt, gates, ys, h2d)


_COL_CONV, _COL_SSM, _COL_KC, _COL_VC, _COL_KS, _COL_KW = 0, 512, 768, 896, 1024, 1152
_ROW_Q, _ROW_VS, _ROW_VW, _ROW_G = 0, 512, 640, 768


def _split_w_in(w_in):
    kvw = NSA_KV_HEADS * HEAD_DIM
    cuts = np.cumsum([0, SSM_WIDTH, 2 * CONV_WIDTH, NSA_WIDTH] + [kvw] * 6 + [N_BRANCH * NSA_HEADS])
    seg = lambda i: w_in[:, cuts[i]:cuts[i + 1]]
    ssm, conv, q, k_c, v_c, k_s, v_s, k_w, v_w, gate = (seg(i) for i in range(10))
    w_tok = jnp.concatenate([conv, ssm, k_c, v_c, k_s, k_w], axis=1).astype(BF16)
    gate = jnp.concatenate([gate, jnp.zeros((w_in.shape[0], 8), F32)], axis=1)
    w_t = jnp.concatenate([q, v_s, v_w, gate], axis=1).T.astype(BF16)
    return w_tok, w_t


def _layer_mixers(h, p, t5_tiles):
    b, s, d = h.shape
    w_tok, w_t = _split_w_in(p['w_in'])
    ztok, zt = _proj(h, p['norm_mix'], w_tok, w_t)

    n_chunks = s // SSM_CHUNK
    u = ztok[:, :, _COL_SSM:_COL_SSM + SSM_WIDTH].astype(BF16)
    xg = u.reshape(b, n_chunks, SSM_CHUNK, SSM_GROUPS, SSM_GROUP).transpose(3, 1, 0, 2, 4)
    xg = xg.reshape(SSM_GROUPS, n_chunks * b, SSM_CHUNK * SSM_GROUP)
    tables = _s5_tables(p['ssm_lambda_re'], p['ssm_lambda_im'], p['ssm_log_dt'], p['ssm_b_re'], p['ssm_b_im'],
                        p['ssm_c_re'], p['ssm_c_im'], p['ssm_d'])
    gy = _s5(xg, tables, b)
    gy = gy.reshape(SSM_GROUPS, n_chunks, b, SSM_CHUNK, SSM_GROUP).transpose(2, 1, 3, 0, 4).reshape(b, s, SSM_WIDTH)

    g_out = p['mix_out_norm']
    yc = _conv(ztok, p['conv_w_dw'], p['conv_b_dw'], p['conv_ln_g'], p['conv_ln_b'], p['conv_w_pw'],
               g_out[SSM_WIDTH:SSM_WIDTH + CONV_WIDTH])

    kvw = NSA_KV_HEADS * HEAD_DIM
    n_cmp = s // CMP_STRIDE
    k_norm = p['nsa_k_norm']
    kc, vct = _compress(ztok, _COL_KC, _COL_VC, p['nsa_cmp_pe'], p['nsa_cmp_w1'], p['nsa_cmp_w2'], k_norm[0])
    ks, kw = _knorm(ztok, _COL_KS, _COL_KW, k_norm[1], k_norm[2])
    qg = jnp.broadcast_to((p['nsa_q_norm'] * (HEAD_DIM ** -0.5 * LOG2E))[:, None], (HEAD_DIM, TQ))
    bias_c, bias_s, bias_w = t5_tiles
    n_sel = s // L_SEL
    cs_ = np.arange(n_cmp) * CMP_STRIDE
    ss_ = np.arange(n_sel) * L_SEL
    ov = np.maximum(np.minimum(cs_[:, None] + L_CMP, ss_[None, :] + L_SEL) - np.maximum(cs_[:, None], ss_[None, :]), 0)
    ovt = jnp.asarray((ov.astype(np.float32) / L_CMP).T, BF16)
    oc_t, sel = _nsa_cmp(zt, qg, kc, vct, bias_c, ovt)

    front = lambda x, n, axis: jnp.pad(x, [(n, 0) if a == axis else (0, 0) for a in range(x.ndim)])
    ones_rows = jnp.concatenate([jnp.ones((b, NSA_KV_HEADS, 1, s), BF16),
                                 jnp.zeros((b, NSA_KV_HEADS, 15, s), BF16)], axis=2)
    heads_t = lambda rows: jnp.concatenate(
        [zt[:, rows:rows + kvw, :].astype(BF16).reshape(b, NSA_KV_HEADS, HEAD_DIM, s), ones_rows], axis=2)
    ks_p = front(ks, SEL_PAD, 2)
    kw_p = front(kw, WINDOW, 2)
    vst_p = front(heads_t(_ROW_VS), SEL_PAD, 3)
    vwt_p = front(heads_t(_ROW_VW), WINDOW, 3)
    sel_p = jnp.pad(sel, ((0, 0), (0, 0), (SEL_PAD // L_SEL, 0), (0, 0)), constant_values=NEG)
    gl = zt[:, _ROW_G:_ROW_G + N_BRANCH * NSA_HEADS, :].reshape(b, NSA_KV_HEADS, NSA_GROUP * N_BRANCH, s)
    gates_t = jnp.pad(gl, ((0, 0), (0, 0), (0, 16 - NSA_GROUP * N_BRANCH), (0, 0)))
    yn_t = _nsa_main(zt, qg, ks_p, vst_p, kw_p, vwt_p, sel_p, bias_s, bias_w, gates_t, oc_t)
    yn = yn_t.transpose(0, 2, 1)

    return _mixout(gy, yc, yn, h, p['ssm_w_glu'], g_out, p['w_out'])


def kernel(x, mem, norm_mix, w_in, ssm_lambda_re, ssm_lambda_im, ssm_log_dt, ssm_b_re, ssm_b_im, ssm_c_re, ssm_c_im, ssm_d, ssm_w_glu, conv_w_dw, conv_b_dw, conv_ln_g, conv_ln_b, conv_w_pw, nsa_q_norm, nsa_k_norm, nsa_cmp_pe, nsa_cmp_w1, nsa_cmp_w2, mix_out_norm, w_out, t5_table, norm_cross, norm_mem, x_w_q, x_w_kv, x_q_norm, x_k_norm, x_w_o, norm_ffn, ffn_w_up, ffn_w_down, moe_router, moe_w_up, moe_w_down):
    b, s, d = x.shape
    depth = w_in.shape[0]
    per_layer = dict(norm_mix=norm_mix, w_in=w_in, ssm_lambda_re=ssm_lambda_re, ssm_lambda_im=ssm_lambda_im,
                     ssm_log_dt=ssm_log_dt, ssm_b_re=ssm_b_re, ssm_b_im=ssm_b_im, ssm_c_re=ssm_c_re,
                     ssm_c_im=ssm_c_im, ssm_d=ssm_d, ssm_w_glu=ssm_w_glu, conv_w_dw=conv_w_dw,
                     conv_b_dw=conv_b_dw, conv_ln_g=conv_ln_g, conv_ln_b=conv_ln_b, conv_w_pw=conv_w_pw,
                     nsa_q_norm=nsa_q_norm, nsa_k_norm=nsa_k_norm, nsa_cmp_pe=nsa_cmp_pe, nsa_cmp_w1=nsa_cmp_w1,
                     nsa_cmp_w2=nsa_cmp_w2, mix_out_norm=mix_out_norm, w_out=w_out)
    t5_tiles = _nsa_bias_tiles(t5_table, s)
    h = x
    for layer in range(depth):
        p = {k: v[layer] for k, v in per_layer.items()}
        h = _layer_mixers(h, p, t5_tiles)
        mk, mv = _memkv(mem, norm_mem[layer], x_w_kv[layer], x_k_norm[layer])
        h = _cross(h, norm_cross[layer], x_w_q[layer], x_q_norm[layer], mk, mv, x_w_o[layer])
        h2d = h.reshape(b * s, d)
        if layer % 2 == 0:
            h2d = _ffn(h2d, norm_ffn[layer], ffn_w_up[layer // 2], ffn_w_down[layer // 2])
        else:
            h2d = _moe(h2d, norm_ffn[layer], moe_router[layer // 2], moe_w_up[layer // 2], moe_w_down[layer // 2])
        h = h2d.reshape(b, s, d)
    return h
```

```python
import functools
import math

import jax
import jax.numpy as jnp
import numpy as np
from jax import lax
from jax.experimental import pallas as pl
from jax.experimental.pallas import tpu as pltpu

F32 = jnp.float32
BF16 = jnp.bfloat16

D_MODEL = 1024
HEAD_DIM = 64
SSM_WIDTH = 256
SSM_GROUP = 16
SSM_GROUPS = 16
SSM_STATE = 64
SSM_CHUNK = 16
CONV_WIDTH = 256
CONV_K = 31
CONV_HALO = 32
NSA_WIDTH = 512
NSA_HEADS = 8
NSA_KV_HEADS = 2
NSA_GROUP = 4
N_BRANCH = 3
L_CMP = 32
CMP_STRIDE = 16
L_SEL = 64
N_SELECT = 16
N_LOCAL = 2
WINDOW = 512
SEL_FORCE = 1e6
T5_BUCKETS = 32
T5_MAX_DIST = 128
X_HEADS = 4
X_WIDTH = 256
D_FF = 2816
N_EXPERTS = 8
TOP_K = 2
EPS = 1e-6
NEG = -1e30
LOG2E = math.log2(math.e)

TQ = 256
SEL_PAD = 256
FF_CHUNK = 256
MOE_TB = 1024
MOE_SUB = 128
MOE_WIN = 512

_VMEM_LIMIT = 56 * 1024 * 1024


def _cparams(sem, vmem=None):
    return pltpu.CompilerParams(dimension_semantics=sem, vmem_limit_bytes=vmem)


def _rms_rows(x):
    return x * lax.rsqrt(jnp.mean(x * x, axis=-1, keepdims=True) + EPS)


def _proj_kernel(x_ref, g_ref, wtok_ref, wt_ref, ztok_ref, zt_ref):
    xn = (_rms_rows(x_ref[0]) * g_ref[...]).astype(BF16)
    ztok_ref[0] = jnp.dot(xn, wtok_ref[...], preferred_element_type=F32)
    zt_ref[0] = lax.dot_general(wt_ref[...], xn, (((1,), (1,)), ((), ())), preferred_element_type=F32)


def _proj(h, gain, w_tok, w_t, tm=512):
    b, s, d = h.shape
    ntok, nt = w_tok.shape[1], w_t.shape[0]
    return pl.pallas_call(
        _proj_kernel,
        grid=(b, s // tm),
        in_specs=[pl.BlockSpec((1, tm, d), lambda i, j: (i, j, 0)),
                  pl.BlockSpec((1, d), lambda i, j: (0, 0)),
                  pl.BlockSpec((d, ntok), lambda i, j: (0, 0)),
                  pl.BlockSpec((nt, d), lambda i, j: (0, 0))],
        out_specs=[pl.BlockSpec((1, tm, ntok), lambda i, j: (i, j, 0)),
                   pl.BlockSpec((1, nt, tm), lambda i, j: (i, 0, j))],
        out_shape=[jax.ShapeDtypeStruct((b, s, ntok), F32), jax.ShapeDtypeStruct((b, nt, s), F32)],
        compiler_params=_cparams(("parallel", "parallel"), _VMEM_LIMIT),
        name="proj",
    )(h, gain.reshape(1, d), w_tok, w_t)


def _s5_tables(lam_re, lam_im, log_dt, b_re, b_im, c_re, c_im, d_skip):
    L, H, P = SSM_CHUNK, SSM_GROUP, SSM_STATE
    dt = jnp.exp(log_dt.astype(F32))[:, None]
    lr, li = lam_re.astype(F32), lam_im.astype(F32)
    mag = jnp.exp(lr * dt)
    ar, ai = mag * jnp.cos(li * dt), mag * jnp.sin(li * dt)
    den = lr * lr + li * li
    fr = ((ar - 1.0) * lr + ai * li) / den
    fi = (ai * lr - (ar - 1.0) * li) / den
    bbr = fr[..., None] * b_re - fi[..., None] * b_im
    bbi = fr[..., None] * b_im + fi[..., None] * b_re
    j = jnp.arange(L + 1, dtype=F32)[:, None, None]
    pmag = jnp.exp(lr[None] * dt[None] * j)
    pr, pi = pmag * jnp.cos(li[None] * dt[None] * j), pmag * jnp.sin(li[None] * dt[None] * j)
    cbr = c_re[:, :, :, None] * bbr[:, None, :, :] - c_im[:, :, :, None] * bbi[:, None, :, :]
    cbi = c_re[:, :, :, None] * bbi[:, None, :, :] + c_im[:, :, :, None] * bbr[:, None, :, :]
    hp = lax.Precision.HIGHEST
    kj = (jnp.einsum('jgp,ghpk->jghk', pr[:L], cbr, precision=hp)
          - jnp.einsum('jgp,ghpk->jghk', pi[:L], cbi, precision=hp))
    lag = np.arange(L)[None, :] - np.arange(L)[:, None]
    place = (lag[None] == np.arange(L)[:, None, None]).astype(np.float32)
    kt = jnp.einsum('jab,jghk->abghk', place, kj, precision=hp)
    kt = kt + (jnp.eye(L)[:, :, None, None, None] * (jnp.eye(H)[None, None, None] * d_skip[None, None, :, :, None]))
    tmat = kt.transpose(2, 0, 4, 1, 3).reshape(SSM_GROUPS, L * H, L * H)
    qr, qi = pr[:L][::-1], pi[:L][::-1]
    wre = qr[..., None] * bbr[None] - qi[..., None] * bbi[None]
    wim = qr[..., None] * bbi[None] + qi[..., None] * bbr[None]
    wre = wre.transpose(1, 0, 3, 2).reshape(SSM_GROUPS, L * H, P)
    wim = wim.transpose(1, 0, 3, 2).reshape(SSM_GROUPS, L * H, P)
    w1 = jnp.concatenate([wre, wim], axis=-1)
    w2 = jnp.concatenate([wim, wre], axis=-1)
    sr, si = pr[1:], pi[1:]
    vr = c_re[None] * sr[:, :, None, :] - c_im[None] * si[:, :, None, :]
    vi = c_re[None] * si[:, :, None, :] + c_im[None] * sr[:, :, None, :]
    vmat = jnp.concatenate([vr, -vi], axis=-1).transpose(1, 3, 0, 2).reshape(SSM_GROUPS, 2 * P, L * H)
    a_r, a_i = pr[L], pi[L]
    am = jnp.stack([jnp.concatenate([a_r, a_r], -1), jnp.concatenate([-a_i, a_i], -1),
                    jnp.concatenate([a_i, -a_i], -1)], axis=1)
    am = jnp.concatenate([am, jnp.zeros((SSM_GROUPS, 5, 2 * P), F32)], axis=1)
    return tmat.astype(BF16), w1.astype(BF16), w2.astype(BF16), vmat.astype(BF16), am


def _s5_kernel(x_ref, t_ref, w1_ref, w2_ref, v_ref, a_ref, o_ref, s1_ref, s2_ref, xin_ref, *, bsz, n_chunks):
    x = x_ref[0]
    s1_ref[...] = jnp.dot(x, w1_ref[0], preferred_element_type=F32)
    s2_ref[...] = jnp.dot(x, w2_ref[0], preferred_element_type=F32)
    a1, a2, a3 = a_ref[0, 0:1, :], a_ref[0, 1:2, :], a_ref[0, 2:3, :]

    def step(c, carry):
        p, q = carry
        rows = pl.ds(pl.multiple_of(c * bsz, bsz), bsz)
        xin_ref[rows, :] = p
        return (p * a1 + q * a2 + s1_ref[rows, :], q * a1 + p * a3 + s2_ref[rows, :])

    zero = jnp.zeros((bsz, 2 * SSM_STATE), F32)
    lax.fori_loop(0, n_chunks, step, (zero, zero))
    y = (jnp.dot(x, t_ref[0], preferred_element_type=F32)
         + jnp.dot(xin_ref[...].astype(BF16), v_ref[0], preferred_element_type=F32))
    o_ref[0] = jax.nn.gelu(y).astype(o_ref.dtype)


def _s5(xg, tables, bsz):
    tmat, w1, w2, vmat, am = tables
    g, r, lh = xg.shape
    p2 = 2 * SSM_STATE
    kern = functools.partial(_s5_kernel, bsz=bsz, n_chunks=r // bsz)
    return pl.pallas_call(
        kern,
        grid=(g,),
        in_specs=[pl.BlockSpec((1, r, lh), lambda i: (i, 0, 0)),
                  pl.BlockSpec((1, lh, lh), lambda i: (i, 0, 0)),
                  pl.BlockSpec((1, lh, p2), lambda i: (i, 0, 0)),
                  pl.BlockSpec((1, lh, p2), lambda i: (i, 0, 0)),
                  pl.BlockSpec((1, p2, lh), lambda i: (i, 0, 0)),
                  pl.BlockSpec((1, 8, p2), lambda i: (i, 0, 0))],
        out_specs=pl.BlockSpec((1, r, lh), lambda i: (i, 0, 0)),
        out_shape=jax.ShapeDtypeStruct((g, r, lh), BF16),
        scratch_shapes=[pltpu.VMEM((r, p2), F32), pltpu.VMEM((r, p2), F32), pltpu.VMEM((r, p2), F32)],
        compiler_params=_cparams(("parallel",), _VMEM_LIMIT),
        name="s5_scan",
    )(xg, tmat, w1, w2, vmat, am)


def _conv_kernel(z_ref, halo_ref, wdw_ref, bdw_ref, lng_ref, lnb_ref, wpw_ref, go_ref, o_ref, buf_ref, *, tt):
    first = pl.program_id(1) == 0
    zc = z_ref[0]
    zh = halo_ref[0]
    vh = zh[:, :CONV_WIDTH] * jax.nn.sigmoid(zh[:, CONV_WIDTH:])
    buf_ref[0:CONV_HALO, :] = vh * jnp.where(first, 0.0, 1.0)
    buf_ref[CONV_HALO:CONV_HALO + tt, :] = zc[:, :CONV_WIDTH] * jax.nn.sigmoid(zc[:, CONV_WIDTH:])
    acc = jnp.zeros((tt, CONV_WIDTH), F32) + bdw_ref[...]
    for k in range(CONV_K):
        acc = acc + wdw_ref[k:k + 1, :] * buf_ref[pl.ds(CONV_HALO - (CONV_K - 1) + k, tt), :]
    mu = jnp.mean(acc, axis=-1, keepdims=True)
    var = jnp.mean(jnp.square(acc - mu), axis=-1, keepdims=True)
    y = (acc - mu) * lax.rsqrt(var + EPS) * lng_ref[...] + lnb_ref[...]
    y = jax.nn.silu(y)
    y = jnp.dot(y.astype(BF16), wpw_ref[...], preferred_element_type=F32)
    o_ref[0] = (_rms_rows(y) * go_ref[...]).astype(o_ref.dtype)


def _conv(ztok, w_dw, b_dw, ln_g, ln_b, w_pw, g_out, tt=512):
    b, s, _ = ztok.shape
    cw = CONV_WIDTH
    hb = tt // CONV_HALO
    kern = functools.partial(_conv_kernel, tt=tt)
    row = lambda v: v.reshape(1, cw)
    return pl.pallas_call(
        kern,
        grid=(b, s // tt),
        in_specs=[pl.BlockSpec((1, tt, 2 * cw), lambda i, j: (i, j, 0)),
                  pl.BlockSpec((1, CONV_HALO, 2 * cw), lambda i, j: (i, jnp.maximum(j * hb - 1, 0), 0)),
                  pl.BlockSpec((CONV_K + 1, cw), lambda i, j: (0, 0)),
                  pl.BlockSpec((1, cw), lambda i, j: (0, 0)),
                  pl.BlockSpec((1, cw), lambda i, j: (0, 0)),
                  pl.BlockSpec((1, cw), lambda i, j: (0, 0)),
                  pl.BlockSpec((cw, cw), lambda i, j: (0, 0)),
                  pl.BlockSpec((1, cw), lambda i, j: (0, 0))],
        out_specs=pl.BlockSpec((1, tt, cw), lambda i, j: (i, j, 0)),
        out_shape=jax.ShapeDtypeStruct((b, s, cw), BF16),
        scratch_shapes=[pltpu.VMEM((CONV_HALO + tt, cw), F32)],
        compiler_params=_cparams(("parallel", "arbitrary")),
        name="conv_mixer",
    )(ztok, ztok, jnp.concatenate([w_dw, jnp.zeros((1, cw), F32)], 0), row(b_dw), row(ln_g), row(ln_b),
      w_pw.astype(BF16), row(g_out))


def _knorm_kernel(ks_ref, kw_ref, gs_ref, gw_ref, os_ref, ow_ref):
    for src, g_ref, dst in ((ks_ref, gs_ref, os_ref), (kw_ref, gw_ref, ow_ref)):
        x = src[0]
        for h in range(NSA_KV_HEADS):
            xh = x[:, h * HEAD_DIM:(h + 1) * HEAD_DIM]
            dst[0, h] = (_rms_rows(xh) * g_ref[...]).astype(dst.dtype)


def _knorm(ztok, col_s, col_w, gain_s, gain_w, tt=512):
    b, s, _ = ztok.shape
    kw = NSA_KV_HEADS * HEAD_DIM
    out = jax.ShapeDtypeStruct((b, NSA_KV_HEADS, s, HEAD_DIM), BF16)
    ospec = pl.BlockSpec((1, NSA_KV_HEADS, tt, HEAD_DIM), lambda i, j: (i, 0, j, 0))
    return pl.pallas_call(
        _knorm_kernel,
        grid=(b, s // tt),
        in_specs=[pl.BlockSpec((1, tt, kw), lambda i, j: (i, j, col_s // kw)),
                  pl.BlockSpec((1, tt, kw), lambda i, j: (i, j, col_w // kw)),
                  pl.BlockSpec((1, HEAD_DIM), lambda i, j: (0, 0)),
                  pl.BlockSpec((1, HEAD_DIM), lambda i, j: (0, 0))],
        out_specs=[ospec, ospec],
        out_shape=[out, out],
        compiler_params=_cparams(("parallel", "parallel")),
        name="nsa_key_norm",
    )(ztok, ztok, gain_s.reshape(1, HEAD_DIM), gain_w.reshape(1, HEAD_DIM))


def _compress_kernel(k_ref, v_ref, wka_ref, wkb_ref, ck_ref, w2k_ref, gk_ref,
                     wva_ref, wvb_ref, cv_ref, w2v_ref, ko_ref, vo_ref):
    hi = lax.Precision.HIGHEST
    n = k_ref.shape[1] // CMP_STRIDE
    kvw = k_ref.shape[2]
    nt = (((1,), (1,)), ((), ()))
    a, bm = jnp.zeros((n, kvw), F32), jnp.zeros((n, kvw), F32)
    at, bt = jnp.zeros((kvw, n), F32), jnp.zeros((kvw, n), F32)
    for l in range(CMP_STRIDE):
        kl = k_ref[0, pl.ds(l, n, stride=CMP_STRIDE), :]
        vl = v_ref[0, pl.ds(l, n, stride=CMP_STRIDE), :]
        a = a + jnp.dot(kl, wka_ref[l], precision=hi, preferred_element_type=F32)
        bm = bm + jnp.dot(kl, wkb_ref[l], precision=hi, preferred_element_type=F32)
        at = at + lax.dot_general(wva_ref[l], vl, nt, precision=hi, preferred_element_type=F32)
        bt = bt + lax.dot_general(wvb_ref[l], vl, nt, precision=hi, preferred_element_type=F32)
    pre = a + pltpu.roll(bm, n - 1, 0) + ck_ref[...]
    kc = jnp.dot(jax.nn.gelu(pre), w2k_ref[...], precision=hi, preferred_element_type=F32)
    for h in range(NSA_KV_HEADS):
        kh = kc[:, h * HEAD_DIM:(h + 1) * HEAD_DIM]
        ko_ref[0, h] = (_rms_rows(kh) * gk_ref[...]).astype(ko_ref.dtype)
    pre_t = at + pltpu.roll(bt, n - 1, 1) + cv_ref[...]
    vt = jnp.dot(w2v_ref[...], jax.nn.gelu(pre_t), precision=hi, preferred_element_type=F32)
    for h in range(NSA_KV_HEADS):
        vo_ref[0, h] = vt[h * HEAD_DIM:(h + 1) * HEAD_DIM, :].astype(vo_ref.dtype)


def _blockdiag2(w):
    z = jnp.zeros_like(w)
    return jnp.concatenate([jnp.concatenate([w, z], 1), jnp.concatenate([z, w], 1)], 0)


def _compress(ztok, col_k, col_v, pe, w1, w2, k_gain):
    b, s, _ = ztok.shape
    n = s // CMP_STRIDE
    hd, kvw = HEAD_DIM, NSA_KV_HEADS * HEAD_DIM
    hp = lax.Precision.HIGHEST

    def expand(w):
        wl = w.reshape(L_CMP, hd, hd)
        e = wl[:, None, :, None, :] * jnp.eye(NSA_KV_HEADS, dtype=F32)[None, :, None, :, None]
        e = e.reshape(L_CMP, kvw, kvw)
        return e[:CMP_STRIDE], e[CMP_STRIDE:]

    wka, wkb = expand(w1[0])
    wva, wvb = expand(w1[1])
    ck = jnp.tile(jnp.dot(pe[0].reshape(1, L_CMP * hd), w1[0], precision=hp), (1, NSA_KV_HEADS))
    cv = jnp.tile(jnp.dot(pe[1].reshape(1, L_CMP * hd), w1[1], precision=hp), (1, NSA_KV_HEADS)).T
    full = lambda shape: pl.BlockSpec(shape, lambda i: tuple(0 for _ in shape))
    return pl.pallas_call(
        _compress_kernel,
        grid=(b,),
        in_specs=[pl.BlockSpec((1, s, kvw), lambda i: (i, 0, col_k // kvw)),
                  pl.BlockSpec((1, s, kvw), lambda i: (i, 0, col_v // kvw)),
                  full((CMP_STRIDE, kvw, kvw)), full((CMP_STRIDE, kvw, kvw)), full((1, kvw)), full((kvw, kvw)),
                  full((1, hd)),
                  full((CMP_STRIDE, kvw, kvw)), full((CMP_STRIDE, kvw, kvw)), full((kvw, 1)), full((kvw, kvw))],
        out_specs=[pl.BlockSpec((1, NSA_KV_HEADS, n, hd), lambda i: (i, 0, 0, 0)),
                   pl.BlockSpec((1, NSA_KV_HEADS, hd, n), lambda i: (i, 0, 0, 0))],
        out_shape=[jax.ShapeDtypeStruct((b, NSA_KV_HEADS, n, hd), BF16),
                   jax.ShapeDtypeStruct((b, NSA_KV_HEADS, hd, n), BF16)],
        compiler_params=_cparams(("parallel",), _VMEM_LIMIT),
        name="nsa_compress",
    )(ztok, ztok, wka, wkb, ck, _blockdiag2(w2[0]), k_gain.reshape(1, hd),
      wva.transpose(0, 2, 1), wvb.transpose(0, 2, 1), cv, _blockdiag2(w2[1]).T)


def _t5_bias_by_dist(t5_table):
    n = np.arange(T5_MAX_DIST + 1)
    max_exact = T5_BUCKETS // 2
    nf = np.maximum(n, 1).astype(np.float32)
    large = max_exact + (np.log(nf / np.float32(max_exact)) / np.float32(math.log(T5_MAX_DIST / max_exact))
                         * np.float32(T5_BUCKETS - max_exact)).astype(np.int32)
    large = np.minimum(large, T5_BUCKETS - 1)
    bucket = np.where(n < max_exact, n, large)
    onehot = (bucket[:, None] == np.arange(T5_BUCKETS)[None, :]).astype(np.float32)
    return jnp.dot(onehot, t5_table, precision=lax.Precision.HIGHEST)


def _bias_tile(fdt, rows, stride, dist00, d_max=None):
    heads = fdt.shape[0]
    a0 = stride * (rows - 1)
    d_lo = dist00 - a0
    length = a0 + TQ
    d_hi = d_lo + length
    d_max = d_hi if d_max is None else d_max
    pieces = []
    for lo, hi, kind in ((d_lo, min(d_hi, 0), 'neg'), (max(d_lo, 0), min(d_hi, T5_MAX_DIST), 'tab'),
                         (max(d_lo, T5_MAX_DIST), min(d_hi, d_max), 'far'), (max(d_lo, d_max), d_hi, 'neg')):
        if hi > lo:
            pieces.append(fdt[:, lo:hi] if kind == 'tab'
                          else jnp.full((heads, hi - lo), NEG if kind == 'neg' else 0.0, F32))
    vec = jnp.concatenate(pieces, axis=1)
    c0 = -(-a0 // 128) * 128
    width = -(-(c0 + TQ) // 128) * 128
    vec = jnp.pad(vec, ((0, 0), (c0 - a0, width - (c0 - a0) - length)))

    def kern(v_ref, o_ref):
        x = jnp.broadcast_to(v_ref[0], (rows, width))
        o_ref[0] = pltpu.roll(x, 0, 1, stride=stride, stride_axis=0)[:, c0:c0 + TQ]

    return pl.pallas_call(
        kern,
        grid=(heads,),
        in_specs=[pl.BlockSpec((1, 1, width), lambda h: (h, 0, 0))],
        out_specs=pl.BlockSpec((1, rows, TQ), lambda h: (h, 0, 0)),
        out_shape=jax.ShapeDtypeStruct((heads, rows, TQ), F32),
        compiler_params=_cparams(("parallel",)),
        name="toeplitz_bias",
    )(vec.reshape(heads, 1, width))


def _nsa_bias_tiles(t5_table, seq):
    fd = _t5_bias_by_dist(t5_table).astype(F32)
    fdt = ((fd - fd[T5_MAX_DIST:]) * LOG2E).T
    n_cmp = seq // CMP_STRIDE
    qt = TQ // CMP_STRIDE
    r0 = n_cmp - qt
    band = _bias_tile(fdt, 2 * qt, CMP_STRIDE, CMP_STRIDE * qt - (L_CMP - 1))
    heads = fdt.shape[0]
    cmp_t = jnp.concatenate([jnp.zeros((heads, r0 - qt, TQ), F32), band,
                             jnp.full((heads, n_cmp - qt, TQ), NEG, F32)], axis=1)
    sel_t = _bias_tile(fdt, SEL_PAD + TQ, 1, SEL_PAD)
    win = _bias_tile(fdt, WINDOW + TQ, 1, WINDOW, d_max=WINDOW)
    rw = np.arange(WINDOW + TQ)[None, :, None]
    win_t = jnp.stack([jnp.where(rw >= WINDOW - q0, win, NEG) for q0 in (0, TQ, 2 * TQ)])
    split = lambda t: t.reshape(*t.shape[:-3], NSA_KV_HEADS, NSA_GROUP, *t.shape[-2:])

    def wide(t):
        t = jnp.swapaxes(split(t), -3, -2)
        return t.reshape(*t.shape[:-2], NSA_GROUP * TQ)

    return wide(cmp_t), wide(sel_t), wide(win_t)


def _q_head(qt_ref, g, qg_ref):
    q = qt_ref[0, g * HEAD_DIM:(g + 1) * HEAD_DIM, :]
    inv = lax.rsqrt(jnp.mean(q * q, axis=0, keepdims=True) + EPS)
    return (q * inv * qg_ref[...]).astype(BF16)


def _nsa_cmp_kernel(qt_ref, qg_ref, kc_ref, vct_ref, bias_ref, ov_ref, oc_ref, sel_ref, imp_ref, *, n_cmp, n_sel):
    qi = pl.program_id(2)
    qt = TQ // CMP_STRIDE
    row0 = pl.multiple_of((n_cmp - qt) - qi * qt, qt)
    qw = jnp.concatenate([_q_head(qt_ref, g, qg_ref) for g in range(NSA_GROUP)], axis=1)

    def attend(n):
        s = (jnp.dot(kc_ref[0, 0, 0:n, :], qw, preferred_element_type=F32) + bias_ref[0, pl.ds(row0, n), :])
        m = jnp.max(s, axis=0, keepdims=True)
        m = jnp.where(m < 0.5 * NEG, 0.0, m)
        p = jnp.exp2(s - m)
        p = p * (1.0 / jnp.maximum(jnp.sum(p, axis=0, keepdims=True), 1e-30))
        oc = jnp.dot(vct_ref[0, 0, :, 0:n], p.astype(BF16), preferred_element_type=F32)
        psum = jnp.zeros((n, TQ), F32)
        for g in range(NSA_GROUP):
            oc_ref[0, g * HEAD_DIM:(g + 1) * HEAD_DIM, :] = oc[:, g * TQ:(g + 1) * TQ]
            psum = psum + p[:, g * TQ:(g + 1) * TQ]
        hi = psum.astype(BF16)
        lo = (psum - hi.astype(F32)).astype(BF16)
        imp_ref[...] = (jnp.dot(ov_ref[:, 0:n], hi, preferred_element_type=F32)
                        + jnp.dot(ov_ref[:, 0:n], lo, preferred_element_type=F32))

    chunk = min(n_cmp, 128)
    n_chunks = n_cmp // chunk
    need = lax.div((qi + 1) * qt + (chunk - 1), chunk)
    for c in range(1, n_chunks + 1):
        pl.when(need == c)(functools.partial(attend, c * chunk))

    imp = imp_ref[...]
    blk = lax.broadcasted_iota(jnp.int32, (n_sel, TQ), 0)
    blk_t = lax.shift_right_logical(qi * TQ + lax.broadcasted_iota(jnp.int32, (n_sel, TQ), 1), L_SEL.bit_length() - 1)
    forced = (blk == 0) | (blk > blk_t - N_LOCAL)
    v0 = jnp.where(blk > blk_t, -jnp.inf, jnp.where(forced, SEL_FORCE, imp))

    def pick(_, v):
        m = jnp.max(v, axis=0, keepdims=True)
        cand = (v == m) & (m > -jnp.inf)
        first = jnp.min(jnp.where(cand, blk, n_sel), axis=0, keepdims=True)
        return jnp.where(blk == first, -jnp.inf, v)

    v = lax.fori_loop(0, min(N_SELECT, n_sel), pick, v0)
    sel_ref[0, 0] = jnp.where((v == -jnp.inf) & (v0 > -jnp.inf), 0.0, NEG)


def _nsa_cmp(zt, qg, kc, vct, bias_c, ovt):
    b, _, s = zt.shape
    n_cmp, n_sel = s // CMP_STRIDE, s // L_SEL
    gw = NSA_GROUP * HEAD_DIM
    kern = functools.partial(_nsa_cmp_kernel, n_cmp=n_cmp, n_sel=n_sel)
    return pl.pallas_call(
        kern,
        grid=(b, NSA_KV_HEADS, s // TQ),
        in_specs=[pl.BlockSpec((1, gw, TQ), lambda i, k, j: (i, k, j)),
                  pl.BlockSpec((HEAD_DIM, TQ), lambda i, k, j: (0, 0)),
                  pl.BlockSpec((1, 1, n_cmp, HEAD_DIM), lambda i, k, j: (i, k, 0, 0)),
                  pl.BlockSpec((1, 1, HEAD_DIM, n_cmp), lambda i, k, j: (i, k, 0, 0)),
                  pl.BlockSpec((1, bias_c.shape[1], NSA_GROUP * TQ), lambda i, k, j: (k, 0, 0)),
                  pl.BlockSpec((n_sel, n_cmp), lambda i, k, j: (0, 0))],
        out_specs=[pl.BlockSpec((1, gw, TQ), lambda i, k, j: (i, k, j)),
                   pl.BlockSpec((1, 1, n_sel, TQ), lambda i, k, j: (i, k, 0, j))],
        out_shape=[jax.ShapeDtypeStruct((b, NSA_WIDTH, s), F32),
                   jax.ShapeDtypeStruct((b, NSA_KV_HEADS, n_sel, s), F32)],
        scratch_shapes=[pltpu.VMEM((n_sel, TQ), F32)],
        compiler_params=_cparams(("parallel", "parallel", "parallel"), _VMEM_LIMIT),
        name="nsa_compressed_select",
    )(zt, qg, kc, vct, bias_c, ovt)


def _nsa_main_kernel(qt_ref, qg_ref, ks_ref, vst_ref, kw_ref, vwt_ref, sel_ref, bs_ref, bw_ref, gate_ref,
                     oc_ref, o_ref, acc_ref, s_ref):
    qi = pl.program_id(1)
    q0 = pl.multiple_of(qi * TQ, TQ)
    near = SEL_PAD + TQ
    gw = NSA_GROUP * HEAD_DIM
    kvs = range(NSA_KV_HEADS)

    def expand_sel(kv, first_blk, n_blk):
        rows = [jnp.broadcast_to(sel_ref[0, kv, pl.ds(first_blk + r, 1), :], (L_SEL, TQ)) for r in range(n_blk)]
        rows = jnp.concatenate(rows, axis=0)
        return jnp.concatenate([rows] * NSA_GROUP, axis=1)

    qw = [jnp.concatenate([_q_head(qt_ref, kv * NSA_GROUP + g, qg_ref) for g in range(NSA_GROUP)], axis=1)
          for kv in kvs]
    m0 = []
    for kv in kvs:
        s = (jnp.dot(ks_ref[0, kv, pl.ds(q0, near), :], qw[kv], preferred_element_type=F32) + bs_ref[kv]
             + expand_sel(kv, qi * (TQ // L_SEL), near // L_SEL))
        m = jnp.max(s, axis=0, keepdims=True)
        p = jnp.exp2(s - m).astype(BF16)
        acc_ref[kv] = jnp.dot(vst_ref[0, kv, :, pl.ds(q0, near)], p, preferred_element_type=F32)
        m0.append(m)

    for kv in kvs:
        s = jnp.dot(kw_ref[0, kv, pl.ds(q0, WINDOW + TQ), :], qw[kv], preferred_element_type=F32) + bw_ref[0, kv]
        p = jnp.exp2(s - jnp.max(s, axis=0, keepdims=True)).astype(BF16)
        ow = jnp.dot(vwt_ref[0, kv, :, pl.ds(q0, WINDOW + TQ)], p, preferred_element_type=F32)
        ow = ow[:HEAD_DIM] * (1.0 / jnp.maximum(ow[HEAD_DIM:HEAD_DIM + 1], 1e-30))
        for g in range(NSA_GROUP):
            gates = jax.nn.sigmoid(gate_ref[0, kv, g * N_BRANCH:(g + 1) * N_BRANCH, :])
            rows = slice(kv * gw + g * HEAD_DIM, kv * gw + (g + 1) * HEAD_DIM)
            o_ref[0, rows, :] = gates[0:1] * oc_ref[0, rows, :] + gates[2:3] * ow[:, g * TQ:(g + 1) * TQ]

    def scores(c, slot):
        r0 = pl.multiple_of(c * TQ, TQ)
        mc = []
        for kv in kvs:
            s = (jnp.dot(ks_ref[0, kv, pl.ds(r0, TQ), :], qw[kv], preferred_element_type=F32)
                 + expand_sel(kv, c * (TQ // L_SEL), TQ // L_SEL))
            s_ref[slot, kv] = s
            mc.append(jnp.max(s, axis=0, keepdims=True))
        return tuple(mc)

    def consume(c, slot, m_old, mc):
        r0 = pl.multiple_of(c * TQ, TQ)
        m_out = []
        for kv in kvs:
            m_new = jnp.maximum(m_old[kv], mc[kv])
            alpha = jnp.exp2(m_old[kv] - m_new)
            p = jnp.exp2((s_ref[slot, kv] - m_new).astype(BF16))
            acc_ref[kv] = alpha * acc_ref[kv] + jnp.dot(vst_ref[0, kv, :, pl.ds(r0, TQ)], p,
                                                        preferred_element_type=F32)
            m_out.append(m_new)
        return tuple(m_out)

    first = SEL_PAD // TQ
    n_far = qi - first

    def pair(i, carry):
        m, mc = carry
        c = first + 2 * i
        mc1 = scores(c + 1, 1)
        m = consume(c, 0, m, mc)
        mc2 = scores(c + 2, 0)
        m = consume(c + 1, 1, m, mc1)
        return m, mc2

    m_far, mc_far = lax.fori_loop(0, n_far // 2, pair, (tuple(m0), scores(first, 0)))

    @pl.when((n_far > 0) & (n_far % 2 == 1))
    def _():
        consume(first + n_far - 1, 0, m_far, mc_far)

    for kv in kvs:
        os = acc_ref[kv]
        os = os[:HEAD_DIM] * (1.0 / jnp.maximum(os[HEAD_DIM:HEAD_DIM + 1], 1e-30))
        for g in range(NSA_GROUP):
            gate = jax.nn.sigmoid(gate_ref[0, kv, g * N_BRANCH + 1:g * N_BRANCH + 2, :])
            rows = slice(kv * gw + g * HEAD_DIM, kv * gw + (g + 1) * HEAD_DIM)
            o_ref[0, rows, :] = o_ref[0, rows, :] + gate * os[:, g * TQ:(g + 1) * TQ]


def _nsa_main(zt, qg, ks_p, vst_p, kw_p, vwt_p, sel_p, bias_s, bias_w, gates_t, oc_t):
    b, _, s = zt.shape
    kvh = NSA_KV_HEADS
    sp, wp = ks_p.shape[2], kw_p.shape[2]
    nb, vr = sel_p.shape[2], vst_p.shape[2]
    once = pl.Buffered(1)
    return pl.pallas_call(
        _nsa_main_kernel,
        grid=(b, s // TQ),
        in_specs=[pl.BlockSpec((1, NSA_WIDTH, TQ), lambda i, j: (i, 0, j)),
                  pl.BlockSpec((HEAD_DIM, TQ), lambda i, j: (0, 0)),
                  pl.BlockSpec((1, kvh, sp, HEAD_DIM), lambda i, j: (i, 0, 0, 0), pipeline_mode=once),
                  pl.BlockSpec((1, kvh, vr, sp), lambda i, j: (i, 0, 0, 0), pipeline_mode=once),
                  pl.BlockSpec((1, kvh, wp, HEAD_DIM), lambda i, j: (i, 0, 0, 0), pipeline_mode=once),
                  pl.BlockSpec((1, kvh, vr, wp), lambda i, j: (i, 0, 0, 0), pipeline_mode=once),
                  pl.BlockSpec((1, kvh, nb, TQ), lambda i, j: (i, 0, 0, j)),
                  pl.BlockSpec((kvh, SEL_PAD + TQ, NSA_GROUP * TQ), lambda i, j: (0, 0, 0), pipeline_mode=once),
                  pl.BlockSpec((1, kvh, WINDOW + TQ, NSA_GROUP * TQ), lambda i, j: (jnp.minimum(j, 2), 0, 0, 0)),
                  pl.BlockSpec((1, kvh, 16, TQ), lambda i, j: (i, 0, 0, j)),
                  pl.BlockSpec((1, NSA_WIDTH, TQ), lambda i, j: (i, 0, j))],
        out_specs=pl.BlockSpec((1, NSA_WIDTH, TQ), lambda i, j: (i, 0, j)),
        out_shape=jax.ShapeDtypeStruct((b, NSA_WIDTH, s), F32),
        scratch_shapes=[pltpu.VMEM((kvh, vr, NSA_GROUP * TQ), F32),
                        pltpu.VMEM((2, kvh, TQ, NSA_GROUP * TQ), F32)],
        compiler_params=_cparams(("parallel", "arbitrary"), _VMEM_LIMIT),
        name="nsa_selected_window",
    )(zt, qg, ks_p, vst_p, kw_p, vwt_p, sel_p, bias_s, bias_w, gates_t, oc_t)


def _mixout_kernel(gy_ref, yc_ref, yn_ref, h_ref, wglu_ref, go_ref, wo_ref, o_ref):
    sw = SSM_WIDTH
    ag = jnp.dot(gy_ref[0], wglu_ref[...], preferred_element_type=F32)
    ys = ag[:, :sw] * jax.nn.sigmoid(ag[:, sw:])
    ys = (_rms_rows(ys) * go_ref[:, 0:sw]).astype(BF16)
    yn = (_rms_rows(yn_ref[0]) * go_ref[:, 2 * sw:]).astype(BF16)
    out = (jnp.dot(ys, wo_ref[0:sw, :], preferred_element_type=F32)
           + jnp.dot(yc_ref[0], wo_ref[sw:2 * sw, :], preferred_element_type=F32)
           + jnp.dot(yn, wo_ref[2 * sw:, :], preferred_element_type=F32))
    o_ref[0] = h_ref[0] + out


def _mixout(gy, yc, yn, h, w_glu, g_out, w_out, tm=512):
    b, s, d = h.shape
    tok = lambda w: pl.BlockSpec((1, tm, w), lambda i, j: (i, j, 0))
    return pl.pallas_call(
        _mixout_kernel,
        grid=(b, s // tm),
        in_specs=[tok(SSM_WIDTH), tok(CONV_WIDTH), tok(NSA_WIDTH), tok(d),
                  pl.BlockSpec((SSM_WIDTH, 2 * SSM_WIDTH), lambda i, j: (0, 0)),
                  pl.BlockSpec((1, d), lambda i, j: (0, 0)),
                  pl.BlockSpec((d, d), lambda i, j: (0, 0))],
        out_specs=tok(d),
        out_shape=jax.ShapeDtypeStruct((b, s, d), F32),
        compiler_params=_cparams(("parallel", "parallel")),
        name="mix_out",
    )(gy, yc, yn, h, w_glu.astype(BF16), g_out.reshape(1, d), w_out.astype(BF16))


def _memkv_kernel(mem_ref, g_ref, w_ref, kg_ref, k_ref, v_ref):
    mn = (_rms_rows(mem_ref[0]) * g_ref[...]).astype(BF16)
    kv = jnp.dot(mn, w_ref[...], preferred_element_type=F32)
    for h in range(X_HEADS):
        cols = slice(h * HEAD_DIM, (h + 1) * HEAD_DIM)
        k_ref[0, :, cols] = (_rms_rows(kv[:, cols]) * kg_ref[...]).astype(k_ref.dtype)
    v_ref[0] = kv[:, X_WIDTH:].astype(v_ref.dtype)


def _memkv(mem, gain, w_kv, k_gain):
    b, m, d = mem.shape
    out = jax.ShapeDtypeStruct((b, m, X_WIDTH), BF16)
    return pl.pallas_call(
        _memkv_kernel,
        grid=(b,),
        in_specs=[pl.BlockSpec((1, m, d), lambda i: (i, 0, 0)),
                  pl.BlockSpec((1, d), lambda i: (0, 0)),
                  pl.BlockSpec((d, 2 * X_WIDTH), lambda i: (0, 0)),
                  pl.BlockSpec((1, HEAD_DIM), lambda i: (0, 0))],
        out_specs=[pl.BlockSpec((1, m, X_WIDTH), lambda i: (i, 0, 0))] * 2,
        out_shape=[out, out],
        compiler_params=_cparams(("parallel",)),
        name="cross_mem_kv",
    )(mem, gain.reshape(1, d), w_kv.astype(BF16), k_gain.reshape(1, HEAD_DIM))


def _cross_kernel(h_ref, g_ref, wq_ref, qg_ref, k_ref, v_ref, wo_ref, o_ref):
    h = h_ref[0]
    hn = (_rms_rows(h) * g_ref[...]).astype(BF16)
    q = jnp.dot(hn, wq_ref[...], preferred_element_type=F32)
    out = h
    for hd in range(X_HEADS):
        cols = slice(hd * HEAD_DIM, (hd + 1) * HEAD_DIM)
        qh = (_rms_rows(q[:, cols]) * qg_ref[...]).astype(BF16)
        s = lax.dot_general(qh, k_ref[0, :, cols], (((1,), (1,)), ((), ())), preferred_element_type=F32)
        p = jnp.exp(s - jnp.max(s, axis=-1, keepdims=True))
        p = p * (1.0 / jnp.sum(p, axis=-1, keepdims=True))
        o = jnp.dot(p.astype(BF16), v_ref[0, :, cols], preferred_element_type=F32)
        out = out + jnp.dot(o.astype(BF16), wo_ref[cols, :], preferred_element_type=F32)
    o_ref[0] = out


def _cross(h, gain, w_q, q_gain, k, v, w_o, tm=1024):
    b, s, d = h.shape
    m = k.shape[1]
    return pl.pallas_call(
        _cross_kernel,
        grid=(b, s // tm),
        in_specs=[pl.BlockSpec((1, tm, d), lambda i, j: (i, j, 0)),
                  pl.BlockSpec((1, d), lambda i, j: (0, 0)),
                  pl.BlockSpec((d, X_WIDTH), lambda i, j: (0, 0)),
                  pl.BlockSpec((1, HEAD_DIM), lambda i, j: (0, 0)),
                  pl.BlockSpec((1, m, X_WIDTH), lambda i, j: (i, 0, 0)),
                  pl.BlockSpec((1, m, X_WIDTH), lambda i, j: (i, 0, 0)),
                  pl.BlockSpec((X_WIDTH, d), lambda i, j: (0, 0))],
        out_specs=pl.BlockSpec((1, tm, d), lambda i, j: (i, j, 0)),
        out_shape=jax.ShapeDtypeStruct((b, s, d), F32),
        compiler_params=_cparams(("parallel", "parallel")),
        name="cross_attention",
    )(h, gain.reshape(1, d), w_q.astype(BF16), (q_gain * HEAD_DIM ** -0.5).reshape(1, HEAD_DIM), k, v,
      w_o.astype(BF16))


def _ffn_kernel(h_ref, g_ref, wg_ref, wv_ref, wd_ref, o_ref, xn_ref, acc_ref):
    f = pl.program_id(1)

    @pl.when(f == 0)
    def _():
        xn_ref[...] = (_rms_rows(h_ref[...]) * g_ref[...]).astype(BF16)
        acc_ref[...] = jnp.zeros_like(acc_ref)

    x = xn_ref[...]
    gate = jnp.dot(x, wg_ref[...], preferred_element_type=F32)
    val = jnp.dot(x, wv_ref[...], preferred_element_type=F32)
    act = (jax.nn.silu(gate) * val).astype(BF16)
    acc_ref[...] += jnp.dot(act, wd_ref[...], preferred_element_type=F32)

    @pl.when(f == pl.num_programs(1) - 1)
    def _():
        o_ref[...] = h_ref[...] + acc_ref[...]


def _ffn(h2d, gain, w_up, w_down, tm=1024):
    t, d = h2d.shape
    nf = D_FF // FF_CHUNK
    wb = w_up.astype(BF16)
    return pl.pallas_call(
        _ffn_kernel,
        grid=(t // tm, nf),
        in_specs=[pl.BlockSpec((tm, d), lambda i, f: (i, 0)),
                  pl.BlockSpec((1, d), lambda i, f: (0, 0)),
                  pl.BlockSpec((d, FF_CHUNK), lambda i, f: (0, f)),
                  pl.BlockSpec((d, FF_CHUNK), lambda i, f: (0, f + nf)),
                  pl.BlockSpec((FF_CHUNK, d), lambda i, f: (f, 0))],
        out_specs=pl.BlockSpec((tm, d), lambda i, f: (i, 0)),
        out_shape=jax.ShapeDtypeStruct((t, d), F32),
        scratch_shapes=[pltpu.VMEM((tm, d), BF16), pltpu.VMEM((tm, d), F32)],
        compiler_params=_cparams(("parallel", "arbitrary"), _VMEM_LIMIT),
        name="ffn_swiglu",
    )(h2d, gain.reshape(1, d), wb, wb, w_down.astype(BF16))


def _router_kernel(h_ref, g_ref, wr_ref, xn_ref, gate_ref, asg_ref):
    xn = _rms_rows(h_ref[...]) * g_ref[...]
    xn_ref[...] = xn.astype(BF16)
    logits = jnp.dot(xn, wr_ref[...], precision=lax.Precision.HIGHEST, preferred_element_type=F32)
    lane = lax.broadcasted_iota(jnp.int32, logits.shape, 1)
    lg = jnp.where(lane < N_EXPERTS, logits, -jnp.inf)
    m1 = jnp.max(lg, axis=-1, keepdims=True)
    i1 = jnp.min(jnp.where(lg == m1, lane, 128), axis=-1, keepdims=True)
    lg2 = jnp.where(lane == i1, -jnp.inf, lg)
    m2 = jnp.max(lg2, axis=-1, keepdims=True)
    i2 = jnp.min(jnp.where(lg2 == m2, lane, 128), axis=-1, keepdims=True)
    e = jnp.exp(m2 - m1)
    den = 1.0 + e
    gate_ref[...] = jnp.where(lane == i1, 1.0 / den, jnp.where(lane == i2, e / den, 0.0))
    asg_ref[...] = ((lane == i1) | (lane == i2)).astype(jnp.int32)


def _router(h2d, gain, w_router, tm=512):
    t, d = h2d.shape
    wr = jnp.concatenate([w_router, jnp.zeros((d, 128 - N_EXPERTS), F32)], axis=1)
    return pl.pallas_call(
        _router_kernel,
        grid=(t // tm,),
        in_specs=[pl.BlockSpec((tm, d), lambda i: (i, 0)),
                  pl.BlockSpec((1, d), lambda i: (0, 0)),
                  pl.BlockSpec((d, 128), lambda i: (0, 0))],
        out_specs=[pl.BlockSpec((tm, d), lambda i: (i, 0)),
                   pl.BlockSpec((tm, 128), lambda i: (i, 0)),
                   pl.BlockSpec((tm, 128), lambda i: (i, 0))],
        out_shape=[jax.ShapeDtypeStruct((t, d), BF16), jax.ShapeDtypeStruct((t, 128), F32),
                   jax.ShapeDtypeStruct((t, 128), jnp.int32)],
        compiler_params=_cparams(("parallel",)),
        name="moe_router",
    )(h2d, gain.reshape(1, d), wr)


def _moe_windows(rb, lo, hi, active):
    lo_l = jnp.clip(lo - rb * MOE_TB, 0, MOE_TB)
    hi_l = jnp.clip(hi - rb * MOE_TB, 0, MOE_TB)
    shift = MOE_SUB.bit_length() - 1
    w0 = jnp.minimum(lax.shift_left(lax.shift_right_logical(lo_l, shift), shift), MOE_TB - MOE_WIN)
    has = active & (hi_l > lo_l)
    return ((w0, 0, has), (MOE_TB - MOE_WIN, w0 + MOE_WIN, has & (hi_l > w0 + MOE_WIN)))


def _moe_gather_kernel(rb_ref, lo_ref, hi_ref, first_ref, tgt_ref, x_ref, o_ref):
    e, j, slot = pl.program_id(0), pl.program_id(1), pl.program_id(2)
    rb = rb_ref[e, j, slot]
    lo, hi = lo_ref[e, j], hi_ref[e, j]

    @pl.when(first_ref[e, j, slot] == 1)
    def _():
        o_ref[...] = jnp.zeros_like(o_ref)

    active = (slot == 0) | (rb != rb_ref[e, j, 0])
    tgt = tgt_ref[0]
    for start, cutoff, needed in _moe_windows(rb, lo, hi, active):

        @pl.when(needed)
        def _():
            local = start + lax.broadcasted_iota(jnp.int32, (MOE_WIN, MOE_TB), 0)
            rows = jnp.where(local >= cutoff, rb * MOE_TB + local, -2)
            onehot = jnp.where(tgt == rows, 1.0, 0.0).astype(BF16)
            part = jnp.dot(onehot, x_ref[...], preferred_element_type=F32)
            sl = pl.ds(pl.multiple_of(start, MOE_SUB), MOE_WIN)
            o_ref[sl, :] = o_ref[sl, :] + part.astype(o_ref.dtype)


def _moe_ffn_kernel(exp_ref, nused_ref, x_ref, wg_ref, wv_ref, wd_ref, o_ref, acc_ref):
    r, f = pl.program_id(0), pl.program_id(1)
    used = r < nused_ref[0]

    @pl.when(f == 0)
    def _():
        acc_ref[...] = jnp.zeros_like(acc_ref)

    @pl.when(used)
    def _():
        x = x_ref[...]
        gate = jnp.dot(x, wg_ref[0], preferred_element_type=F32)
        val = jnp.dot(x, wv_ref[0], preferred_element_type=F32)
        act = (jax.nn.silu(gate) * val).astype(BF16)
        acc_ref[...] += jnp.dot(act, wd_ref[0], preferred_element_type=F32)

    @pl.when(f == pl.num_programs(1) - 1)
    def _():
        o_ref[...] = acc_ref[...].astype(o_ref.dtype)
    del exp_ref


def _moe_scatter_kernel(rb_ref, lo_ref, hi_ref, tgt_ref, gate_ref, y_ref, h_ref, o_ref):
    j, e, slot = pl.program_id(0), pl.program_id(1), pl.program_id(2)
    rb = rb_ref[e, j, slot]
    lo, hi = lo_ref[e, j], hi_ref[e, j]

    @pl.when((e == 0) & (slot == 0))
    def _():
        o_ref[...] = h_ref[...]

    active = (slot == 0) | (rb != rb_ref[e, j, 0])
    mine = lax.broadcasted_iota(jnp.int32, tgt_ref.shape, 1) == e
    tgt = jnp.sum(jnp.where(mine, tgt_ref[...], 0), axis=1, keepdims=True)
    gate = jnp.sum(jnp.where(mine, gate_ref[...], 0.0), axis=1, keepdims=True)
    for start, cutoff, needed in _moe_windows(rb, lo, hi, active):

        @pl.when(needed)
        def _():
            local = start + lax.broadcasted_iota(jnp.int32, (MOE_TB, MOE_WIN), 1)
            rows = jnp.where(local >= cutoff, rb * MOE_TB + local, -2)
            onehot = jnp.where(tgt == rows, 1.0, 0.0).astype(BF16)
            y = y_ref[pl.ds(pl.multiple_of(start, MOE_SUB), MOE_WIN), :]
            o_ref[...] = o_ref[...] + gate * jnp.dot(onehot, y, preferred_element_type=F32)


def _moe(h2d, gain, w_router, w_up, w_down):
    t, d = h2d.shape
    tb = MOE_TB
    nj = t // tb
    n_rb = (t * TOP_K) // tb + N_EXPERTS
    xn, gates, asg = _router(h2d, gain, w_router)
    asg = asg[:, :N_EXPERTS]
    gates = gates[:, :N_EXPERTS]
    cs = jnp.cumsum(asg, axis=0)
    rank = cs - asg
    counts = cs[-1]
    padded = (counts + tb - 1) // tb * tb
    pad_end = jnp.cumsum(padded)
    start_p = pad_end - padded
    tgt = jnp.where(asg == 1, start_p[None, :] + rank, -1).astype(jnp.int32)
    cb = jnp.concatenate([jnp.zeros((1, N_EXPERTS), jnp.int32), cs[tb - 1::tb]], axis=0)
    lo = (start_p[None, :] + cb[:-1]).T.astype(jnp.int32)
    hi = (start_p[None, :] + cb[1:]).T.astype(jnp.int32)
    rb0 = lo // tb
    rb1 = jnp.maximum(rb0, (hi - 1) // tb)
    rb = jnp.stack([rb0, rb1], axis=-1).astype(jnp.int32)
    flat = rb.reshape(-1)
    first = jnp.concatenate([jnp.ones((1,), jnp.int32), (flat[1:] != flat[:-1]).astype(jnp.int32)])
    first = first.reshape(N_EXPERTS, nj, 2)
    n_used = (pad_end[-1] // tb).astype(jnp.int32).reshape(1)
    blk_exp = jnp.minimum(jnp.searchsorted(pad_end, jnp.arange(n_rb) * tb, side='right'),
                          N_EXPERTS - 1).astype(jnp.int32)

    xs = pl.pallas_call(
        _moe_gather_kernel,
        grid_spec=pltpu.PrefetchScalarGridSpec(
            num_scalar_prefetch=4,
            grid=(N_EXPERTS, nj, 2),
            in_specs=[pl.BlockSpec((1, 1, tb), lambda e, j, s, *_: (e, 0, j)),
                      pl.BlockSpec((tb, d), lambda e, j, s, *_: (j, 0))],
            out_specs=pl.BlockSpec((tb, d), lambda e, j, s, rb_ref, *_: (rb_ref[e, j, s], 0))),
        out_shape=jax.ShapeDtypeStruct((n_rb * tb, d), BF16),
        compiler_params=_cparams(("arbitrary", "arbitrary", "arbitrary"), _VMEM_LIMIT),
        name="moe_gather",
    )(rb, lo, hi, first, tgt.T.reshape(N_EXPERTS, 1, t), xn)

    nf = D_FF // FF_CHUNK
    wub = w_up.astype(BF16)
    ys = pl.pallas_call(
        _moe_ffn_kernel,
        grid_spec=pltpu.PrefetchScalarGridSpec(
            num_scalar_prefetch=2,
            grid=(n_rb, nf),
            in_specs=[pl.BlockSpec((tb, d), lambda r, f, *_: (r, 0)),
                      pl.BlockSpec((1, d, FF_CHUNK), lambda r, f, ex, nu: (ex[r], 0, f)),
                      pl.BlockSpec((1, d, FF_CHUNK), lambda r, f, ex, nu: (ex[r], 0, f + nf)),
                      pl.BlockSpec((1, FF_CHUNK, d), lambda r, f, ex, nu: (ex[r], f, 0))],
            out_specs=pl.BlockSpec((tb, d), lambda r, f, *_: (r, 0)),
            scratch_shapes=[pltpu.VMEM((tb, d), F32)]),
        out_shape=jax.ShapeDtypeStruct((n_rb * tb, d), BF16),
        compiler_params=_cparams(("arbitrary", "arbitrary"), _VMEM_LIMIT),
        name="moe_expert_ffn",
    )(blk_exp, n_used, xs, wub, wub, w_down.astype(BF16))

    return pl.pallas_call(
        _moe_scatter_kernel,
        grid_spec=pltpu.PrefetchScalarGridSpec(
            num_scalar_prefetch=3,
            grid=(nj, N_EXPERTS, 2),
            in_specs=[pl.BlockSpec((tb, N_EXPERTS), lambda j, e, s, *_: (j, 0)),
                      pl.BlockSpec((tb, N_EXPERTS), lambda j, e, s, *_: (j, 0)),
                      pl.BlockSpec((tb, d), lambda j, e, s, rb_ref, *_: (rb_ref[e, j, s], 0)),
                      pl.BlockSpec((tb, d), lambda j, e, s, *_: (j, 0))],
            out_specs=pl.BlockSpec((tb, d), lambda j, e, s, *_: (j, 0))),
        out_shape=jax.ShapeDtypeStruct((t, d), F32),
        compiler_params=_cparams(("arbitrary", "arbitrary", "arbitrary"), _VMEM_LIMIT),
        name="moe_scatter",
    )(rb, lo, hi, tgt, gates, ys, h2d)


_COL_CONV, _COL_SSM, _COL_KC, _COL_VC, _COL_KS, _COL_KW = 0, 512, 768, 896, 1024, 1152
_ROW_Q, _ROW_VS, _ROW_VW, _ROW_G = 0, 512, 640, 768


def _split_w_in(w_in):
    kvw = NSA_KV_HEADS * HEAD_DIM
    cuts = np.cumsum([0, SSM_WIDTH, 2 * CONV_WIDTH, NSA_WIDTH] + [kvw] * 6 + [N_BRANCH * NSA_HEADS])
    seg = lambda i: w_in[:, cuts[i]:cuts[i + 1]]
    ssm, conv, q, k_c, v_c, k_s, v_s, k_w, v_w, gate = (seg(i) for i in range(10))
    w_tok = jnp.concatenate([conv, ssm, k_c, v_c, k_s, k_w], axis=1).astype(BF16)
    gate = jnp.concatenate([gate, jnp.zeros((w_in.shape[0], 8), F32)], axis=1)
    w_t = jnp.concatenate([q, v_s, v_w, gate], axis=1).T.astype(BF16)
    return w_tok, w_t


def _layer_mixers(h, p, t5_tiles):
    b, s, d = h.shape
    w_tok, w_t = _split_w_in(p['w_in'])
    ztok, zt = _proj(h, p['norm_mix'], w_tok, w_t)

    n_chunks = s // SSM_CHUNK
    u = ztok[:, :, _COL_SSM:_COL_SSM + SSM_WIDTH].astype(BF16)
    xg = u.reshape(b, n_chunks, SSM_CHUNK, SSM_GROUPS, SSM_GROUP).transpose(3, 1, 0, 2, 4)
    xg = xg.reshape(SSM_GROUPS, n_chunks * b, SSM_CHUNK * SSM_GROUP)
    tables = _s5_tables(p['ssm_lambda_re'], p['ssm_lambda_im'], p['ssm_log_dt'], p['ssm_b_re'], p['ssm_b_im'],
                        p['ssm_c_re'], p['ssm_c_im'], p['ssm_d'])
    gy = _s5(xg, tables, b)
    gy = gy.reshape(SSM_GROUPS, n_chunks, b, SSM_CHUNK, SSM_GROUP).transpose(2, 1, 3, 0, 4).reshape(b, s, SSM_WIDTH)

    g_out = p['mix_out_norm']
    yc = _conv(ztok, p['conv_w_dw'], p['conv_b_dw'], p['conv_ln_g'], p['conv_ln_b'], p['conv_w_pw'],
               g_out[SSM_WIDTH:SSM_WIDTH + CONV_WIDTH])

    kvw = NSA_KV_HEADS * HEAD_DIM
    n_cmp = s // CMP_STRIDE
    k_norm = p['nsa_k_norm']
    kc, vct = _compress(ztok, _COL_KC, _COL_VC, p['nsa_cmp_pe'], p['nsa_cmp_w1'], p['nsa_cmp_w2'], k_norm[0])
    ks, kw = _knorm(ztok, _COL_KS, _COL_KW, k_norm[1], k_norm[2])
    qg = jnp.broadcast_to((p['nsa_q_norm'] * (HEAD_DIM ** -0.5 * LOG2E))[:, None], (HEAD_DIM, TQ))
    bias_c, bias_s, bias_w = t5_tiles
    n_sel = s // L_SEL
    cs_ = np.arange(n_cmp) * CMP_STRIDE
    ss_ = np.arange(n_sel) * L_SEL
    ov = np.maximum(np.minimum(cs_[:, None] + L_CMP, ss_[None, :] + L_SEL) - np.maximum(cs_[:, None], ss_[None, :]), 0)
    ovt = jnp.asarray((ov.astype(np.float32) / L_CMP).T, BF16)
    oc_t, sel = _nsa_cmp(zt, qg, kc, vct, bias_c, ovt)

    front = lambda x, n, axis: jnp.pad(x, [(n, 0) if a == axis else (0, 0) for a in range(x.ndim)])
    ones_rows = jnp.concatenate([jnp.ones((b, NSA_KV_HEADS, 1, s), BF16),
                                 jnp.zeros((b, NSA_KV_HEADS, 15, s), BF16)], axis=2)
    heads_t = lambda rows: jnp.concatenate(
        [zt[:, rows:rows + kvw, :].astype(BF16).reshape(b, NSA_KV_HEADS, HEAD_DIM, s), ones_rows], axis=2)
    ks_p = front(ks, SEL_PAD, 2)
    kw_p = front(kw, WINDOW, 2)
    vst_p = front(heads_t(_ROW_VS), SEL_PAD, 3)
    vwt_p = front(heads_t(_ROW_VW), WINDOW, 3)
    sel_p = jnp.pad(sel, ((0, 0), (0, 0), (SEL_PAD // L_SEL, 0), (0, 0)), constant_values=NEG)
    gl = zt[:, _ROW_G:_ROW_G + N_BRANCH * NSA_HEADS, :].reshape(b, NSA_KV_HEADS, NSA_GROUP * N_BRANCH, s)
    gates_t = jnp.pad(gl, ((0, 0), (0, 0), (0, 16 - NSA_GROUP * N_BRANCH), (0, 0)))
    yn_t = _nsa_main(zt, qg, ks_p, vst_p, kw_p, vwt_p, sel_p, bias_s, bias_w, gates_t, oc_t)
    yn = yn_t.transpose(0, 2, 1)

    return _mixout(gy, yc, yn, h, p['ssm_w_glu'], g_out, p['w_out'])


def kernel(x, mem, norm_mix, w_in, ssm_lambda_re, ssm_lambda_im, ssm_log_dt, ssm_b_re, ssm_b_im, ssm_c_re, ssm_c_im, ssm_d, ssm_w_glu, conv_w_dw, conv_b_dw, conv_ln_g, conv_ln_b, conv_w_pw, nsa_q_norm, nsa_k_norm, nsa_cmp_pe, nsa_cmp_w1, nsa_cmp_w2, mix_out_norm, w_out, t5_table, norm_cross, norm_mem, x_w_q, x_w_kv, x_q_norm, x_k_norm, x_w_o, norm_ffn, ffn_w_up, ffn_w_down, moe_router, moe_w_up, moe_w_down):
    b, s, d = x.shape
    depth = w_in.shape[0]
    per_layer = dict(norm_mix=norm_mix, w_in=w_in, ssm_lambda_re=ssm_lambda_re, ssm_lambda_im=ssm_lambda_im,
                     ssm_log_dt=ssm_log_dt, ssm_b_re=ssm_b_re, ssm_b_im=ssm_b_im, ssm_c_re=ssm_c_re,
                     ssm_c_im=ssm_c_im, ssm_d=ssm_d, ssm_w_glu=ssm_w_glu, conv_w_dw=conv_w_dw,
                     conv_b_dw=conv_b_dw, conv_ln_g=conv_ln_g, conv_ln_b=conv_ln_b, conv_w_pw=conv_w_pw,
                     nsa_q_norm=nsa_q_norm, nsa_k_norm=nsa_k_norm, nsa_cmp_pe=nsa_cmp_pe, nsa_cmp_w1=nsa_cmp_w1,
                     nsa_cmp_w2=nsa_cmp_w2, mix_out_norm=mix_out_norm, w_out=w_out)
    t5_tiles = _nsa_bias_tiles(t5_table, s)
    h = x
    for layer in range(depth):
        p = {k: v[layer] for k, v in per_layer.items()}
        h = _layer_mixers(h, p, t5_tiles)
        mk, mv = _memkv(mem, norm_mem[layer], x_w_kv[layer], x_k_norm[layer])
        h = _cross(h, norm_cross[layer], x_w_q[layer], x_q_norm[layer], mk, mv, x_w_o[layer])
        h2d = h.reshape(b * s, d)
        if layer % 2 == 0:
            h2d = _ffn(h2d, norm_ffn[layer], ffn_w_up[layer // 2], ffn_w_down[layer // 2])
        else:
            h2d = _moe(h2d, norm_ffn[layer], moe_router[layer // 2], moe_w_up[layer // 2], moe_w_down[layer // 2])
        h = h2d.reshape(b, s, d)
    return h
```

```python
import functools
import math

import jax
import jax.numpy as jnp
import numpy as np
from jax import lax
from jax.experimental import pallas as pl
from jax.experimental.pallas import tpu as pltpu

F32 = jnp.float32
BF16 = jnp.bfloat16

D_MODEL = 1024
HEAD_DIM = 64
SSM_WIDTH = 256
SSM_GROUP = 16
SSM_GROUPS = 16
SSM_STATE = 64
SSM_CHUNK = 16
CONV_WIDTH = 256
CONV_K = 31
CONV_HALO = 32
NSA_WIDTH = 512
NSA_HEADS = 8
NSA_KV_HEADS = 2
NSA_GROUP = 4
N_BRANCH = 3
L_CMP = 32
CMP_STRIDE = 16
L_SEL = 64
N_SELECT = 16
N_LOCAL = 2
WINDOW = 512
SEL_FORCE = 1e6
T5_BUCKETS = 32
T5_MAX_DIST = 128
X_HEADS = 4
X_WIDTH = 256
D_FF = 2816
N_EXPERTS = 8
TOP_K = 2
EPS = 1e-6
NEG = -1e30
LOG2E = math.log2(math.e)

TQ = 256
SEL_PAD = 256
FF_CHUNK = 256
MOE_TB = 1024
MOE_SUB = 128
MOE_WIN = 512

_VMEM_LIMIT = 56 * 1024 * 1024


def _cparams(sem, vmem=None):
    return pltpu.CompilerParams(dimension_semantics=sem, vmem_limit_bytes=vmem)


def _rms_rows(x):
    return x * lax.rsqrt(jnp.mean(x * x, axis=-1, keepdims=True) + EPS)


def _proj_kernel(x_ref, g_ref, wtok_ref, wt_ref, ztok_ref, zt_ref):
    xn = (_rms_rows(x_ref[0]) * g_ref[...]).astype(BF16)
    ztok_ref[0] = jnp.dot(xn, wtok_ref[...], preferred_element_type=F32)
    zt_ref[0] = lax.dot_general(wt_ref[...], xn, (((1,), (1,)), ((), ())), preferred_element_type=F32)


def _proj(h, gain, w_tok, w_t, tm=512):
    b, s, d = h.shape
    ntok, nt = w_tok.shape[1], w_t.shape[0]
    return pl.pallas_call(
        _proj_kernel,
        grid=(b, s // tm),
        in_specs=[pl.BlockSpec((1, tm, d), lambda i, j: (i, j, 0)),
                  pl.BlockSpec((1, d), lambda i, j: (0, 0)),
                  pl.BlockSpec((d, ntok), lambda i, j: (0, 0)),
                  pl.BlockSpec((nt, d), lambda i, j: (0, 0))],
        out_specs=[pl.BlockSpec((1, tm, ntok), lambda i, j: (i, j, 0)),
                   pl.BlockSpec((1, nt, tm), lambda i, j: (i, 0, j))],
        out_shape=[jax.ShapeDtypeStruct((b, s, ntok), F32), jax.ShapeDtypeStruct((b, nt, s), F32)],
        compiler_params=_cparams(("parallel", "parallel"), _VMEM_LIMIT),
        name="proj",
    )(h, gain.reshape(1, d), w_tok, w_t)


def _s5_tables(lam_re, lam_im, log_dt, b_re, b_im, c_re, c_im, d_skip):
    L, H, P = SSM_CHUNK, SSM_GROUP, SSM_STATE
    dt = jnp.exp(log_dt.astype(F32))[:, None]
    lr, li = lam_re.astype(F32), lam_im.astype(F32)
    mag = jnp.exp(lr * dt)
    ar, ai = mag * jnp.cos(li * dt), mag * jnp.sin(li * dt)
    den = lr * lr + li * li
    fr = ((ar - 1.0) * lr + ai * li) / den
    fi = (ai * lr - (ar - 1.0) * li) / den
    bbr = fr[..., None] * b_re - fi[..., None] * b_im
    bbi = fr[..., None] * b_im + fi[..., None] * b_re
    j = jnp.arange(L + 1, dtype=F32)[:, None, None]
    pmag = jnp.exp(lr[None] * dt[None] * j)
    pr, pi = pmag * jnp.cos(li[None] * dt[None] * j), pmag * jnp.sin(li[None] * dt[None] * j)
    cbr = c_re[:, :, :, None] * bbr[:, None, :, :] - c_im[:, :, :, None] * bbi[:, None, :, :]
    cbi = c_re[:, :, :, None] * bbi[:, None, :, :] + c_im[:, :, :, None] * bbr[:, None, :, :]
    hp = lax.Precision.HIGHEST
    kj = (jnp.einsum('jgp,ghpk->jghk', pr[:L], cbr, precision=hp)
          - jnp.einsum('jgp,ghpk->jghk', pi[:L], cbi, precision=hp))
    lag = np.arange(L)[None, :] - np.arange(L)[:, None]
    place = (lag[None] == np.arange(L)[:, None, None]).astype(np.float32)
    kt = jnp.einsum('jab,jghk->abghk', place, kj, precision=hp)
    kt = kt + (jnp.eye(L)[:, :, None, None, None] * (jnp.eye(H)[None, None, None] * d_skip[None, None, :, :, None]))
    tmat = kt.transpose(2, 0, 4, 1, 3).reshape(SSM_GROUPS, L * H, L * H)
    qr, qi = pr[:L][::-1], pi[:L][::-1]
    wre = qr[..., None] * bbr[None] - qi[..., None] * bbi[None]
    wim = qr[..., None] * bbi[None] + qi[..., None] * bbr[None]
    wre = wre.transpose(1, 0, 3, 2).reshape(SSM_GROUPS, L * H, P)
    wim = wim.transpose(1, 0, 3, 2).reshape(SSM_GROUPS, L * H, P)
    w1 = jnp.concatenate([wre, wim], axis=-1)
    w2 = jnp.concatenate([wim, wre], axis=-1)
    sr, si = pr[1:], pi[1:]
    vr = c_re[None] * sr[:, :, None, :] - c_im[None] * si[:, :, None, :]
    vi = c_re[None] * si[:, :, None, :] + c_im[None] * sr[:, :, None, :]
    vmat = jnp.concatenate([vr, -vi], axis=-1).transpose(1, 3, 0, 2).reshape(SSM_GROUPS, 2 * P, L * H)
    a_r, a_i = pr[L], pi[L]
    am = jnp.stack([jnp.concatenate([a_r, a_r], -1), jnp.concatenate([-a_i, a_i], -1),
                    jnp.concatenate([a_i, -a_i], -1)], axis=1)
    am = jnp.concatenate([am, jnp.zeros((SSM_GROUPS, 5, 2 * P), F32)], axis=1)
    return tmat.astype(BF16), w1.astype(BF16), w2.astype(BF16), vmat.astype(BF16), am


def _s5_perm_tables():
    L, G, H = SSM_CHUNK, SSM_GROUPS, SSM_GROUP
    i = jnp.arange(L * SSM_WIDTH)
    ti, gi, hi = i // SSM_WIDTH, (i // H) % G, i % H
    o = jnp.arange(L * H)
    pack = ((gi[None, :, None] == jnp.arange(G)[:, None, None]) & (ti[None, :, None] == (o // H)[None, None, :])
            & (hi[None, :, None] == (o % H)[None, None, :]))
    j = jnp.arange(G * L * H)
    gj, tj, hj = j // (L * H), (j // H) % L, j % H
    w = jnp.arange(SSM_WIDTH)
    unpack = ((tj[None, :, None] == jnp.arange(L)[:, None, None]) & (gj[None, :, None] == (w // H)[None, None, :])
              & (hj[None, :, None] == (w % H)[None, None, :]))
    return pack.astype(BF16), unpack.astype(BF16)


def _s5_pack_kernel(ulo_ref, uhi_ref, p_ref, x_ref, u2_ref, *, n_chunks):
    half = SSM_WIDTH // 2

    @pl.when(pl.program_id(1) == 0)
    def _():
        for t in range(SSM_CHUNK):
            for k, u_ref in enumerate((ulo_ref, uhi_ref)):
                u2_ref[:, t * SSM_WIDTH + k * half:t * SSM_WIDTH + (k + 1) * half] = (
                    u_ref[0, pl.ds(t, n_chunks, stride=SSM_CHUNK), :].astype(BF16))

    x_ref[0] = jnp.dot(u2_ref[...], p_ref[0], preferred_element_type=F32).astype(x_ref.dtype)


def _s5_pack(ztok, col, pack):
    b, s, _ = ztok.shape
    n_chunks = s // SSM_CHUNK
    lw, lh = SSM_CHUNK * SSM_WIDTH, SSM_CHUNK * SSM_GROUP
    half = SSM_WIDTH // 2
    return pl.pallas_call(
        functools.partial(_s5_pack_kernel, n_chunks=n_chunks),
        grid=(b, SSM_GROUPS),
        in_specs=[pl.BlockSpec((1, s, half), lambda i, g: (i, 0, col // half)),
                  pl.BlockSpec((1, s, half), lambda i, g: (i, 0, col // half + 1)),
                  pl.BlockSpec((1, lw, lh), lambda i, g: (g, 0, 0))],
        out_specs=pl.BlockSpec((1, n_chunks, lh), lambda i, g: (g, i, 0)),
        out_shape=jax.ShapeDtypeStruct((SSM_GROUPS, b * n_chunks, lh), BF16),
        scratch_shapes=[pltpu.VMEM((n_chunks, lw), BF16)],
        compiler_params=_cparams(("parallel", "arbitrary"), _VMEM_LIMIT),
        name="s5_pack",
    )(ztok, ztok, pack)


def _s5_unpack_kernel(g_ref, r_ref, o_ref, *, n_chunks):
    t = pl.program_id(1)
    rows = jnp.concatenate([g_ref[g] for g in range(SSM_GROUPS)], axis=1)
    y = jnp.dot(rows, r_ref[0], preferred_element_type=F32)
    half = SSM_WIDTH // 2
    for k in range(SSM_CHUNK):
        @pl.when(t == k)
        def _():
            for part in range(2):
                o_ref[part, 0, pl.ds(k, n_chunks, stride=SSM_CHUNK), :] = y[:, part * half:(part + 1) * half]


def _s5_unpack(gy, unpack, bsz):
    g, r, lh = gy.shape
    n_chunks = r // bsz
    half = SSM_WIDTH // 2
    return pl.pallas_call(
        functools.partial(_s5_unpack_kernel, n_chunks=n_chunks),
        grid=(bsz, SSM_CHUNK),
        in_specs=[pl.BlockSpec((g, n_chunks, lh), lambda i, t: (0, i, 0)),
                  pl.BlockSpec((1, g * lh, SSM_WIDTH), lambda i, t: (t, 0, 0))],
        out_specs=pl.BlockSpec((2, 1, n_chunks * SSM_CHUNK, half), lambda i, t: (0, i, 0, 0)),
        out_shape=jax.ShapeDtypeStruct((2, bsz, n_chunks * SSM_CHUNK, half), F32),
        compiler_params=_cparams(("parallel", "arbitrary"), _VMEM_LIMIT),
        name="s5_unpack",
    )(gy, unpack)


def _s5_kernel(x_ref, t_ref, w1_ref, w2_ref, v_ref, a_ref, o_ref, s1_ref, s2_ref, xin_ref, *, bsz, n_chunks):
    x = x_ref[0]
    s1_ref[...] = jnp.dot(x, w1_ref[0], preferred_element_type=F32)
    s2_ref[...] = jnp.dot(x, w2_ref[0], preferred_element_type=F32)
    a1, a2, a3 = a_ref[0, 0:1, :], a_ref[0, 1:2, :], a_ref[0, 2:3, :]

    def step(c, carry):
        ps, qs = carry
        new_p, new_q = [], []
        for bi in range(bsz):
            row = pl.ds(bi * n_chunks + c, 1)
            xin_ref[row, :] = ps[bi]
            new_p.append(ps[bi] * a1 + qs[bi] * a2 + s1_ref[row, :])
            new_q.append(qs[bi] * a1 + ps[bi] * a3 + s2_ref[row, :])
        return tuple(new_p), tuple(new_q)

    zero = tuple(jnp.zeros((1, 2 * SSM_STATE), F32) for _ in range(bsz))
    lax.fori_loop(0, n_chunks, step, (zero, zero))
    y = (jnp.dot(x, t_ref[0], preferred_element_type=F32)
         + jnp.dot(xin_ref[...].astype(BF16), v_ref[0], preferred_element_type=F32))
    o_ref[0] = jax.nn.gelu(y).astype(o_ref.dtype)


def _s5(xg, tables, bsz):
    tmat, w1, w2, vmat, am = tables
    g, r, lh = xg.shape
    p2 = 2 * SSM_STATE
    kern = functools.partial(_s5_kernel, bsz=bsz, n_chunks=r // bsz)
    return pl.pallas_call(
        kern,
        grid=(g,),
        in_specs=[pl.BlockSpec((1, r, lh), lambda i: (i, 0, 0)),
                  pl.BlockSpec((1, lh, lh), lambda i: (i, 0, 0)),
                  pl.BlockSpec((1, lh, p2), lambda i: (i, 0, 0)),
                  pl.BlockSpec((1, lh, p2), lambda i: (i, 0, 0)),
                  pl.BlockSpec((1, p2, lh), lambda i: (i, 0, 0)),
                  pl.BlockSpec((1, 8, p2), lambda i: (i, 0, 0))],
        out_specs=pl.BlockSpec((1, r, lh), lambda i: (i, 0, 0)),
        out_shape=jax.ShapeDtypeStruct((g, r, lh), BF16),
        scratch_shapes=[pltpu.VMEM((r, p2), F32), pltpu.VMEM((r, p2), F32), pltpu.VMEM((r, p2), F32)],
        compiler_params=_cparams(("parallel",), _VMEM_LIMIT),
        name="s5_scan",
    )(xg, tmat, w1, w2, vmat, am)


def _conv_kernel(z_ref, halo_ref, wdw_ref, bdw_ref, lng_ref, lnb_ref, wpw_ref, go_ref, o_ref, buf_ref, *, tt):
    first = pl.program_id(1) == 0
    zc = z_ref[0]
    zh = halo_ref[0]
    vh = zh[:, :CONV_WIDTH] * jax.nn.sigmoid(zh[:, CONV_WIDTH:])
    buf_ref[0:CONV_HALO, :] = vh * jnp.where(first, 0.0, 1.0)
    buf_ref[CONV_HALO:CONV_HALO + tt, :] = zc[:, :CONV_WIDTH] * jax.nn.sigmoid(zc[:, CONV_WIDTH:])
    acc = jnp.zeros((tt, CONV_WIDTH), F32) + bdw_ref[...]
    for k in range(CONV_K):
        acc = acc + wdw_ref[k:k + 1, :] * buf_ref[pl.ds(CONV_HALO - (CONV_K - 1) + k, tt), :]
    mu = jnp.mean(acc, axis=-1, keepdims=True)
    var = jnp.mean(jnp.square(acc - mu), axis=-1, keepdims=True)
    y = (acc - mu) * lax.rsqrt(var + EPS) * lng_ref[...] + lnb_ref[...]
    y = jax.nn.silu(y)
    y = jnp.dot(y.astype(BF16), wpw_ref[...], preferred_element_type=F32)
    o_ref[0] = (_rms_rows(y) * go_ref[...]).astype(o_ref.dtype)


def _conv(ztok, w_dw, b_dw, ln_g, ln_b, w_pw, g_out, tt=512):
    b, s, _ = ztok.shape
    cw = CONV_WIDTH
    hb = tt // CONV_HALO
    kern = functools.partial(_conv_kernel, tt=tt)
    row = lambda v: v.reshape(1, cw)
    return pl.pallas_call(
        kern,
        grid=(b, s // tt),
        in_specs=[pl.BlockSpec((1, tt, 2 * cw), lambda i, j: (i, j, 0)),
                  pl.BlockSpec((1, CONV_HALO, 2 * cw), lambda i, j: (i, jnp.maximum(j * hb - 1, 0), 0)),
                  pl.BlockSpec((CONV_K + 1, cw), lambda i, j: (0, 0)),
                  pl.BlockSpec((1, cw), lambda i, j: (0, 0)),
                  pl.BlockSpec((1, cw), lambda i, j: (0, 0)),
                  pl.BlockSpec((1, cw), lambda i, j: (0, 0)),
                  pl.BlockSpec((cw, cw), lambda i, j: (0, 0)),
                  pl.BlockSpec((1, cw), lambda i, j: (0, 0))],
        out_specs=pl.BlockSpec((1, tt, cw), lambda i, j: (i, j, 0)),
        out_shape=jax.ShapeDtypeStruct((b, s, cw), BF16),
        scratch_shapes=[pltpu.VMEM((CONV_HALO + tt, cw), F32)],
        compiler_params=_cparams(("parallel", "arbitrary")),
        name="conv_mixer",
    )(ztok, ztok, jnp.concatenate([w_dw, jnp.zeros((1, cw), F32)], 0), row(b_dw), row(ln_g), row(ln_b),
      w_pw.astype(BF16), row(g_out))


def _knorm_kernel(ks_ref, kw_ref, gs_ref, gw_ref, os_ref, ow_ref):
    for src, g_ref, dst in ((ks_ref, gs_ref, os_ref), (kw_ref, gw_ref, ow_ref)):
        x = src[0]
        for h in range(NSA_KV_HEADS):
            xh = x[:, h * HEAD_DIM:(h + 1) * HEAD_DIM]
            dst[0, h] = (_rms_rows(xh) * g_ref[...]).astype(dst.dtype)


def _knorm(ztok, col_s, col_w, gain_s, gain_w, tt=512):
    b, s, _ = ztok.shape
    kw = NSA_KV_HEADS * HEAD_DIM
    out = jax.ShapeDtypeStruct((b, NSA_KV_HEADS, s, HEAD_DIM), BF16)
    ospec = pl.BlockSpec((1, NSA_KV_HEADS, tt, HEAD_DIM), lambda i, j: (i, 0, j, 0))
    return pl.pallas_call(
        _knorm_kernel,
        grid=(b, s // tt),
        in_specs=[pl.BlockSpec((1, tt, kw), lambda i, j: (i, j, col_s // kw)),
                  pl.BlockSpec((1, tt, kw), lambda i, j: (i, j, col_w // kw)),
                  pl.BlockSpec((1, HEAD_DIM), lambda i, j: (0, 0)),
                  pl.BlockSpec((1, HEAD_DIM), lambda i, j: (0, 0))],
        out_specs=[ospec, ospec],
        out_shape=[out, out],
        compiler_params=_cparams(("parallel", "parallel")),
        name="nsa_key_norm",
    )(ztok, ztok, gain_s.reshape(1, HEAD_DIM), gain_w.reshape(1, HEAD_DIM))


def _compress_kernel(k_ref, v_ref, wka_ref, wkb_ref, ck_ref, w2k_ref, gk_ref,
                     wva_ref, wvb_ref, cv_ref, w2v_ref, ko_ref, vo_ref):
    hi = lax.Precision.HIGHEST
    n = k_ref.shape[1] // CMP_STRIDE
    kvw = k_ref.shape[2]
    nt = (((1,), (1,)), ((), ()))
    a, bm = jnp.zeros((n, kvw), F32), jnp.zeros((n, kvw), F32)
    at, bt = jnp.zeros((kvw, n), F32), jnp.zeros((kvw, n), F32)
    for l in range(CMP_STRIDE):
        kl = k_ref[0, pl.ds(l, n, stride=CMP_STRIDE), :]
        vl = v_ref[0, pl.ds(l, n, stride=CMP_STRIDE), :]
        a = a + jnp.dot(kl, wka_ref[l], precision=hi, preferred_element_type=F32)
        bm = bm + jnp.dot(kl, wkb_ref[l], precision=hi, preferred_element_type=F32)
        at = at + lax.dot_general(wva_ref[l], vl, nt, precision=hi, preferred_element_type=F32)
        bt = bt + lax.dot_general(wvb_ref[l], vl, nt, precision=hi, preferred_element_type=F32)
    pre = a + pltpu.roll(bm, n - 1, 0) + ck_ref[...]
    kc = jnp.dot(jax.nn.gelu(pre), w2k_ref[...], precision=hi, preferred_element_type=F32)
    for h in range(NSA_KV_HEADS):
        kh = kc[:, h * HEAD_DIM:(h + 1) * HEAD_DIM]
        ko_ref[0, h] = (_rms_rows(kh) * gk_ref[...]).astype(ko_ref.dtype)
    pre_t = at + pltpu.roll(bt, n - 1, 1) + cv_ref[...]
    vt = jnp.dot(w2v_ref[...], jax.nn.gelu(pre_t), precision=hi, preferred_element_type=F32)
    for h in range(NSA_KV_HEADS):
        vo_ref[0, h] = vt[h * HEAD_DIM:(h + 1) * HEAD_DIM, :].astype(vo_ref.dtype)


def _blockdiag2(w):
    z = jnp.zeros_like(w)
    return jnp.concatenate([jnp.concatenate([w, z], 1), jnp.concatenate([z, w], 1)], 0)


def _compress(ztok, col_k, col_v, pe, w1, w2, k_gain):
    b, s, _ = ztok.shape
    n = s // CMP_STRIDE
    hd, kvw = HEAD_DIM, NSA_KV_HEADS * HEAD_DIM
    hp = lax.Precision.HIGHEST

    def expand(w):
        wl = w.reshape(L_CMP, hd, hd)
        e = wl[:, None, :, None, :] * jnp.eye(NSA_KV_HEADS, dtype=F32)[None, :, None, :, None]
        e = e.reshape(L_CMP, kvw, kvw)
        return e[:CMP_STRIDE], e[CMP_STRIDE:]

    wka, wkb = expand(w1[0])
    wva, wvb = expand(w1[1])
    ck = jnp.tile(jnp.dot(pe[0].reshape(1, L_CMP * hd), w1[0], precision=hp), (1, NSA_KV_HEADS))
    cv = jnp.tile(jnp.dot(pe[1].reshape(1, L_CMP * hd), w1[1], precision=hp), (1, NSA_KV_HEADS)).T
    full = lambda shape: pl.BlockSpec(shape, lambda i: tuple(0 for _ in shape))
    return pl.pallas_call(
        _compress_kernel,
        grid=(b,),
        in_specs=[pl.BlockSpec((1, s, kvw), lambda i: (i, 0, col_k // kvw)),
                  pl.BlockSpec((1, s, kvw), lambda i: (i, 0, col_v // kvw)),
                  full((CMP_STRIDE, kvw, kvw)), full((CMP_STRIDE, kvw, kvw)), full((1, kvw)), full((kvw, kvw)),
                  full((1, hd)),
                  full((CMP_STRIDE, kvw, kvw)), full((CMP_STRIDE, kvw, kvw)), full((kvw, 1)), full((kvw, kvw))],
        out_specs=[pl.BlockSpec((1, NSA_KV_HEADS, n, hd), lambda i: (i, 0, 0, 0)),
                   pl.BlockSpec((1, NSA_KV_HEADS, hd, n), lambda i: (i, 0, 0, 0))],
        out_shape=[jax.ShapeDtypeStruct((b, NSA_KV_HEADS, n, hd), BF16),
                   jax.ShapeDtypeStruct((b, NSA_KV_HEADS, hd, n), BF16)],
        compiler_params=_cparams(("parallel",), _VMEM_LIMIT),
        name="nsa_compress",
    )(ztok, ztok, wka, wkb, ck, _blockdiag2(w2[0]), k_gain.reshape(1, hd),
      wva.transpose(0, 2, 1), wvb.transpose(0, 2, 1), cv, _blockdiag2(w2[1]).T)


def _t5_bias_by_dist(t5_table):
    n = np.arange(T5_MAX_DIST + 1)
    max_exact = T5_BUCKETS // 2
    nf = np.maximum(n, 1).astype(np.float32)
    large = max_exact + (np.log(nf / np.float32(max_exact)) / np.float32(math.log(T5_MAX_DIST / max_exact))
                         * np.float32(T5_BUCKETS - max_exact)).astype(np.int32)
    large = np.minimum(large, T5_BUCKETS - 1)
    bucket = np.where(n < max_exact, n, large)
    onehot = (bucket[:, None] == np.arange(T5_BUCKETS)[None, :]).astype(np.float32)
    return jnp.dot(onehot, t5_table, precision=lax.Precision.HIGHEST)


def _bias_tile(fdt, rows, stride, dist00, d_max=None):
    heads = fdt.shape[0]
    a0 = stride * (rows - 1)
    d_lo = dist00 - a0
    length = a0 + TQ
    d_hi = d_lo + length
    d_max = d_hi if d_max is None else d_max
    pieces = []
    for lo, hi, kind in ((d_lo, min(d_hi, 0), 'neg'), (max(d_lo, 0), min(d_hi, T5_MAX_DIST), 'tab'),
                         (max(d_lo, T5_MAX_DIST), min(d_hi, d_max), 'far'), (max(d_lo, d_max), d_hi, 'neg')):
        if hi > lo:
            pieces.append(fdt[:, lo:hi] if kind == 'tab'
                          else jnp.full((heads, hi - lo), NEG if kind == 'neg' else 0.0, F32))
    vec = jnp.concatenate(pieces, axis=1)
    c0 = -(-a0 // 128) * 128
    width = -(-(c0 + TQ) // 128) * 128
    vec = jnp.pad(vec, ((0, 0), (c0 - a0, width - (c0 - a0) - length)))

    def kern(v_ref, o_ref):
        x = jnp.broadcast_to(v_ref[0], (rows, width))
        o_ref[0] = pltpu.roll(x, 0, 1, stride=stride, stride_axis=0)[:, c0:c0 + TQ]

    return pl.pallas_call(
        kern,
        grid=(heads,),
        in_specs=[pl.BlockSpec((1, 1, width), lambda h: (h, 0, 0))],
        out_specs=pl.BlockSpec((1, rows, TQ), lambda h: (h, 0, 0)),
        out_shape=jax.ShapeDtypeStruct((heads, rows, TQ), F32),
        compiler_params=_cparams(("parallel",)),
        name="toeplitz_bias",
    )(vec.reshape(heads, 1, width))


def _nsa_bias_tiles(t5_table, seq):
    fd = _t5_bias_by_dist(t5_table).astype(F32)
    fdt = ((fd - fd[T5_MAX_DIST:]) * LOG2E).T
    n_cmp = seq // CMP_STRIDE
    qt = TQ // CMP_STRIDE
    r0 = n_cmp - qt
    band = _bias_tile(fdt, 2 * qt, CMP_STRIDE, CMP_STRIDE * qt - (L_CMP - 1))
    heads = fdt.shape[0]
    cmp_t = jnp.concatenate([jnp.zeros((heads, r0 - qt, TQ), F32), band,
                             jnp.full((heads, n_cmp - qt, TQ), NEG, F32)], axis=1)
    sel_t = _bias_tile(fdt, SEL_PAD + TQ, 1, SEL_PAD)
    win = _bias_tile(fdt, WINDOW + TQ, 1, WINDOW, d_max=WINDOW)
    rw = np.arange(WINDOW + TQ)[None, :, None]
    win_t = jnp.stack([jnp.where(rw >= WINDOW - q0, win, NEG) for q0 in (0, TQ, 2 * TQ)])
    split = lambda t: t.reshape(*t.shape[:-3], NSA_KV_HEADS, NSA_GROUP, *t.shape[-2:])

    def wide(t):
        t = jnp.swapaxes(split(t), -3, -2)
        return t.reshape(*t.shape[:-2], NSA_GROUP * TQ)

    return wide(cmp_t), wide(sel_t), wide(win_t)


def _q_head(qt_ref, g, qg_ref):
    q = qt_ref[0, g * HEAD_DIM:(g + 1) * HEAD_DIM, :]
    inv = lax.rsqrt(jnp.mean(q * q, axis=0, keepdims=True) + EPS)
    return (q * inv * qg_ref[...]).astype(BF16)


def _nsa_cmp_kernel(qt_ref, qg_ref, kc_ref, vct_ref, bias_ref, ov_ref, oc_ref, sel_ref, imp_ref, *, n_cmp, n_sel):
    qi = pl.program_id(2)
    qt = TQ // CMP_STRIDE
    row0 = pl.multiple_of((n_cmp - qt) - qi * qt, qt)
    qw = jnp.concatenate([_q_head(qt_ref, g, qg_ref) for g in range(NSA_GROUP)], axis=1)

    def attend(n):
        s = (jnp.dot(kc_ref[0, 0, 0:n, :], qw, preferred_element_type=F32) + bias_ref[0, pl.ds(row0, n), :])
        m = jnp.max(s, axis=0, keepdims=True)
        m = jnp.where(m < 0.5 * NEG, 0.0, m)
        p = jnp.exp2(s - m)
        p = p * (1.0 / jnp.maximum(jnp.sum(p, axis=0, keepdims=True), 1e-30))
        oc = jnp.dot(vct_ref[0, 0, :, 0:n], p.astype(BF16), preferred_element_type=F32)
        psum = jnp.zeros((n, TQ), F32)
        for g in range(NSA_GROUP):
            oc_ref[0, g * HEAD_DIM:(g + 1) * HEAD_DIM, :] = oc[:, g * TQ:(g + 1) * TQ]
            psum = psum + p[:, g * TQ:(g + 1) * TQ]
        hi = psum.astype(BF16)
        lo = (psum - hi.astype(F32)).astype(BF16)
        imp_ref[...] = (jnp.dot(ov_ref[:, 0:n], hi, preferred_element_type=F32)
                        + jnp.dot(ov_ref[:, 0:n], lo, preferred_element_type=F32))

    chunk = min(n_cmp, 128)
    n_chunks = n_cmp // chunk
    need = lax.div((qi + 1) * qt + (chunk - 1), chunk)
    for c in range(1, n_chunks + 1):
        pl.when(need == c)(functools.partial(attend, c * chunk))

    imp = imp_ref[...]
    blk = lax.broadcasted_iota(jnp.int32, (n_sel, TQ), 0)
    blk_t = lax.shift_right_logical(qi * TQ + lax.broadcasted_iota(jnp.int32, (n_sel, TQ), 1), L_SEL.bit_length() - 1)
    forced = (blk == 0) | (blk > blk_t - N_LOCAL)
    v0 = jnp.where(blk > blk_t, -jnp.inf, jnp.where(forced, SEL_FORCE, imp))

    def pick(_, v):
        m = jnp.max(v, axis=0, keepdims=True)
        cand = (v == m) & (m > -jnp.inf)
        first = jnp.min(jnp.where(cand, blk, n_sel), axis=0, keepdims=True)
        return jnp.where(blk == first, -jnp.inf, v)

    v = lax.fori_loop(0, min(N_SELECT, n_sel), pick, v0)
    sel_ref[0, 0] = jnp.where((v == -jnp.inf) & (v0 > -jnp.inf), 0.0, NEG)


def _nsa_cmp(zt, qg, kc, vct, bias_c, ovt):
    b, _, s = zt.shape
    n_cmp, n_sel = s // CMP_STRIDE, s // L_SEL
    gw = NSA_GROUP * HEAD_DIM
    kern = functools.partial(_nsa_cmp_kernel, n_cmp=n_cmp, n_sel=n_sel)
    return pl.pallas_call(
        kern,
        grid=(b, NSA_KV_HEADS, s // TQ),
        in_specs=[pl.BlockSpec((1, gw, TQ), lambda i, k, j: (i, k, j)),
                  pl.BlockSpec((HEAD_DIM, TQ), lambda i, k, j: (0, 0)),
                  pl.BlockSpec((1, 1, n_cmp, HEAD_DIM), lambda i, k, j: (i, k, 0, 0)),
                  pl.BlockSpec((1, 1, HEAD_DIM, n_cmp), lambda i, k, j: (i, k, 0, 0)),
                  pl.BlockSpec((1, bias_c.shape[1], NSA_GROUP * TQ), lambda i, k, j: (k, 0, 0)),
                  pl.BlockSpec((n_sel, n_cmp), lambda i, k, j: (0, 0))],
        out_specs=[pl.BlockSpec((1, gw, TQ), lambda i, k, j: (i, k, j)),
                   pl.BlockSpec((1, 1, n_sel, TQ), lambda i, k, j: (i, k, 0, j))],
        out_shape=[jax.ShapeDtypeStruct((b, NSA_WIDTH, s), F32),
                   jax.ShapeDtypeStruct((b, NSA_KV_HEADS, n_sel, s), F32)],
        scratch_shapes=[pltpu.VMEM((n_sel, TQ), F32)],
        compiler_params=_cparams(("parallel", "parallel", "parallel"), _VMEM_LIMIT),
        name="nsa_compressed_select",
    )(zt, qg, kc, vct, bias_c, ovt)


def _nsa_main_kernel(qt_ref, qg_ref, ks_ref, vst_ref, kw_ref, vwt_ref, sel_ref, bs_ref, bw_ref, gate_ref,
                     oc_ref, o_ref, acc_ref, s_ref):
    qi = pl.program_id(1)
    q0 = pl.multiple_of(qi * TQ, TQ)
    near = SEL_PAD + TQ
    gw = NSA_GROUP * HEAD_DIM
    kvs = range(NSA_KV_HEADS)

    def expand_sel(kv, first_blk, n_blk):
        rows = [jnp.broadcast_to(sel_ref[0, kv, pl.ds(first_blk + r, 1), :], (L_SEL, TQ)) for r in range(n_blk)]
        rows = jnp.concatenate(rows, axis=0)
        return jnp.concatenate([rows] * NSA_GROUP, axis=1)

    qw = [jnp.concatenate([_q_head(qt_ref, kv * NSA_GROUP + g, qg_ref) for g in range(NSA_GROUP)], axis=1)
          for kv in kvs]
    m0 = []
    for kv in kvs:
        s = (jnp.dot(ks_ref[0, kv, pl.ds(q0, near), :], qw[kv], preferred_element_type=F32) + bs_ref[kv]
             + expand_sel(kv, qi * (TQ // L_SEL), near // L_SEL))
        m = jnp.max(s, axis=0, keepdims=True)
        p = jnp.exp2(s - m).astype(BF16)
        acc_ref[kv] = jnp.dot(vst_ref[0, kv, :, pl.ds(q0, near)], p, preferred_element_type=F32)
        m0.append(m)

    for kv in kvs:
        s = jnp.dot(kw_ref[0, kv, pl.ds(q0, WINDOW + TQ), :], qw[kv], preferred_element_type=F32) + bw_ref[0, kv]
        p = jnp.exp2(s - jnp.max(s, axis=0, keepdims=True)).astype(BF16)
        ow = jnp.dot(vwt_ref[0, kv, :, pl.ds(q0, WINDOW + TQ)], p, preferred_element_type=F32)
        ow = ow[:HEAD_DIM] * (1.0 / jnp.maximum(ow[HEAD_DIM:HEAD_DIM + 1], 1e-30))
        for g in range(NSA_GROUP):
            gates = jax.nn.sigmoid(gate_ref[0, kv, g * N_BRANCH:(g + 1) * N_BRANCH, :])
            rows = slice(kv * gw + g * HEAD_DIM, kv * gw + (g + 1) * HEAD_DIM)
            o_ref[0, rows, :] = gates[0:1] * oc_ref[0, rows, :] + gates[2:3] * ow[:, g * TQ:(g + 1) * TQ]

    def scores(c, slot):
        r0 = pl.multiple_of(c * TQ, TQ)
        mc = []
        for kv in kvs:
            s = (jnp.dot(ks_ref[0, kv, pl.ds(r0, TQ), :], qw[kv], preferred_element_type=F32)
                 + expand_sel(kv, c * (TQ // L_SEL), TQ // L_SEL))
            s_ref[slot, kv] = s
            mc.append(jnp.max(s, axis=0, keepdims=True))
        return tuple(mc)

    def consume(c, slot, m_old, mc):
        r0 = pl.multiple_of(c * TQ, TQ)
        m_out = []
        for kv in kvs:
            m_new = jnp.maximum(m_old[kv], mc[kv])
            alpha = jnp.exp2(m_old[kv] - m_new)
            p = jnp.exp2((s_ref[slot, kv] - m_new).astype(BF16))
            acc_ref[kv] = alpha * acc_ref[kv] + jnp.dot(vst_ref[0, kv, :, pl.ds(r0, TQ)], p,
                                                        preferred_element_type=F32)
            m_out.append(m_new)
        return tuple(m_out)

    first = SEL_PAD // TQ
    n_far = qi - first

    def pair(i, carry):
        m, mc = carry
        c = first + 2 * i
        mc1 = scores(c + 1, 1)
        m = consume(c, 0, m, mc)
        mc2 = scores(c + 2, 0)
        m = consume(c + 1, 1, m, mc1)
        return m, mc2

    m_far, mc_far = lax.fori_loop(0, n_far // 2, pair, (tuple(m0), scores(first, 0)))

    @pl.when((n_far > 0) & (n_far % 2 == 1))
    def _():
        consume(first + n_far - 1, 0, m_far, mc_far)

    for kv in kvs:
        os = acc_ref[kv]
        os = os[:HEAD_DIM] * (1.0 / jnp.maximum(os[HEAD_DIM:HEAD_DIM + 1], 1e-30))
        for g in range(NSA_GROUP):
            gate = jax.nn.sigmoid(gate_ref[0, kv, g * N_BRANCH + 1:g * N_BRANCH + 2, :])
            rows = slice(kv * gw + g * HEAD_DIM, kv * gw + (g + 1) * HEAD_DIM)
            o_ref[0, rows, :] = o_ref[0, rows, :] + gate * os[:, g * TQ:(g + 1) * TQ]


def _nsa_main(zt, qg, ks_p, vst_p, kw_p, vwt_p, sel_p, bias_s, bias_w, gates_t, oc_t):
    b, _, s = zt.shape
    kvh = NSA_KV_HEADS
    sp, wp = ks_p.shape[2], kw_p.shape[2]
    nb, vr = sel_p.shape[2], vst_p.shape[2]
    once = pl.Buffered(1)
    return pl.pallas_call(
        _nsa_main_kernel,
        grid=(b, s // TQ),
        in_specs=[pl.BlockSpec((1, NSA_WIDTH, TQ), lambda i, j: (i, 0, j)),
                  pl.BlockSpec((HEAD_DIM, TQ), lambda i, j: (0, 0)),
                  pl.BlockSpec((1, kvh, sp, HEAD_DIM), lambda i, j: (i, 0, 0, 0), pipeline_mode=once),
                  pl.BlockSpec((1, kvh, vr, sp), lambda i, j: (i, 0, 0, 0), pipeline_mode=once),
                  pl.BlockSpec((1, kvh, wp, HEAD_DIM), lambda i, j: (i, 0, 0, 0), pipeline_mode=once),
                  pl.BlockSpec((1, kvh, vr, wp), lambda i, j: (i, 0, 0, 0), pipeline_mode=once),
                  pl.BlockSpec((1, kvh, nb, TQ), lambda i, j: (i, 0, 0, j)),
                  pl.BlockSpec((kvh, SEL_PAD + TQ, NSA_GROUP * TQ), lambda i, j: (0, 0, 0), pipeline_mode=once),
                  pl.BlockSpec((1, kvh, WINDOW + TQ, NSA_GROUP * TQ), lambda i, j: (jnp.minimum(j, 2), 0, 0, 0)),
                  pl.BlockSpec((1, kvh, 16, TQ), lambda i, j: (i, 0, 0, j)),
                  pl.BlockSpec((1, NSA_WIDTH, TQ), lambda i, j: (i, 0, j))],
        out_specs=pl.BlockSpec((1, NSA_WIDTH, TQ), lambda i, j: (i, 0, j)),
        out_shape=jax.ShapeDtypeStruct((b, NSA_WIDTH, s), F32),
        scratch_shapes=[pltpu.VMEM((kvh, vr, NSA_GROUP * TQ), F32),
                        pltpu.VMEM((2, kvh, TQ, NSA_GROUP * TQ), F32)],
        compiler_params=_cparams(("parallel", "arbitrary"), _VMEM_LIMIT),
        name="nsa_selected_window",
    )(zt, qg, ks_p, vst_p, kw_p, vwt_p, sel_p, bias_s, bias_w, gates_t, oc_t)


def _mixout_kernel(gy_ref, yc_ref, yn_ref, h_ref, wglu_ref, go_ref, wo_ref, o_ref):
    sw = SSM_WIDTH
    half = sw // 2
    ag = (jnp.dot(gy_ref[0, 0].astype(BF16), wglu_ref[0:half, :], preferred_element_type=F32)
          + jnp.dot(gy_ref[1, 0].astype(BF16), wglu_ref[half:, :], preferred_element_type=F32))
    ys = ag[:, :sw] * jax.nn.sigmoid(ag[:, sw:])
    ys = (_rms_rows(ys) * go_ref[:, 0:sw]).astype(BF16)
    yn = (_rms_rows(yn_ref[0]) * go_ref[:, 2 * sw:]).astype(BF16)
    out = (jnp.dot(ys, wo_ref[0:sw, :], preferred_element_type=F32)
           + jnp.dot(yc_ref[0], wo_ref[sw:2 * sw, :], preferred_element_type=F32)
           + jnp.dot(yn, wo_ref[2 * sw:, :], preferred_element_type=F32))
    o_ref[0] = h_ref[0] + out


def _mixout(gy, yc, yn, h, w_glu, g_out, w_out, tm=512):
    b, s, d = h.shape
    tok = lambda w: pl.BlockSpec((1, tm, w), lambda i, j: (i, j, 0))
    return pl.pallas_call(
        _mixout_kernel,
        grid=(b, s // tm),
        in_specs=[pl.BlockSpec((2, 1, tm, SSM_WIDTH // 2), lambda i, j: (0, i, j, 0)),
                  tok(CONV_WIDTH), tok(NSA_WIDTH), tok(d),
                  pl.BlockSpec((SSM_WIDTH, 2 * SSM_WIDTH), lambda i, j: (0, 0)),
                  pl.BlockSpec((1, d), lambda i, j: (0, 0)),
                  pl.BlockSpec((d, d), lambda i, j: (0, 0))],
        out_specs=tok(d),
        out_shape=jax.ShapeDtypeStruct((b, s, d), F32),
        compiler_params=_cparams(("parallel", "parallel")),
        name="mix_out",
    )(gy, yc, yn, h, w_glu.astype(BF16), g_out.reshape(1, d), w_out.astype(BF16))


def _memkv_kernel(mem_ref, g_ref, w_ref, kg_ref, k_ref, v_ref):
    mn = (_rms_rows(mem_ref[0]) * g_ref[...]).astype(BF16)
    kv = jnp.dot(mn, w_ref[...], preferred_element_type=F32)
    for h in range(X_HEADS):
        cols = slice(h * HEAD_DIM, (h + 1) * HEAD_DIM)
        k_ref[0, :, cols] = (_rms_rows(kv[:, cols]) * kg_ref[...]).astype(k_ref.dtype)
    v_ref[0] = kv[:, X_WIDTH:].astype(v_ref.dtype)


def _memkv(mem, gain, w_kv, k_gain):
    b, m, d = mem.shape
    out = jax.ShapeDtypeStruct((b, m, X_WIDTH), BF16)
    return pl.pallas_call(
        _memkv_kernel,
        grid=(b,),
        in_specs=[pl.BlockSpec((1, m, d), lambda i: (i, 0, 0)),
                  pl.BlockSpec((1, d), lambda i: (0, 0)),
                  pl.BlockSpec((d, 2 * X_WIDTH), lambda i: (0, 0)),
                  pl.BlockSpec((1, HEAD_DIM), lambda i: (0, 0))],
        out_specs=[pl.BlockSpec((1, m, X_WIDTH), lambda i: (i, 0, 0))] * 2,
        out_shape=[out, out],
        compiler_params=_cparams(("parallel",)),
        name="cross_mem_kv",
    )(mem, gain.reshape(1, d), w_kv.astype(BF16), k_gain.reshape(1, HEAD_DIM))


def _cross_kernel(h_ref, g_ref, wq_ref, qg_ref, k_ref, v_ref, wo_ref, o_ref):
    h = h_ref[0]
    hn = (_rms_rows(h) * g_ref[...]).astype(BF16)
    q = jnp.dot(hn, wq_ref[...], preferred_element_type=F32)
    out = h
    for hd in range(X_HEADS):
        cols = slice(hd * HEAD_DIM, (hd + 1) * HEAD_DIM)
        qh = (_rms_rows(q[:, cols]) * qg_ref[...]).astype(BF16)
        s = lax.dot_general(qh, k_ref[0, :, cols], (((1,), (1,)), ((), ())), preferred_element_type=F32)
        p = jnp.exp(s - jnp.max(s, axis=-1, keepdims=True))
        p = p * (1.0 / jnp.sum(p, axis=-1, keepdims=True))
        o = jnp.dot(p.astype(BF16), v_ref[0, :, cols], preferred_element_type=F32)
        out = out + jnp.dot(o.astype(BF16), wo_ref[cols, :], preferred_element_type=F32)
    o_ref[0] = out


def _cross(h, gain, w_q, q_gain, k, v, w_o, tm=1024):
    b, s, d = h.shape
    m = k.shape[1]
    return pl.pallas_call(
        _cross_kernel,
        grid=(b, s // tm),
        in_specs=[pl.BlockSpec((1, tm, d), lambda i, j: (i, j, 0)),
                  pl.BlockSpec((1, d), lambda i, j: (0, 0)),
                  pl.BlockSpec((d, X_WIDTH), lambda i, j: (0, 0)),
                  pl.BlockSpec((1, HEAD_DIM), lambda i, j: (0, 0)),
                  pl.BlockSpec((1, m, X_WIDTH), lambda i, j: (i, 0, 0)),
                  pl.BlockSpec((1, m, X_WIDTH), lambda i, j: (i, 0, 0)),
                  pl.BlockSpec((X_WIDTH, d), lambda i, j: (0, 0))],
        out_specs=pl.BlockSpec((1, tm, d), lambda i, j: (i, j, 0)),
        out_shape=jax.ShapeDtypeStruct((b, s, d), F32),
        compiler_params=_cparams(("parallel", "parallel")),
        name="cross_attention",
    )(h, gain.reshape(1, d), w_q.astype(BF16), (q_gain * HEAD_DIM ** -0.5).reshape(1, HEAD_DIM), k, v,
      w_o.astype(BF16))


def _ffn_kernel(h_ref, g_ref, wg_ref, wv_ref, wd_ref, o_ref, xn_ref, acc_ref):
    f = pl.program_id(1)

    @pl.when(f == 0)
    def _():
        xn_ref[...] = (_rms_rows(h_ref[...]) * g_ref[...]).astype(BF16)
        acc_ref[...] = jnp.zeros_like(acc_ref)

    x = xn_ref[...]
    gate = jnp.dot(x, wg_ref[...], preferred_element_type=F32)
    val = jnp.dot(x, wv_ref[...], preferred_element_type=F32)
    act = (jax.nn.silu(gate) * val).astype(BF16)
    acc_ref[...] += jnp.dot(act, wd_ref[...], preferred_element_type=F32)

    @pl.when(f == pl.num_programs(1) - 1)
    def _():
        o_ref[...] = h_ref[...] + acc_ref[...]


def _ffn(h2d, gain, w_up, w_down, tm=1024):
    t, d = h2d.shape
    nf = D_FF // FF_CHUNK
    wb = w_up.astype(BF16)
    return pl.pallas_call(
        _ffn_kernel,
        grid=(t // tm, nf),
        in_specs=[pl.BlockSpec((tm, d), lambda i, f: (i, 0)),
                  pl.BlockSpec((1, d), lambda i, f: (0, 0)),
                  pl.BlockSpec((d, FF_CHUNK), lambda i, f: (0, f)),
                  pl.BlockSpec((d, FF_CHUNK), lambda i, f: (0, f + nf)),
                  pl.BlockSpec((FF_CHUNK, d), lambda i, f: (f, 0))],
        out_specs=pl.BlockSpec((tm, d), lambda i, f: (i, 0)),
        out_shape=jax.ShapeDtypeStruct((t, d), F32),
        scratch_shapes=[pltpu.VMEM((tm, d), BF16), pltpu.VMEM((tm, d), F32)],
        compiler_params=_cparams(("parallel", "arbitrary"), _VMEM_LIMIT),
        name="ffn_swiglu",
    )(h2d, gain.reshape(1, d), wb, wb, w_down.astype(BF16))


def _router_kernel(h_ref, g_ref, wr_ref, xn_ref, gate_ref, asg_ref):
    xn = _rms_rows(h_ref[...]) * g_ref[...]
    xn_ref[...] = xn.astype(BF16)
    logits = jnp.dot(xn, wr_ref[...], precision=lax.Precision.HIGHEST, preferred_element_type=F32)
    lane = lax.broadcasted_iota(jnp.int32, logits.shape, 1)
    lg = jnp.where(lane < N_EXPERTS, logits, -jnp.inf)
    m1 = jnp.max(lg, axis=-1, keepdims=True)
    i1 = jnp.min(jnp.where(lg == m1, lane, 128), axis=-1, keepdims=True)
    lg2 = jnp.where(lane == i1, -jnp.inf, lg)
    m2 = jnp.max(lg2, axis=-1, keepdims=True)
    i2 = jnp.min(jnp.where(lg2 == m2, lane, 128), axis=-1, keepdims=True)
    e = jnp.exp(m2 - m1)
    den = 1.0 + e
    gate_ref[...] = jnp.where(lane == i1, 1.0 / den, jnp.where(lane == i2, e / den, 0.0))
    asg_ref[...] = ((lane == i1) | (lane == i2)).astype(jnp.int32)


def _router(h2d, gain, w_router, tm=512):
    t, d = h2d.shape
    wr = jnp.concatenate([w_router, jnp.zeros((d, 128 - N_EXPERTS), F32)], axis=1)
    return pl.pallas_call(
        _router_kernel,
        grid=(t // tm,),
        in_specs=[pl.BlockSpec((tm, d), lambda i: (i, 0)),
                  pl.BlockSpec((1, d), lambda i: (0, 0)),
                  pl.BlockSpec((d, 128), lambda i: (0, 0))],
        out_specs=[pl.BlockSpec((tm, d), lambda i: (i, 0)),
                   pl.BlockSpec((tm, 128), lambda i: (i, 0)),
                   pl.BlockSpec((tm, 128), lambda i: (i, 0))],
        out_shape=[jax.ShapeDtypeStruct((t, d), BF16), jax.ShapeDtypeStruct((t, 128), F32),
                   jax.ShapeDtypeStruct((t, 128), jnp.int32)],
        compiler_params=_cparams(("parallel",)),
        name="moe_router",
    )(h2d, gain.reshape(1, d), wr)


def _moe_windows(rb, lo, hi, active):
    lo_l = jnp.clip(lo - rb * MOE_TB, 0, MOE_TB)
    hi_l = jnp.clip(hi - rb * MOE_TB, 0, MOE_TB)
    shift = MOE_SUB.bit_length() - 1
    w0 = jnp.minimum(lax.shift_left(lax.shift_right_logical(lo_l, shift), shift), MOE_TB - MOE_WIN)
    has = active & (hi_l > lo_l)
    return ((w0, 0, has), (MOE_TB - MOE_WIN, w0 + MOE_WIN, has & (hi_l > w0 + MOE_WIN)))


def _moe_gather_kernel(rb_ref, lo_ref, hi_ref, first_ref, tgt_ref, x_ref, o_ref):
    e, j, slot = pl.program_id(0), pl.program_id(1), pl.program_id(2)
    rb = rb_ref[e, j, slot]
    lo, hi = lo_ref[e, j], hi_ref[e, j]

    @pl.when(first_ref[e, j, slot] == 1)
    def _():
        o_ref[...] = jnp.zeros_like(o_ref)

    active = (slot == 0) | (rb != rb_ref[e, j, 0])
    tgt = tgt_ref[0]
    for start, cutoff, needed in _moe_windows(rb, lo, hi, active):

        @pl.when(needed)
        def _():
            local = start + lax.broadcasted_iota(jnp.int32, (MOE_WIN, MOE_TB), 0)
            rows = jnp.where(local >= cutoff, rb * MOE_TB + local, -2)
            onehot = jnp.where(tgt == rows, 1.0, 0.0).astype(BF16)
            part = jnp.dot(onehot, x_ref[...], preferred_element_type=F32)
            sl = pl.ds(pl.multiple_of(start, MOE_SUB), MOE_WIN)
            o_ref[sl, :] = o_ref[sl, :] + part.astype(o_ref.dtype)


def _moe_ffn_kernel(exp_ref, nused_ref, x_ref, wg_ref, wv_ref, wd_ref, o_ref, acc_ref):
    r, f = pl.program_id(0), pl.program_id(1)
    used = r < nused_ref[0]

    @pl.when(f == 0)
    def _():
        acc_ref[...] = jnp.zeros_like(acc_ref)

    @pl.when(used)
    def _():
        x = x_ref[...]
        gate = jnp.dot(x, wg_ref[0], preferred_element_type=F32)
        val = jnp.dot(x, wv_ref[0], preferred_element_type=F32)
        act = (jax.nn.silu(gate) * val).astype(BF16)
        acc_ref[...] += jnp.dot(act, wd_ref[0], preferred_element_type=F32)

    @pl.when(f == pl.num_programs(1) - 1)
    def _():
        o_ref[...] = acc_ref[...].astype(o_ref.dtype)
    del exp_ref


def _moe_scatter_kernel(rb_ref, lo_ref, hi_ref, tgt_ref, gate_ref, y_ref, h_ref, o_ref):
    j, e, slot = pl.program_id(0), pl.program_id(1), pl.program_id(2)
    rb = rb_ref[e, j, slot]
    lo, hi = lo_ref[e, j], hi_ref[e, j]

    @pl.when((e == 0) & (slot == 0))
    def _():
        o_ref[...] = h_ref[...]

    active = (slot == 0) | (rb != rb_ref[e, j, 0])
    mine = lax.broadcasted_iota(jnp.int32, tgt_ref.shape, 1) == e
    tgt = jnp.sum(jnp.where(mine, tgt_ref[...], 0), axis=1, keepdims=True)
    gate = jnp.sum(jnp.where(mine, gate_ref[...], 0.0), axis=1, keepdims=True)
    for start, cutoff, needed in _moe_windows(rb, lo, hi, active):

        @pl.when(needed)
        def _():
            local = start + lax.broadcasted_iota(jnp.int32, (MOE_TB, MOE_WIN), 1)
            rows = jnp.where(local >= cutoff, rb * MOE_TB + local, -2)
            onehot = jnp.where(tgt == rows, 1.0, 0.0).astype(BF16)
            y = y_ref[pl.ds(pl.multiple_of(start, MOE_SUB), MOE_WIN), :]
            o_ref[...] = o_ref[...] + gate * jnp.dot(onehot, y, preferred_element_type=F32)


def _moe(h2d, gain, w_router, w_up, w_down):
    t, d = h2d.shape
    tb = MOE_TB
    nj = t // tb
    n_rb = (t * TOP_K) // tb + N_EXPERTS
    xn, gates, asg = _router(h2d, gain, w_router)
    asg = asg[:, :N_EXPERTS]
    gates = gates[:, :N_EXPERTS]
    cs = jnp.cumsum(asg, axis=0)
    rank = cs - asg
    counts = cs[-1]
    padded = (counts + tb - 1) // tb * tb
    pad_end = jnp.cumsum(padded)
    start_p = pad_end - padded
    tgt = jnp.where(asg == 1, start_p[None, :] + rank, -1).astype(jnp.int32)
    cb = jnp.concatenate([jnp.zeros((1, N_EXPERTS), jnp.int32), cs[tb - 1::tb]], axis=0)
    lo = (start_p[None, :] + cb[:-1]).T.astype(jnp.int32)
    hi = (start_p[None, :] + cb[1:]).T.astype(jnp.int32)
    rb0 = lo // tb
    rb1 = jnp.maximum(rb0, (hi - 1) // tb)
    rb = jnp.stack([rb0, rb1], axis=-1).astype(jnp.int32)
    flat = rb.reshape(-1)
    first = jnp.concatenate([jnp.ones((1,), jnp.int32), (flat[1:] != flat[:-1]).astype(jnp.int32)])
    first = first.reshape(N_EXPERTS, nj, 2)
    n_used = (pad_end[-1] // tb).astype(jnp.int32).reshape(1)
    blk_exp = jnp.minimum(jnp.searchsorted(pad_end, jnp.arange(n_rb) * tb, side='right'),
                          N_EXPERTS - 1).astype(jnp.int32)

    xs = pl.pallas_call(
        _moe_gather_kernel,
        grid_spec=pltpu.PrefetchScalarGridSpec(
            num_scalar_prefetch=4,
            grid=(N_EXPERTS, nj, 2),
            in_specs=[pl.BlockSpec((1, 1, tb), lambda e, j, s, *_: (e, 0, j)),
                      pl.BlockSpec((tb, d), lambda e, j, s, *_: (j, 0))],
            out_specs=pl.BlockSpec((tb, d), lambda e, j, s, rb_ref, *_: (rb_ref[e, j, s], 0))),
        out_shape=jax.ShapeDtypeStruct((n_rb * tb, d), BF16),
        compiler_params=_cparams(("arbitrary", "arbitrary", "arbitrary"), _VMEM_LIMIT),
        name="moe_gather",
    )(rb, lo, hi, first, tgt.T.reshape(N_EXPERTS, 1, t), xn)

    nf = D_FF // FF_CHUNK
    wub = w_up.astype(BF16)
    ys = pl.pallas_call(
        _moe_ffn_kernel,
        grid_spec=pltpu.PrefetchScalarGridSpec(
            num_scalar_prefetch=2,
            grid=(n_rb, nf),
            in_specs=[pl.BlockSpec((tb, d), lambda r, f, *_: (r, 0)),
                      pl.BlockSpec((1, d, FF_CHUNK), lambda r, f, ex, nu: (ex[r], 0, f)),
                      pl.BlockSpec((1, d, FF_CHUNK), lambda r, f, ex, nu: (ex[r], 0, f + nf)),
                      pl.BlockSpec((1, FF_CHUNK, d), lambda r, f, ex, nu: (ex[r], f, 0))],
            out_specs=pl.BlockSpec((tb, d), lambda r, f, *_: (r, 0)),
            scratch_shapes=[pltpu.VMEM((tb, d), F32)]),
        out_shape=jax.ShapeDtypeStruct((n_rb * tb, d), BF16),
        compiler_params=_cparams(("arbitrary", "arbitrary"), _VMEM_LIMIT),
        name="moe_expert_ffn",
    )(blk_exp, n_used, xs, wub, wub, w_down.astype(BF16))

    return pl.pallas_call(
        _moe_scatter_kernel,
        grid_spec=pltpu.PrefetchScalarGridSpec(
            num_scalar_prefetch=3,
            grid=(nj, N_EXPERTS, 2),
            in_specs=[pl.BlockSpec((tb, N_EXPERTS), lambda j, e, s, *_: (j, 0)),
                      pl.BlockSpec((tb, N_EXPERTS), lambda j, e, s, *_: (j, 0)),
                      pl.BlockSpec((tb, d), lambda j, e, s, rb_ref, *_: (rb_ref[e, j, s], 0)),
                      pl.BlockSpec((tb, d), lambda j, e, s, *_: (j, 0))],
            out_specs=pl.BlockSpec((tb, d), lambda j, e, s, *_: (j, 0))),
        out_shape=jax.ShapeDtypeStruct((t, d), F32),
        compiler_params=_cparams(("arbitrary", "arbitrary", "arbitrary"), _VMEM_LIMIT),
        name="moe_scatter",
    )(rb, lo, hi, tgt, gates, ys, h2d)


_COL_CONV, _COL_SSM, _COL_KC, _COL_VC, _COL_KS, _COL_KW = 0, 512, 768, 896, 1024, 1152
_ROW_Q, _ROW_VS, _ROW_VW, _ROW_G = 0, 512, 640, 768


def _split_w_in(w_in):
    kvw = NSA_KV_HEADS * HEAD_DIM
    cuts = np.cumsum([0, SSM_WIDTH, 2 * CONV_WIDTH, NSA_WIDTH] + [kvw] * 6 + [N_BRANCH * NSA_HEADS])
    seg = lambda i: w_in[:, cuts[i]:cuts[i + 1]]
    ssm, conv, q, k_c, v_c, k_s, v_s, k_w, v_w, gate = (seg(i) for i in range(10))
    w_tok = jnp.concatenate([conv, ssm, k_c, v_c, k_s, k_w], axis=1).astype(BF16)
    gate = jnp.concatenate([gate, jnp.zeros((w_in.shape[0], 8), F32)], axis=1)
    w_t = jnp.concatenate([q, v_s, v_w, gate], axis=1).T.astype(BF16)
    return w_tok, w_t


def _layer_mixers(h, p, t5_tiles, s5_perm):
    b, s, d = h.shape
    w_tok, w_t = _split_w_in(p['w_in'])
    ztok, zt = _proj(h, p['norm_mix'], w_tok, w_t)

    tables = _s5_tables(p['ssm_lambda_re'], p['ssm_lambda_im'], p['ssm_log_dt'], p['ssm_b_re'], p['ssm_b_im'],
                        p['ssm_c_re'], p['ssm_c_im'], p['ssm_d'])
    gy = _s5_unpack(_s5(_s5_pack(ztok, _COL_SSM, s5_perm[0]), tables, b), s5_perm[1], b)

    g_out = p['mix_out_norm']
    yc = _conv(ztok, p['conv_w_dw'], p['conv_b_dw'], p['conv_ln_g'], p['conv_ln_b'], p['conv_w_pw'],
               g_out[SSM_WIDTH:SSM_WIDTH + CONV_WIDTH])

    kvw = NSA_KV_HEADS * HEAD_DIM
    n_cmp = s // CMP_STRIDE
    k_norm = p['nsa_k_norm']
    kc, vct = _compress(ztok, _COL_KC, _COL_VC, p['nsa_cmp_pe'], p['nsa_cmp_w1'], p['nsa_cmp_w2'], k_norm[0])
    ks, kw = _knorm(ztok, _COL_KS, _COL_KW, k_norm[1], k_norm[2])
    qg = jnp.broadcast_to((p['nsa_q_norm'] * (HEAD_DIM ** -0.5 * LOG2E))[:, None], (HEAD_DIM, TQ))
    bias_c, bias_s, bias_w = t5_tiles
    n_sel = s // L_SEL
    cs_ = np.arange(n_cmp) * CMP_STRIDE
    ss_ = np.arange(n_sel) * L_SEL
    ov = np.maximum(np.minimum(cs_[:, None] + L_CMP, ss_[None, :] + L_SEL) - np.maximum(cs_[:, None], ss_[None, :]), 0)
    ovt = jnp.asarray((ov.astype(np.float32) / L_CMP).T, BF16)
    oc_t, sel = _nsa_cmp(zt, qg, kc, vct, bias_c, ovt)

    front = lambda x, n, axis: jnp.pad(x, [(n, 0) if a == axis else (0, 0) for a in range(x.ndim)])
    ones_rows = jnp.concatenate([jnp.ones((b, NSA_KV_HEADS, 1, s), BF16),
                                 jnp.zeros((b, NSA_KV_HEADS, 15, s), BF16)], axis=2)
    heads_t = lambda rows: jnp.concatenate(
        [zt[:, rows:rows + kvw, :].astype(BF16).reshape(b, NSA_KV_HEADS, HEAD_DIM, s), ones_rows], axis=2)
    ks_p = front(ks, SEL_PAD, 2)
    kw_p = front(kw, WINDOW, 2)
    vst_p = front(heads_t(_ROW_VS), SEL_PAD, 3)
    vwt_p = front(heads_t(_ROW_VW), WINDOW, 3)
    sel_p = jnp.pad(sel, ((0, 0), (0, 0), (SEL_PAD // L_SEL, 0), (0, 0)), constant_values=NEG)
    gl = zt[:, _ROW_G:_ROW_G + N_BRANCH * NSA_HEADS, :].reshape(b, NSA_KV_HEADS, NSA_GROUP * N_BRANCH, s)
    gates_t = jnp.pad(gl, ((0, 0), (0, 0), (0, 16 - NSA_GROUP * N_BRANCH), (0, 0)))
    yn_t = _nsa_main(zt, qg, ks_p, vst_p, kw_p, vwt_p, sel_p, bias_s, bias_w, gates_t, oc_t)
    yn = yn_t.transpose(0, 2, 1)

    return _mixout(gy, yc, yn, h, p['ssm_w_glu'], g_out, p['w_out'])


def kernel(x, mem, norm_mix, w_in, ssm_lambda_re, ssm_lambda_im, ssm_log_dt, ssm_b_re, ssm_b_im, ssm_c_re, ssm_c_im, ssm_d, ssm_w_glu, conv_w_dw, conv_b_dw, conv_ln_g, conv_ln_b, conv_w_pw, nsa_q_norm, nsa_k_norm, nsa_cmp_pe, nsa_cmp_w1, nsa_cmp_w2, mix_out_norm, w_out, t5_table, norm_cross, norm_mem, x_w_q, x_w_kv, x_q_norm, x_k_norm, x_w_o, norm_ffn, ffn_w_up, ffn_w_down, moe_router, moe_w_up, moe_w_down):
    b, s, d = x.shape
    depth = w_in.shape[0]
    per_layer = dict(norm_mix=norm_mix, w_in=w_in, ssm_lambda_re=ssm_lambda_re, ssm_lambda_im=ssm_lambda_im,
                     ssm_log_dt=ssm_log_dt, ssm_b_re=ssm_b_re, ssm_b_im=ssm_b_im, ssm_c_re=ssm_c_re,
                     ssm_c_im=ssm_c_im, ssm_d=ssm_d, ssm_w_glu=ssm_w_glu, conv_w_dw=conv_w_dw,
                     conv_b_dw=conv_b_dw, conv_ln_g=conv_ln_g, conv_ln_b=conv_ln_b, conv_w_pw=conv_w_pw,
                     nsa_q_norm=nsa_q_norm, nsa_k_norm=nsa_k_norm, nsa_cmp_pe=nsa_cmp_pe, nsa_cmp_w1=nsa_cmp_w1,
                     nsa_cmp_w2=nsa_cmp_w2, mix_out_norm=mix_out_norm, w_out=w_out)
    t5_tiles = _nsa_bias_tiles(t5_table, s)
    s5_perm = _s5_perm_tables()
    h = x
    for layer in range(depth):
        p = {k: v[layer] for k, v in per_layer.items()}
        h = _layer_mixers(h, p, t5_tiles, s5_perm)
        mk, mv = _memkv(mem, norm_mem[layer], x_w_kv[layer], x_k_norm[layer])
        h = _cross(h, norm_cross[layer], x_w_q[layer], x_q_norm[layer], mk, mv, x_w_o[layer])
        h2d = h.reshape(b * s, d)
        if layer % 2 == 0:
            h2d = _ffn(h2d, norm_ffn[layer], ffn_w_up[layer // 2], ffn_w_down[layer // 2])
        else:
            h2d = _moe(h2d, norm_ffn[layer], moe_router[layer // 2], moe_w_up[layer // 2], moe_w_down[layer // 2])
        h = h2d.reshape(b, s, d)
    return h
```

```python
import functools
import math

import jax
import jax.numpy as jnp
import numpy as np
from jax import lax
from jax.experimental import pallas as pl
from jax.experimental.pallas import tpu as pltpu

F32 = jnp.float32
BF16 = jnp.bfloat16

D_MODEL = 1024
HEAD_DIM = 64
SSM_WIDTH = 256
SSM_GROUP = 16
SSM_GROUPS = 16
SSM_STATE = 64
SSM_CHUNK = 16
CONV_WIDTH = 256
CONV_K = 31
CONV_HALO = 32
NSA_WIDTH = 512
NSA_HEADS = 8
NSA_KV_HEADS = 2
NSA_GROUP = 4
N_BRANCH = 3
L_CMP = 32
CMP_STRIDE = 16
L_SEL = 64
N_SELECT = 16
N_LOCAL = 2
WINDOW = 512
SEL_FORCE = 1e6
T5_BUCKETS = 32
T5_MAX_DIST = 128
X_HEADS = 4
X_WIDTH = 256
D_FF = 2816
N_EXPERTS = 8
TOP_K = 2
EPS = 1e-6
NEG = -1e30
LOG2E = math.log2(math.e)

TQ = 256
SEL_PAD = 256
FF_CHUNK = 256
MOE_TB = 1024
MOE_TS = 256
MOE_SUB = 64
MOE_WIN = 192

_VMEM_LIMIT = 56 * 1024 * 1024


def _cparams(sem, vmem=None):
    return pltpu.CompilerParams(dimension_semantics=sem, vmem_limit_bytes=vmem)


def _rms_rows(x):
    return x * lax.rsqrt(jnp.mean(x * x, axis=-1, keepdims=True) + EPS)


def _proj_kernel(x_ref, g_ref, wtok_ref, wt_ref, ztok_ref, zt_ref):
    xn = (_rms_rows(x_ref[0]) * g_ref[...]).astype(BF16)
    ztok_ref[0] = jnp.dot(xn, wtok_ref[...], preferred_element_type=F32)
    zt_ref[0] = lax.dot_general(wt_ref[...], xn, (((1,), (1,)), ((), ())), preferred_element_type=F32)


def _proj(h, gain, w_tok, w_t, tm=512):
    b, s, d = h.shape
    ntok, nt = w_tok.shape[1], w_t.shape[0]
    return pl.pallas_call(
        _proj_kernel,
        grid=(b, s // tm),
        in_specs=[pl.BlockSpec((1, tm, d), lambda i, j: (i, j, 0)),
                  pl.BlockSpec((1, d), lambda i, j: (0, 0)),
                  pl.BlockSpec((d, ntok), lambda i, j: (0, 0)),
                  pl.BlockSpec((nt, d), lambda i, j: (0, 0))],
        out_specs=[pl.BlockSpec((1, tm, ntok), lambda i, j: (i, j, 0)),
                   pl.BlockSpec((1, nt, tm), lambda i, j: (i, 0, j))],
        out_shape=[jax.ShapeDtypeStruct((b, s, ntok), F32), jax.ShapeDtypeStruct((b, nt, s), F32)],
        compiler_params=_cparams(("parallel", "parallel"), _VMEM_LIMIT),
        name="proj",
    )(h, gain.reshape(1, d), w_tok, w_t)


def _s5_tables(lam_re, lam_im, log_dt, b_re, b_im, c_re, c_im, d_skip):
    L, H, P = SSM_CHUNK, SSM_GROUP, SSM_STATE
    dt = jnp.exp(log_dt.astype(F32))[:, None]
    lr, li = lam_re.astype(F32), lam_im.astype(F32)
    mag = jnp.exp(lr * dt)
    ar, ai = mag * jnp.cos(li * dt), mag * jnp.sin(li * dt)
    den = lr * lr + li * li
    fr = ((ar - 1.0) * lr + ai * li) / den
    fi = (ai * lr - (ar - 1.0) * li) / den
    bbr = fr[..., None] * b_re - fi[..., None] * b_im
    bbi = fr[..., None] * b_im + fi[..., None] * b_re
    j = jnp.arange(L + 1, dtype=F32)[:, None, None]
    pmag = jnp.exp(lr[None] * dt[None] * j)
    pr, pi = pmag * jnp.cos(li[None] * dt[None] * j), pmag * jnp.sin(li[None] * dt[None] * j)
    cbr = c_re[:, :, :, None] * bbr[:, None, :, :] - c_im[:, :, :, None] * bbi[:, None, :, :]
    cbi = c_re[:, :, :, None] * bbi[:, None, :, :] + c_im[:, :, :, None] * bbr[:, None, :, :]
    hp = lax.Precision.HIGHEST
    kj = (jnp.einsum('jgp,ghpk->jghk', pr[:L], cbr, precision=hp)
          - jnp.einsum('jgp,ghpk->jghk', pi[:L], cbi, precision=hp))
    lag = np.arange(L)[None, :] - np.arange(L)[:, None]
    place = (lag[None] == np.arange(L)[:, None, None]).astype(np.float32)
    kt = jnp.einsum('jab,jghk->abghk', place, kj, precision=hp)
    kt = kt + (jnp.eye(L)[:, :, None, None, None] * (jnp.eye(H)[None, None, None] * d_skip[None, None, :, :, None]))
    tmat = kt.transpose(2, 0, 4, 1, 3).reshape(SSM_GROUPS, L * H, L * H)
    qr, qi = pr[:L][::-1], pi[:L][::-1]
    wre = qr[..., None] * bbr[None] - qi[..., None] * bbi[None]
    wim = qr[..., None] * bbi[None] + qi[..., None] * bbr[None]
    wre = wre.transpose(1, 0, 3, 2).reshape(SSM_GROUPS, L * H, P)
    wim = wim.transpose(1, 0, 3, 2).reshape(SSM_GROUPS, L * H, P)
    w1 = jnp.concatenate([wre, wim], axis=-1)
    w2 = jnp.concatenate([wim, wre], axis=-1)
    sr, si = pr[1:], pi[1:]
    vr = c_re[None] * sr[:, :, None, :] - c_im[None] * si[:, :, None, :]
    vi = c_re[None] * si[:, :, None, :] + c_im[None] * sr[:, :, None, :]
    vmat = jnp.concatenate([vr, -vi], axis=-1).transpose(1, 3, 0, 2).reshape(SSM_GROUPS, 2 * P, L * H)
    a_r, a_i = pr[L], pi[L]
    am = jnp.stack([jnp.concatenate([a_r, a_r], -1), jnp.concatenate([-a_i, a_i], -1),
                    jnp.concatenate([a_i, -a_i], -1)], axis=1)
    am = jnp.concatenate([am, jnp.zeros((SSM_GROUPS, 5, 2 * P), F32)], axis=1)
    return tmat.astype(BF16), w1.astype(BF16), w2.astype(BF16), vmat.astype(BF16), am


def _s5_perm_tables():
    L, G, H = SSM_CHUNK, SSM_GROUPS, SSM_GROUP
    i = jnp.arange(L * SSM_WIDTH)
    ti, gi, hi = i // SSM_WIDTH, (i // H) % G, i % H
    o = jnp.arange(L * H)
    pack = ((gi[None, :, None] == jnp.arange(G)[:, None, None]) & (ti[None, :, None] == (o // H)[None, None, :])
            & (hi[None, :, None] == (o % H)[None, None, :]))
    j = jnp.arange(G * L * H)
    gj, tj, hj = j // (L * H), (j // H) % L, j % H
    w = jnp.arange(SSM_WIDTH)
    unpack = ((tj[None, :, None] == jnp.arange(L)[:, None, None]) & (gj[None, :, None] == (w // H)[None, None, :])
              & (hj[None, :, None] == (w % H)[None, None, :]))
    return pack.astype(BF16), unpack.astype(BF16)


def _s5_pack_kernel(ulo_ref, uhi_ref, p_ref, x_ref, u2_ref, *, n_chunks):
    half = SSM_WIDTH // 2

    @pl.when(pl.program_id(1) == 0)
    def _():
        for t in range(SSM_CHUNK):
            for k, u_ref in enumerate((ulo_ref, uhi_ref)):
                u2_ref[:, t * SSM_WIDTH + k * half:t * SSM_WIDTH + (k + 1) * half] = (
                    u_ref[0, pl.ds(t, n_chunks, stride=SSM_CHUNK), :].astype(BF16))

    x_ref[0] = jnp.dot(u2_ref[...], p_ref[0], preferred_element_type=F32).astype(x_ref.dtype)


def _s5_pack(ztok, col, pack):
    b, s, _ = ztok.shape
    n_chunks = s // SSM_CHUNK
    lw, lh = SSM_CHUNK * SSM_WIDTH, SSM_CHUNK * SSM_GROUP
    half = SSM_WIDTH // 2
    return pl.pallas_call(
        functools.partial(_s5_pack_kernel, n_chunks=n_chunks),
        grid=(b, SSM_GROUPS),
        in_specs=[pl.BlockSpec((1, s, half), lambda i, g: (i, 0, col // half)),
                  pl.BlockSpec((1, s, half), lambda i, g: (i, 0, col // half + 1)),
                  pl.BlockSpec((1, lw, lh), lambda i, g: (g, 0, 0))],
        out_specs=pl.BlockSpec((1, n_chunks, lh), lambda i, g: (g, i, 0)),
        out_shape=jax.ShapeDtypeStruct((SSM_GROUPS, b * n_chunks, lh), BF16),
        scratch_shapes=[pltpu.VMEM((n_chunks, lw), BF16)],
        compiler_params=_cparams(("parallel", "arbitrary"), _VMEM_LIMIT),
        name="s5_pack",
    )(ztok, ztok, pack)


def _s5_unpack_kernel(g_ref, r_ref, o_ref, *, n_chunks):
    t = pl.program_id(1)
    rows = jnp.concatenate([g_ref[g] for g in range(SSM_GROUPS)], axis=1)
    y = jnp.dot(rows, r_ref[0], preferred_element_type=F32)
    half = SSM_WIDTH // 2
    for k in range(SSM_CHUNK):
        @pl.when(t == k)
        def _():
            for part in range(2):
                o_ref[part, 0, pl.ds(k, n_chunks, stride=SSM_CHUNK), :] = y[:, part * half:(part + 1) * half]


def _s5_unpack(gy, unpack, bsz):
    g, r, lh = gy.shape
    n_chunks = r // bsz
    half = SSM_WIDTH // 2
    return pl.pallas_call(
        functools.partial(_s5_unpack_kernel, n_chunks=n_chunks),
        grid=(bsz, SSM_CHUNK),
        in_specs=[pl.BlockSpec((g, n_chunks, lh), lambda i, t: (0, i, 0)),
                  pl.BlockSpec((1, g * lh, SSM_WIDTH), lambda i, t: (t, 0, 0))],
        out_specs=pl.BlockSpec((2, 1, n_chunks * SSM_CHUNK, half), lambda i, t: (0, i, 0, 0)),
        out_shape=jax.ShapeDtypeStruct((2, bsz, n_chunks * SSM_CHUNK, half), F32),
        compiler_params=_cparams(("parallel", "arbitrary"), _VMEM_LIMIT),
        name="s5_unpack",
    )(gy, unpack)


def _s5_kernel(x_ref, t_ref, w1_ref, w2_ref, v_ref, a_ref, o_ref, s1_ref, s2_ref, xin_ref, *, bsz, n_chunks):
    x = x_ref[0]
    s1_ref[...] = jnp.dot(x, w1_ref[0], preferred_element_type=F32)
    s2_ref[...] = jnp.dot(x, w2_ref[0], preferred_element_type=F32)
    a1, a2, a3 = a_ref[0, 0:1, :], a_ref[0, 1:2, :], a_ref[0, 2:3, :]

    def step(c, carry):
        ps, qs = carry
        new_p, new_q = [], []
        for bi in range(bsz):
            row = pl.ds(bi * n_chunks + c, 1)
            xin_ref[row, :] = ps[bi]
            new_p.append(ps[bi] * a1 + qs[bi] * a2 + s1_ref[row, :])
            new_q.append(qs[bi] * a1 + ps[bi] * a3 + s2_ref[row, :])
        return tuple(new_p), tuple(new_q)

    zero = tuple(jnp.zeros((1, 2 * SSM_STATE), F32) for _ in range(bsz))
    lax.fori_loop(0, n_chunks, step, (zero, zero))
    y = (jnp.dot(x, t_ref[0], preferred_element_type=F32)
         + jnp.dot(xin_ref[...].astype(BF16), v_ref[0], preferred_element_type=F32))
    o_ref[0] = jax.nn.gelu(y).astype(o_ref.dtype)


def _s5(xg, tables, bsz):
    tmat, w1, w2, vmat, am = tables
    g, r, lh = xg.shape
    p2 = 2 * SSM_STATE
    kern = functools.partial(_s5_kernel, bsz=bsz, n_chunks=r // bsz)
    return pl.pallas_call(
        kern,
        grid=(g,),
        in_specs=[pl.BlockSpec((1, r, lh), lambda i: (i, 0, 0)),
                  pl.BlockSpec((1, lh, lh), lambda i: (i, 0, 0)),
                  pl.BlockSpec((1, lh, p2), lambda i: (i, 0, 0)),
                  pl.BlockSpec((1, lh, p2), lambda i: (i, 0, 0)),
                  pl.BlockSpec((1, p2, lh), lambda i: (i, 0, 0)),
                  pl.BlockSpec((1, 8, p2), lambda i: (i, 0, 0))],
        out_specs=pl.BlockSpec((1, r, lh), lambda i: (i, 0, 0)),
        out_shape=jax.ShapeDtypeStruct((g, r, lh), BF16),
        scratch_shapes=[pltpu.VMEM((r, p2), F32), pltpu.VMEM((r, p2), F32), pltpu.VMEM((r, p2), F32)],
        compiler_params=_cparams(("parallel",), _VMEM_LIMIT),
        name="s5_scan",
    )(xg, tmat, w1, w2, vmat, am)


def _conv_kernel(z_ref, halo_ref, wdw_ref, bdw_ref, lng_ref, lnb_ref, wpw_ref, go_ref, o_ref, buf_ref, *, tt):
    first = pl.program_id(1) == 0
    zc = z_ref[0]
    zh = halo_ref[0]
    vh = zh[:, :CONV_WIDTH] * jax.nn.sigmoid(zh[:, CONV_WIDTH:])
    buf_ref[0:CONV_HALO, :] = vh * jnp.where(first, 0.0, 1.0)
    buf_ref[CONV_HALO:CONV_HALO + tt, :] = zc[:, :CONV_WIDTH] * jax.nn.sigmoid(zc[:, CONV_WIDTH:])
    acc = jnp.zeros((tt, CONV_WIDTH), F32) + bdw_ref[...]
    for k in range(CONV_K):
        acc = acc + wdw_ref[k:k + 1, :] * buf_ref[pl.ds(CONV_HALO - (CONV_K - 1) + k, tt), :]
    mu = jnp.mean(acc, axis=-1, keepdims=True)
    var = jnp.mean(jnp.square(acc - mu), axis=-1, keepdims=True)
    y = (acc - mu) * lax.rsqrt(var + EPS) * lng_ref[...] + lnb_ref[...]
    y = jax.nn.silu(y)
    y = jnp.dot(y.astype(BF16), wpw_ref[...], preferred_element_type=F32)
    o_ref[0] = (_rms_rows(y) * go_ref[...]).astype(o_ref.dtype)


def _conv(ztok, w_dw, b_dw, ln_g, ln_b, w_pw, g_out, tt=512):
    b, s, _ = ztok.shape
    cw = CONV_WIDTH
    hb = tt // CONV_HALO
    kern = functools.partial(_conv_kernel, tt=tt)
    row = lambda v: v.reshape(1, cw)
    return pl.pallas_call(
        kern,
        grid=(b, s // tt),
        in_specs=[pl.BlockSpec((1, tt, 2 * cw), lambda i, j: (i, j, 0)),
                  pl.BlockSpec((1, CONV_HALO, 2 * cw), lambda i, j: (i, jnp.maximum(j * hb - 1, 0), 0)),
                  pl.BlockSpec((CONV_K + 1, cw), lambda i, j: (0, 0)),
                  pl.BlockSpec((1, cw), lambda i, j: (0, 0)),
                  pl.BlockSpec((1, cw), lambda i, j: (0, 0)),
                  pl.BlockSpec((1, cw), lambda i, j: (0, 0)),
                  pl.BlockSpec((cw, cw), lambda i, j: (0, 0)),
                  pl.BlockSpec((1, cw), lambda i, j: (0, 0))],
        out_specs=pl.BlockSpec((1, tt, cw), lambda i, j: (i, j, 0)),
        out_shape=jax.ShapeDtypeStruct((b, s, cw), BF16),
        scratch_shapes=[pltpu.VMEM((CONV_HALO + tt, cw), F32)],
        compiler_params=_cparams(("parallel", "arbitrary")),
        name="conv_mixer",
    )(ztok, ztok, jnp.concatenate([w_dw, jnp.zeros((1, cw), F32)], 0), row(b_dw), row(ln_g), row(ln_b),
      w_pw.astype(BF16), row(g_out))


def _knorm_kernel(ks_ref, kw_ref, gs_ref, gw_ref, os_ref, ow_ref):
    for src, g_ref, dst in ((ks_ref, gs_ref, os_ref), (kw_ref, gw_ref, ow_ref)):
        x = src[0]
        for h in range(NSA_KV_HEADS):
            xh = x[:, h * HEAD_DIM:(h + 1) * HEAD_DIM]
            dst[0, h] = (_rms_rows(xh) * g_ref[...]).astype(dst.dtype)


def _knorm(ztok, col_s, col_w, gain_s, gain_w, tt=512):
    b, s, _ = ztok.shape
    kw = NSA_KV_HEADS * HEAD_DIM
    out = jax.ShapeDtypeStruct((b, NSA_KV_HEADS, s, HEAD_DIM), BF16)
    ospec = pl.BlockSpec((1, NSA_KV_HEADS, tt, HEAD_DIM), lambda i, j: (i, 0, j, 0))
    return pl.pallas_call(
        _knorm_kernel,
        grid=(b, s // tt),
        in_specs=[pl.BlockSpec((1, tt, kw), lambda i, j: (i, j, col_s // kw)),
                  pl.BlockSpec((1, tt, kw), lambda i, j: (i, j, col_w // kw)),
                  pl.BlockSpec((1, HEAD_DIM), lambda i, j: (0, 0)),
                  pl.BlockSpec((1, HEAD_DIM), lambda i, j: (0, 0))],
        out_specs=[ospec, ospec],
        out_shape=[out, out],
        compiler_params=_cparams(("parallel", "parallel")),
        name="nsa_key_norm",
    )(ztok, ztok, gain_s.reshape(1, HEAD_DIM), gain_w.reshape(1, HEAD_DIM))


def _compress_kernel(k_ref, v_ref, wka_ref, wkb_ref, ck_ref, w2k_ref, gk_ref,
                     wva_ref, wvb_ref, cv_ref, w2v_ref, ko_ref, vo_ref):
    hi = lax.Precision.HIGHEST
    n = k_ref.shape[1] // CMP_STRIDE
    kvw = k_ref.shape[2]
    nt = (((1,), (1,)), ((), ()))
    a, bm = jnp.zeros((n, kvw), F32), jnp.zeros((n, kvw), F32)
    at, bt = jnp.zeros((kvw, n), F32), jnp.zeros((kvw, n), F32)
    for l in range(CMP_STRIDE):
        kl = k_ref[0, pl.ds(l, n, stride=CMP_STRIDE), :]
        vl = v_ref[0, pl.ds(l, n, stride=CMP_STRIDE), :]
        a = a + jnp.dot(kl, wka_ref[l], precision=hi, preferred_element_type=F32)
        bm = bm + jnp.dot(kl, wkb_ref[l], precision=hi, preferred_element_type=F32)
        at = at + lax.dot_general(wva_ref[l], vl, nt, precision=hi, preferred_element_type=F32)
        bt = bt + lax.dot_general(wvb_ref[l], vl, nt, precision=hi, preferred_element_type=F32)
    pre = a + pltpu.roll(bm, n - 1, 0) + ck_ref[...]
    kc = jnp.dot(jax.nn.gelu(pre), w2k_ref[...], precision=hi, preferred_element_type=F32)
    for h in range(NSA_KV_HEADS):
        kh = kc[:, h * HEAD_DIM:(h + 1) * HEAD_DIM]
        ko_ref[0, h] = (_rms_rows(kh) * gk_ref[...]).astype(ko_ref.dtype)
    pre_t = at + pltpu.roll(bt, n - 1, 1) + cv_ref[...]
    vt = jnp.dot(w2v_ref[...], jax.nn.gelu(pre_t), precision=hi, preferred_element_type=F32)
    for h in range(NSA_KV_HEADS):
        vo_ref[0, h] = vt[h * HEAD_DIM:(h + 1) * HEAD_DIM, :].astype(vo_ref.dtype)


def _blockdiag2(w):
    z = jnp.zeros_like(w)
    return jnp.concatenate([jnp.concatenate([w, z], 1), jnp.concatenate([z, w], 1)], 0)


def _compress(ztok, col_k, col_v, pe, w1, w2, k_gain):
    b, s, _ = ztok.shape
    n = s // CMP_STRIDE
    hd, kvw = HEAD_DIM, NSA_KV_HEADS * HEAD_DIM
    hp = lax.Precision.HIGHEST

    def expand(w):
        wl = w.reshape(L_CMP, hd, hd)
        e = wl[:, None, :, None, :] * jnp.eye(NSA_KV_HEADS, dtype=F32)[None, :, None, :, None]
        e = e.reshape(L_CMP, kvw, kvw)
        return e[:CMP_STRIDE], e[CMP_STRIDE:]

    wka, wkb = expand(w1[0])
    wva, wvb = expand(w1[1])
    ck = jnp.tile(jnp.dot(pe[0].reshape(1, L_CMP * hd), w1[0], precision=hp), (1, NSA_KV_HEADS))
    cv = jnp.tile(jnp.dot(pe[1].reshape(1, L_CMP * hd), w1[1], precision=hp), (1, NSA_KV_HEADS)).T
    full = lambda shape: pl.BlockSpec(shape, lambda i: tuple(0 for _ in shape))
    return pl.pallas_call(
        _compress_kernel,
        grid=(b,),
        in_specs=[pl.BlockSpec((1, s, kvw), lambda i: (i, 0, col_k // kvw)),
                  pl.BlockSpec((1, s, kvw), lambda i: (i, 0, col_v // kvw)),
                  full((CMP_STRIDE, kvw, kvw)), full((CMP_STRIDE, kvw, kvw)), full((1, kvw)), full((kvw, kvw)),
                  full((1, hd)),
                  full((CMP_STRIDE, kvw, kvw)), full((CMP_STRIDE, kvw, kvw)), full((kvw, 1)), full((kvw, kvw))],
        out_specs=[pl.BlockSpec((1, NSA_KV_HEADS, n, hd), lambda i: (i, 0, 0, 0)),
                   pl.BlockSpec((1, NSA_KV_HEADS, hd, n), lambda i: (i, 0, 0, 0))],
        out_shape=[jax.ShapeDtypeStruct((b, NSA_KV_HEADS, n, hd), BF16),
                   jax.ShapeDtypeStruct((b, NSA_KV_HEADS, hd, n), BF16)],
        compiler_params=_cparams(("parallel",), _VMEM_LIMIT),
        name="nsa_compress",
    )(ztok, ztok, wka, wkb, ck, _blockdiag2(w2[0]), k_gain.reshape(1, hd),
      wva.transpose(0, 2, 1), wvb.transpose(0, 2, 1), cv, _blockdiag2(w2[1]).T)


def _t5_bias_by_dist(t5_table):
    n = np.arange(T5_MAX_DIST + 1)
    max_exact = T5_BUCKETS // 2
    nf = np.maximum(n, 1).astype(np.float32)
    large = max_exact + (np.log(nf / np.float32(max_exact)) / np.float32(math.log(T5_MAX_DIST / max_exact))
                         * np.float32(T5_BUCKETS - max_exact)).astype(np.int32)
    large = np.minimum(large, T5_BUCKETS - 1)
    bucket = np.where(n < max_exact, n, large)
    onehot = (bucket[:, None] == np.arange(T5_BUCKETS)[None, :]).astype(np.float32)
    return jnp.dot(onehot, t5_table, precision=lax.Precision.HIGHEST)


def _bias_tile(fdt, rows, stride, dist00, d_max=None):
    heads = fdt.shape[0]
    a0 = stride * (rows - 1)
    d_lo = dist00 - a0
    length = a0 + TQ
    d_hi = d_lo + length
    d_max = d_hi if d_max is None else d_max
    pieces = []
    for lo, hi, kind in ((d_lo, min(d_hi, 0), 'neg'), (max(d_lo, 0), min(d_hi, T5_MAX_DIST), 'tab'),
                         (max(d_lo, T5_MAX_DIST), min(d_hi, d_max), 'far'), (max(d_lo, d_max), d_hi, 'neg')):
        if hi > lo:
            pieces.append(fdt[:, lo:hi] if kind == 'tab'
                          else jnp.full((heads, hi - lo), NEG if kind == 'neg' else 0.0, F32))
    vec = jnp.concatenate(pieces, axis=1)
    c0 = -(-a0 // 128) * 128
    width = -(-(c0 + TQ) // 128) * 128
    vec = jnp.pad(vec, ((0, 0), (c0 - a0, width - (c0 - a0) - length)))

    def kern(v_ref, o_ref):
        x = jnp.broadcast_to(v_ref[0], (rows, width))
        o_ref[0] = pltpu.roll(x, 0, 1, stride=stride, stride_axis=0)[:, c0:c0 + TQ]

    return pl.pallas_call(
        kern,
        grid=(heads,),
        in_specs=[pl.BlockSpec((1, 1, width), lambda h: (h, 0, 0))],
        out_specs=pl.BlockSpec((1, rows, TQ), lambda h: (h, 0, 0)),
        out_shape=jax.ShapeDtypeStruct((heads, rows, TQ), F32),
        compiler_params=_cparams(("parallel",)),
        name="toeplitz_bias",
    )(vec.reshape(heads, 1, width))


def _nsa_bias_tiles(t5_table, seq):
    fd = _t5_bias_by_dist(t5_table).astype(F32)
    fdt = ((fd - fd[T5_MAX_DIST:]) * LOG2E).T
    n_cmp = seq // CMP_STRIDE
    qt = TQ // CMP_STRIDE
    r0 = n_cmp - qt
    band = _bias_tile(fdt, 2 * qt, CMP_STRIDE, CMP_STRIDE * qt - (L_CMP - 1))
    heads = fdt.shape[0]
    cmp_t = jnp.concatenate([jnp.zeros((heads, r0 - qt, TQ), F32), band,
                             jnp.full((heads, n_cmp - qt, TQ), NEG, F32)], axis=1)
    sel_t = _bias_tile(fdt, SEL_PAD + TQ, 1, SEL_PAD)
    win = _bias_tile(fdt, WINDOW + TQ, 1, WINDOW, d_max=WINDOW)
    rw = np.arange(WINDOW + TQ)[None, :, None]
    win_t = jnp.stack([jnp.where(rw >= WINDOW - q0, win, NEG) for q0 in (0, TQ, 2 * TQ)])
    split = lambda t: t.reshape(*t.shape[:-3], NSA_KV_HEADS, NSA_GROUP, *t.shape[-2:])

    def wide(t):
        t = jnp.swapaxes(split(t), -3, -2)
        return t.reshape(*t.shape[:-2], NSA_GROUP * TQ)

    return wide(cmp_t), wide(sel_t), wide(win_t)


def _q_head(qt_ref, g, qg_ref):
    q = qt_ref[0, g * HEAD_DIM:(g + 1) * HEAD_DIM, :]
    inv = lax.rsqrt(jnp.mean(q * q, axis=0, keepdims=True) + EPS)
    return (q * inv * qg_ref[...]).astype(BF16)


def _nsa_cmp_kernel(qt_ref, qg_ref, kc_ref, vct_ref, bias_ref, ov_ref, oc_ref, sel_ref, imp_ref, *, n_cmp, n_sel):
    qi = pl.program_id(1)
    qt = TQ // CMP_STRIDE
    gw = NSA_GROUP * HEAD_DIM
    kvs = range(NSA_KV_HEADS)
    row0 = pl.multiple_of((n_cmp - qt) - qi * qt, qt)
    qw = [jnp.concatenate([_q_head(qt_ref, kv * NSA_GROUP + g, qg_ref) for g in range(NSA_GROUP)], axis=1)
          for kv in kvs]

    def attend(n):
        for kv in kvs:
            s = (jnp.dot(kc_ref[0, kv, 0:n, :], qw[kv], preferred_element_type=F32)
                 + bias_ref[kv, pl.ds(row0, n), :])
            m = jnp.max(s, axis=0, keepdims=True)
            m = jnp.where(m < 0.5 * NEG, 0.0, m)
            p = jnp.exp2(s - m)
            p = p * (1.0 / jnp.maximum(jnp.sum(p, axis=0, keepdims=True), 1e-30))
            oc = jnp.dot(vct_ref[0, kv, :, 0:n], p.astype(BF16), preferred_element_type=F32)
            psum = jnp.zeros((n, TQ), F32)
            for g in range(NSA_GROUP):
                oc_ref[0, kv * gw + g * HEAD_DIM:kv * gw + (g + 1) * HEAD_DIM, :] = oc[:, g * TQ:(g + 1) * TQ]
                psum = psum + p[:, g * TQ:(g + 1) * TQ]
            hi = psum.astype(BF16)
            lo = (psum - hi.astype(F32)).astype(BF16)
            imp_ref[kv] = (jnp.dot(ov_ref[:, 0:n], hi, preferred_element_type=F32)
                           + jnp.dot(ov_ref[:, 0:n], lo, preferred_element_type=F32))

    chunk = min(n_cmp, 128)
    n_chunks = n_cmp // chunk
    need = lax.div((qi + 1) * qt + (chunk - 1), chunk)
    for c in range(1, n_chunks + 1):
        pl.when(need == c)(functools.partial(attend, c * chunk))

    blk = lax.broadcasted_iota(jnp.int32, (n_sel, TQ), 0)
    blk_t = lax.shift_right_logical(qi * TQ + lax.broadcasted_iota(jnp.int32, (n_sel, TQ), 1), L_SEL.bit_length() - 1)
    forced = (blk == 0) | (blk > blk_t - N_LOCAL)
    v0 = tuple(jnp.where(blk > blk_t, -jnp.inf, jnp.where(forced, SEL_FORCE, imp_ref[kv])) for kv in kvs)

    def pick(_, vs):
        out = []
        for v in vs:
            m = jnp.max(v, axis=0, keepdims=True)
            first = jnp.min(jnp.where(v == m, blk, n_sel), axis=0, keepdims=True)
            out.append(jnp.where(blk == first, -jnp.inf, v))
        return tuple(out)

    vs = lax.fori_loop(0, min(N_SELECT, n_sel), pick, v0)
    for kv in kvs:
        sel_ref[0, kv] = jnp.where((vs[kv] == -jnp.inf) & (v0[kv] > -jnp.inf), 0.0, NEG)


def _nsa_cmp(zt, qg, kc, vct, bias_c, ovt):
    b, _, s = zt.shape
    n_cmp, n_sel = s // CMP_STRIDE, s // L_SEL
    kvh = NSA_KV_HEADS
    kern = functools.partial(_nsa_cmp_kernel, n_cmp=n_cmp, n_sel=n_sel)
    return pl.pallas_call(
        kern,
        grid=(b, s // TQ),
        in_specs=[pl.BlockSpec((1, NSA_WIDTH, TQ), lambda i, j: (i, 0, j)),
                  pl.BlockSpec((HEAD_DIM, TQ), lambda i, j: (0, 0)),
                  pl.BlockSpec((1, kvh, n_cmp, HEAD_DIM), lambda i, j: (i, 0, 0, 0)),
                  pl.BlockSpec((1, kvh, HEAD_DIM, n_cmp), lambda i, j: (i, 0, 0, 0)),
                  pl.BlockSpec((kvh, bias_c.shape[1], NSA_GROUP * TQ), lambda i, j: (0, 0, 0)),
                  pl.BlockSpec((n_sel, n_cmp), lambda i, j: (0, 0))],
        out_specs=[pl.BlockSpec((1, NSA_WIDTH, TQ), lambda i, j: (i, 0, j)),
                   pl.BlockSpec((1, kvh, n_sel, TQ), lambda i, j: (i, 0, 0, j))],
        out_shape=[jax.ShapeDtypeStruct((b, NSA_WIDTH, s), F32),
                   jax.ShapeDtypeStruct((b, kvh, n_sel, s), F32)],
        scratch_shapes=[pltpu.VMEM((kvh, n_sel, TQ), F32)],
        compiler_params=_cparams(("parallel", "parallel"), _VMEM_LIMIT),
        name="nsa_compressed_select",
    )(zt, qg, kc, vct, bias_c, ovt)


def _nsa_main_kernel(qt_ref, qg_ref, ks_ref, vst_ref, kw_ref, vwt_ref, sel_ref, bs_ref, bw_ref, gate_ref,
                     oc_ref, o_ref, acc_ref, s_ref):
    qi = pl.program_id(1)
    q0 = pl.multiple_of(qi * TQ, TQ)
    near = SEL_PAD + TQ
    gw = NSA_GROUP * HEAD_DIM
    kvs = range(NSA_KV_HEADS)

    def expand_sel(kv, first_blk, n_blk):
        rows = [jnp.broadcast_to(sel_ref[0, kv, pl.ds(first_blk + r, 1), :], (L_SEL, TQ)) for r in range(n_blk)]
        rows = jnp.concatenate(rows, axis=0)
        return jnp.concatenate([rows] * NSA_GROUP, axis=1)

    qw = [jnp.concatenate([_q_head(qt_ref, kv * NSA_GROUP + g, qg_ref) for g in range(NSA_GROUP)], axis=1)
          for kv in kvs]
    m0 = []
    for kv in kvs:
        s = (jnp.dot(ks_ref[0, kv, pl.ds(q0, near), :], qw[kv], preferred_element_type=F32) + bs_ref[kv]
             + expand_sel(kv, qi * (TQ // L_SEL), near // L_SEL))
        m = jnp.max(s, axis=0, keepdims=True)
        p = jnp.exp2(s - m).astype(BF16)
        acc_ref[kv] = jnp.dot(vst_ref[0, kv, :, pl.ds(q0, near)], p, preferred_element_type=F32)
        m0.append(m)

    for kv in kvs:
        s = jnp.dot(kw_ref[0, kv, pl.ds(q0, WINDOW + TQ), :], qw[kv], preferred_element_type=F32) + bw_ref[0, kv]
        p = jnp.exp2(s - jnp.max(s, axis=0, keepdims=True)).astype(BF16)
        ow = jnp.dot(vwt_ref[0, kv, :, pl.ds(q0, WINDOW + TQ)], p, preferred_element_type=F32)
        ow = ow[:HEAD_DIM] * (1.0 / jnp.maximum(ow[HEAD_DIM:HEAD_DIM + 1], 1e-30))
        for g in range(NSA_GROUP):
            gates = jax.nn.sigmoid(gate_ref[0, kv, g * N_BRANCH:(g + 1) * N_BRANCH, :])
            rows = slice(kv * gw + g * HEAD_DIM, kv * gw + (g + 1) * HEAD_DIM)
            o_ref[0, rows, :] = gates[0:1] * oc_ref[0, rows, :] + gates[2:3] * ow[:, g * TQ:(g + 1) * TQ]

    def scores(c, slot):
        r0 = pl.multiple_of(c * TQ, TQ)
        mc = []
        for kv in kvs:
            s = (jnp.dot(ks_ref[0, kv, pl.ds(r0, TQ), :], qw[kv], preferred_element_type=F32)
                 + expand_sel(kv, c * (TQ // L_SEL), TQ // L_SEL))
            s_ref[slot, kv] = s
            mc.append(jnp.max(s, axis=0, keepdims=True))
        return tuple(mc)

    def consume(c, slot, m_old, mc):
        r0 = pl.multiple_of(c * TQ, TQ)
        m_out = []
        for kv in kvs:
            m_new = jnp.maximum(m_old[kv], mc[kv])
            alpha = jnp.exp2(m_old[kv] - m_new)
            p = jnp.exp2((s_ref[slot, kv] - m_new).astype(BF16))
            acc_ref[kv] = alpha * acc_ref[kv] + jnp.dot(vst_ref[0, kv, :, pl.ds(r0, TQ)], p,
                                                        preferred_element_type=F32)
            m_out.append(m_new)
        return tuple(m_out)

    first = SEL_PAD // TQ
    n_far = qi - first

    def pair(i, carry):
        m, mc = carry
        c = first + 2 * i
        mc1 = scores(c + 1, 1)
        m = consume(c, 0, m, mc)
        mc2 = scores(c + 2, 0)
        m = consume(c + 1, 1, m, mc1)
        return m, mc2

    m_far, mc_far = lax.fori_loop(0, n_far // 2, pair, (tuple(m0), scores(first, 0)))

    @pl.when((n_far > 0) & (n_far % 2 == 1))
    def _():
        consume(first + n_far - 1, 0, m_far, mc_far)

    for kv in kvs:
        os = acc_ref[kv]
        os = os[:HEAD_DIM] * (1.0 / jnp.maximum(os[HEAD_DIM:HEAD_DIM + 1], 1e-30))
        for g in range(NSA_GROUP):
            gate = jax.nn.sigmoid(gate_ref[0, kv, g * N_BRANCH + 1:g * N_BRANCH + 2, :])
            rows = slice(kv * gw + g * HEAD_DIM, kv * gw + (g + 1) * HEAD_DIM)
            o_ref[0, rows, :] = o_ref[0, rows, :] + gate * os[:, g * TQ:(g + 1) * TQ]


def _nsa_main(zt, qg, ks_p, vst_p, kw_p, vwt_p, sel_p, bias_s, bias_w, gates_t, oc_t):
    b, _, s = zt.shape
    kvh = NSA_KV_HEADS
    sp, wp = ks_p.shape[2], kw_p.shape[2]
    nb, vr = sel_p.shape[2], vst_p.shape[2]
    once = pl.Buffered(1)
    return pl.pallas_call(
        _nsa_main_kernel,
        grid=(b, s // TQ),
        in_specs=[pl.BlockSpec((1, NSA_WIDTH, TQ), lambda i, j: (i, 0, j)),
                  pl.BlockSpec((HEAD_DIM, TQ), lambda i, j: (0, 0)),
                  pl.BlockSpec((1, kvh, sp, HEAD_DIM), lambda i, j: (i, 0, 0, 0), pipeline_mode=once),
                  pl.BlockSpec((1, kvh, vr, sp), lambda i, j: (i, 0, 0, 0), pipeline_mode=once),
                  pl.BlockSpec((1, kvh, wp, HEAD_DIM), lambda i, j: (i, 0, 0, 0), pipeline_mode=once),
                  pl.BlockSpec((1, kvh, vr, wp), lambda i, j: (i, 0, 0, 0), pipeline_mode=once),
                  pl.BlockSpec((1, kvh, nb, TQ), lambda i, j: (i, 0, 0, j)),
                  pl.BlockSpec((kvh, SEL_PAD + TQ, NSA_GROUP * TQ), lambda i, j: (0, 0, 0), pipeline_mode=once),
                  pl.BlockSpec((1, kvh, WINDOW + TQ, NSA_GROUP * TQ), lambda i, j: (jnp.minimum(j, 2), 0, 0, 0)),
                  pl.BlockSpec((1, kvh, 16, TQ), lambda i, j: (i, 0, 0, j)),
                  pl.BlockSpec((1, NSA_WIDTH, TQ), lambda i, j: (i, 0, j))],
        out_specs=pl.BlockSpec((1, NSA_WIDTH, TQ), lambda i, j: (i, 0, j)),
        out_shape=jax.ShapeDtypeStruct((b, NSA_WIDTH, s), F32),
        scratch_shapes=[pltpu.VMEM((kvh, vr, NSA_GROUP * TQ), F32),
                        pltpu.VMEM((2, kvh, TQ, NSA_GROUP * TQ), F32)],
        compiler_params=_cparams(("parallel", "arbitrary"), _VMEM_LIMIT),
        name="nsa_selected_window",
    )(zt, qg, ks_p, vst_p, kw_p, vwt_p, sel_p, bias_s, bias_w, gates_t, oc_t)


def _mixout_kernel(gy_ref, yc_ref, yn_ref, h_ref, wglu_ref, go_ref, wo_ref, o_ref):
    sw = SSM_WIDTH
    half = sw // 2
    ag = (jnp.dot(gy_ref[0, 0].astype(BF16), wglu_ref[0:half, :], preferred_element_type=F32)
          + jnp.dot(gy_ref[1, 0].astype(BF16), wglu_ref[half:, :], preferred_element_type=F32))
    ys = ag[:, :sw] * jax.nn.sigmoid(ag[:, sw:])
    ys = (_rms_rows(ys) * go_ref[:, 0:sw]).astype(BF16)
    yn = (_rms_rows(yn_ref[0]) * go_ref[:, 2 * sw:]).astype(BF16)
    out = (jnp.dot(ys, wo_ref[0:sw, :], preferred_element_type=F32)
           + jnp.dot(yc_ref[0], wo_ref[sw:2 * sw, :], preferred_element_type=F32)
           + jnp.dot(yn, wo_ref[2 * sw:, :], preferred_element_type=F32))
    o_ref[0] = h_ref[0] + out


def _mixout(gy, yc, yn, h, w_glu, g_out, w_out, tm=512):
    b, s, d = h.shape
    tok = lambda w: pl.BlockSpec((1, tm, w), lambda i, j: (i, j, 0))
    return pl.pallas_call(
        _mixout_kernel,
        grid=(b, s // tm),
        in_specs=[pl.BlockSpec((2, 1, tm, SSM_WIDTH // 2), lambda i, j: (0, i, j, 0)),
                  tok(CONV_WIDTH), tok(NSA_WIDTH), tok(d),
                  pl.BlockSpec((SSM_WIDTH, 2 * SSM_WIDTH), lambda i, j: (0, 0)),
                  pl.BlockSpec((1, d), lambda i, j: (0, 0)),
                  pl.BlockSpec((d, d), lambda i, j: (0, 0))],
        out_specs=tok(d),
        out_shape=jax.ShapeDtypeStruct((b, s, d), F32),
        compiler_params=_cparams(("parallel", "parallel")),
        name="mix_out",
    )(gy, yc, yn, h, w_glu.astype(BF16), g_out.reshape(1, d), w_out.astype(BF16))


def _memkv_kernel(mem_ref, g_ref, w_ref, kg_ref, k_ref, v_ref):
    mn = (_rms_rows(mem_ref[0]) * g_ref[...]).astype(BF16)
    kv = jnp.dot(mn, w_ref[...], preferred_element_type=F32)
    for h in range(X_HEADS):
        cols = slice(h * HEAD_DIM, (h + 1) * HEAD_DIM)
        k_ref[0, :, cols] = (_rms_rows(kv[:, cols]) * kg_ref[...]).astype(k_ref.dtype)
    v_ref[0] = kv[:, X_WIDTH:].astype(v_ref.dtype)


def _memkv(mem, gain, w_kv, k_gain):
    b, m, d = mem.shape
    out = jax.ShapeDtypeStruct((b, m, X_WIDTH), BF16)
    return pl.pallas_call(
        _memkv_kernel,
        grid=(b,),
        in_specs=[pl.BlockSpec((1, m, d), lambda i: (i, 0, 0)),
                  pl.BlockSpec((1, d), lambda i: (0, 0)),
                  pl.BlockSpec((d, 2 * X_WIDTH), lambda i: (0, 0)),
                  pl.BlockSpec((1, HEAD_DIM), lambda i: (0, 0))],
        out_specs=[pl.BlockSpec((1, m, X_WIDTH), lambda i: (i, 0, 0))] * 2,
        out_shape=[out, out],
        compiler_params=_cparams(("parallel",)),
        name="cross_mem_kv",
    )(mem, gain.reshape(1, d), w_kv.astype(BF16), k_gain.reshape(1, HEAD_DIM))


def _cross_kernel(h_ref, g_ref, wq_ref, qg_ref, k_ref, v_ref, wo_ref, o_ref):
    h = h_ref[0]
    hn = (_rms_rows(h) * g_ref[...]).astype(BF16)
    q = jnp.dot(hn, wq_ref[...], preferred_element_type=F32)
    out = h
    for hd in range(X_HEADS):
        cols = slice(hd * HEAD_DIM, (hd + 1) * HEAD_DIM)
        qh = (_rms_rows(q[:, cols]) * qg_ref[...]).astype(BF16)
        s = lax.dot_general(qh, k_ref[0, :, cols], (((1,), (1,)), ((), ())), preferred_element_type=F32)
        p = jnp.exp(s - jnp.max(s, axis=-1, keepdims=True))
        p = p * (1.0 / jnp.sum(p, axis=-1, keepdims=True))
        o = jnp.dot(p.astype(BF16), v_ref[0, :, cols], preferred_element_type=F32)
        out = out + jnp.dot(o.astype(BF16), wo_ref[cols, :], preferred_element_type=F32)
    o_ref[0] = out


def _cross(h, gain, w_q, q_gain, k, v, w_o, tm=1024):
    b, s, d = h.shape
    m = k.shape[1]
    return pl.pallas_call(
        _cross_kernel,
        grid=(b, s // tm),
        in_specs=[pl.BlockSpec((1, tm, d), lambda i, j: (i, j, 0)),
                  pl.BlockSpec((1, d), lambda i, j: (0, 0)),
                  pl.BlockSpec((d, X_WIDTH), lambda i, j: (0, 0)),
                  pl.BlockSpec((1, HEAD_DIM), lambda i, j: (0, 0)),
                  pl.BlockSpec((1, m, X_WIDTH), lambda i, j: (i, 0, 0)),
                  pl.BlockSpec((1, m, X_WIDTH), lambda i, j: (i, 0, 0)),
                  pl.BlockSpec((X_WIDTH, d), lambda i, j: (0, 0))],
        out_specs=pl.BlockSpec((1, tm, d), lambda i, j: (i, j, 0)),
        out_shape=jax.ShapeDtypeStruct((b, s, d), F32),
        compiler_params=_cparams(("parallel", "parallel")),
        name="cross_attention",
    )(h, gain.reshape(1, d), w_q.astype(BF16), (q_gain * HEAD_DIM ** -0.5).reshape(1, HEAD_DIM), k, v,
      w_o.astype(BF16))


def _ffn_kernel(h_ref, g_ref, wg_ref, wv_ref, wd_ref, o_ref, xn_ref, acc_ref):
    f = pl.program_id(1)

    @pl.when(f == 0)
    def _():
        xn_ref[...] = (_rms_rows(h_ref[...]) * g_ref[...]).astype(BF16)
        acc_ref[...] = jnp.zeros_like(acc_ref)

    x = xn_ref[...]
    gate = jnp.dot(x, wg_ref[...], preferred_element_type=F32)
    val = jnp.dot(x, wv_ref[...], preferred_element_type=F32)
    act = (jax.nn.silu(gate) * val).astype(BF16)
    acc_ref[...] += jnp.dot(act, wd_ref[...], preferred_element_type=F32)

    @pl.when(f == pl.num_programs(1) - 1)
    def _():
        o_ref[...] = h_ref[...] + acc_ref[...]


def _ffn(h2d, gain, w_up, w_down, tm=1024):
    t, d = h2d.shape
    nf = D_FF // FF_CHUNK
    wb = w_up.astype(BF16)
    return pl.pallas_call(
        _ffn_kernel,
        grid=(t // tm, nf),
        in_specs=[pl.BlockSpec((tm, d), lambda i, f: (i, 0)),
                  pl.BlockSpec((1, d), lambda i, f: (0, 0)),
                  pl.BlockSpec((d, FF_CHUNK), lambda i, f: (0, f)),
                  pl.BlockSpec((d, FF_CHUNK), lambda i, f: (0, f + nf)),
                  pl.BlockSpec((FF_CHUNK, d), lambda i, f: (f, 0))],
        out_specs=pl.BlockSpec((tm, d), lambda i, f: (i, 0)),
        out_shape=jax.ShapeDtypeStruct((t, d), F32),
        scratch_shapes=[pltpu.VMEM((tm, d), BF16), pltpu.VMEM((tm, d), F32)],
        compiler_params=_cparams(("parallel", "arbitrary"), _VMEM_LIMIT),
        name="ffn_swiglu",
    )(h2d, gain.reshape(1, d), wb, wb, w_down.astype(BF16))


def _router_kernel(h_ref, g_ref, wr_ref, xn_ref, gate_ref, asg_ref):
    xn = _rms_rows(h_ref[...]) * g_ref[...]
    xn_ref[...] = xn.astype(BF16)
    logits = jnp.dot(xn, wr_ref[...], precision=lax.Precision.HIGHEST, preferred_element_type=F32)
    lane = lax.broadcasted_iota(jnp.int32, logits.shape, 1)
    lg = jnp.where(lane < N_EXPERTS, logits, -jnp.inf)
    m1 = jnp.max(lg, axis=-1, keepdims=True)
    i1 = jnp.min(jnp.where(lg == m1, lane, 128), axis=-1, keepdims=True)
    lg2 = jnp.where(lane == i1, -jnp.inf, lg)
    m2 = jnp.max(lg2, axis=-1, keepdims=True)
    i2 = jnp.min(jnp.where(lg2 == m2, lane, 128), axis=-1, keepdims=True)
    e = jnp.exp(m2 - m1)
    den = 1.0 + e
    gate_ref[...] = jnp.where(lane == i1, 1.0 / den, jnp.where(lane == i2, e / den, 0.0))
    asg_ref[...] = ((lane == i1) | (lane == i2)).astype(jnp.int32)


def _router(h2d, gain, w_router, tm=512):
    t, d = h2d.shape
    wr = jnp.concatenate([w_router, jnp.zeros((d, 128 - N_EXPERTS), F32)], axis=1)
    return pl.pallas_call(
        _router_kernel,
        grid=(t // tm,),
        in_specs=[pl.BlockSpec((tm, d), lambda i: (i, 0)),
                  pl.BlockSpec((1, d), lambda i: (0, 0)),
                  pl.BlockSpec((d, 128), lambda i: (0, 0))],
        out_specs=[pl.BlockSpec((tm, d), lambda i: (i, 0)),
                   pl.BlockSpec((tm, 128), lambda i: (i, 0)),
                   pl.BlockSpec((tm, 128), lambda i: (i, 0))],
        out_shape=[jax.ShapeDtypeStruct((t, d), BF16), jax.ShapeDtypeStruct((t, 128), F32),
                   jax.ShapeDtypeStruct((t, 128), jnp.int32)],
        compiler_params=_cparams(("parallel",)),
        name="moe_router",
    )(h2d, gain.reshape(1, d), wr)


def _moe_windows(rb, lo, hi, active):
    lo_l = jnp.clip(lo - rb * MOE_TB, 0, MOE_TB)
    hi_l = jnp.clip(hi - rb * MOE_TB, 0, MOE_TB)
    shift = MOE_SUB.bit_length() - 1
    w0 = jnp.minimum(lax.shift_left(lax.shift_right_logical(lo_l, shift), shift), MOE_TB - MOE_WIN)
    has = active & (hi_l > lo_l)
    w1 = jnp.minimum(w0 + MOE_WIN, MOE_TB - MOE_WIN)
    return ((w0, 0, has), (w1, w0 + MOE_WIN, has & (hi_l > w0 + MOE_WIN)))


def _moe_gather_kernel(rb_ref, lo_ref, hi_ref, first_ref, tgt_ref, x_ref, o_ref):
    e, j, slot = pl.program_id(0), pl.program_id(1), pl.program_id(2)
    rb = rb_ref[e, j, slot]

    @pl.when(first_ref[e, j, slot] == 1)
    def _():
        o_ref[...] = jnp.zeros_like(o_ref)

    active = (slot == 0) | (rb != rb_ref[e, j, 0])
    tgt = tgt_ref[0]
    n_sub = MOE_TB // MOE_TS
    for sub in range(n_sub):
        toks = slice(sub * MOE_TS, (sub + 1) * MOE_TS)
        for start, cutoff, needed in _moe_windows(rb, lo_ref[e, j * n_sub + sub], hi_ref[e, j * n_sub + sub], active):

            @pl.when(needed)
            def _():
                local = start + lax.broadcasted_iota(jnp.int32, (MOE_WIN, MOE_TS), 0)
                rows = jnp.where(local >= cutoff, rb * MOE_TB + local, -2)
                onehot = jnp.where(tgt[:, toks] == rows, 1.0, 0.0).astype(BF16)
                part = jnp.dot(onehot, x_ref[toks, :], preferred_element_type=F32)
                sl = pl.ds(pl.multiple_of(start, MOE_SUB), MOE_WIN)
                o_ref[sl, :] = o_ref[sl, :] + part.astype(o_ref.dtype)


def _moe_ffn_kernel(exp_ref, nused_ref, x_ref, wg_ref, wv_ref, wd_ref, o_ref, acc_ref):
    r, f = pl.program_id(0), pl.program_id(1)
    used = r < nused_ref[0]

    @pl.when(f == 0)
    def _():
        acc_ref[...] = jnp.zeros_like(acc_ref)

    @pl.when(used)
    def _():
        x = x_ref[...]
        gate = jnp.dot(x, wg_ref[0], preferred_element_type=F32)
        val = jnp.dot(x, wv_ref[0], preferred_element_type=F32)
        act = (jax.nn.silu(gate) * val).astype(BF16)
        acc_ref[...] += jnp.dot(act, wd_ref[0], preferred_element_type=F32)

    @pl.when(f == pl.num_programs(1) - 1)
    def _():
        o_ref[...] = acc_ref[...].astype(o_ref.dtype)
    del exp_ref


def _moe_scatter_kernel(rb_ref, lo_ref, hi_ref, tgt_ref, gate_ref, y_ref, h_ref, o_ref):
    j, e, slot = pl.program_id(0), pl.program_id(1), pl.program_id(2)
    rb = rb_ref[e, j, slot]

    @pl.when((e == 0) & (slot == 0))
    def _():
        o_ref[...] = h_ref[...]

    active = (slot == 0) | (rb != rb_ref[e, j, 0])
    mine = lax.broadcasted_iota(jnp.int32, tgt_ref.shape, 1) == e
    tgt = jnp.sum(jnp.where(mine, tgt_ref[...], 0), axis=1, keepdims=True)
    gate = jnp.sum(jnp.where(mine, gate_ref[...], 0.0), axis=1, keepdims=True)
    n_sub = MOE_TB // MOE_TS
    for sub in range(n_sub):
        toks = slice(sub * MOE_TS, (sub + 1) * MOE_TS)
        for start, cutoff, needed in _moe_windows(rb, lo_ref[e, j * n_sub + sub], hi_ref[e, j * n_sub + sub], active):

            @pl.when(needed)
            def _():
                local = start + lax.broadcasted_iota(jnp.int32, (MOE_TS, MOE_WIN), 1)
                rows = jnp.where(local >= cutoff, rb * MOE_TB + local, -2)
                onehot = jnp.where(tgt[toks] == rows, 1.0, 0.0).astype(BF16)
                y = y_ref[pl.ds(pl.multiple_of(start, MOE_SUB), MOE_WIN), :]
                o_ref[toks, :] = o_ref[toks, :] + gate[toks] * jnp.dot(onehot, y, preferred_element_type=F32)


def _moe(h2d, gain, w_router, w_up, w_down):
    t, d = h2d.shape
    tb = MOE_TB
    nj = t // tb
    n_rb = (t * TOP_K) // tb + N_EXPERTS
    xn, gates, asg = _router(h2d, gain, w_router)
    asg = asg[:, :N_EXPERTS]
    gates = gates[:, :N_EXPERTS]
    cs = jnp.cumsum(asg, axis=0)
    rank = cs - asg
    counts = cs[-1]
    padded = (counts + tb - 1) // tb * tb
    pad_end = jnp.cumsum(padded)
    start_p = pad_end - padded
    tgt = jnp.where(asg == 1, start_p[None, :] + rank, -1).astype(jnp.int32)
    ts = MOE_TS
    cb = jnp.concatenate([jnp.zeros((1, N_EXPERTS), jnp.int32), cs[ts - 1::ts]], axis=0)
    lo = (start_p[None, :] + cb[:-1]).T.astype(jnp.int32)
    hi = (start_p[None, :] + cb[1:]).T.astype(jnp.int32)
    rb0 = lo[:, ::tb // ts] // tb
    rb1 = jnp.maximum(rb0, (hi[:, tb // ts - 1::tb // ts] - 1) // tb)
    rb = jnp.stack([rb0, rb1], axis=-1).astype(jnp.int32)
    flat = rb.reshape(-1)
    first = jnp.concatenate([jnp.ones((1,), jnp.int32), (flat[1:] != flat[:-1]).astype(jnp.int32)])
    first = first.reshape(N_EXPERTS, nj, 2)
    n_used = (pad_end[-1] // tb).astype(jnp.int32).reshape(1)
    blk_exp = jnp.minimum(jnp.searchsorted(pad_end, jnp.arange(n_rb) * tb, side='right'),
                          N_EXPERTS - 1).astype(jnp.int32)

    xs = pl.pallas_call(
        _moe_gather_kernel,
        grid_spec=pltpu.PrefetchScalarGridSpec(
            num_scalar_prefetch=4,
            grid=(N_EXPERTS, nj, 2),
            in_specs=[pl.BlockSpec((1, 1, tb), lambda e, j, s, *_: (e, 0, j)),
                      pl.BlockSpec((tb, d), lambda e, j, s, *_: (j, 0))],
            out_specs=pl.BlockSpec((tb, d), lambda e, j, s, rb_ref, *_: (rb_ref[e, j, s], 0))),
        out_shape=jax.ShapeDtypeStruct((n_rb * tb, d), BF16),
        compiler_params=_cparams(("arbitrary", "arbitrary", "arbitrary"), _VMEM_LIMIT),
        name="moe_gather",
    )(rb, lo, hi, first, tgt.T.reshape(N_EXPERTS, 1, t), xn)

    nf = D_FF // FF_CHUNK
    wub = w_up.astype(BF16)
    ys = pl.pallas_call(
        _moe_ffn_kernel,
        grid_spec=pltpu.PrefetchScalarGridSpec(
            num_scalar_prefetch=2,
            grid=(n_rb, nf),
            in_specs=[pl.BlockSpec((tb, d), lambda r, f, *_: (r, 0)),
                      pl.BlockSpec((1, d, FF_CHUNK), lambda r, f, ex, nu: (ex[r], 0, f)),
                      pl.BlockSpec((1, d, FF_CHUNK), lambda r, f, ex, nu: (ex[r], 0, f + nf)),
                      pl.BlockSpec((1, FF_CHUNK, d), lambda r, f, ex, nu: (ex[r], f, 0))],
            out_specs=pl.BlockSpec((tb, d), lambda r, f, *_: (r, 0)),
            scratch_shapes=[pltpu.VMEM((tb, d), F32)]),
        out_shape=jax.ShapeDtypeStruct((n_rb * tb, d), BF16),
        compiler_params=_cparams(("arbitrary", "arbitrary"), _VMEM_LIMIT),
        name="moe_expert_ffn",
    )(blk_exp, n_used, xs, wub, wub, w_down.astype(BF16))

    return pl.pallas_call(
        _moe_scatter_kernel,
        grid_spec=pltpu.PrefetchScalarGridSpec(
            num_scalar_prefetch=3,
            grid=(nj, N_EXPERTS, 2),
            in_specs=[pl.BlockSpec((tb, N_EXPERTS), lambda j, e, s, *_: (j, 0)),
                      pl.BlockSpec((tb, N_EXPERTS), lambda j, e, s, *_: (j, 0)),
                      pl.BlockSpec((tb, d), lambda j, e, s, rb_ref, *_: (rb_ref[e, j, s], 0)),
                      pl.BlockSpec((tb, d), lambda j, e, s, *_: (j, 0))],
            out_specs=pl.BlockSpec((tb, d), lambda j, e, s, *_: (j, 0))),
        out_shape=jax.ShapeDtypeStruct((t, d), F32),
        compiler_params=_cparams(("arbitrary", "arbitrary", "arbitrary"), _VMEM_LIMIT),
        name="moe_scatter",
    )(rb, lo, hi, tgt, gates, ys, h2d)


_COL_CONV, _COL_SSM, _COL_KC, _COL_VC, _COL_KS, _COL_KW = 0, 512, 768, 896, 1024, 1152
_ROW_Q, _ROW_VS, _ROW_VW, _ROW_G = 0, 512, 640, 768


def _split_w_in(w_in):
    kvw = NSA_KV_HEADS * HEAD_DIM
    cuts = np.cumsum([0, SSM_WIDTH, 2 * CONV_WIDTH, NSA_WIDTH] + [kvw] * 6 + [N_BRANCH * NSA_HEADS])
    seg = lambda i: w_in[:, cuts[i]:cuts[i + 1]]
    ssm, conv, q, k_c, v_c, k_s, v_s, k_w, v_w, gate = (seg(i) for i in range(10))
    w_tok = jnp.concatenate([conv, ssm, k_c, v_c, k_s, k_w], axis=1).astype(BF16)
    gate = jnp.concatenate([gate, jnp.zeros((w_in.shape[0], 8), F32)], axis=1)
    w_t = jnp.concatenate([q, v_s, v_w, gate], axis=1).T.astype(BF16)
    return w_tok, w_t


def _layer_mixers(h, p, t5_tiles, s5_perm):
    b, s, d = h.shape
    w_tok, w_t = _split_w_in(p['w_in'])
    ztok, zt = _proj(h, p['norm_mix'], w_tok, w_t)

    tables = _s5_tables(p['ssm_lambda_re'], p['ssm_lambda_im'], p['ssm_log_dt'], p['ssm_b_re'], p['ssm_b_im'],
                        p['ssm_c_re'], p['ssm_c_im'], p['ssm_d'])
    gy = _s5_unpack(_s5(_s5_pack(ztok, _COL_SSM, s5_perm[0]), tables, b), s5_perm[1], b)

    g_out = p['mix_out_norm']
    yc = _conv(ztok, p['conv_w_dw'], p['conv_b_dw'], p['conv_ln_g'], p['conv_ln_b'], p['conv_w_pw'],
               g_out[SSM_WIDTH:SSM_WIDTH + CONV_WIDTH])

    kvw = NSA_KV_HEADS * HEAD_DIM
    n_cmp = s // CMP_STRIDE
    k_norm = p['nsa_k_norm']
    kc, vct = _compress(ztok, _COL_KC, _COL_VC, p['nsa_cmp_pe'], p['nsa_cmp_w1'], p['nsa_cmp_w2'], k_norm[0])
    ks, kw = _knorm(ztok, _COL_KS, _COL_KW, k_norm[1], k_norm[2])
    qg = jnp.broadcast_to((p['nsa_q_norm'] * (HEAD_DIM ** -0.5 * LOG2E))[:, None], (HEAD_DIM, TQ))
    bias_c, bias_s, bias_w = t5_tiles
    n_sel = s // L_SEL
    cs_ = np.arange(n_cmp) * CMP_STRIDE
    ss_ = np.arange(n_sel) * L_SEL
    ov = np.maximum(np.minimum(cs_[:, None] + L_CMP, ss_[None, :] + L_SEL) - np.maximum(cs_[:, None], ss_[None, :]), 0)
    ovt = jnp.asarray((ov.astype(np.float32) / L_CMP).T, BF16)
    oc_t, sel = _nsa_cmp(zt, qg, kc, vct, bias_c, ovt)

    front = lambda x, n, axis: jnp.pad(x, [(n, 0) if a == axis else (0, 0) for a in range(x.ndim)])
    ones_rows = jnp.concatenate([jnp.ones((b, NSA_KV_HEADS, 1, s), BF16),
                                 jnp.zeros((b, NSA_KV_HEADS, 15, s), BF16)], axis=2)
    heads_t = lambda rows: jnp.concatenate(
        [zt[:, rows:rows + kvw, :].astype(BF16).reshape(b, NSA_KV_HEADS, HEAD_DIM, s), ones_rows], axis=2)
    ks_p = front(ks, SEL_PAD, 2)
    kw_p = front(kw, WINDOW, 2)
    vst_p = front(heads_t(_ROW_VS), SEL_PAD, 3)
    vwt_p = front(heads_t(_ROW_VW), WINDOW, 3)
    sel_p = jnp.pad(sel, ((0, 0), (0, 0), (SEL_PAD // L_SEL, 0), (0, 0)), constant_values=NEG)
    gl = zt[:, _ROW_G:_ROW_G + N_BRANCH * NSA_HEADS, :].reshape(b, NSA_KV_HEADS, NSA_GROUP * N_BRANCH, s)
    gates_t = jnp.pad(gl, ((0, 0), (0, 0), (0, 16 - NSA_GROUP * N_BRANCH), (0, 0)))
    yn_t = _nsa_main(zt, qg, ks_p, vst_p, kw_p, vwt_p, sel_p, bias_s, bias_w, gates_t, oc_t)
    yn = yn_t.transpose(0, 2, 1)

    return _mixout(gy, yc, yn, h, p['ssm_w_glu'], g_out, p['w_out'])


def kernel(x, mem, norm_mix, w_in, ssm_lambda_re, ssm_lambda_im, ssm_log_dt, ssm_b_re, ssm_b_im, ssm_c_re, ssm_c_im, ssm_d, ssm_w_glu, conv_w_dw, conv_b_dw, conv_ln_g, conv_ln_b, conv_w_pw, nsa_q_norm, nsa_k_norm, nsa_cmp_pe, nsa_cmp_w1, nsa_cmp_w2, mix_out_norm, w_out, t5_table, norm_cross, norm_mem, x_w_q, x_w_kv, x_q_norm, x_k_norm, x_w_o, norm_ffn, ffn_w_up, ffn_w_down, moe_router, moe_w_up, moe_w_down):
    b, s, d = x.shape
    depth = w_in.shape[0]
    per_layer = dict(norm_mix=norm_mix, w_in=w_in, ssm_lambda_re=ssm_lambda_re, ssm_lambda_im=ssm_lambda_im,
                     ssm_log_dt=ssm_log_dt, ssm_b_re=ssm_b_re, ssm_b_im=ssm_b_im, ssm_c_re=ssm_c_re,
                     ssm_c_im=ssm_c_im, ssm_d=ssm_d, ssm_w_glu=ssm_w_glu, conv_w_dw=conv_w_dw,
                     conv_b_dw=conv_b_dw, conv_ln_g=conv_ln_g, conv_ln_b=conv_ln_b, conv_w_pw=conv_w_pw,
                     nsa_q_norm=nsa_q_norm, nsa_k_norm=nsa_k_norm, nsa_cmp_pe=nsa_cmp_pe, nsa_cmp_w1=nsa_cmp_w1,
                     nsa_cmp_w2=nsa_cmp_w2, mix_out_norm=mix_out_norm, w_out=w_out)
    t5_tiles = _nsa_bias_tiles(t5_table, s)
    s5_perm = _s5_perm_tables()
    h = x
    for layer in range(depth):
        p = {k: v[layer] for k, v in per_layer.items()}
        h = _layer_mixers(h, p, t5_tiles, s5_perm)
        mk, mv = _memkv(mem, norm_mem[layer], x_w_kv[layer], x_k_norm[layer])
        h = _cross(h, norm_cross[layer], x_w_q[layer], x_q_norm[layer], mk, mv, x_w_o[layer])
        h2d = h.reshape(b * s, d)
        if layer % 2 == 0:
            h2d = _ffn(h2d, norm_ffn[layer], ffn_w_up[layer // 2], ffn_w_down[layer // 2])
        else:
            h2d = _moe(h2d, norm_ffn[layer], moe_router[layer // 2], moe_w_up[layer // 2], moe_w_down[layer // 2])
        h = h2d.reshape(b, s, d)
    return h
```

```python
import functools
import math

import jax
import jax.numpy as jnp
import numpy as np
from jax import lax
from jax.experimental import pallas as pl
from jax.experimental.pallas import tpu as pltpu

F32 = jnp.float32
BF16 = jnp.bfloat16

D_MODEL = 1024
HEAD_DIM = 64
SSM_WIDTH = 256
SSM_GROUP = 16
SSM_GROUPS = 16
SSM_STATE = 64
SSM_CHUNK = 16
CONV_WIDTH = 256
CONV_K = 31
CONV_HALO = 32
NSA_WIDTH = 512
NSA_HEADS = 8
NSA_KV_HEADS = 2
NSA_GROUP = 4
N_BRANCH = 3
L_CMP = 32
CMP_STRIDE = 16
L_SEL = 64
N_SELECT = 16
N_LOCAL = 2
WINDOW = 512
SEL_FORCE = 1e6
T5_BUCKETS = 32
T5_MAX_DIST = 128
X_HEADS = 4
X_WIDTH = 256
D_FF = 2816
N_EXPERTS = 8
TOP_K = 2
EPS = 1e-6
NEG = -1e30
LOG2E = math.log2(math.e)

TQ = 256
SEL_PAD = 256
FF_CHUNK = 256
MOE_TB = 2048
MOE_TS = 256
MOE_SUB = 64
MOE_WIN = 192

_VMEM_LIMIT = 56 * 1024 * 1024


def _cparams(sem, vmem=None):
    return pltpu.CompilerParams(dimension_semantics=sem, vmem_limit_bytes=vmem)


def _rms_rows(x):
    return x * lax.rsqrt(jnp.mean(x * x, axis=-1, keepdims=True) + EPS)


def _proj_kernel(x_ref, g_ref, wtok_ref, wt_ref, ztok_ref, zt_ref):
    xn = (_rms_rows(x_ref[0]) * g_ref[...]).astype(BF16)
    ztok_ref[0] = jnp.dot(xn, wtok_ref[...], preferred_element_type=F32)
    zt_ref[0] = lax.dot_general(wt_ref[...], xn, (((1,), (1,)), ((), ())), preferred_element_type=F32)


def _proj(h, gain, w_tok, w_t, tm=512):
    b, s, d = h.shape
    ntok, nt = w_tok.shape[1], w_t.shape[0]
    return pl.pallas_call(
        _proj_kernel,
        grid=(b, s // tm),
        in_specs=[pl.BlockSpec((1, tm, d), lambda i, j: (i, j, 0)),
                  pl.BlockSpec((1, d), lambda i, j: (0, 0)),
                  pl.BlockSpec((d, ntok), lambda i, j: (0, 0)),
                  pl.BlockSpec((nt, d), lambda i, j: (0, 0))],
        out_specs=[pl.BlockSpec((1, tm, ntok), lambda i, j: (i, j, 0)),
                   pl.BlockSpec((1, nt, tm), lambda i, j: (i, 0, j))],
        out_shape=[jax.ShapeDtypeStruct((b, s, ntok), F32), jax.ShapeDtypeStruct((b, nt, s), F32)],
        compiler_params=_cparams(("parallel", "parallel"), _VMEM_LIMIT),
        name="proj",
    )(h, gain.reshape(1, d), w_tok, w_t)


def _s5_tables(lam_re, lam_im, log_dt, b_re, b_im, c_re, c_im, d_skip):
    L, H, P = SSM_CHUNK, SSM_GROUP, SSM_STATE
    dt = jnp.exp(log_dt.astype(F32))[:, None]
    lr, li = lam_re.astype(F32), lam_im.astype(F32)
    mag = jnp.exp(lr * dt)
    ar, ai = mag * jnp.cos(li * dt), mag * jnp.sin(li * dt)
    den = lr * lr + li * li
    fr = ((ar - 1.0) * lr + ai * li) / den
    fi = (ai * lr - (ar - 1.0) * li) / den
    bbr = fr[..., None] * b_re - fi[..., None] * b_im
    bbi = fr[..., None] * b_im + fi[..., None] * b_re
    j = jnp.arange(L + 1, dtype=F32)[:, None, None]
    pmag = jnp.exp(lr[None] * dt[None] * j)
    pr, pi = pmag * jnp.cos(li[None] * dt[None] * j), pmag * jnp.sin(li[None] * dt[None] * j)
    cbr = c_re[:, :, :, None] * bbr[:, None, :, :] - c_im[:, :, :, None] * bbi[:, None, :, :]
    cbi = c_re[:, :, :, None] * bbi[:, None, :, :] + c_im[:, :, :, None] * bbr[:, None, :, :]
    hp = lax.Precision.HIGHEST
    kj = (jnp.einsum('jgp,ghpk->jghk', pr[:L], cbr, precision=hp)
          - jnp.einsum('jgp,ghpk->jghk', pi[:L], cbi, precision=hp))
    lag = np.arange(L)[None, :] - np.arange(L)[:, None]
    place = (lag[None] == np.arange(L)[:, None, None]).astype(np.float32)
    kt = jnp.einsum('jab,jghk->abghk', place, kj, precision=hp)
    kt = kt + (jnp.eye(L)[:, :, None, None, None] * (jnp.eye(H)[None, None, None] * d_skip[None, None, :, :, None]))
    tmat = kt.transpose(2, 0, 4, 1, 3).reshape(SSM_GROUPS, L * H, L * H)
    qr, qi = pr[:L][::-1], pi[:L][::-1]
    wre = qr[..., None] * bbr[None] - qi[..., None] * bbi[None]
    wim = qr[..., None] * bbi[None] + qi[..., None] * bbr[None]
    wre = wre.transpose(1, 0, 3, 2).reshape(SSM_GROUPS, L * H, P)
    wim = wim.transpose(1, 0, 3, 2).reshape(SSM_GROUPS, L * H, P)
    w1 = jnp.concatenate([wre, wim], axis=-1)
    w2 = jnp.concatenate([wim, wre], axis=-1)
    sr, si = pr[1:], pi[1:]
    vr = c_re[None] * sr[:, :, None, :] - c_im[None] * si[:, :, None, :]
    vi = c_re[None] * si[:, :, None, :] + c_im[None] * sr[:, :, None, :]
    vmat = jnp.concatenate([vr, -vi], axis=-1).transpose(1, 3, 0, 2).reshape(SSM_GROUPS, 2 * P, L * H)
    a_r, a_i = pr[L], pi[L]
    am = jnp.stack([jnp.concatenate([a_r, a_r], -1), jnp.concatenate([-a_i, a_i], -1),
                    jnp.concatenate([a_i, -a_i], -1)], axis=1)
    am = jnp.concatenate([am, jnp.zeros((SSM_GROUPS, 5, 2 * P), F32)], axis=1)
    return tmat.astype(BF16), w1.astype(BF16), w2.astype(BF16), vmat.astype(BF16), am


def _s5_perm_tables():
    L, G, H = SSM_CHUNK, SSM_GROUPS, SSM_GROUP
    i = jnp.arange(L * SSM_WIDTH)
    ti, gi, hi = i // SSM_WIDTH, (i // H) % G, i % H
    o = jnp.arange(L * H)
    pack = ((gi[None, :, None] == jnp.arange(G)[:, None, None]) & (ti[None, :, None] == (o // H)[None, None, :])
            & (hi[None, :, None] == (o % H)[None, None, :]))
    j = jnp.arange(G * L * H)
    gj, tj, hj = j // (L * H), (j // H) % L, j % H
    w = jnp.arange(SSM_WIDTH)
    unpack = ((tj[None, :, None] == jnp.arange(L)[:, None, None]) & (gj[None, :, None] == (w // H)[None, None, :])
              & (hj[None, :, None] == (w % H)[None, None, :]))
    return pack.astype(BF16), unpack.astype(BF16)


def _s5_pack_kernel(ulo_ref, uhi_ref, p_ref, x_ref, u2_ref, *, n_chunks):
    half = SSM_WIDTH // 2

    @pl.when(pl.program_id(1) == 0)
    def _():
        for t in range(SSM_CHUNK):
            for k, u_ref in enumerate((ulo_ref, uhi_ref)):
                u2_ref[:, t * SSM_WIDTH + k * half:t * SSM_WIDTH + (k + 1) * half] = (
                    u_ref[0, pl.ds(t, n_chunks, stride=SSM_CHUNK), :].astype(BF16))

    x_ref[0] = jnp.dot(u2_ref[...], p_ref[0], preferred_element_type=F32).astype(x_ref.dtype)


def _s5_pack(ztok, col, pack):
    b, s, _ = ztok.shape
    n_chunks = s // SSM_CHUNK
    lw, lh = SSM_CHUNK * SSM_WIDTH, SSM_CHUNK * SSM_GROUP
    half = SSM_WIDTH // 2
    return pl.pallas_call(
        functools.partial(_s5_pack_kernel, n_chunks=n_chunks),
        grid=(b, SSM_GROUPS),
        in_specs=[pl.BlockSpec((1, s, half), lambda i, g: (i, 0, col // half)),
                  pl.BlockSpec((1, s, half), lambda i, g: (i, 0, col // half + 1)),
                  pl.BlockSpec((1, lw, lh), lambda i, g: (g, 0, 0))],
        out_specs=pl.BlockSpec((1, n_chunks, lh), lambda i, g: (g, i, 0)),
        out_shape=jax.ShapeDtypeStruct((SSM_GROUPS, b * n_chunks, lh), BF16),
        scratch_shapes=[pltpu.VMEM((n_chunks, lw), BF16)],
        compiler_params=_cparams(("parallel", "arbitrary"), _VMEM_LIMIT),
        name="s5_pack",
    )(ztok, ztok, pack)


def _s5_unpack_kernel(g_ref, r_ref, o_ref, *, n_chunks):
    t = pl.program_id(1)
    rows = jnp.concatenate([g_ref[g] for g in range(SSM_GROUPS)], axis=1)
    y = jnp.dot(rows, r_ref[0], preferred_element_type=F32)
    half = SSM_WIDTH // 2
    for k in range(SSM_CHUNK):
        @pl.when(t == k)
        def _():
            for part in range(2):
                o_ref[part, 0, pl.ds(k, n_chunks, stride=SSM_CHUNK), :] = y[:, part * half:(part + 1) * half]


def _s5_unpack(gy, unpack, bsz):
    g, r, lh = gy.shape
    n_chunks = r // bsz
    half = SSM_WIDTH // 2
    return pl.pallas_call(
        functools.partial(_s5_unpack_kernel, n_chunks=n_chunks),
        grid=(bsz, SSM_CHUNK),
        in_specs=[pl.BlockSpec((g, n_chunks, lh), lambda i, t: (0, i, 0)),
                  pl.BlockSpec((1, g * lh, SSM_WIDTH), lambda i, t: (t, 0, 0))],
        out_specs=pl.BlockSpec((2, 1, n_chunks * SSM_CHUNK, half), lambda i, t: (0, i, 0, 0)),
        out_shape=jax.ShapeDtypeStruct((2, bsz, n_chunks * SSM_CHUNK, half), F32),
        compiler_params=_cparams(("parallel", "arbitrary"), _VMEM_LIMIT),
        name="s5_unpack",
    )(gy, unpack)


def _s5_kernel(x_ref, t_ref, w1_ref, w2_ref, v_ref, a_ref, o_ref, s1_ref, s2_ref, xin_ref, *, bsz, n_chunks):
    x = x_ref[0]
    s1_ref[...] = jnp.dot(x, w1_ref[0], preferred_element_type=F32)
    s2_ref[...] = jnp.dot(x, w2_ref[0], preferred_element_type=F32)
    a1, a2, a3 = a_ref[0, 0:1, :], a_ref[0, 1:2, :], a_ref[0, 2:3, :]

    def step(c, carry):
        ps, qs = carry
        new_p, new_q = [], []
        for bi in range(bsz):
            row = pl.ds(bi * n_chunks + c, 1)
            xin_ref[row, :] = ps[bi]
            new_p.append(ps[bi] * a1 + qs[bi] * a2 + s1_ref[row, :])
            new_q.append(qs[bi] * a1 + ps[bi] * a3 + s2_ref[row, :])
        return tuple(new_p), tuple(new_q)

    zero = tuple(jnp.zeros((1, 2 * SSM_STATE), F32) for _ in range(bsz))
    lax.fori_loop(0, n_chunks, step, (zero, zero))
    y = (jnp.dot(x, t_ref[0], preferred_element_type=F32)
         + jnp.dot(xin_ref[...].astype(BF16), v_ref[0], preferred_element_type=F32))
    o_ref[0] = jax.nn.gelu(y).astype(o_ref.dtype)


def _s5(xg, tables, bsz):
    tmat, w1, w2, vmat, am = tables
    g, r, lh = xg.shape
    p2 = 2 * SSM_STATE
    kern = functools.partial(_s5_kernel, bsz=bsz, n_chunks=r // bsz)
    return pl.pallas_call(
        kern,
        grid=(g,),
        in_specs=[pl.BlockSpec((1, r, lh), lambda i: (i, 0, 0)),
                  pl.BlockSpec((1, lh, lh), lambda i: (i, 0, 0)),
                  pl.BlockSpec((1, lh, p2), lambda i: (i, 0, 0)),
                  pl.BlockSpec((1, lh, p2), lambda i: (i, 0, 0)),
                  pl.BlockSpec((1, p2, lh), lambda i: (i, 0, 0)),
                  pl.BlockSpec((1, 8, p2), lambda i: (i, 0, 0))],
        out_specs=pl.BlockSpec((1, r, lh), lambda i: (i, 0, 0)),
        out_shape=jax.ShapeDtypeStruct((g, r, lh), BF16),
        scratch_shapes=[pltpu.VMEM((r, p2), F32), pltpu.VMEM((r, p2), F32), pltpu.VMEM((r, p2), F32)],
        compiler_params=_cparams(("parallel",), _VMEM_LIMIT),
        name="s5_scan",
    )(xg, tmat, w1, w2, vmat, am)


def _conv_kernel(z_ref, halo_ref, wdw_ref, bdw_ref, lng_ref, lnb_ref, wpw_ref, go_ref, o_ref, buf_ref, *, tt):
    first = pl.program_id(1) == 0
    zc = z_ref[0]
    zh = halo_ref[0]
    vh = zh[:, :CONV_WIDTH] * jax.nn.sigmoid(zh[:, CONV_WIDTH:])
    buf_ref[0:CONV_HALO, :] = vh * jnp.where(first, 0.0, 1.0)
    buf_ref[CONV_HALO:CONV_HALO + tt, :] = zc[:, :CONV_WIDTH] * jax.nn.sigmoid(zc[:, CONV_WIDTH:])
    acc = jnp.zeros((tt, CONV_WIDTH), F32) + bdw_ref[...]
    for k in range(CONV_K):
        acc = acc + wdw_ref[k:k + 1, :] * buf_ref[pl.ds(CONV_HALO - (CONV_K - 1) + k, tt), :]
    mu = jnp.mean(acc, axis=-1, keepdims=True)
    var = jnp.mean(jnp.square(acc - mu), axis=-1, keepdims=True)
    y = (acc - mu) * lax.rsqrt(var + EPS) * lng_ref[...] + lnb_ref[...]
    y = jax.nn.silu(y)
    y = jnp.dot(y.astype(BF16), wpw_ref[...], preferred_element_type=F32)
    o_ref[0] = (_rms_rows(y) * go_ref[...]).astype(o_ref.dtype)


def _conv(ztok, w_dw, b_dw, ln_g, ln_b, w_pw, g_out, tt=512):
    b, s, _ = ztok.shape
    cw = CONV_WIDTH
    hb = tt // CONV_HALO
    kern = functools.partial(_conv_kernel, tt=tt)
    row = lambda v: v.reshape(1, cw)
    return pl.pallas_call(
        kern,
        grid=(b, s // tt),
        in_specs=[pl.BlockSpec((1, tt, 2 * cw), lambda i, j: (i, j, 0)),
                  pl.BlockSpec((1, CONV_HALO, 2 * cw), lambda i, j: (i, jnp.maximum(j * hb - 1, 0), 0)),
                  pl.BlockSpec((CONV_K + 1, cw), lambda i, j: (0, 0)),
                  pl.BlockSpec((1, cw), lambda i, j: (0, 0)),
                  pl.BlockSpec((1, cw), lambda i, j: (0, 0)),
                  pl.BlockSpec((1, cw), lambda i, j: (0, 0)),
                  pl.BlockSpec((cw, cw), lambda i, j: (0, 0)),
                  pl.BlockSpec((1, cw), lambda i, j: (0, 0))],
        out_specs=pl.BlockSpec((1, tt, cw), lambda i, j: (i, j, 0)),
        out_shape=jax.ShapeDtypeStruct((b, s, cw), BF16),
        scratch_shapes=[pltpu.VMEM((CONV_HALO + tt, cw), F32)],
        compiler_params=_cparams(("parallel", "arbitrary")),
        name="conv_mixer",
    )(ztok, ztok, jnp.concatenate([w_dw, jnp.zeros((1, cw), F32)], 0), row(b_dw), row(ln_g), row(ln_b),
      w_pw.astype(BF16), row(g_out))


def _knorm_kernel(ks_ref, kw_ref, gs_ref, gw_ref, os_ref, ow_ref):
    for src, g_ref, dst in ((ks_ref, gs_ref, os_ref), (kw_ref, gw_ref, ow_ref)):
        x = src[0]
        for h in range(NSA_KV_HEADS):
            xh = x[:, h * HEAD_DIM:(h + 1) * HEAD_DIM]
            dst[0, h] = (_rms_rows(xh) * g_ref[...]).astype(dst.dtype)


def _knorm(ztok, col_s, col_w, gain_s, gain_w, tt=512):
    b, s, _ = ztok.shape
    kw = NSA_KV_HEADS * HEAD_DIM
    out = jax.ShapeDtypeStruct((b, NSA_KV_HEADS, s, HEAD_DIM), BF16)
    ospec = pl.BlockSpec((1, NSA_KV_HEADS, tt, HEAD_DIM), lambda i, j: (i, 0, j, 0))
    return pl.pallas_call(
        _knorm_kernel,
        grid=(b, s // tt),
        in_specs=[pl.BlockSpec((1, tt, kw), lambda i, j: (i, j, col_s // kw)),
                  pl.BlockSpec((1, tt, kw), lambda i, j: (i, j, col_w // kw)),
                  pl.BlockSpec((1, HEAD_DIM), lambda i, j: (0, 0)),
                  pl.BlockSpec((1, HEAD_DIM), lambda i, j: (0, 0))],
        out_specs=[ospec, ospec],
        out_shape=[out, out],
        compiler_params=_cparams(("parallel", "parallel")),
        name="nsa_key_norm",
    )(ztok, ztok, gain_s.reshape(1, HEAD_DIM), gain_w.reshape(1, HEAD_DIM))


def _compress_kernel(k_ref, v_ref, wka_ref, wkb_ref, ck_ref, w2k_ref, gk_ref,
                     wva_ref, wvb_ref, cv_ref, w2v_ref, ko_ref, vo_ref):
    hi = lax.Precision.HIGHEST
    n = k_ref.shape[1] // CMP_STRIDE
    kvw = k_ref.shape[2]
    nt = (((1,), (1,)), ((), ()))
    a, bm = jnp.zeros((n, kvw), F32), jnp.zeros((n, kvw), F32)
    at, bt = jnp.zeros((kvw, n), F32), jnp.zeros((kvw, n), F32)
    for l in range(CMP_STRIDE):
        kl = k_ref[0, pl.ds(l, n, stride=CMP_STRIDE), :]
        vl = v_ref[0, pl.ds(l, n, stride=CMP_STRIDE), :]
        a = a + jnp.dot(kl, wka_ref[l], precision=hi, preferred_element_type=F32)
        bm = bm + jnp.dot(kl, wkb_ref[l], precision=hi, preferred_element_type=F32)
        at = at + lax.dot_general(wva_ref[l], vl, nt, precision=hi, preferred_element_type=F32)
        bt = bt + lax.dot_general(wvb_ref[l], vl, nt, precision=hi, preferred_element_type=F32)
    pre = a + pltpu.roll(bm, n - 1, 0) + ck_ref[...]
    kc = jnp.dot(jax.nn.gelu(pre), w2k_ref[...], precision=hi, preferred_element_type=F32)
    for h in range(NSA_KV_HEADS):
        kh = kc[:, h * HEAD_DIM:(h + 1) * HEAD_DIM]
        ko_ref[0, h] = (_rms_rows(kh) * gk_ref[...]).astype(ko_ref.dtype)
    pre_t = at + pltpu.roll(bt, n - 1, 1) + cv_ref[...]
    vt = jnp.dot(w2v_ref[...], jax.nn.gelu(pre_t), precision=hi, preferred_element_type=F32)
    for h in range(NSA_KV_HEADS):
        vo_ref[0, h] = vt[h * HEAD_DIM:(h + 1) * HEAD_DIM, :].astype(vo_ref.dtype)


def _blockdiag2(w):
    z = jnp.zeros_like(w)
    return jnp.concatenate([jnp.concatenate([w, z], 1), jnp.concatenate([z, w], 1)], 0)


def _compress(ztok, col_k, col_v, pe, w1, w2, k_gain):
    b, s, _ = ztok.shape
    n = s // CMP_STRIDE
    hd, kvw = HEAD_DIM, NSA_KV_HEADS * HEAD_DIM
    hp = lax.Precision.HIGHEST

    def expand(w):
        wl = w.reshape(L_CMP, hd, hd)
        e = wl[:, None, :, None, :] * jnp.eye(NSA_KV_HEADS, dtype=F32)[None, :, None, :, None]
        e = e.reshape(L_CMP, kvw, kvw)
        return e[:CMP_STRIDE], e[CMP_STRIDE:]

    wka, wkb = expand(w1[0])
    wva, wvb = expand(w1[1])
    ck = jnp.tile(jnp.dot(pe[0].reshape(1, L_CMP * hd), w1[0], precision=hp), (1, NSA_KV_HEADS))
    cv = jnp.tile(jnp.dot(pe[1].reshape(1, L_CMP * hd), w1[1], precision=hp), (1, NSA_KV_HEADS)).T
    full = lambda shape: pl.BlockSpec(shape, lambda i: tuple(0 for _ in shape))
    return pl.pallas_call(
        _compress_kernel,
        grid=(b,),
        in_specs=[pl.BlockSpec((1, s, kvw), lambda i: (i, 0, col_k // kvw)),
                  pl.BlockSpec((1, s, kvw), lambda i: (i, 0, col_v // kvw)),
                  full((CMP_STRIDE, kvw, kvw)), full((CMP_STRIDE, kvw, kvw)), full((1, kvw)), full((kvw, kvw)),
                  full((1, hd)),
                  full((CMP_STRIDE, kvw, kvw)), full((CMP_STRIDE, kvw, kvw)), full((kvw, 1)), full((kvw, kvw))],
        out_specs=[pl.BlockSpec((1, NSA_KV_HEADS, n, hd), lambda i: (i, 0, 0, 0)),
                   pl.BlockSpec((1, NSA_KV_HEADS, hd, n), lambda i: (i, 0, 0, 0))],
        out_shape=[jax.ShapeDtypeStruct((b, NSA_KV_HEADS, n, hd), BF16),
                   jax.ShapeDtypeStruct((b, NSA_KV_HEADS, hd, n), BF16)],
        compiler_params=_cparams(("parallel",), _VMEM_LIMIT),
        name="nsa_compress",
    )(ztok, ztok, wka, wkb, ck, _blockdiag2(w2[0]), k_gain.reshape(1, hd),
      wva.transpose(0, 2, 1), wvb.transpose(0, 2, 1), cv, _blockdiag2(w2[1]).T)


def _t5_bias_by_dist(t5_table):
    n = np.arange(T5_MAX_DIST + 1)
    max_exact = T5_BUCKETS // 2
    nf = np.maximum(n, 1).astype(np.float32)
    large = max_exact + (np.log(nf / np.float32(max_exact)) / np.float32(math.log(T5_MAX_DIST / max_exact))
                         * np.float32(T5_BUCKETS - max_exact)).astype(np.int32)
    large = np.minimum(large, T5_BUCKETS - 1)
    bucket = np.where(n < max_exact, n, large)
    onehot = (bucket[:, None] == np.arange(T5_BUCKETS)[None, :]).astype(np.float32)
    return jnp.dot(onehot, t5_table, precision=lax.Precision.HIGHEST)


def _bias_tile(fdt, rows, stride, dist00, d_max=None):
    heads = fdt.shape[0]
    a0 = stride * (rows - 1)
    d_lo = dist00 - a0
    length = a0 + TQ
    d_hi = d_lo + length
    d_max = d_hi if d_max is None else d_max
    pieces = []
    for lo, hi, kind in ((d_lo, min(d_hi, 0), 'neg'), (max(d_lo, 0), min(d_hi, T5_MAX_DIST), 'tab'),
                         (max(d_lo, T5_MAX_DIST), min(d_hi, d_max), 'far'), (max(d_lo, d_max), d_hi, 'neg')):
        if hi > lo:
            pieces.append(fdt[:, lo:hi] if kind == 'tab'
                          else jnp.full((heads, hi - lo), NEG if kind == 'neg' else 0.0, F32))
    vec = jnp.concatenate(pieces, axis=1)
    c0 = -(-a0 // 128) * 128
    width = -(-(c0 + TQ) // 128) * 128
    vec = jnp.pad(vec, ((0, 0), (c0 - a0, width - (c0 - a0) - length)))

    def kern(v_ref, o_ref):
        x = jnp.broadcast_to(v_ref[0], (rows, width))
        o_ref[0] = pltpu.roll(x, 0, 1, stride=stride, stride_axis=0)[:, c0:c0 + TQ]

    return pl.pallas_call(
        kern,
        grid=(heads,),
        in_specs=[pl.BlockSpec((1, 1, width), lambda h: (h, 0, 0))],
        out_specs=pl.BlockSpec((1, rows, TQ), lambda h: (h, 0, 0)),
        out_shape=jax.ShapeDtypeStruct((heads, rows, TQ), F32),
        compiler_params=_cparams(("parallel",)),
        name="toeplitz_bias",
    )(vec.reshape(heads, 1, width))


def _nsa_bias_tiles(t5_table, seq):
    fd = _t5_bias_by_dist(t5_table).astype(F32)
    fdt = ((fd - fd[T5_MAX_DIST:]) * LOG2E).T
    n_cmp = seq // CMP_STRIDE
    qt = TQ // CMP_STRIDE
    r0 = n_cmp - qt
    band = _bias_tile(fdt, 2 * qt, CMP_STRIDE, CMP_STRIDE * qt - (L_CMP - 1))
    heads = fdt.shape[0]
    cmp_t = jnp.concatenate([jnp.zeros((heads, r0 - qt, TQ), F32), band,
                             jnp.full((heads, n_cmp - qt, TQ), NEG, F32)], axis=1)
    sel_t = _bias_tile(fdt, SEL_PAD + TQ, 1, SEL_PAD)
    win = _bias_tile(fdt, WINDOW + TQ, 1, WINDOW, d_max=WINDOW)
    rw = np.arange(WINDOW + TQ)[None, :, None]
    win_t = jnp.stack([jnp.where(rw >= WINDOW - q0, win, NEG) for q0 in (0, TQ, 2 * TQ)])
    split = lambda t: t.reshape(*t.shape[:-3], NSA_KV_HEADS, NSA_GROUP, *t.shape[-2:])

    def wide(t):
        t = jnp.swapaxes(split(t), -3, -2)
        return t.reshape(*t.shape[:-2], NSA_GROUP * TQ)

    return wide(cmp_t), wide(sel_t), wide(win_t)


def _q_head(qt_ref, g, qg_ref):
    q = qt_ref[0, g * HEAD_DIM:(g + 1) * HEAD_DIM, :]
    inv = lax.rsqrt(jnp.mean(q * q, axis=0, keepdims=True) + EPS)
    return (q * inv * qg_ref[...]).astype(BF16)


def _nsa_cmp_kernel(qt_ref, qg_ref, kc_ref, vct_ref, bias_ref, ov_ref, oc_ref, sel_ref, imp_ref, *, n_cmp, n_sel):
    qi = pl.program_id(1)
    qt = TQ // CMP_STRIDE
    gw = NSA_GROUP * HEAD_DIM
    kvs = range(NSA_KV_HEADS)
    row0 = pl.multiple_of((n_cmp - qt) - qi * qt, qt)
    qw = [jnp.concatenate([_q_head(qt_ref, kv * NSA_GROUP + g, qg_ref) for g in range(NSA_GROUP)], axis=1)
          for kv in kvs]

    def attend(n):
        for kv in kvs:
            s = (jnp.dot(kc_ref[0, kv, 0:n, :], qw[kv], preferred_element_type=F32)
                 + bias_ref[kv, pl.ds(row0, n), :])
            m = jnp.max(s, axis=0, keepdims=True)
            m = jnp.where(m < 0.5 * NEG, 0.0, m)
            p = jnp.exp2(s - m)
            p = p * (1.0 / jnp.maximum(jnp.sum(p, axis=0, keepdims=True), 1e-30))
            oc = jnp.dot(vct_ref[0, kv, :, 0:n], p.astype(BF16), preferred_element_type=F32)
            psum = jnp.zeros((n, TQ), F32)
            for g in range(NSA_GROUP):
                oc_ref[0, kv * gw + g * HEAD_DIM:kv * gw + (g + 1) * HEAD_DIM, :] = oc[:, g * TQ:(g + 1) * TQ]
                psum = psum + p[:, g * TQ:(g + 1) * TQ]
            hi = psum.astype(BF16)
            lo = (psum - hi.astype(F32)).astype(BF16)
            imp_ref[kv] = (jnp.dot(ov_ref[:, 0:n], hi, preferred_element_type=F32)
                           + jnp.dot(ov_ref[:, 0:n], lo, preferred_element_type=F32))

    chunk = min(n_cmp, 128)
    n_chunks = n_cmp // chunk
    need = lax.div((qi + 1) * qt + (chunk - 1), chunk)
    for c in range(1, n_chunks + 1):
        pl.when(need == c)(functools.partial(attend, c * chunk))

    blk = lax.broadcasted_iota(jnp.int32, (n_sel, TQ), 0)
    blk_t = lax.shift_right_logical(qi * TQ + lax.broadcasted_iota(jnp.int32, (n_sel, TQ), 1), L_SEL.bit_length() - 1)
    forced = (blk == 0) | (blk > blk_t - N_LOCAL)
    v0 = tuple(jnp.where(blk > blk_t, -jnp.inf, jnp.where(forced, SEL_FORCE, imp_ref[kv])) for kv in kvs)

    def pick(_, vs):
        out = []
        for v in vs:
            m = jnp.max(v, axis=0, keepdims=True)
            first = jnp.min(jnp.where(v == m, blk, n_sel), axis=0, keepdims=True)
            out.append(jnp.where(blk == first, -jnp.inf, v))
        return tuple(out)

    vs = lax.fori_loop(0, min(N_SELECT, n_sel), pick, v0)
    for kv in kvs:
        sel_ref[0, kv] = jnp.where((vs[kv] == -jnp.inf) & (v0[kv] > -jnp.inf), 0.0, NEG)


def _nsa_cmp(zt, qg, kc, vct, bias_c, ovt):
    b, _, s = zt.shape
    n_cmp, n_sel = s // CMP_STRIDE, s // L_SEL
    kvh = NSA_KV_HEADS
    kern = functools.partial(_nsa_cmp_kernel, n_cmp=n_cmp, n_sel=n_sel)
    return pl.pallas_call(
        kern,
        grid=(b, s // TQ),
        in_specs=[pl.BlockSpec((1, NSA_WIDTH, TQ), lambda i, j: (i, 0, j)),
                  pl.BlockSpec((HEAD_DIM, TQ), lambda i, j: (0, 0)),
                  pl.BlockSpec((1, kvh, n_cmp, HEAD_DIM), lambda i, j: (i, 0, 0, 0)),
                  pl.BlockSpec((1, kvh, HEAD_DIM, n_cmp), lambda i, j: (i, 0, 0, 0)),
                  pl.BlockSpec((kvh, bias_c.shape[1], NSA_GROUP * TQ), lambda i, j: (0, 0, 0)),
                  pl.BlockSpec((n_sel, n_cmp), lambda i, j: (0, 0))],
        out_specs=[pl.BlockSpec((1, NSA_WIDTH, TQ), lambda i, j: (i, 0, j)),
                   pl.BlockSpec((1, kvh, n_sel, TQ), lambda i, j: (i, 0, 0, j))],
        out_shape=[jax.ShapeDtypeStruct((b, NSA_WIDTH, s), F32),
                   jax.ShapeDtypeStruct((b, kvh, n_sel, s), F32)],
        scratch_shapes=[pltpu.VMEM((kvh, n_sel, TQ), F32)],
        compiler_params=_cparams(("parallel", "parallel"), _VMEM_LIMIT),
        name="nsa_compressed_select",
    )(zt, qg, kc, vct, bias_c, ovt)


def _nsa_main_kernel(qt_ref, qg_ref, ks_ref, vst_ref, kw_ref, vwt_ref, sel_ref, bs_ref, bw_ref, gate_ref,
                     oc_ref, o_ref, acc_ref, s_ref):
    qi = pl.program_id(1)
    q0 = pl.multiple_of(qi * TQ, TQ)
    near = SEL_PAD + TQ
    gw = NSA_GROUP * HEAD_DIM
    kvs = range(NSA_KV_HEADS)

    def expand_sel(kv, first_blk, n_blk):
        rows = [jnp.broadcast_to(sel_ref[0, kv, pl.ds(first_blk + r, 1), :], (L_SEL, TQ)) for r in range(n_blk)]
        rows = jnp.concatenate(rows, axis=0)
        return jnp.concatenate([rows] * NSA_GROUP, axis=1)

    qw = [jnp.concatenate([_q_head(qt_ref, kv * NSA_GROUP + g, qg_ref) for g in range(NSA_GROUP)], axis=1)
          for kv in kvs]
    m0 = []
    for kv in kvs:
        s = (jnp.dot(ks_ref[0, kv, pl.ds(q0, near), :], qw[kv], preferred_element_type=F32) + bs_ref[kv]
             + expand_sel(kv, qi * (TQ // L_SEL), near // L_SEL))
        m = jnp.max(s, axis=0, keepdims=True)
        p = jnp.exp2(s - m).astype(BF16)
        acc_ref[kv] = jnp.dot(vst_ref[0, kv, :, pl.ds(q0, near)], p, preferred_element_type=F32)
        m0.append(m)

    for kv in kvs:
        s = jnp.dot(kw_ref[0, kv, pl.ds(q0, WINDOW + TQ), :], qw[kv], preferred_element_type=F32) + bw_ref[0, kv]
        p = jnp.exp2(s - jnp.max(s, axis=0, keepdims=True)).astype(BF16)
        ow = jnp.dot(vwt_ref[0, kv, :, pl.ds(q0, WINDOW + TQ)], p, preferred_element_type=F32)
        ow = ow[:HEAD_DIM] * (1.0 / jnp.maximum(ow[HEAD_DIM:HEAD_DIM + 1], 1e-30))
        for g in range(NSA_GROUP):
            gates = jax.nn.sigmoid(gate_ref[0, kv, g * N_BRANCH:(g + 1) * N_BRANCH, :])
            rows = slice(kv * gw + g * HEAD_DIM, kv * gw + (g + 1) * HEAD_DIM)
            o_ref[0, rows, :] = gates[0:1] * oc_ref[0, rows, :] + gates[2:3] * ow[:, g * TQ:(g + 1) * TQ]

    def scores(c, slot):
        r0 = pl.multiple_of(c * TQ, TQ)
        mc = []
        for kv in kvs:
            s = (jnp.dot(ks_ref[0, kv, pl.ds(r0, TQ), :], qw[kv], preferred_element_type=F32)
                 + expand_sel(kv, c * (TQ // L_SEL), TQ // L_SEL))
            s_ref[slot, kv] = s
            mc.append(jnp.max(s, axis=0, keepdims=True))
        return tuple(mc)

    def consume(c, slot, m_old, mc):
        r0 = pl.multiple_of(c * TQ, TQ)
        m_out = []
        for kv in kvs:
            m_new = jnp.maximum(m_old[kv], mc[kv])
            alpha = jnp.exp2(m_old[kv] - m_new)
            p = jnp.exp2((s_ref[slot, kv] - m_new).astype(BF16))
            acc_ref[kv] = alpha * acc_ref[kv] + jnp.dot(vst_ref[0, kv, :, pl.ds(r0, TQ)], p,
                                                        preferred_element_type=F32)
            m_out.append(m_new)
        return tuple(m_out)

    first = SEL_PAD // TQ
    n_far = qi - first

    def pair(i, carry):
        m, mc = carry
        c = first + 2 * i
        mc1 = scores(c + 1, 1)
        m = consume(c, 0, m, mc)
        mc2 = scores(c + 2, 0)
        m = consume(c + 1, 1, m, mc1)
        return m, mc2

    m_far, mc_far = lax.fori_loop(0, n_far // 2, pair, (tuple(m0), scores(first, 0)))

    @pl.when((n_far > 0) & (n_far % 2 == 1))
    def _():
        consume(first + n_far - 1, 0, m_far, mc_far)

    for kv in kvs:
        os = acc_ref[kv]
        os = os[:HEAD_DIM] * (1.0 / jnp.maximum(os[HEAD_DIM:HEAD_DIM + 1], 1e-30))
        for g in range(NSA_GROUP):
            gate = jax.nn.sigmoid(gate_ref[0, kv, g * N_BRANCH + 1:g * N_BRANCH + 2, :])
            rows = slice(kv * gw + g * HEAD_DIM, kv * gw + (g + 1) * HEAD_DIM)
            o_ref[0, rows, :] = o_ref[0, rows, :] + gate * os[:, g * TQ:(g + 1) * TQ]


def _nsa_main(zt, qg, ks_p, vst_p, kw_p, vwt_p, sel_p, bias_s, bias_w, gates_t, oc_t):
    b, _, s = zt.shape
    kvh = NSA_KV_HEADS
    sp, wp = ks_p.shape[2], kw_p.shape[2]
    nb, vr = sel_p.shape[2], vst_p.shape[2]
    once = pl.Buffered(1)
    return pl.pallas_call(
        _nsa_main_kernel,
        grid=(b, s // TQ),
        in_specs=[pl.BlockSpec((1, NSA_WIDTH, TQ), lambda i, j: (i, 0, j)),
                  pl.BlockSpec((HEAD_DIM, TQ), lambda i, j: (0, 0)),
                  pl.BlockSpec((1, kvh, sp, HEAD_DIM), lambda i, j: (i, 0, 0, 0), pipeline_mode=once),
                  pl.BlockSpec((1, kvh, vr, sp), lambda i, j: (i, 0, 0, 0), pipeline_mode=once),
                  pl.BlockSpec((1, kvh, wp, HEAD_DIM), lambda i, j: (i, 0, 0, 0), pipeline_mode=once),
                  pl.BlockSpec((1, kvh, vr, wp), lambda i, j: (i, 0, 0, 0), pipeline_mode=once),
                  pl.BlockSpec((1, kvh, nb, TQ), lambda i, j: (i, 0, 0, j)),
                  pl.BlockSpec((kvh, SEL_PAD + TQ, NSA_GROUP * TQ), lambda i, j: (0, 0, 0), pipeline_mode=once),
                  pl.BlockSpec((1, kvh, WINDOW + TQ, NSA_GROUP * TQ), lambda i, j: (jnp.minimum(j, 2), 0, 0, 0)),
                  pl.BlockSpec((1, kvh, 16, TQ), lambda i, j: (i, 0, 0, j)),
                  pl.BlockSpec((1, NSA_WIDTH, TQ), lambda i, j: (i, 0, j))],
        out_specs=pl.BlockSpec((1, NSA_WIDTH, TQ), lambda i, j: (i, 0, j)),
        out_shape=jax.ShapeDtypeStruct((b, NSA_WIDTH, s), F32),
        scratch_shapes=[pltpu.VMEM((kvh, vr, NSA_GROUP * TQ), F32),
                        pltpu.VMEM((2, kvh, TQ, NSA_GROUP * TQ), F32)],
        compiler_params=_cparams(("parallel", "arbitrary"), _VMEM_LIMIT),
        name="nsa_selected_window",
    )(zt, qg, ks_p, vst_p, kw_p, vwt_p, sel_p, bias_s, bias_w, gates_t, oc_t)


def _mixout_kernel(gy_ref, yc_ref, yn_ref, h_ref, wglu_ref, go_ref, wo_ref, o_ref):
    sw = SSM_WIDTH
    half = sw // 2
    ag = (jnp.dot(gy_ref[0, 0].astype(BF16), wglu_ref[0:half, :], preferred_element_type=F32)
          + jnp.dot(gy_ref[1, 0].astype(BF16), wglu_ref[half:, :], preferred_element_type=F32))
    ys = ag[:, :sw] * jax.nn.sigmoid(ag[:, sw:])
    ys = (_rms_rows(ys) * go_ref[:, 0:sw]).astype(BF16)
    yn = (_rms_rows(yn_ref[0]) * go_ref[:, 2 * sw:]).astype(BF16)
    out = (jnp.dot(ys, wo_ref[0:sw, :], preferred_element_type=F32)
           + jnp.dot(yc_ref[0], wo_ref[sw:2 * sw, :], preferred_element_type=F32)
           + jnp.dot(yn, wo_ref[2 * sw:, :], preferred_element_type=F32))
    o_ref[0] = h_ref[0] + out


def _mixout(gy, yc, yn, h, w_glu, g_out, w_out, tm=512):
    b, s, d = h.shape
    tok = lambda w: pl.BlockSpec((1, tm, w), lambda i, j: (i, j, 0))
    return pl.pallas_call(
        _mixout_kernel,
        grid=(b, s // tm),
        in_specs=[pl.BlockSpec((2, 1, tm, SSM_WIDTH // 2), lambda i, j: (0, i, j, 0)),
                  tok(CONV_WIDTH), tok(NSA_WIDTH), tok(d),
                  pl.BlockSpec((SSM_WIDTH, 2 * SSM_WIDTH), lambda i, j: (0, 0)),
                  pl.BlockSpec((1, d), lambda i, j: (0, 0)),
                  pl.BlockSpec((d, d), lambda i, j: (0, 0))],
        out_specs=tok(d),
        out_shape=jax.ShapeDtypeStruct((b, s, d), F32),
        compiler_params=_cparams(("parallel", "parallel")),
        name="mix_out",
    )(gy, yc, yn, h, w_glu.astype(BF16), g_out.reshape(1, d), w_out.astype(BF16))


def _memkv_kernel(mem_ref, g_ref, w_ref, kg_ref, k_ref, v_ref):
    mn = (_rms_rows(mem_ref[0]) * g_ref[...]).astype(BF16)
    kv = jnp.dot(mn, w_ref[...], preferred_element_type=F32)
    for h in range(X_HEADS):
        cols = slice(h * HEAD_DIM, (h + 1) * HEAD_DIM)
        k_ref[0, :, cols] = (_rms_rows(kv[:, cols]) * kg_ref[...]).astype(k_ref.dtype)
    v_ref[0] = kv[:, X_WIDTH:].astype(v_ref.dtype)


def _memkv(mem, gain, w_kv, k_gain):
    b, m, d = mem.shape
    out = jax.ShapeDtypeStruct((b, m, X_WIDTH), BF16)
    return pl.pallas_call(
        _memkv_kernel,
        grid=(b,),
        in_specs=[pl.BlockSpec((1, m, d), lambda i: (i, 0, 0)),
                  pl.BlockSpec((1, d), lambda i: (0, 0)),
                  pl.BlockSpec((d, 2 * X_WIDTH), lambda i: (0, 0)),
                  pl.BlockSpec((1, HEAD_DIM), lambda i: (0, 0))],
        out_specs=[pl.BlockSpec((1, m, X_WIDTH), lambda i: (i, 0, 0))] * 2,
        out_shape=[out, out],
        compiler_params=_cparams(("parallel",)),
        name="cross_mem_kv",
    )(mem, gain.reshape(1, d), w_kv.astype(BF16), k_gain.reshape(1, HEAD_DIM))


def _cross_kernel(h_ref, g_ref, wq_ref, qg_ref, k_ref, v_ref, wo_ref, o_ref):
    h = h_ref[0]
    hn = (_rms_rows(h) * g_ref[...]).astype(BF16)
    q = jnp.dot(hn, wq_ref[...], preferred_element_type=F32)
    out = h
    for hd in range(X_HEADS):
        cols = slice(hd * HEAD_DIM, (hd + 1) * HEAD_DIM)
        qh = (_rms_rows(q[:, cols]) * qg_ref[...]).astype(BF16)
        s = lax.dot_general(qh, k_ref[0, :, cols], (((1,), (1,)), ((), ())), preferred_element_type=F32)
        p = jnp.exp(s - jnp.max(s, axis=-1, keepdims=True))
        p = p * (1.0 / jnp.sum(p, axis=-1, keepdims=True))
        o = jnp.dot(p.astype(BF16), v_ref[0, :, cols], preferred_element_type=F32)
        out = out + jnp.dot(o.astype(BF16), wo_ref[cols, :], preferred_element_type=F32)
    o_ref[0] = out


def _cross(h, gain, w_q, q_gain, k, v, w_o, tm=1024):
    b, s, d = h.shape
    m = k.shape[1]
    return pl.pallas_call(
        _cross_kernel,
        grid=(b, s // tm),
        in_specs=[pl.BlockSpec((1, tm, d), lambda i, j: (i, j, 0)),
                  pl.BlockSpec((1, d), lambda i, j: (0, 0)),
                  pl.BlockSpec((d, X_WIDTH), lambda i, j: (0, 0)),
                  pl.BlockSpec((1, HEAD_DIM), lambda i, j: (0, 0)),
                  pl.BlockSpec((1, m, X_WIDTH), lambda i, j: (i, 0, 0)),
                  pl.BlockSpec((1, m, X_WIDTH), lambda i, j: (i, 0, 0)),
                  pl.BlockSpec((X_WIDTH, d), lambda i, j: (0, 0))],
        out_specs=pl.BlockSpec((1, tm, d), lambda i, j: (i, j, 0)),
        out_shape=jax.ShapeDtypeStruct((b, s, d), F32),
        compiler_params=_cparams(("parallel", "parallel")),
        name="cross_attention",
    )(h, gain.reshape(1, d), w_q.astype(BF16), (q_gain * HEAD_DIM ** -0.5).reshape(1, HEAD_DIM), k, v,
      w_o.astype(BF16))


def _ffn_kernel(h_ref, g_ref, wg_ref, wv_ref, wd_ref, o_ref, xn_ref, acc_ref):
    f = pl.program_id(1)

    @pl.when(f == 0)
    def _():
        xn_ref[...] = (_rms_rows(h_ref[...]) * g_ref[...]).astype(BF16)
        acc_ref[...] = jnp.zeros_like(acc_ref)

    x = xn_ref[...]
    gate = jnp.dot(x, wg_ref[...], preferred_element_type=F32)
    val = jnp.dot(x, wv_ref[...], preferred_element_type=F32)
    act = (jax.nn.silu(gate) * val).astype(BF16)
    acc_ref[...] += jnp.dot(act, wd_ref[...], preferred_element_type=F32)

    @pl.when(f == pl.num_programs(1) - 1)
    def _():
        o_ref[...] = h_ref[...] + acc_ref[...]


def _ffn(h2d, gain, w_up, w_down, tm=1024):
    t, d = h2d.shape
    nf = D_FF // FF_CHUNK
    wb = w_up.astype(BF16)
    return pl.pallas_call(
        _ffn_kernel,
        grid=(t // tm, nf),
        in_specs=[pl.BlockSpec((tm, d), lambda i, f: (i, 0)),
                  pl.BlockSpec((1, d), lambda i, f: (0, 0)),
                  pl.BlockSpec((d, FF_CHUNK), lambda i, f: (0, f)),
                  pl.BlockSpec((d, FF_CHUNK), lambda i, f: (0, f + nf)),
                  pl.BlockSpec((FF_CHUNK, d), lambda i, f: (f, 0))],
        out_specs=pl.BlockSpec((tm, d), lambda i, f: (i, 0)),
        out_shape=jax.ShapeDtypeStruct((t, d), F32),
        scratch_shapes=[pltpu.VMEM((tm, d), BF16), pltpu.VMEM((tm, d), F32)],
        compiler_params=_cparams(("parallel", "arbitrary"), _VMEM_LIMIT),
        name="ffn_swiglu",
    )(h2d, gain.reshape(1, d), wb, wb, w_down.astype(BF16))


def _router_kernel(h_ref, g_ref, wr_ref, xn_ref, gate_ref, asg_ref):
    xn = _rms_rows(h_ref[...]) * g_ref[...]
    xn_ref[...] = xn.astype(BF16)
    logits = jnp.dot(xn, wr_ref[...], precision=lax.Precision.HIGHEST, preferred_element_type=F32)
    lane = lax.broadcasted_iota(jnp.int32, logits.shape, 1)
    lg = jnp.where(lane < N_EXPERTS, logits, -jnp.inf)
    m1 = jnp.max(lg, axis=-1, keepdims=True)
    i1 = jnp.min(jnp.where(lg == m1, lane, 128), axis=-1, keepdims=True)
    lg2 = jnp.where(lane == i1, -jnp.inf, lg)
    m2 = jnp.max(lg2, axis=-1, keepdims=True)
    i2 = jnp.min(jnp.where(lg2 == m2, lane, 128), axis=-1, keepdims=True)
    e = jnp.exp(m2 - m1)
    den = 1.0 + e
    gate_ref[...] = jnp.where(lane == i1, 1.0 / den, jnp.where(lane == i2, e / den, 0.0))
    asg_ref[...] = ((lane == i1) | (lane == i2)).astype(jnp.int32)


def _router(h2d, gain, w_router, tm=512):
    t, d = h2d.shape
    wr = jnp.concatenate([w_router, jnp.zeros((d, 128 - N_EXPERTS), F32)], axis=1)
    return pl.pallas_call(
        _router_kernel,
        grid=(t // tm,),
        in_specs=[pl.BlockSpec((tm, d), lambda i: (i, 0)),
                  pl.BlockSpec((1, d), lambda i: (0, 0)),
                  pl.BlockSpec((d, 128), lambda i: (0, 0))],
        out_specs=[pl.BlockSpec((tm, d), lambda i: (i, 0)),
                   pl.BlockSpec((tm, 128), lambda i: (i, 0)),
                   pl.BlockSpec((tm, 128), lambda i: (i, 0))],
        out_shape=[jax.ShapeDtypeStruct((t, d), BF16), jax.ShapeDtypeStruct((t, 128), F32),
                   jax.ShapeDtypeStruct((t, 128), jnp.int32)],
        compiler_params=_cparams(("parallel",)),
        name="moe_router",
    )(h2d, gain.reshape(1, d), wr)


def _moe_windows(rb, lo, hi, active):
    lo_l = jnp.clip(lo - rb * MOE_TB, 0, MOE_TB)
    hi_l = jnp.clip(hi - rb * MOE_TB, 0, MOE_TB)
    shift = MOE_SUB.bit_length() - 1
    w0 = jnp.minimum(lax.shift_left(lax.shift_right_logical(lo_l, shift), shift), MOE_TB - MOE_WIN)
    has = active & (hi_l > lo_l)
    w1 = jnp.minimum(w0 + MOE_WIN, MOE_TB - MOE_WIN)
    return ((w0, 0, has), (w1, w0 + MOE_WIN, has & (hi_l > w0 + MOE_WIN)))


def _moe_gather_kernel(rb_ref, lo_ref, hi_ref, first_ref, tgt_ref, x_ref, o_ref):
    e, j, slot = pl.program_id(0), pl.program_id(1), pl.program_id(2)
    rb = rb_ref[e, j, slot]

    @pl.when(first_ref[e, j, slot] == 1)
    def _():
        o_ref[...] = jnp.zeros_like(o_ref)

    active = (slot == 0) | (rb != rb_ref[e, j, 0])
    tgt = tgt_ref[0]
    n_sub = MOE_TB // MOE_TS
    for sub in range(n_sub):
        toks = slice(sub * MOE_TS, (sub + 1) * MOE_TS)
        for start, cutoff, needed in _moe_windows(rb, lo_ref[e, j * n_sub + sub], hi_ref[e, j * n_sub + sub], active):

            @pl.when(needed)
            def _():
                local = start + lax.broadcasted_iota(jnp.int32, (MOE_WIN, MOE_TS), 0)
                rows = jnp.where(local >= cutoff, rb * MOE_TB + local, -2)
                onehot = jnp.where(tgt[:, toks] == rows, 1.0, 0.0).astype(BF16)
                part = jnp.dot(onehot, x_ref[toks, :], preferred_element_type=F32)
                sl = pl.ds(pl.multiple_of(start, MOE_SUB), MOE_WIN)
                o_ref[sl, :] = o_ref[sl, :] + part.astype(o_ref.dtype)


def _moe_ffn_kernel(exp_ref, nused_ref, x_ref, wg_ref, wv_ref, wd_ref, o_ref, acc_ref):
    r, f = pl.program_id(0), pl.program_id(1)
    used = r < nused_ref[0]

    @pl.when(f == 0)
    def _():
        acc_ref[...] = jnp.zeros_like(acc_ref)

    @pl.when(used)
    def _():
        x = x_ref[...]
        gate = jnp.dot(x, wg_ref[0], preferred_element_type=F32)
        val = jnp.dot(x, wv_ref[0], preferred_element_type=F32)
        act = (jax.nn.silu(gate) * val).astype(BF16)
        acc_ref[...] += jnp.dot(act, wd_ref[0], preferred_element_type=F32)

    @pl.when(f == pl.num_programs(1) - 1)
    def _():
        o_ref[...] = acc_ref[...].astype(o_ref.dtype)
    del exp_ref


def _moe_scatter_kernel(rb_ref, lo_ref, hi_ref, tgt_ref, gate_ref, y_ref, h_ref, o_ref):
    j, e, slot = pl.program_id(0), pl.program_id(1), pl.program_id(2)
    rb = rb_ref[e, j, slot]

    @pl.when((e == 0) & (slot == 0))
    def _():
        o_ref[...] = h_ref[...]

    active = (slot == 0) | (rb != rb_ref[e, j, 0])
    n_sub = MOE_TB // MOE_TS
    for sub in range(n_sub):
        toks = slice(sub * MOE_TS, (sub + 1) * MOE_TS)
        for start, cutoff, needed in _moe_windows(rb, lo_ref[e, j * n_sub + sub], hi_ref[e, j * n_sub + sub], active):

            @pl.when(needed)
            def _():
                mine = lax.broadcasted_iota(jnp.int32, (MOE_TS, N_EXPERTS), 1) == e
                tgt = jnp.sum(jnp.where(mine, tgt_ref[toks, :], 0), axis=1, keepdims=True)
                gate = jnp.sum(jnp.where(mine, gate_ref[toks, :], 0.0), axis=1, keepdims=True)
                local = start + lax.broadcasted_iota(jnp.int32, (MOE_TS, MOE_WIN), 1)
                rows = jnp.where(local >= cutoff, rb * MOE_TB + local, -2)
                onehot = jnp.where(tgt == rows, 1.0, 0.0).astype(BF16)
                y = y_ref[pl.ds(pl.multiple_of(start, MOE_SUB), MOE_WIN), :]
                o_ref[toks, :] = o_ref[toks, :] + gate * jnp.dot(onehot, y, preferred_element_type=F32)


def _moe(h2d, gain, w_router, w_up, w_down):
    t, d = h2d.shape
    tb = MOE_TB
    nj = t // tb
    n_rb = (t * TOP_K) // tb + N_EXPERTS
    xn, gates, asg = _router(h2d, gain, w_router)
    asg = asg[:, :N_EXPERTS]
    gates = gates[:, :N_EXPERTS]
    cs = jnp.cumsum(asg, axis=0)
    rank = cs - asg
    counts = cs[-1]
    padded = (counts + tb - 1) // tb * tb
    pad_end = jnp.cumsum(padded)
    start_p = pad_end - padded
    tgt = jnp.where(asg == 1, start_p[None, :] + rank, -1).astype(jnp.int32)
    ts = MOE_TS
    cb = jnp.concatenate([jnp.zeros((1, N_EXPERTS), jnp.int32), cs[ts - 1::ts]], axis=0)
    lo = (start_p[None, :] + cb[:-1]).T.astype(jnp.int32)
    hi = (start_p[None, :] + cb[1:]).T.astype(jnp.int32)
    rb0 = lo[:, ::tb // ts] // tb
    rb1 = jnp.maximum(rb0, (hi[:, tb // ts - 1::tb // ts] - 1) // tb)
    rb = jnp.stack([rb0, rb1], axis=-1).astype(jnp.int32)
    flat = rb.reshape(-1)
    first = jnp.concatenate([jnp.ones((1,), jnp.int32), (flat[1:] != flat[:-1]).astype(jnp.int32)])
    first = first.reshape(N_EXPERTS, nj, 2)
    n_used = (pad_end[-1] // tb).astype(jnp.int32).reshape(1)
    blk_exp = jnp.minimum(jnp.searchsorted(pad_end, jnp.arange(n_rb) * tb, side='right'),
                          N_EXPERTS - 1).astype(jnp.int32)

    xs = pl.pallas_call(
        _moe_gather_kernel,
        grid_spec=pltpu.PrefetchScalarGridSpec(
            num_scalar_prefetch=4,
            grid=(N_EXPERTS, nj, 2),
            in_specs=[pl.BlockSpec((1, 1, tb), lambda e, j, s, *_: (e, 0, j)),
                      pl.BlockSpec((tb, d), lambda e, j, s, *_: (j, 0))],
            out_specs=pl.BlockSpec((tb, d), lambda e, j, s, rb_ref, *_: (rb_ref[e, j, s], 0))),
        out_shape=jax.ShapeDtypeStruct((n_rb * tb, d), BF16),
        compiler_params=_cparams(("arbitrary", "arbitrary", "arbitrary"), _VMEM_LIMIT),
        name="moe_gather",
    )(rb, lo, hi, first, tgt.T.reshape(N_EXPERTS, 1, t), xn)

    nf = D_FF // FF_CHUNK
    wub = w_up.astype(BF16)
    ys = pl.pallas_call(
        _moe_ffn_kernel,
        grid_spec=pltpu.PrefetchScalarGridSpec(
            num_scalar_prefetch=2,
            grid=(n_rb, nf),
            in_specs=[pl.BlockSpec((tb, d), lambda r, f, *_: (r, 0)),
                      pl.BlockSpec((1, d, FF_CHUNK), lambda r, f, ex, nu: (ex[r], 0, f)),
                      pl.BlockSpec((1, d, FF_CHUNK), lambda r, f, ex, nu: (ex[r], 0, f + nf)),
                      pl.BlockSpec((1, FF_CHUNK, d), lambda r, f, ex, nu: (ex[r], f, 0))],
            out_specs=pl.BlockSpec((tb, d), lambda r, f, *_: (r, 0)),
            scratch_shapes=[pltpu.VMEM((tb, d), F32)]),
        out_shape=jax.ShapeDtypeStruct((n_rb * tb, d), BF16),
        compiler_params=_cparams(("arbitrary", "arbitrary"), _VMEM_LIMIT),
        name="moe_expert_ffn",
    )(blk_exp, n_used, xs, wub, wub, w_down.astype(BF16))

    return pl.pallas_call(
        _moe_scatter_kernel,
        grid_spec=pltpu.PrefetchScalarGridSpec(
            num_scalar_prefetch=3,
            grid=(nj, N_EXPERTS, 2),
            in_specs=[pl.BlockSpec((tb, N_EXPERTS), lambda j, e, s, *_: (j, 0)),
                      pl.BlockSpec((tb, N_EXPERTS), lambda j, e, s, *_: (j, 0)),
                      pl.BlockSpec((tb, d), lambda j, e, s, rb_ref, *_: (rb_ref[e, j, s], 0)),
                      pl.BlockSpec((tb, d), lambda j, e, s, *_: (j, 0))],
            out_specs=pl.BlockSpec((tb, d), lambda j, e, s, *_: (j, 0))),
        out_shape=jax.ShapeDtypeStruct((t, d), F32),
        compiler_params=_cparams(("arbitrary", "arbitrary", "arbitrary"), _VMEM_LIMIT),
        name="moe_scatter",
    )(rb, lo, hi, tgt, gates, ys, h2d)


_COL_CONV, _COL_SSM, _COL_KC, _COL_VC, _COL_KS, _COL_KW = 0, 512, 768, 896, 1024, 1152
_ROW_Q, _ROW_VS, _ROW_VW, _ROW_G = 0, 512, 640, 768


def _split_w_in(w_in):
    kvw = NSA_KV_HEADS * HEAD_DIM
    cuts = np.cumsum([0, SSM_WIDTH, 2 * CONV_WIDTH, NSA_WIDTH] + [kvw] * 6 + [N_BRANCH * NSA_HEADS])
    seg = lambda i: w_in[:, cuts[i]:cuts[i + 1]]
    ssm, conv, q, k_c, v_c, k_s, v_s, k_w, v_w, gate = (seg(i) for i in range(10))
    w_tok = jnp.concatenate([conv, ssm, k_c, v_c, k_s, k_w], axis=1).astype(BF16)
    gate = jnp.concatenate([gate, jnp.zeros((w_in.shape[0], 8), F32)], axis=1)
    w_t = jnp.concatenate([q, v_s, v_w, gate], axis=1).T.astype(BF16)
    return w_tok, w_t


def _layer_mixers(h, p, t5_tiles, s5_perm):
    b, s, d = h.shape
    w_tok, w_t = _split_w_in(p['w_in'])
    ztok, zt = _proj(h, p['norm_mix'], w_tok, w_t)

    tables = _s5_tables(p['ssm_lambda_re'], p['ssm_lambda_im'], p['ssm_log_dt'], p['ssm_b_re'], p['ssm_b_im'],
                        p['ssm_c_re'], p['ssm_c_im'], p['ssm_d'])
    gy = _s5_unpack(_s5(_s5_pack(ztok, _COL_SSM, s5_perm[0]), tables, b), s5_perm[1], b)

    g_out = p['mix_out_norm']
    yc = _conv(ztok, p['conv_w_dw'], p['conv_b_dw'], p['conv_ln_g'], p['conv_ln_b'], p['conv_w_pw'],
               g_out[SSM_WIDTH:SSM_WIDTH + CONV_WIDTH])

    kvw = NSA_KV_HEADS * HEAD_DIM
    n_cmp = s // CMP_STRIDE
    k_norm = p['nsa_k_norm']
    kc, vct = _compress(ztok, _COL_KC, _COL_VC, p['nsa_cmp_pe'], p['nsa_cmp_w1'], p['nsa_cmp_w2'], k_norm[0])
    ks, kw = _knorm(ztok, _COL_KS, _COL_KW, k_norm[1], k_norm[2])
    qg = jnp.broadcast_to((p['nsa_q_norm'] * (HEAD_DIM ** -0.5 * LOG2E))[:, None], (HEAD_DIM, TQ))
    bias_c, bias_s, bias_w = t5_tiles
    n_sel = s // L_SEL
    cs_ = np.arange(n_cmp) * CMP_STRIDE
    ss_ = np.arange(n_sel) * L_SEL
    ov = np.maximum(np.minimum(cs_[:, None] + L_CMP, ss_[None, :] + L_SEL) - np.maximum(cs_[:, None], ss_[None, :]), 0)
    ovt = jnp.asarray((ov.astype(np.float32) / L_CMP).T, BF16)
    oc_t, sel = _nsa_cmp(zt, qg, kc, vct, bias_c, ovt)

    front = lambda x, n, axis: jnp.pad(x, [(n, 0) if a == axis else (0, 0) for a in range(x.ndim)])
    ones_rows = jnp.concatenate([jnp.ones((b, NSA_KV_HEADS, 1, s), BF16),
                                 jnp.zeros((b, NSA_KV_HEADS, 15, s), BF16)], axis=2)
    heads_t = lambda rows: jnp.concatenate(
        [zt[:, rows:rows + kvw, :].astype(BF16).reshape(b, NSA_KV_HEADS, HEAD_DIM, s), ones_rows], axis=2)
    ks_p = front(ks, SEL_PAD, 2)
    kw_p = front(kw, WINDOW, 2)
    vst_p = front(heads_t(_ROW_VS), SEL_PAD, 3)
    vwt_p = front(heads_t(_ROW_VW), WINDOW, 3)
    sel_p = jnp.pad(sel, ((0, 0), (0, 0), (SEL_PAD // L_SEL, 0), (0, 0)), constant_values=NEG)
    gl = zt[:, _ROW_G:_ROW_G + N_BRANCH * NSA_HEADS, :].reshape(b, NSA_KV_HEADS, NSA_GROUP * N_BRANCH, s)
    gates_t = jnp.pad(gl, ((0, 0), (0, 0), (0, 16 - NSA_GROUP * N_BRANCH), (0, 0)))
    yn_t = _nsa_main(zt, qg, ks_p, vst_p, kw_p, vwt_p, sel_p, bias_s, bias_w, gates_t, oc_t)
    yn = yn_t.transpose(0, 2, 1)

    return _mixout(gy, yc, yn, h, p['ssm_w_glu'], g_out, p['w_out'])


def kernel(x, mem, norm_mix, w_in, ssm_lambda_re, ssm_lambda_im, ssm_log_dt, ssm_b_re, ssm_b_im, ssm_c_re, ssm_c_im, ssm_d, ssm_w_glu, conv_w_dw, conv_b_dw, conv_ln_g, conv_ln_b, conv_w_pw, nsa_q_norm, nsa_k_norm, nsa_cmp_pe, nsa_cmp_w1, nsa_cmp_w2, mix_out_norm, w_out, t5_table, norm_cross, norm_mem, x_w_q, x_w_kv, x_q_norm, x_k_norm, x_w_o, norm_ffn, ffn_w_up, ffn_w_down, moe_router, moe_w_up, moe_w_down):
    b, s, d = x.shape
    depth = w_in.shape[0]
    per_layer = dict(norm_mix=norm_mix, w_in=w_in, ssm_lambda_re=ssm_lambda_re, ssm_lambda_im=ssm_lambda_im,
                     ssm_log_dt=ssm_log_dt, ssm_b_re=ssm_b_re, ssm_b_im=ssm_b_im, ssm_c_re=ssm_c_re,
                     ssm_c_im=ssm_c_im, ssm_d=ssm_d, ssm_w_glu=ssm_w_glu, conv_w_dw=conv_w_dw,
                     conv_b_dw=conv_b_dw, conv_ln_g=conv_ln_g, conv_ln_b=conv_ln_b, conv_w_pw=conv_w_pw,
                     nsa_q_norm=nsa_q_norm, nsa_k_norm=nsa_k_norm, nsa_cmp_pe=nsa_cmp_pe, nsa_cmp_w1=nsa_cmp_w1,
                     nsa_cmp_w2=nsa_cmp_w2, mix_out_norm=mix_out_norm, w_out=w_out)
    t5_tiles = _nsa_bias_tiles(t5_table, s)
    s5_perm = _s5_perm_tables()
    h = x
    for layer in range(depth):
        p = {k: v[layer] for k, v in per_layer.items()}
        h = _layer_mixers(h, p, t5_tiles, s5_perm)
        mk, mv = _memkv(mem, norm_mem[layer], x_w_kv[layer], x_k_norm[layer])
        h = _cross(h, norm_cross[layer], x_w_q[layer], x_q_norm[layer], mk, mv, x_w_o[layer])
        h2d = h.reshape(b * s, d)
        if layer % 2 == 0:
            h2d = _ffn(h2d, norm_ffn[layer], ffn_w_up[layer // 2], ffn_w_down[layer // 2])
        else:
            h2d = _moe(h2d, norm_ffn[layer], moe_router[layer // 2], moe_w_up[layer // 2], moe_w_down[layer // 2])
        h = h2d.reshape(b, s, d)
    return h
```

```python
import functools
import math

import jax
import jax.numpy as jnp
import numpy as np
from jax import lax
from jax.experimental import pallas as pl
from jax.experimental.pallas import tpu as pltpu

F32 = jnp.float32
BF16 = jnp.bfloat16

D_MODEL = 1024
HEAD_DIM = 64
SSM_WIDTH = 256
SSM_GROUP = 16
SSM_GROUPS = 16
SSM_STATE = 64
SSM_CHUNK = 16
CONV_WIDTH = 256
CONV_K = 31
CONV_HALO = 32
NSA_WIDTH = 512
NSA_HEADS = 8
NSA_KV_HEADS = 2
NSA_GROUP = 4
N_BRANCH = 3
L_CMP = 32
CMP_STRIDE = 16
L_SEL = 64
N_SELECT = 16
N_LOCAL = 2
WINDOW = 512
SEL_FORCE = 1e6
T5_BUCKETS = 32
T5_MAX_DIST = 128
X_HEADS = 4
X_WIDTH = 256
D_FF = 2816
N_EXPERTS = 8
TOP_K = 2
EPS = 1e-6
NEG = -1e30
LOG2E = math.log2(math.e)

TQ = 256
SEL_PAD = 256
FF_CHUNK = 256
MOE_TB = 2048
MOE_TS = 256
MOE_SUB = 64
MOE_WIN = 192

_VMEM_LIMIT = 56 * 1024 * 1024


def _cparams(sem, vmem=None):
    return pltpu.CompilerParams(dimension_semantics=sem, vmem_limit_bytes=vmem)


def _rms_rows(x):
    return x * lax.rsqrt(jnp.mean(x * x, axis=-1, keepdims=True) + EPS)


def _proj_kernel(x_ref, g_ref, wtok_ref, wt_ref, gs_ref, gw_ref, ztok_ref, zt_ref, ks_ref, kw_ref):
    xn = (_rms_rows(x_ref[0]) * g_ref[...]).astype(BF16)
    z = jnp.dot(xn, wtok_ref[...], preferred_element_type=F32)
    ntok = ztok_ref.shape[2]
    ztok_ref[0] = z[:, :ntok]
    zt_ref[0] = lax.dot_general(wt_ref[...], xn, (((1,), (1,)), ((), ())), preferred_element_type=F32)
    for k, (g_ref2, dst) in enumerate(((gs_ref, ks_ref), (gw_ref, kw_ref))):
        for h in range(NSA_KV_HEADS):
            c0 = ntok + (k * NSA_KV_HEADS + h) * HEAD_DIM
            dst[0, h] = (_rms_rows(z[:, c0:c0 + HEAD_DIM]) * g_ref2[...]).astype(dst.dtype)


def _proj(h, gain, w_tok, w_t, gain_s, gain_w, tm=512):
    b, s, d = h.shape
    nall, nt = w_tok.shape[1], w_t.shape[0]
    ntok = nall - 2 * NSA_KV_HEADS * HEAD_DIM
    kout = jax.ShapeDtypeStruct((b, NSA_KV_HEADS, s, HEAD_DIM), BF16)
    kspec = pl.BlockSpec((1, NSA_KV_HEADS, tm, HEAD_DIM), lambda i, j: (i, 0, j, 0))
    return pl.pallas_call(
        _proj_kernel,
        grid=(b, s // tm),
        in_specs=[pl.BlockSpec((1, tm, d), lambda i, j: (i, j, 0)),
                  pl.BlockSpec((1, d), lambda i, j: (0, 0)),
                  pl.BlockSpec((d, nall), lambda i, j: (0, 0)),
                  pl.BlockSpec((nt, d), lambda i, j: (0, 0)),
                  pl.BlockSpec((1, HEAD_DIM), lambda i, j: (0, 0)),
                  pl.BlockSpec((1, HEAD_DIM), lambda i, j: (0, 0))],
        out_specs=[pl.BlockSpec((1, tm, ntok), lambda i, j: (i, j, 0)),
                   pl.BlockSpec((1, nt, tm), lambda i, j: (i, 0, j)), kspec, kspec],
        out_shape=[jax.ShapeDtypeStruct((b, s, ntok), F32), jax.ShapeDtypeStruct((b, nt, s), F32), kout, kout],
        compiler_params=_cparams(("parallel", "parallel"), _VMEM_LIMIT),
        name="proj",
    )(h, gain.reshape(1, d), w_tok, w_t, gain_s.reshape(1, HEAD_DIM), gain_w.reshape(1, HEAD_DIM))


def _s5_tables(lam_re, lam_im, log_dt, b_re, b_im, c_re, c_im, d_skip):
    L, H, P = SSM_CHUNK, SSM_GROUP, SSM_STATE
    dt = jnp.exp(log_dt.astype(F32))[:, None]
    lr, li = lam_re.astype(F32), lam_im.astype(F32)
    mag = jnp.exp(lr * dt)
    ar, ai = mag * jnp.cos(li * dt), mag * jnp.sin(li * dt)
    den = lr * lr + li * li
    fr = ((ar - 1.0) * lr + ai * li) / den
    fi = (ai * lr - (ar - 1.0) * li) / den
    bbr = fr[..., None] * b_re - fi[..., None] * b_im
    bbi = fr[..., None] * b_im + fi[..., None] * b_re
    j = jnp.arange(L + 1, dtype=F32)[:, None, None]
    pmag = jnp.exp(lr[None] * dt[None] * j)
    pr, pi = pmag * jnp.cos(li[None] * dt[None] * j), pmag * jnp.sin(li[None] * dt[None] * j)
    cbr = c_re[:, :, :, None] * bbr[:, None, :, :] - c_im[:, :, :, None] * bbi[:, None, :, :]
    cbi = c_re[:, :, :, None] * bbi[:, None, :, :] + c_im[:, :, :, None] * bbr[:, None, :, :]
    hp = lax.Precision.HIGHEST
    kj = (jnp.einsum('jgp,ghpk->jghk', pr[:L], cbr, precision=hp)
          - jnp.einsum('jgp,ghpk->jghk', pi[:L], cbi, precision=hp))
    lag = np.arange(L)[None, :] - np.arange(L)[:, None]
    place = (lag[None] == np.arange(L)[:, None, None]).astype(np.float32)
    kt = jnp.einsum('jab,jghk->abghk', place, kj, precision=hp)
    kt = kt + (jnp.eye(L)[:, :, None, None, None] * (jnp.eye(H)[None, None, None] * d_skip[None, None, :, :, None]))
    tmat = kt.transpose(2, 0, 4, 1, 3).reshape(SSM_GROUPS, L * H, L * H)
    qr, qi = pr[:L][::-1], pi[:L][::-1]
    wre = qr[..., None] * bbr[None] - qi[..., None] * bbi[None]
    wim = qr[..., None] * bbi[None] + qi[..., None] * bbr[None]
    wre = wre.transpose(1, 0, 3, 2).reshape(SSM_GROUPS, L * H, P)
    wim = wim.transpose(1, 0, 3, 2).reshape(SSM_GROUPS, L * H, P)
    w1 = jnp.concatenate([wre, wim], axis=-1)
    w2 = jnp.concatenate([wim, wre], axis=-1)
    sr, si = pr[1:], pi[1:]
    vr = c_re[None] * sr[:, :, None, :] - c_im[None] * si[:, :, None, :]
    vi = c_re[None] * si[:, :, None, :] + c_im[None] * sr[:, :, None, :]
    vmat = jnp.concatenate([vr, -vi], axis=-1).transpose(1, 3, 0, 2).reshape(SSM_GROUPS, 2 * P, L * H)
    a_r, a_i = pr[L], pi[L]
    am = jnp.stack([jnp.concatenate([a_r, a_r], -1), jnp.concatenate([-a_i, a_i], -1),
                    jnp.concatenate([a_i, -a_i], -1)], axis=1)
    am = jnp.concatenate([am, jnp.zeros((SSM_GROUPS, 5, 2 * P), F32)], axis=1)
    return tmat.astype(BF16), w1.astype(BF16), w2.astype(BF16), vmat.astype(BF16), am


def _s5_perm_tables():
    L, G, H = SSM_CHUNK, SSM_GROUPS, SSM_GROUP
    i = jnp.arange(L * SSM_WIDTH)
    ti, gi, hi = i // SSM_WIDTH, (i // H) % G, i % H
    o = jnp.arange(L * H)
    pack = ((gi[None, :, None] == jnp.arange(G)[:, None, None]) & (ti[None, :, None] == (o // H)[None, None, :])
            & (hi[None, :, None] == (o % H)[None, None, :]))
    j = jnp.arange(G * L * H)
    gj, tj, hj = j // (L * H), (j // H) % L, j % H
    w = jnp.arange(SSM_WIDTH)
    unpack = ((tj[None, :, None] == jnp.arange(L)[:, None, None]) & (gj[None, :, None] == (w // H)[None, None, :])
              & (hj[None, :, None] == (w % H)[None, None, :]))
    return pack.astype(BF16), unpack.astype(BF16)


def _s5_pack_kernel(ulo_ref, uhi_ref, p_ref, x_ref, u2_ref, *, n_chunks):
    half = SSM_WIDTH // 2

    @pl.when(pl.program_id(1) == 0)
    def _():
        for t in range(SSM_CHUNK):
            for k, u_ref in enumerate((ulo_ref, uhi_ref)):
                u2_ref[:, t * SSM_WIDTH + k * half:t * SSM_WIDTH + (k + 1) * half] = (
                    u_ref[0, pl.ds(t, n_chunks, stride=SSM_CHUNK), :].astype(BF16))

    x_ref[0] = jnp.dot(u2_ref[...], p_ref[0], preferred_element_type=F32).astype(x_ref.dtype)


def _s5_pack(ztok, col, pack):
    b, s, _ = ztok.shape
    n_chunks = s // SSM_CHUNK
    lw, lh = SSM_CHUNK * SSM_WIDTH, SSM_CHUNK * SSM_GROUP
    half = SSM_WIDTH // 2
    return pl.pallas_call(
        functools.partial(_s5_pack_kernel, n_chunks=n_chunks),
        grid=(b, SSM_GROUPS),
        in_specs=[pl.BlockSpec((1, s, half), lambda i, g: (i, 0, col // half)),
                  pl.BlockSpec((1, s, half), lambda i, g: (i, 0, col // half + 1)),
                  pl.BlockSpec((1, lw, lh), lambda i, g: (g, 0, 0))],
        out_specs=pl.BlockSpec((1, n_chunks, lh), lambda i, g: (g, i, 0)),
        out_shape=jax.ShapeDtypeStruct((SSM_GROUPS, b * n_chunks, lh), BF16),
        scratch_shapes=[pltpu.VMEM((n_chunks, lw), BF16)],
        compiler_params=_cparams(("parallel", "arbitrary"), _VMEM_LIMIT),
        name="s5_pack",
    )(ztok, ztok, pack)


def _s5_unpack_kernel(g_ref, r_ref, o_ref, *, n_chunks):
    t = pl.program_id(1)
    rows = jnp.concatenate([g_ref[g] for g in range(SSM_GROUPS)], axis=1)
    y = jnp.dot(rows, r_ref[0], preferred_element_type=F32)
    half = SSM_WIDTH // 2
    for k in range(SSM_CHUNK):
        @pl.when(t == k)
        def _():
            for part in range(2):
                o_ref[part, 0, pl.ds(k, n_chunks, stride=SSM_CHUNK), :] = y[:, part * half:(part + 1) * half]


def _s5_unpack(gy, unpack, bsz):
    g, r, lh = gy.shape
    n_chunks = r // bsz
    half = SSM_WIDTH // 2
    return pl.pallas_call(
        functools.partial(_s5_unpack_kernel, n_chunks=n_chunks),
        grid=(bsz, SSM_CHUNK),
        in_specs=[pl.BlockSpec((g, n_chunks, lh), lambda i, t: (0, i, 0)),
                  pl.BlockSpec((1, g * lh, SSM_WIDTH), lambda i, t: (t, 0, 0))],
        out_specs=pl.BlockSpec((2, 1, n_chunks * SSM_CHUNK, half), lambda i, t: (0, i, 0, 0)),
        out_shape=jax.ShapeDtypeStruct((2, bsz, n_chunks * SSM_CHUNK, half), F32),
        compiler_params=_cparams(("parallel", "arbitrary"), _VMEM_LIMIT),
        name="s5_unpack",
    )(gy, unpack)


def _s5_kernel(x_ref, t_ref, w1_ref, w2_ref, v_ref, a_ref, o_ref, s1_ref, s2_ref, xin_ref, *, bsz, n_chunks):
    x = x_ref[0]
    s1_ref[...] = jnp.dot(x, w1_ref[0], preferred_element_type=F32)
    s2_ref[...] = jnp.dot(x, w2_ref[0], preferred_element_type=F32)
    a1, a2, a3 = a_ref[0, 0:1, :], a_ref[0, 1:2, :], a_ref[0, 2:3, :]

    def step(c, carry):
        ps, qs = carry
        new_p, new_q = [], []
        for bi in range(bsz):
            row = pl.ds(bi * n_chunks + c, 1)
            xin_ref[row, :] = ps[bi]
            new_p.append(ps[bi] * a1 + qs[bi] * a2 + s1_ref[row, :])
            new_q.append(qs[bi] * a1 + ps[bi] * a3 + s2_ref[row, :])
        return tuple(new_p), tuple(new_q)

    zero = tuple(jnp.zeros((1, 2 * SSM_STATE), F32) for _ in range(bsz))
    lax.fori_loop(0, n_chunks, step, (zero, zero))
    y = (jnp.dot(x, t_ref[0], preferred_element_type=F32)
         + jnp.dot(xin_ref[...].astype(BF16), v_ref[0], preferred_element_type=F32))
    o_ref[0] = jax.nn.gelu(y).astype(o_ref.dtype)


def _s5(xg, tables, bsz):
    tmat, w1, w2, vmat, am = tables
    g, r, lh = xg.shape
    p2 = 2 * SSM_STATE
    kern = functools.partial(_s5_kernel, bsz=bsz, n_chunks=r // bsz)
    return pl.pallas_call(
        kern,
        grid=(g,),
        in_specs=[pl.BlockSpec((1, r, lh), lambda i: (i, 0, 0)),
                  pl.BlockSpec((1, lh, lh), lambda i: (i, 0, 0)),
                  pl.BlockSpec((1, lh, p2), lambda i: (i, 0, 0)),
                  pl.BlockSpec((1, lh, p2), lambda i: (i, 0, 0)),
                  pl.BlockSpec((1, p2, lh), lambda i: (i, 0, 0)),
                  pl.BlockSpec((1, 8, p2), lambda i: (i, 0, 0))],
        out_specs=pl.BlockSpec((1, r, lh), lambda i: (i, 0, 0)),
        out_shape=jax.ShapeDtypeStruct((g, r, lh), BF16),
        scratch_shapes=[pltpu.VMEM((r, p2), F32), pltpu.VMEM((r, p2), F32), pltpu.VMEM((r, p2), F32)],
        compiler_params=_cparams(("parallel",), _VMEM_LIMIT),
        name="s5_scan",
    )(xg, tmat, w1, w2, vmat, am)


def _conv_kernel(z_ref, halo_ref, wdw_ref, bdw_ref, lng_ref, lnb_ref, wpw_ref, go_ref, o_ref, buf_ref, sh_ref, *, tt):
    first = pl.program_id(1) == 0
    zc = z_ref[0]
    zh = halo_ref[0]
    vh = zh[:, :CONV_WIDTH] * jax.nn.sigmoid(zh[:, CONV_WIDTH:])
    buf_ref[0:CONV_HALO, :] = vh * jnp.where(first, 0.0, 1.0)
    buf_ref[CONV_HALO:CONV_HALO + tt, :] = zc[:, :CONV_WIDTH] * jax.nn.sigmoid(zc[:, CONV_WIDTH:])
    for r in range(1, 8):
        sh_ref[r, 0:tt + CONV_HALO - 8, :] = buf_ref[pl.ds(r, tt + CONV_HALO - 8), :]
    acc = jnp.zeros((tt, CONV_WIDTH), F32) + bdw_ref[...]
    for k in range(CONV_K):
        off = CONV_HALO - (CONV_K - 1) + k
        r, a = off % 8, off - off % 8
        rows = buf_ref[pl.ds(a, tt), :] if r == 0 else sh_ref[r, pl.ds(a, tt), :]
        acc = acc + wdw_ref[k:k + 1, :] * rows
    mu = jnp.mean(acc, axis=-1, keepdims=True)
    var = jnp.mean(jnp.square(acc - mu), axis=-1, keepdims=True)
    y = (acc - mu) * lax.rsqrt(var + EPS) * lng_ref[...] + lnb_ref[...]
    y = jax.nn.silu(y)
    y = jnp.dot(y.astype(BF16), wpw_ref[...], preferred_element_type=F32)
    o_ref[0] = (_rms_rows(y) * go_ref[...]).astype(o_ref.dtype)


def _conv(ztok, w_dw, b_dw, ln_g, ln_b, w_pw, g_out, tt=512):
    b, s, _ = ztok.shape
    cw = CONV_WIDTH
    hb = tt // CONV_HALO
    kern = functools.partial(_conv_kernel, tt=tt)
    row = lambda v: v.reshape(1, cw)
    return pl.pallas_call(
        kern,
        grid=(b, s // tt),
        in_specs=[pl.BlockSpec((1, tt, 2 * cw), lambda i, j: (i, j, 0)),
                  pl.BlockSpec((1, CONV_HALO, 2 * cw), lambda i, j: (i, jnp.maximum(j * hb - 1, 0), 0)),
                  pl.BlockSpec((CONV_K + 1, cw), lambda i, j: (0, 0)),
                  pl.BlockSpec((1, cw), lambda i, j: (0, 0)),
                  pl.BlockSpec((1, cw), lambda i, j: (0, 0)),
                  pl.BlockSpec((1, cw), lambda i, j: (0, 0)),
                  pl.BlockSpec((cw, cw), lambda i, j: (0, 0)),
                  pl.BlockSpec((1, cw), lambda i, j: (0, 0))],
        out_specs=pl.BlockSpec((1, tt, cw), lambda i, j: (i, j, 0)),
        out_shape=jax.ShapeDtypeStruct((b, s, cw), BF16),
        scratch_shapes=[pltpu.VMEM((CONV_HALO + tt, cw), F32), pltpu.VMEM((8, CONV_HALO + tt, cw), F32)],
        compiler_params=_cparams(("parallel", "arbitrary")),
        name="conv_mixer",
    )(ztok, ztok, jnp.concatenate([w_dw, jnp.zeros((1, cw), F32)], 0), row(b_dw), row(ln_g), row(ln_b),
      w_pw.astype(BF16), row(g_out))


def _compress_kernel(k_ref, v_ref, wka_ref, wkb_ref, ck_ref, w2k_ref, gk_ref,
                     wva_ref, wvb_ref, cv_ref, w2v_ref, ko_ref, vo_ref):
    hi = lax.Precision.HIGHEST
    n = k_ref.shape[1] // CMP_STRIDE
    kvw = k_ref.shape[2]
    nt = (((1,), (1,)), ((), ()))
    a, bm = jnp.zeros((n, kvw), F32), jnp.zeros((n, kvw), F32)
    at, bt = jnp.zeros((kvw, n), F32), jnp.zeros((kvw, n), F32)
    for l in range(CMP_STRIDE):
        kl = k_ref[0, pl.ds(l, n, stride=CMP_STRIDE), :]
        vl = v_ref[0, pl.ds(l, n, stride=CMP_STRIDE), :]
        a = a + jnp.dot(kl, wka_ref[l], precision=hi, preferred_element_type=F32)
        bm = bm + jnp.dot(kl, wkb_ref[l], precision=hi, preferred_element_type=F32)
        at = at + lax.dot_general(wva_ref[l], vl, nt, precision=hi, preferred_element_type=F32)
        bt = bt + lax.dot_general(wvb_ref[l], vl, nt, precision=hi, preferred_element_type=F32)
    pre = a + pltpu.roll(bm, n - 1, 0) + ck_ref[...]
    kc = jnp.dot(jax.nn.gelu(pre), w2k_ref[...], precision=hi, preferred_element_type=F32)
    for h in range(NSA_KV_HEADS):
        kh = kc[:, h * HEAD_DIM:(h + 1) * HEAD_DIM]
        ko_ref[0, h] = (_rms_rows(kh) * gk_ref[...]).astype(ko_ref.dtype)
    pre_t = at + pltpu.roll(bt, n - 1, 1) + cv_ref[...]
    vt = jnp.dot(w2v_ref[...], jax.nn.gelu(pre_t), precision=hi, preferred_element_type=F32)
    for h in range(NSA_KV_HEADS):
        vo_ref[0, h] = vt[h * HEAD_DIM:(h + 1) * HEAD_DIM, :].astype(vo_ref.dtype)


def _blockdiag2(w):
    z = jnp.zeros_like(w)
    return jnp.concatenate([jnp.concatenate([w, z], 1), jnp.concatenate([z, w], 1)], 0)


def _compress(ztok, col_k, col_v, pe, w1, w2, k_gain):
    b, s, _ = ztok.shape
    n = s // CMP_STRIDE
    hd, kvw = HEAD_DIM, NSA_KV_HEADS * HEAD_DIM
    hp = lax.Precision.HIGHEST

    def expand(w):
        wl = w.reshape(L_CMP, hd, hd)
        e = wl[:, None, :, None, :] * jnp.eye(NSA_KV_HEADS, dtype=F32)[None, :, None, :, None]
        e = e.reshape(L_CMP, kvw, kvw)
        return e[:CMP_STRIDE], e[CMP_STRIDE:]

    wka, wkb = expand(w1[0])
    wva, wvb = expand(w1[1])
    ck = jnp.tile(jnp.dot(pe[0].reshape(1, L_CMP * hd), w1[0], precision=hp), (1, NSA_KV_HEADS))
    cv = jnp.tile(jnp.dot(pe[1].reshape(1, L_CMP * hd), w1[1], precision=hp), (1, NSA_KV_HEADS)).T
    full = lambda shape: pl.BlockSpec(shape, lambda i: tuple(0 for _ in shape))
    return pl.pallas_call(
        _compress_kernel,
        grid=(b,),
        in_specs=[pl.BlockSpec((1, s, kvw), lambda i: (i, 0, col_k // kvw)),
                  pl.BlockSpec((1, s, kvw), lambda i: (i, 0, col_v // kvw)),
                  full((CMP_STRIDE, kvw, kvw)), full((CMP_STRIDE, kvw, kvw)), full((1, kvw)), full((kvw, kvw)),
                  full((1, hd)),
                  full((CMP_STRIDE, kvw, kvw)), full((CMP_STRIDE, kvw, kvw)), full((kvw, 1)), full((kvw, kvw))],
        out_specs=[pl.BlockSpec((1, NSA_KV_HEADS, n, hd), lambda i: (i, 0, 0, 0)),
                   pl.BlockSpec((1, NSA_KV_HEADS, hd, n), lambda i: (i, 0, 0, 0))],
        out_shape=[jax.ShapeDtypeStruct((b, NSA_KV_HEADS, n, hd), BF16),
                   jax.ShapeDtypeStruct((b, NSA_KV_HEADS, hd, n), BF16)],
        compiler_params=_cparams(("parallel",), _VMEM_LIMIT),
        name="nsa_compress",
    )(ztok, ztok, wka, wkb, ck, _blockdiag2(w2[0]), k_gain.reshape(1, hd),
      wva.transpose(0, 2, 1), wvb.transpose(0, 2, 1), cv, _blockdiag2(w2[1]).T)


def _t5_bias_by_dist(t5_table):
    n = np.arange(T5_MAX_DIST + 1)
    max_exact = T5_BUCKETS // 2
    nf = np.maximum(n, 1).astype(np.float32)
    large = max_exact + (np.log(nf / np.float32(max_exact)) / np.float32(math.log(T5_MAX_DIST / max_exact))
                         * np.float32(T5_BUCKETS - max_exact)).astype(np.int32)
    large = np.minimum(large, T5_BUCKETS - 1)
    bucket = np.where(n < max_exact, n, large)
    onehot = (bucket[:, None] == np.arange(T5_BUCKETS)[None, :]).astype(np.float32)
    return jnp.dot(onehot, t5_table, precision=lax.Precision.HIGHEST)


def _bias_tile(fdt, rows, stride, dist00, d_max=None):
    heads = fdt.shape[0]
    a0 = stride * (rows - 1)
    d_lo = dist00 - a0
    length = a0 + TQ
    d_hi = d_lo + length
    d_max = d_hi if d_max is None else d_max
    pieces = []
    for lo, hi, kind in ((d_lo, min(d_hi, 0), 'neg'), (max(d_lo, 0), min(d_hi, T5_MAX_DIST), 'tab'),
                         (max(d_lo, T5_MAX_DIST), min(d_hi, d_max), 'far'), (max(d_lo, d_max), d_hi, 'neg')):
        if hi > lo:
            pieces.append(fdt[:, lo:hi] if kind == 'tab'
                          else jnp.full((heads, hi - lo), NEG if kind == 'neg' else 0.0, F32))
    vec = jnp.concatenate(pieces, axis=1)
    c0 = -(-a0 // 128) * 128
    width = -(-(c0 + TQ) // 128) * 128
    vec = jnp.pad(vec, ((0, 0), (c0 - a0, width - (c0 - a0) - length)))

    def kern(v_ref, o_ref):
        x = jnp.broadcast_to(v_ref[0], (rows, width))
        o_ref[0] = pltpu.roll(x, 0, 1, stride=stride, stride_axis=0)[:, c0:c0 + TQ]

    return pl.pallas_call(
        kern,
        grid=(heads,),
        in_specs=[pl.BlockSpec((1, 1, width), lambda h: (h, 0, 0))],
        out_specs=pl.BlockSpec((1, rows, TQ), lambda h: (h, 0, 0)),
        out_shape=jax.ShapeDtypeStruct((heads, rows, TQ), F32),
        compiler_params=_cparams(("parallel",)),
        name="toeplitz_bias",
    )(vec.reshape(heads, 1, width))


def _nsa_bias_tiles(t5_table, seq):
    fd = _t5_bias_by_dist(t5_table).astype(F32)
    fdt = ((fd - fd[T5_MAX_DIST:]) * LOG2E).T
    n_cmp = seq // CMP_STRIDE
    qt = TQ // CMP_STRIDE
    r0 = n_cmp - qt
    band = _bias_tile(fdt, 2 * qt, CMP_STRIDE, CMP_STRIDE * qt - (L_CMP - 1))
    heads = fdt.shape[0]
    cmp_t = jnp.concatenate([jnp.zeros((heads, r0 - qt, TQ), F32), band,
                             jnp.full((heads, n_cmp - qt, TQ), NEG, F32)], axis=1)
    sel_t = _bias_tile(fdt, SEL_PAD + TQ, 1, SEL_PAD)
    win = _bias_tile(fdt, WINDOW + TQ, 1, WINDOW, d_max=WINDOW)
    rw = np.arange(WINDOW + TQ)[None, :, None]
    win_t = jnp.stack([jnp.where(rw >= WINDOW - q0, win, NEG) for q0 in (0, TQ, 2 * TQ)])
    split = lambda t: t.reshape(*t.shape[:-3], NSA_KV_HEADS, NSA_GROUP, *t.shape[-2:])

    def wide(t):
        t = jnp.swapaxes(split(t), -3, -2)
        return t.reshape(*t.shape[:-2], NSA_GROUP * TQ)

    return wide(cmp_t), wide(sel_t), wide(win_t)


def _q_head(qt_ref, g, qg_ref):
    q = qt_ref[0, g * HEAD_DIM:(g + 1) * HEAD_DIM, :]
    inv = lax.rsqrt(jnp.mean(q * q, axis=0, keepdims=True) + EPS)
    return (q * inv * qg_ref[...]).astype(BF16)


def _nsa_cmp_kernel(qt_ref, qg_ref, kc_ref, vct_ref, bias_ref, ov_ref, oc_ref, sel_ref, imp_ref, *, n_cmp, n_sel):
    qi = pl.program_id(1)
    qt = TQ // CMP_STRIDE
    gw = NSA_GROUP * HEAD_DIM
    kvs = range(NSA_KV_HEADS)
    row0 = pl.multiple_of((n_cmp - qt) - qi * qt, qt)
    qw = [jnp.concatenate([_q_head(qt_ref, kv * NSA_GROUP + g, qg_ref) for g in range(NSA_GROUP)], axis=1)
          for kv in kvs]

    def attend(n):
        for kv in kvs:
            s = (jnp.dot(kc_ref[0, kv, 0:n, :], qw[kv], preferred_element_type=F32)
                 + bias_ref[kv, pl.ds(row0, n), :])
            m = jnp.max(s, axis=0, keepdims=True)
            m = jnp.where(m < 0.5 * NEG, 0.0, m)
            p = jnp.exp2(s - m)
            p = p * (1.0 / jnp.maximum(jnp.sum(p, axis=0, keepdims=True), 1e-30))
            oc = jnp.dot(vct_ref[0, kv, :, 0:n], p.astype(BF16), preferred_element_type=F32)
            psum = jnp.zeros((n, TQ), F32)
            for g in range(NSA_GROUP):
                oc_ref[0, kv * gw + g * HEAD_DIM:kv * gw + (g + 1) * HEAD_DIM, :] = oc[:, g * TQ:(g + 1) * TQ]
                psum = psum + p[:, g * TQ:(g + 1) * TQ]
            hi = psum.astype(BF16)
            lo = (psum - hi.astype(F32)).astype(BF16)
            imp_ref[kv] = (jnp.dot(ov_ref[:, 0:n], hi, preferred_element_type=F32)
                           + jnp.dot(ov_ref[:, 0:n], lo, preferred_element_type=F32))

    chunk = min(n_cmp, 128)
    n_chunks = n_cmp // chunk
    need = lax.div((qi + 1) * qt + (chunk - 1), chunk)
    for c in range(1, n_chunks + 1):
        pl.when(need == c)(functools.partial(attend, c * chunk))

    blk = lax.broadcasted_iota(jnp.int32, (n_sel, TQ), 0)
    blk_t = lax.shift_right_logical(qi * TQ + lax.broadcasted_iota(jnp.int32, (n_sel, TQ), 1), L_SEL.bit_length() - 1)
    forced = (blk == 0) | (blk > blk_t - N_LOCAL)
    v0 = tuple(jnp.where(blk > blk_t, -jnp.inf, jnp.where(forced, SEL_FORCE, imp_ref[kv])) for kv in kvs)

    def pick(_, vs):
        out = []
        for v in vs:
            m = jnp.max(v, axis=0, keepdims=True)
            first = jnp.min(jnp.where(v == m, blk, n_sel), axis=0, keepdims=True)
            out.append(jnp.where(blk == first, -jnp.inf, v))
        return tuple(out)

    vs = lax.fori_loop(0, min(N_SELECT, n_sel), pick, v0)
    for kv in kvs:
        sel_ref[0, kv] = jnp.where((vs[kv] == -jnp.inf) & (v0[kv] > -jnp.inf), 0.0, NEG)


def _nsa_cmp(zt, qg, kc, vct, bias_c, ovt):
    b, _, s = zt.shape
    n_cmp, n_sel = s // CMP_STRIDE, s // L_SEL
    kvh = NSA_KV_HEADS
    kern = functools.partial(_nsa_cmp_kernel, n_cmp=n_cmp, n_sel=n_sel)
    return pl.pallas_call(
        kern,
        grid=(b, s // TQ),
        in_specs=[pl.BlockSpec((1, NSA_WIDTH, TQ), lambda i, j: (i, 0, j)),
                  pl.BlockSpec((HEAD_DIM, TQ), lambda i, j: (0, 0)),
                  pl.BlockSpec((1, kvh, n_cmp, HEAD_DIM), lambda i, j: (i, 0, 0, 0)),
                  pl.BlockSpec((1, kvh, HEAD_DIM, n_cmp), lambda i, j: (i, 0, 0, 0)),
                  pl.BlockSpec((kvh, bias_c.shape[1], NSA_GROUP * TQ), lambda i, j: (0, 0, 0)),
                  pl.BlockSpec((n_sel, n_cmp), lambda i, j: (0, 0))],
        out_specs=[pl.BlockSpec((1, NSA_WIDTH, TQ), lambda i, j: (i, 0, j)),
                   pl.BlockSpec((1, kvh, n_sel, TQ), lambda i, j: (i, 0, 0, j))],
        out_shape=[jax.ShapeDtypeStruct((b, NSA_WIDTH, s), F32),
                   jax.ShapeDtypeStruct((b, kvh, n_sel, s), F32)],
        scratch_shapes=[pltpu.VMEM((kvh, n_sel, TQ), F32)],
        compiler_params=_cparams(("parallel", "parallel"), _VMEM_LIMIT),
        name="nsa_compressed_select",
    )(zt, qg, kc, vct, bias_c, ovt)


def _nsa_main_kernel(qt_ref, qg_ref, ks_ref, vst_ref, kw_ref, vwt_ref, sel_ref, bs_ref, bw_ref, gate_ref,
                     oc_ref, o_ref, acc_ref, s_ref):
    qi = pl.program_id(1)
    q0 = pl.multiple_of(qi * TQ, TQ)
    near = SEL_PAD + TQ
    gw = NSA_GROUP * HEAD_DIM
    kvs = range(NSA_KV_HEADS)

    def expand_sel(kv, first_blk, n_blk):
        rows = [jnp.broadcast_to(sel_ref[0, kv, pl.ds(first_blk + r, 1), :], (L_SEL, TQ)) for r in range(n_blk)]
        rows = jnp.concatenate(rows, axis=0)
        return jnp.concatenate([rows] * NSA_GROUP, axis=1)

    qw = [jnp.concatenate([_q_head(qt_ref, kv * NSA_GROUP + g, qg_ref) for g in range(NSA_GROUP)], axis=1)
          for kv in kvs]
    m0 = []
    for kv in kvs:
        s = (jnp.dot(ks_ref[0, kv, pl.ds(q0, near), :], qw[kv], preferred_element_type=F32) + bs_ref[kv]
             + expand_sel(kv, qi * (TQ // L_SEL), near // L_SEL))
        m = jnp.max(s, axis=0, keepdims=True)
        p = jnp.exp2(s - m).astype(BF16)
        acc_ref[kv] = jnp.dot(vst_ref[0, kv, :, pl.ds(q0, near)], p, preferred_element_type=F32)
        m0.append(m)

    for kv in kvs:
        s = jnp.dot(kw_ref[0, kv, pl.ds(q0, WINDOW + TQ), :], qw[kv], preferred_element_type=F32) + bw_ref[0, kv]
        p = jnp.exp2(s - jnp.max(s, axis=0, keepdims=True)).astype(BF16)
        ow = jnp.dot(vwt_ref[0, kv, :, pl.ds(q0, WINDOW + TQ)], p, preferred_element_type=F32)
        ow = ow[:HEAD_DIM] * (1.0 / jnp.maximum(ow[HEAD_DIM:HEAD_DIM + 1], 1e-30))
        for g in range(NSA_GROUP):
            gates = jax.nn.sigmoid(gate_ref[0, kv, g * N_BRANCH:(g + 1) * N_BRANCH, :])
            rows = slice(kv * gw + g * HEAD_DIM, kv * gw + (g + 1) * HEAD_DIM)
            o_ref[0, rows, :] = gates[0:1] * oc_ref[0, rows, :] + gates[2:3] * ow[:, g * TQ:(g + 1) * TQ]

    def scores(c, slot):
        r0 = pl.multiple_of(c * TQ, TQ)
        mc = []
        for kv in kvs:
            s = (jnp.dot(ks_ref[0, kv, pl.ds(r0, TQ), :], qw[kv], preferred_element_type=F32)
                 + expand_sel(kv, c * (TQ // L_SEL), TQ // L_SEL))
            s_ref[slot, kv] = s
            mc.append(jnp.max(s, axis=0, keepdims=True))
        return tuple(mc)

    def consume(c, slot, m_old, mc):
        r0 = pl.multiple_of(c * TQ, TQ)
        m_out = []
        for kv in kvs:
            m_new = jnp.maximum(m_old[kv], mc[kv])
            alpha = jnp.exp2(m_old[kv] - m_new)
            p = jnp.exp2((s_ref[slot, kv] - m_new).astype(BF16))
            acc_ref[kv] = alpha * acc_ref[kv] + jnp.dot(vst_ref[0, kv, :, pl.ds(r0, TQ)], p,
                                                        preferred_element_type=F32)
            m_out.append(m_new)
        return tuple(m_out)

    first = SEL_PAD // TQ
    n_far = qi - first

    def pair(i, carry):
        m, mc = carry
        c = first + 2 * i
        mc1 = scores(c + 1, 1)
        m = consume(c, 0, m, mc)
        mc2 = scores(c + 2, 0)
        m = consume(c + 1, 1, m, mc1)
        return m, mc2

    m_far, mc_far = lax.fori_loop(0, n_far // 2, pair, (tuple(m0), scores(first, 0)))

    @pl.when((n_far > 0) & (n_far % 2 == 1))
    def _():
        consume(first + n_far - 1, 0, m_far, mc_far)

    for kv in kvs:
        os = acc_ref[kv]
        os = os[:HEAD_DIM] * (1.0 / jnp.maximum(os[HEAD_DIM:HEAD_DIM + 1], 1e-30))
        for g in range(NSA_GROUP):
            gate = jax.nn.sigmoid(gate_ref[0, kv, g * N_BRANCH + 1:g * N_BRANCH + 2, :])
            rows = slice(kv * gw + g * HEAD_DIM, kv * gw + (g + 1) * HEAD_DIM)
            o_ref[0, rows, :] = o_ref[0, rows, :] + gate * os[:, g * TQ:(g + 1) * TQ]


def _nsa_main(zt, qg, ks_p, vst_p, kw_p, vwt_p, sel_p, bias_s, bias_w, gates_t, oc_t):
    b, _, s = zt.shape
    kvh = NSA_KV_HEADS
    sp, wp = ks_p.shape[2], kw_p.shape[2]
    nb, vr = sel_p.shape[2], vst_p.shape[2]
    once = pl.Buffered(1)
    return pl.pallas_call(
        _nsa_main_kernel,
        grid=(b, s // TQ),
        in_specs=[pl.BlockSpec((1, NSA_WIDTH, TQ), lambda i, j: (i, 0, j)),
                  pl.BlockSpec((HEAD_DIM, TQ), lambda i, j: (0, 0)),
                  pl.BlockSpec((1, kvh, sp, HEAD_DIM), lambda i, j: (i, 0, 0, 0), pipeline_mode=once),
                  pl.BlockSpec((1, kvh, vr, sp), lambda i, j: (i, 0, 0, 0), pipeline_mode=once),
                  pl.BlockSpec((1, kvh, wp, HEAD_DIM), lambda i, j: (i, 0, 0, 0), pipeline_mode=once),
                  pl.BlockSpec((1, kvh, vr, wp), lambda i, j: (i, 0, 0, 0), pipeline_mode=once),
                  pl.BlockSpec((1, kvh, nb, TQ), lambda i, j: (i, 0, 0, j)),
                  pl.BlockSpec((kvh, SEL_PAD + TQ, NSA_GROUP * TQ), lambda i, j: (0, 0, 0), pipeline_mode=once),
                  pl.BlockSpec((1, kvh, WINDOW + TQ, NSA_GROUP * TQ), lambda i, j: (jnp.minimum(j, 2), 0, 0, 0)),
                  pl.BlockSpec((1, kvh, 16, TQ), lambda i, j: (i, 0, 0, j)),
                  pl.BlockSpec((1, NSA_WIDTH, TQ), lambda i, j: (i, 0, j))],
        out_specs=pl.BlockSpec((1, NSA_WIDTH, TQ), lambda i, j: (i, 0, j)),
        out_shape=jax.ShapeDtypeStruct((b, NSA_WIDTH, s), F32),
        scratch_shapes=[pltpu.VMEM((kvh, vr, NSA_GROUP * TQ), F32),
                        pltpu.VMEM((2, kvh, TQ, NSA_GROUP * TQ), F32)],
        compiler_params=_cparams(("parallel", "arbitrary"), _VMEM_LIMIT),
        name="nsa_selected_window",
    )(zt, qg, ks_p, vst_p, kw_p, vwt_p, sel_p, bias_s, bias_w, gates_t, oc_t)


def _mixout_kernel(gy_ref, yc_ref, yn_ref, h_ref, wglu_ref, go_ref, wo_ref, o_ref):
    sw = SSM_WIDTH
    half = sw // 2
    ag = (jnp.dot(gy_ref[0, 0].astype(BF16), wglu_ref[0:half, :], preferred_element_type=F32)
          + jnp.dot(gy_ref[1, 0].astype(BF16), wglu_ref[half:, :], preferred_element_type=F32))
    ys = ag[:, :sw] * jax.nn.sigmoid(ag[:, sw:])
    ys = (_rms_rows(ys) * go_ref[:, 0:sw]).astype(BF16)
    yn = (_rms_rows(yn_ref[0]) * go_ref[:, 2 * sw:]).astype(BF16)
    out = (jnp.dot(ys, wo_ref[0:sw, :], preferred_element_type=F32)
           + jnp.dot(yc_ref[0], wo_ref[sw:2 * sw, :], preferred_element_type=F32)
           + jnp.dot(yn, wo_ref[2 * sw:, :], preferred_element_type=F32))
    o_ref[0] = h_ref[0] + out


def _mixout(gy, yc, yn, h, w_glu, g_out, w_out, tm=512):
    b, s, d = h.shape
    tok = lambda w: pl.BlockSpec((1, tm, w), lambda i, j: (i, j, 0))
    return pl.pallas_call(
        _mixout_kernel,
        grid=(b, s // tm),
        in_specs=[pl.BlockSpec((2, 1, tm, SSM_WIDTH // 2), lambda i, j: (0, i, j, 0)),
                  tok(CONV_WIDTH), tok(NSA_WIDTH), tok(d),
                  pl.BlockSpec((SSM_WIDTH, 2 * SSM_WIDTH), lambda i, j: (0, 0)),
                  pl.BlockSpec((1, d), lambda i, j: (0, 0)),
                  pl.BlockSpec((d, d), lambda i, j: (0, 0))],
        out_specs=tok(d),
        out_shape=jax.ShapeDtypeStruct((b, s, d), F32),
        compiler_params=_cparams(("parallel", "parallel")),
        name="mix_out",
    )(gy, yc, yn, h, w_glu.astype(BF16), g_out.reshape(1, d), w_out.astype(BF16))


def _memkv_kernel(mem_ref, g_ref, w_ref, kg_ref, k_ref, v_ref):
    mn = (_rms_rows(mem_ref[0]) * g_ref[...]).astype(BF16)
    kv = jnp.dot(mn, w_ref[...], preferred_element_type=F32)
    for h in range(X_HEADS):
        cols = slice(h * HEAD_DIM, (h + 1) * HEAD_DIM)
        k_ref[0, :, cols] = (_rms_rows(kv[:, cols]) * kg_ref[...]).astype(k_ref.dtype)
    v_ref[0] = kv[:, X_WIDTH:].astype(v_ref.dtype)


def _memkv(mem, gain, w_kv, k_gain):
    b, m, d = mem.shape
    out = jax.ShapeDtypeStruct((b, m, X_WIDTH), BF16)
    return pl.pallas_call(
        _memkv_kernel,
        grid=(b,),
        in_specs=[pl.BlockSpec((1, m, d), lambda i: (i, 0, 0)),
                  pl.BlockSpec((1, d), lambda i: (0, 0)),
                  pl.BlockSpec((d, 2 * X_WIDTH), lambda i: (0, 0)),
                  pl.BlockSpec((1, HEAD_DIM), lambda i: (0, 0))],
        out_specs=[pl.BlockSpec((1, m, X_WIDTH), lambda i: (i, 0, 0))] * 2,
        out_shape=[out, out],
        compiler_params=_cparams(("parallel",)),
        name="cross_mem_kv",
    )(mem, gain.reshape(1, d), w_kv.astype(BF16), k_gain.reshape(1, HEAD_DIM))


def _cross_kernel(h_ref, g_ref, wq_ref, qg_ref, k_ref, v_ref, wo_ref, o_ref):
    h = h_ref[0]
    hn = (_rms_rows(h) * g_ref[...]).astype(BF16)
    q = jnp.dot(hn, wq_ref[...], preferred_element_type=F32)
    out = h
    for hd in range(X_HEADS):
        cols = slice(hd * HEAD_DIM, (hd + 1) * HEAD_DIM)
        qh = (_rms_rows(q[:, cols]) * qg_ref[...]).astype(BF16)
        s = lax.dot_general(qh, k_ref[0, :, cols], (((1,), (1,)), ((), ())), preferred_element_type=F32)
        p = jnp.exp(s - jnp.max(s, axis=-1, keepdims=True))
        p = p * (1.0 / jnp.sum(p, axis=-1, keepdims=True))
        o = jnp.dot(p.astype(BF16), v_ref[0, :, cols], preferred_element_type=F32)
        out = out + jnp.dot(o.astype(BF16), wo_ref[cols, :], preferred_element_type=F32)
    o_ref[0] = out


def _cross(h, gain, w_q, q_gain, k, v, w_o, tm=1024):
    b, s, d = h.shape
    m = k.shape[1]
    return pl.pallas_call(
        _cross_kernel,
        grid=(b, s // tm),
        in_specs=[pl.BlockSpec((1, tm, d), lambda i, j: (i, j, 0)),
                  pl.BlockSpec((1, d), lambda i, j: (0, 0)),
                  pl.BlockSpec((d, X_WIDTH), lambda i, j: (0, 0)),
                  pl.BlockSpec((1, HEAD_DIM), lambda i, j: (0, 0)),
                  pl.BlockSpec((1, m, X_WIDTH), lambda i, j: (i, 0, 0)),
                  pl.BlockSpec((1, m, X_WIDTH), lambda i, j: (i, 0, 0)),
                  pl.BlockSpec((X_WIDTH, d), lambda i, j: (0, 0))],
        out_specs=pl.BlockSpec((1, tm, d), lambda i, j: (i, j, 0)),
        out_shape=jax.ShapeDtypeStruct((b, s, d), F32),
        compiler_params=_cparams(("parallel", "parallel")),
        name="cross_attention",
    )(h, gain.reshape(1, d), w_q.astype(BF16), (q_gain * HEAD_DIM ** -0.5).reshape(1, HEAD_DIM), k, v,
      w_o.astype(BF16))


def _ffn_kernel(h_ref, g_ref, wg_ref, wv_ref, wd_ref, o_ref, xn_ref, acc_ref):
    f = pl.program_id(1)

    @pl.when(f == 0)
    def _():
        xn_ref[...] = (_rms_rows(h_ref[...]) * g_ref[...]).astype(BF16)
        acc_ref[...] = jnp.zeros_like(acc_ref)

    x = xn_ref[...]
    gate = jnp.dot(x, wg_ref[...], preferred_element_type=F32)
    val = jnp.dot(x, wv_ref[...], preferred_element_type=F32)
    act = (jax.nn.silu(gate) * val).astype(BF16)
    acc_ref[...] += jnp.dot(act, wd_ref[...], preferred_element_type=F32)

    @pl.when(f == pl.num_programs(1) - 1)
    def _():
        o_ref[...] = h_ref[...] + acc_ref[...]


def _ffn(h2d, gain, w_up, w_down, tm=2048):
    t, d = h2d.shape
    nf = D_FF // FF_CHUNK
    wb = w_up.astype(BF16)
    return pl.pallas_call(
        _ffn_kernel,
        grid=(t // tm, nf),
        in_specs=[pl.BlockSpec((tm, d), lambda i, f: (i, 0)),
                  pl.BlockSpec((1, d), lambda i, f: (0, 0)),
                  pl.BlockSpec((d, FF_CHUNK), lambda i, f: (0, f)),
                  pl.BlockSpec((d, FF_CHUNK), lambda i, f: (0, f + nf)),
                  pl.BlockSpec((FF_CHUNK, d), lambda i, f: (f, 0))],
        out_specs=pl.BlockSpec((tm, d), lambda i, f: (i, 0)),
        out_shape=jax.ShapeDtypeStruct((t, d), F32),
        scratch_shapes=[pltpu.VMEM((tm, d), BF16), pltpu.VMEM((tm, d), F32)],
        compiler_params=_cparams(("parallel", "arbitrary"), _VMEM_LIMIT),
        name="ffn_swiglu",
    )(h2d, gain.reshape(1, d), wb, wb, w_down.astype(BF16))


def _router_kernel(h_ref, g_ref, wr_ref, xn_ref, gate_ref, asg_ref):
    xn = _rms_rows(h_ref[...]) * g_ref[...]
    xn_ref[...] = xn.astype(BF16)
    logits = jnp.dot(xn, wr_ref[...], precision=lax.Precision.HIGHEST, preferred_element_type=F32)
    lane = lax.broadcasted_iota(jnp.int32, logits.shape, 1)
    lg = jnp.where(lane < N_EXPERTS, logits, -jnp.inf)
    m1 = jnp.max(lg, axis=-1, keepdims=True)
    i1 = jnp.min(jnp.where(lg == m1, lane, 128), axis=-1, keepdims=True)
    lg2 = jnp.where(lane == i1, -jnp.inf, lg)
    m2 = jnp.max(lg2, axis=-1, keepdims=True)
    i2 = jnp.min(jnp.where(lg2 == m2, lane, 128), axis=-1, keepdims=True)
    e = jnp.exp(m2 - m1)
    den = 1.0 + e
    gate_ref[...] = jnp.where(lane == i1, 1.0 / den, jnp.where(lane == i2, e / den, 0.0))
    asg_ref[...] = ((lane == i1) | (lane == i2)).astype(jnp.int32)


def _router(h2d, gain, w_router, tm=512):
    t, d = h2d.shape
    wr = jnp.concatenate([w_router, jnp.zeros((d, 128 - N_EXPERTS), F32)], axis=1)
    return pl.pallas_call(
        _router_kernel,
        grid=(t // tm,),
        in_specs=[pl.BlockSpec((tm, d), lambda i: (i, 0)),
                  pl.BlockSpec((1, d), lambda i: (0, 0)),
                  pl.BlockSpec((d, 128), lambda i: (0, 0))],
        out_specs=[pl.BlockSpec((tm, d), lambda i: (i, 0)),
                   pl.BlockSpec((tm, 128), lambda i: (i, 0)),
                   pl.BlockSpec((tm, 128), lambda i: (i, 0))],
        out_shape=[jax.ShapeDtypeStruct((t, d), BF16), jax.ShapeDtypeStruct((t, 128), F32),
                   jax.ShapeDtypeStruct((t, 128), jnp.int32)],
        compiler_params=_cparams(("parallel",)),
        name="moe_router",
    )(h2d, gain.reshape(1, d), wr)


def _moe_windows(rb, lo, hi, active):
    lo_l = jnp.clip(lo - rb * MOE_TB, 0, MOE_TB)
    hi_l = jnp.clip(hi - rb * MOE_TB, 0, MOE_TB)
    shift = MOE_SUB.bit_length() - 1
    w0 = jnp.minimum(lax.shift_left(lax.shift_right_logical(lo_l, shift), shift), MOE_TB - MOE_WIN)
    has = active & (hi_l > lo_l)
    w1 = jnp.minimum(w0 + MOE_WIN, MOE_TB - MOE_WIN)
    return ((w0, 0, has), (w1, w0 + MOE_WIN, has & (hi_l > w0 + MOE_WIN)))


def _moe_gather_kernel(rb_ref, lo_ref, hi_ref, first_ref, tgt_ref, x_ref, o_ref):
    e, j, slot = pl.program_id(0), pl.program_id(1), pl.program_id(2)
    rb = rb_ref[e, j, slot]

    @pl.when(first_ref[e, j, slot] == 1)
    def _():
        o_ref[...] = jnp.zeros_like(o_ref)

    active = (slot == 0) | (rb != rb_ref[e, j, 0])
    tgt = tgt_ref[0]
    n_sub = MOE_TB // MOE_TS
    for sub in range(n_sub):
        toks = slice(sub * MOE_TS, (sub + 1) * MOE_TS)
        for start, cutoff, needed in _moe_windows(rb, lo_ref[e, j * n_sub + sub], hi_ref[e, j * n_sub + sub], active):

            @pl.when(needed)
            def _():
                local = start + lax.broadcasted_iota(jnp.int32, (MOE_WIN, MOE_TS), 0)
                rows = jnp.where(local >= cutoff, rb * MOE_TB + local, -2)
                onehot = jnp.where(tgt[:, toks] == rows, 1.0, 0.0).astype(BF16)
                part = jnp.dot(onehot, x_ref[toks, :], preferred_element_type=F32)
                sl = pl.ds(pl.multiple_of(start, MOE_SUB), MOE_WIN)
                o_ref[sl, :] = o_ref[sl, :] + part.astype(o_ref.dtype)


def _moe_ffn_kernel(exp_ref, nused_ref, x_ref, wg_ref, wv_ref, wd_ref, o_ref, acc_ref):
    r, f = pl.program_id(0), pl.program_id(1)
    used = r < nused_ref[0]

    @pl.when(f == 0)
    def _():
        acc_ref[...] = jnp.zeros_like(acc_ref)

    @pl.when(used)
    def _():
        x = x_ref[...]
        gate = jnp.dot(x, wg_ref[0], preferred_element_type=F32)
        val = jnp.dot(x, wv_ref[0], preferred_element_type=F32)
        act = (jax.nn.silu(gate) * val).astype(BF16)
        acc_ref[...] += jnp.dot(act, wd_ref[0], preferred_element_type=F32)

    @pl.when(f == pl.num_programs(1) - 1)
    def _():
        o_ref[...] = acc_ref[...].astype(o_ref.dtype)
    del exp_ref


def _moe_scatter_kernel(rb_ref, lo_ref, hi_ref, tgt_ref, gate_ref, y_ref, h_ref, o_ref):
    j, e, slot = pl.program_id(0), pl.program_id(1), pl.program_id(2)
    rb = rb_ref[e, j, slot]

    @pl.when((e == 0) & (slot == 0))
    def _():
        o_ref[...] = h_ref[...]

    active = (slot == 0) | (rb != rb_ref[e, j, 0])
    n_sub = MOE_TB // MOE_TS
    for sub in range(n_sub):
        toks = slice(sub * MOE_TS, (sub + 1) * MOE_TS)
        for start, cutoff, needed in _moe_windows(rb, lo_ref[e, j * n_sub + sub], hi_ref[e, j * n_sub + sub], active):

            @pl.when(needed)
            def _():
                mine = lax.broadcasted_iota(jnp.int32, (MOE_TS, N_EXPERTS), 1) == e
                tgt = jnp.sum(jnp.where(mine, tgt_ref[toks, :], 0), axis=1, keepdims=True)
                gate = jnp.sum(jnp.where(mine, gate_ref[toks, :], 0.0), axis=1, keepdims=True)
                local = start + lax.broadcasted_iota(jnp.int32, (MOE_TS, MOE_WIN), 1)
                rows = jnp.where(local >= cutoff, rb * MOE_TB + local, -2)
                onehot = jnp.where(tgt == rows, 1.0, 0.0).astype(BF16)
                y = y_ref[pl.ds(pl.multiple_of(start, MOE_SUB), MOE_WIN), :]
                o_ref[toks, :] = o_ref[toks, :] + gate * jnp.dot(onehot, y, preferred_element_type=F32)


def _moe(h2d, gain, w_router, w_up, w_down):
    t, d = h2d.shape
    tb = MOE_TB
    nj = t // tb
    n_rb = (t * TOP_K) // tb + N_EXPERTS
    xn, gates, asg = _router(h2d, gain, w_router)
    asg = asg[:, :N_EXPERTS]
    gates = gates[:, :N_EXPERTS]
    cs = jnp.cumsum(asg, axis=0)
    rank = cs - asg
    counts = cs[-1]
    padded = (counts + tb - 1) // tb * tb
    pad_end = jnp.cumsum(padded)
    start_p = pad_end - padded
    tgt = jnp.where(asg == 1, start_p[None, :] + rank, -1).astype(jnp.int32)
    ts = MOE_TS
    cb = jnp.concatenate([jnp.zeros((1, N_EXPERTS), jnp.int32), cs[ts - 1::ts]], axis=0)
    lo = (start_p[None, :] + cb[:-1]).T.astype(jnp.int32)
    hi = (start_p[None, :] + cb[1:]).T.astype(jnp.int32)
    rb0 = lo[:, ::tb // ts] // tb
    rb1 = jnp.maximum(rb0, (hi[:, tb // ts - 1::tb // ts] - 1) // tb)
    rb = jnp.stack([rb0, rb1], axis=-1).astype(jnp.int32)
    flat = rb.reshape(-1)
    first = jnp.concatenate([jnp.ones((1,), jnp.int32), (flat[1:] != flat[:-1]).astype(jnp.int32)])
    first = first.reshape(N_EXPERTS, nj, 2)
    n_used = (pad_end[-1] // tb).astype(jnp.int32).reshape(1)
    blk_exp = jnp.minimum(jnp.searchsorted(pad_end, jnp.arange(n_rb) * tb, side='right'),
                          N_EXPERTS - 1).astype(jnp.int32)

    xs = pl.pallas_call(
        _moe_gather_kernel,
        grid_spec=pltpu.PrefetchScalarGridSpec(
            num_scalar_prefetch=4,
            grid=(N_EXPERTS, nj, 2),
            in_specs=[pl.BlockSpec((1, 1, tb), lambda e, j, s, *_: (e, 0, j)),
                      pl.BlockSpec((tb, d), lambda e, j, s, *_: (j, 0))],
            out_specs=pl.BlockSpec((tb, d), lambda e, j, s, rb_ref, *_: (rb_ref[e, j, s], 0))),
        out_shape=jax.ShapeDtypeStruct((n_rb * tb, d), BF16),
        compiler_params=_cparams(("arbitrary", "arbitrary", "arbitrary"), _VMEM_LIMIT),
        name="moe_gather",
    )(rb, lo, hi, first, tgt.T.reshape(N_EXPERTS, 1, t), xn)

    nf = D_FF // FF_CHUNK
    wub = w_up.astype(BF16)
    ys = pl.pallas_call(
        _moe_ffn_kernel,
        grid_spec=pltpu.PrefetchScalarGridSpec(
            num_scalar_prefetch=2,
            grid=(n_rb, nf),
            in_specs=[pl.BlockSpec((tb, d), lambda r, f, *_: (r, 0)),
                      pl.BlockSpec((1, d, FF_CHUNK), lambda r, f, ex, nu: (ex[r], 0, f)),
                      pl.BlockSpec((1, d, FF_CHUNK), lambda r, f, ex, nu: (ex[r], 0, f + nf)),
                      pl.BlockSpec((1, FF_CHUNK, d), lambda r, f, ex, nu: (ex[r], f, 0))],
            out_specs=pl.BlockSpec((tb, d), lambda r, f, *_: (r, 0)),
            scratch_shapes=[pltpu.VMEM((tb, d), F32)]),
        out_shape=jax.ShapeDtypeStruct((n_rb * tb, d), BF16),
        compiler_params=_cparams(("arbitrary", "arbitrary"), _VMEM_LIMIT),
        name="moe_expert_ffn",
    )(blk_exp, n_used, xs, wub, wub, w_down.astype(BF16))

    return pl.pallas_call(
        _moe_scatter_kernel,
        grid_spec=pltpu.PrefetchScalarGridSpec(
            num_scalar_prefetch=3,
            grid=(nj, N_EXPERTS, 2),
            in_specs=[pl.BlockSpec((tb, N_EXPERTS), lambda j, e, s, *_: (j, 0)),
                      pl.BlockSpec((tb, N_EXPERTS), lambda j, e, s, *_: (j, 0)),
                      pl.BlockSpec((tb, d), lambda j, e, s, rb_ref, *_: (rb_ref[e, j, s], 0)),
                      pl.BlockSpec((tb, d), lambda j, e, s, *_: (j, 0))],
            out_specs=pl.BlockSpec((tb, d), lambda j, e, s, *_: (j, 0))),
        out_shape=jax.ShapeDtypeStruct((t, d), F32),
        compiler_params=_cparams(("arbitrary", "arbitrary", "arbitrary"), _VMEM_LIMIT),
        name="moe_scatter",
    )(rb, lo, hi, tgt, gates, ys, h2d)


_COL_CONV, _COL_SSM, _COL_KC, _COL_VC = 0, 512, 768, 896
_ROW_Q, _ROW_VS, _ROW_VW, _ROW_G = 0, 512, 640, 768


def _split_w_in(w_in):
    kvw = NSA_KV_HEADS * HEAD_DIM
    cuts = np.cumsum([0, SSM_WIDTH, 2 * CONV_WIDTH, NSA_WIDTH] + [kvw] * 6 + [N_BRANCH * NSA_HEADS])
    seg = lambda i: w_in[:, cuts[i]:cuts[i + 1]]
    ssm, conv, q, k_c, v_c, k_s, v_s, k_w, v_w, gate = (seg(i) for i in range(10))
    w_tok = jnp.concatenate([conv, ssm, k_c, v_c, k_s, k_w], axis=1).astype(BF16)
    gate = jnp.concatenate([gate, jnp.zeros((w_in.shape[0], 8), F32)], axis=1)
    w_t = jnp.concatenate([q, v_s, v_w, gate], axis=1).T.astype(BF16)
    return w_tok, w_t


def _layer_mixers(h, p, t5_tiles, s5_perm):
    b, s, d = h.shape
    w_tok, w_t = _split_w_in(p['w_in'])
    k_norm = p['nsa_k_norm']
    ztok, zt, ks, kw = _proj(h, p['norm_mix'], w_tok, w_t, k_norm[1], k_norm[2])

    tables = _s5_tables(p['ssm_lambda_re'], p['ssm_lambda_im'], p['ssm_log_dt'], p['ssm_b_re'], p['ssm_b_im'],
                        p['ssm_c_re'], p['ssm_c_im'], p['ssm_d'])
    gy = _s5_unpack(_s5(_s5_pack(ztok, _COL_SSM, s5_perm[0]), tables, b), s5_perm[1], b)

    g_out = p['mix_out_norm']
    yc = _conv(ztok, p['conv_w_dw'], p['conv_b_dw'], p['conv_ln_g'], p['conv_ln_b'], p['conv_w_pw'],
               g_out[SSM_WIDTH:SSM_WIDTH + CONV_WIDTH])

    kvw = NSA_KV_HEADS * HEAD_DIM
    n_cmp = s // CMP_STRIDE
    kc, vct = _compress(ztok, _COL_KC, _COL_VC, p['nsa_cmp_pe'], p['nsa_cmp_w1'], p['nsa_cmp_w2'], k_norm[0])
    qg = jnp.broadcast_to((p['nsa_q_norm'] * (HEAD_DIM ** -0.5 * LOG2E))[:, None], (HEAD_DIM, TQ))
    bias_c, bias_s, bias_w = t5_tiles
    n_sel = s // L_SEL
    cs_ = np.arange(n_cmp) * CMP_STRIDE
    ss_ = np.arange(n_sel) * L_SEL
    ov = np.maximum(np.minimum(cs_[:, None] + L_CMP, ss_[None, :] + L_SEL) - np.maximum(cs_[:, None], ss_[None, :]), 0)
    ovt = jnp.asarray((ov.astype(np.float32) / L_CMP).T, BF16)
    oc_t, sel = _nsa_cmp(zt, qg, kc, vct, bias_c, ovt)

    front = lambda x, n, axis: jnp.pad(x, [(n, 0) if a == axis else (0, 0) for a in range(x.ndim)])
    ones_rows = jnp.concatenate([jnp.ones((b, NSA_KV_HEADS, 1, s), BF16),
                                 jnp.zeros((b, NSA_KV_HEADS, 15, s), BF16)], axis=2)
    heads_t = lambda rows: jnp.concatenate(
        [zt[:, rows:rows + kvw, :].astype(BF16).reshape(b, NSA_KV_HEADS, HEAD_DIM, s), ones_rows], axis=2)
    ks_p = front(ks, SEL_PAD, 2)
    kw_p = front(kw, WINDOW, 2)
    vst_p = front(heads_t(_ROW_VS), SEL_PAD, 3)
    vwt_p = front(heads_t(_ROW_VW), WINDOW, 3)
    sel_p = jnp.pad(sel, ((0, 0), (0, 0), (SEL_PAD // L_SEL, 0), (0, 0)), constant_values=NEG)
    gl = zt[:, _ROW_G:_ROW_G + N_BRANCH * NSA_HEADS, :].reshape(b, NSA_KV_HEADS, NSA_GROUP * N_BRANCH, s)
    gates_t = jnp.pad(gl, ((0, 0), (0, 0), (0, 16 - NSA_GROUP * N_BRANCH), (0, 0)))
    yn_t = _nsa_main(zt, qg, ks_p, vst_p, kw_p, vwt_p, sel_p, bias_s, bias_w, gates_t, oc_t)
    yn = yn_t.transpose(0, 2, 1)

    return _mixout(gy, yc, yn, h, p['ssm_w_glu'], g_out, p['w_out'])


def kernel(x, mem, norm_mix, w_in, ssm_lambda_re, ssm_lambda_im, ssm_log_dt, ssm_b_re, ssm_b_im, ssm_c_re, ssm_c_im, ssm_d, ssm_w_glu, conv_w_dw, conv_b_dw, conv_ln_g, conv_ln_b, conv_w_pw, nsa_q_norm, nsa_k_norm, nsa_cmp_pe, nsa_cmp_w1, nsa_cmp_w2, mix_out_norm, w_out, t5_table, norm_cross, norm_mem, x_w_q, x_w_kv, x_q_norm, x_k_norm, x_w_o, norm_ffn, ffn_w_up, ffn_w_down, moe_router, moe_w_up, moe_w_down):
    b, s, d = x.shape
    depth = w_in.shape[0]
    per_layer = dict(norm_mix=norm_mix, w_in=w_in, ssm_lambda_re=ssm_lambda_re, ssm_lambda_im=ssm_lambda_im,
                     ssm_log_dt=ssm_log_dt, ssm_b_re=ssm_b_re, ssm_b_im=ssm_b_im, ssm_c_re=ssm_c_re,
                     ssm_c_im=ssm_c_im, ssm_d=ssm_d, ssm_w_glu=ssm_w_glu, conv_w_dw=conv_w_dw,
                     conv_b_dw=conv_b_dw, conv_ln_g=conv_ln_g, conv_ln_b=conv_ln_b, conv_w_pw=conv_w_pw,
                     nsa_q_norm=nsa_q_norm, nsa_k_norm=nsa_k_norm, nsa_cmp_pe=nsa_cmp_pe, nsa_cmp_w1=nsa_cmp_w1,
                     nsa_cmp_w2=nsa_cmp_w2, mix_out_norm=mix_out_norm, w_out=w_out)
    t5_tiles = _nsa_bias_tiles(t5_table, s)
    s5_perm = _s5_perm_tables()
    h = x
    for layer in range(depth):
        p = {k: v[layer] for k, v in per_layer.items()}
        h = _layer_mixers(h, p, t5_tiles, s5_perm)
        mk, mv = _memkv(mem, norm_mem[layer], x_w_kv[layer], x_k_norm[layer])
        h = _cross(h, norm_cross[layer], x_w_q[layer], x_q_norm[layer], mk, mv, x_w_o[layer])
        h2d = h.reshape(b * s, d)
        if layer % 2 == 0:
            h2d = _ffn(h2d, norm_ffn[layer], ffn_w_up[layer // 2], ffn_w_down[layer // 2])
        else:
            h2d = _moe(h2d, norm_ffn[layer], moe_router[layer // 2], moe_w_up[layer // 2], moe_w_down[layer // 2])
        h = h2d.reshape(b, s, d)
    return h
```

```python
import functools
import math

import jax
import jax.numpy as jnp
import numpy as np
from jax import lax
from jax.experimental import pallas as pl
from jax.experimental.pallas import tpu as pltpu

F32 = jnp.float32
BF16 = jnp.bfloat16

D_MODEL = 1024
HEAD_DIM = 64
SSM_WIDTH = 256
SSM_GROUP = 16
SSM_GROUPS = 16
SSM_STATE = 64
SSM_CHUNK = 16
CONV_WIDTH = 256
CONV_K = 31
CONV_HALO = 32
NSA_WIDTH = 512
NSA_HEADS = 8
NSA_KV_HEADS = 2
NSA_GROUP = 4
N_BRANCH = 3
L_CMP = 32
CMP_STRIDE = 16
L_SEL = 64
N_SELECT = 16
N_LOCAL = 2
WINDOW = 512
SEL_FORCE = 1e6
T5_BUCKETS = 32
T5_MAX_DIST = 128
X_HEADS = 4
X_WIDTH = 256
D_FF = 2816
N_EXPERTS = 8
TOP_K = 2
EPS = 1e-6
NEG = -1e30
LOG2E = math.log2(math.e)

TQ = 256
SEL_PAD = 256
FF_CHUNK = 256
MOE_TB = 2048
MOE_TS = 256
MOE_SUB = 64
MOE_WIN = 192

_VMEM_LIMIT = 56 * 1024 * 1024


def _cparams(sem, vmem=None):
    return pltpu.CompilerParams(dimension_semantics=sem, vmem_limit_bytes=vmem)


def _rms_rows(x):
    return x * lax.rsqrt(jnp.mean(x * x, axis=-1, keepdims=True) + EPS)


def _proj_kernel(x_ref, g_ref, wtok_ref, wt_ref, ztok_ref, zt_ref):
    xn = (_rms_rows(x_ref[0]) * g_ref[...]).astype(BF16)
    ztok_ref[0] = jnp.dot(xn, wtok_ref[...], preferred_element_type=F32)
    zt_ref[0] = lax.dot_general(wt_ref[...], xn, (((1,), (1,)), ((), ())), preferred_element_type=F32)


def _proj(h, gain, w_tok, w_t, tm=1024):
    b, s, d = h.shape
    ntok, nt = w_tok.shape[1], w_t.shape[0]
    return pl.pallas_call(
        _proj_kernel,
        grid=(b, s // tm),
        in_specs=[pl.BlockSpec((1, tm, d), lambda i, j: (i, j, 0)),
                  pl.BlockSpec((1, d), lambda i, j: (0, 0)),
                  pl.BlockSpec((d, ntok), lambda i, j: (0, 0)),
                  pl.BlockSpec((nt, d), lambda i, j: (0, 0))],
        out_specs=[pl.BlockSpec((1, tm, ntok), lambda i, j: (i, j, 0)),
                   pl.BlockSpec((1, nt, tm), lambda i, j: (i, 0, j))],
        out_shape=[jax.ShapeDtypeStruct((b, s, ntok), F32), jax.ShapeDtypeStruct((b, nt, s), F32)],
        compiler_params=_cparams(("parallel", "parallel"), _VMEM_LIMIT),
        name="proj",
    )(h, gain.reshape(1, d), w_tok, w_t)


def _s5_tables(lam_re, lam_im, log_dt, b_re, b_im, c_re, c_im, d_skip):
    L, H, P = SSM_CHUNK, SSM_GROUP, SSM_STATE
    dt = jnp.exp(log_dt.astype(F32))[:, None]
    lr, li = lam_re.astype(F32), lam_im.astype(F32)
    mag = jnp.exp(lr * dt)
    ar, ai = mag * jnp.cos(li * dt), mag * jnp.sin(li * dt)
    den = lr * lr + li * li
    fr = ((ar - 1.0) * lr + ai * li) / den
    fi = (ai * lr - (ar - 1.0) * li) / den
    bbr = fr[..., None] * b_re - fi[..., None] * b_im
    bbi = fr[..., None] * b_im + fi[..., None] * b_re
    j = jnp.arange(L + 1, dtype=F32)[:, None, None]
    pmag = jnp.exp(lr[None] * dt[None] * j)
    pr, pi = pmag * jnp.cos(li[None] * dt[None] * j), pmag * jnp.sin(li[None] * dt[None] * j)
    cbr = c_re[:, :, :, None] * bbr[:, None, :, :] - c_im[:, :, :, None] * bbi[:, None, :, :]
    cbi = c_re[:, :, :, None] * bbi[:, None, :, :] + c_im[:, :, :, None] * bbr[:, None, :, :]
    hp = lax.Precision.HIGHEST
    kj = (jnp.einsum('jgp,ghpk->jghk', pr[:L], cbr, precision=hp)
          - jnp.einsum('jgp,ghpk->jghk', pi[:L], cbi, precision=hp))
    lag = np.arange(L)[None, :] - np.arange(L)[:, None]
    place = (lag[None] == np.arange(L)[:, None, None]).astype(np.float32)
    kt = jnp.einsum('jab,jghk->abghk', place, kj, precision=hp)
    kt = kt + (jnp.eye(L)[:, :, None, None, None] * (jnp.eye(H)[None, None, None] * d_skip[None, None, :, :, None]))
    tmat = kt.transpose(2, 0, 4, 1, 3).reshape(SSM_GROUPS, L * H, L * H)
    qr, qi = pr[:L][::-1], pi[:L][::-1]
    wre = qr[..., None] * bbr[None] - qi[..., None] * bbi[None]
    wim = qr[..., None] * bbi[None] + qi[..., None] * bbr[None]
    wre = wre.transpose(1, 0, 3, 2).reshape(SSM_GROUPS, L * H, P)
    wim = wim.transpose(1, 0, 3, 2).reshape(SSM_GROUPS, L * H, P)
    w1 = jnp.concatenate([wre, wim], axis=-1)
    w2 = jnp.concatenate([wim, wre], axis=-1)
    sr, si = pr[1:], pi[1:]
    vr = c_re[None] * sr[:, :, None, :] - c_im[None] * si[:, :, None, :]
    vi = c_re[None] * si[:, :, None, :] + c_im[None] * sr[:, :, None, :]
    vmat = jnp.concatenate([vr, -vi], axis=-1).transpose(1, 3, 0, 2).reshape(SSM_GROUPS, 2 * P, L * H)
    a_r, a_i = pr[L], pi[L]
    am = jnp.stack([jnp.concatenate([a_r, a_r], -1), jnp.concatenate([-a_i, a_i], -1),
                    jnp.concatenate([a_i, -a_i], -1)], axis=1)
    am = jnp.concatenate([am, jnp.zeros((SSM_GROUPS, 5, 2 * P), F32)], axis=1)
    return tmat.astype(BF16), w1.astype(BF16), w2.astype(BF16), vmat.astype(BF16), am


def _s5_perm_tables():
    L, G, H = SSM_CHUNK, SSM_GROUPS, SSM_GROUP
    i = jnp.arange(L * SSM_WIDTH)
    ti, gi, hi = i // SSM_WIDTH, (i // H) % G, i % H
    o = jnp.arange(L * H)
    pack = ((gi[None, :, None] == jnp.arange(G)[:, None, None]) & (ti[None, :, None] == (o // H)[None, None, :])
            & (hi[None, :, None] == (o % H)[None, None, :]))
    j = jnp.arange(G * L * H)
    gj, tj, hj = j // (L * H), (j // H) % L, j % H
    w = jnp.arange(SSM_WIDTH)
    unpack = ((tj[None, :, None] == jnp.arange(L)[:, None, None]) & (gj[None, :, None] == (w // H)[None, None, :])
              & (hj[None, :, None] == (w % H)[None, None, :]))
    return pack.astype(BF16), unpack.astype(BF16)


def _s5_pack_kernel(ulo_ref, uhi_ref, p_ref, x_ref, u2_ref, *, n_chunks):
    half = SSM_WIDTH // 2

    @pl.when(pl.program_id(1) == 0)
    def _():
        for t in range(SSM_CHUNK):
            for k, u_ref in enumerate((ulo_ref, uhi_ref)):
                u2_ref[:, t * SSM_WIDTH + k * half:t * SSM_WIDTH + (k + 1) * half] = (
                    u_ref[0, pl.ds(t, n_chunks, stride=SSM_CHUNK), :].astype(BF16))

    x_ref[0] = jnp.dot(u2_ref[...], p_ref[0], preferred_element_type=F32).astype(x_ref.dtype)


def _s5_pack(ztok, col, pack):
    b, s, _ = ztok.shape
    n_chunks = s // SSM_CHUNK
    lw, lh = SSM_CHUNK * SSM_WIDTH, SSM_CHUNK * SSM_GROUP
    half = SSM_WIDTH // 2
    return pl.pallas_call(
        functools.partial(_s5_pack_kernel, n_chunks=n_chunks),
        grid=(b, SSM_GROUPS),
        in_specs=[pl.BlockSpec((1, s, half), lambda i, g: (i, 0, col // half)),
                  pl.BlockSpec((1, s, half), lambda i, g: (i, 0, col // half + 1)),
                  pl.BlockSpec((1, lw, lh), lambda i, g: (g, 0, 0))],
        out_specs=pl.BlockSpec((1, n_chunks, lh), lambda i, g: (g, i, 0)),
        out_shape=jax.ShapeDtypeStruct((SSM_GROUPS, b * n_chunks, lh), BF16),
        scratch_shapes=[pltpu.VMEM((n_chunks, lw), BF16)],
        compiler_params=_cparams(("parallel", "arbitrary"), _VMEM_LIMIT),
        name="s5_pack",
    )(ztok, ztok, pack)


def _s5_unpack_kernel(g_ref, r_ref, o_ref, *, n_chunks):
    t = pl.program_id(1)
    rows = jnp.concatenate([g_ref[g] for g in range(SSM_GROUPS)], axis=1)
    y = jnp.dot(rows, r_ref[0], preferred_element_type=F32)
    half = SSM_WIDTH // 2
    for k in range(SSM_CHUNK):
        @pl.when(t == k)
        def _():
            for part in range(2):
                o_ref[part, 0, pl.ds(k, n_chunks, stride=SSM_CHUNK), :] = y[:, part * half:(part + 1) * half]


def _s5_unpack(gy, unpack, bsz):
    g, r, lh = gy.shape
    n_chunks = r // bsz
    half = SSM_WIDTH // 2
    return pl.pallas_call(
        functools.partial(_s5_unpack_kernel, n_chunks=n_chunks),
        grid=(bsz, SSM_CHUNK),
        in_specs=[pl.BlockSpec((g, n_chunks, lh), lambda i, t: (0, i, 0)),
                  pl.BlockSpec((1, g * lh, SSM_WIDTH), lambda i, t: (t, 0, 0))],
        out_specs=pl.BlockSpec((2, 1, n_chunks * SSM_CHUNK, half), lambda i, t: (0, i, 0, 0)),
        out_shape=jax.ShapeDtypeStruct((2, bsz, n_chunks * SSM_CHUNK, half), F32),
        compiler_params=_cparams(("parallel", "arbitrary"), _VMEM_LIMIT),
        name="s5_unpack",
    )(gy, unpack)


def _s5_kernel(x_ref, t_ref, w1_ref, w2_ref, v_ref, a_ref, o_ref, s1_ref, s2_ref, xin_ref, *, bsz, n_chunks):
    x = x_ref[0]
    s1_ref[...] = jnp.dot(x, w1_ref[0], preferred_element_type=F32)
    s2_ref[...] = jnp.dot(x, w2_ref[0], preferred_element_type=F32)
    a1, a2, a3 = a_ref[0, 0:1, :], a_ref[0, 1:2, :], a_ref[0, 2:3, :]

    def step(c, carry):
        ps, qs = carry
        new_p, new_q = [], []
        for bi in range(bsz):
            row = pl.ds(bi * n_chunks + c, 1)
            xin_ref[row, :] = ps[bi]
            new_p.append(ps[bi] * a1 + qs[bi] * a2 + s1_ref[row, :])
            new_q.append(qs[bi] * a1 + ps[bi] * a3 + s2_ref[row, :])
        return tuple(new_p), tuple(new_q)

    zero = tuple(jnp.zeros((1, 2 * SSM_STATE), F32) for _ in range(bsz))
    lax.fori_loop(0, n_chunks, step, (zero, zero))
    y = (jnp.dot(x, t_ref[0], preferred_element_type=F32)
         + jnp.dot(xin_ref[...].astype(BF16), v_ref[0], preferred_element_type=F32))
    o_ref[0] = jax.nn.gelu(y).astype(o_ref.dtype)


def _s5(xg, tables, bsz):
    tmat, w1, w2, vmat, am = tables
    g, r, lh = xg.shape
    p2 = 2 * SSM_STATE
    kern = functools.partial(_s5_kernel, bsz=bsz, n_chunks=r // bsz)
    return pl.pallas_call(
        kern,
        grid=(g,),
        in_specs=[pl.BlockSpec((1, r, lh), lambda i: (i, 0, 0)),
                  pl.BlockSpec((1, lh, lh), lambda i: (i, 0, 0)),
                  pl.BlockSpec((1, lh, p2), lambda i: (i, 0, 0)),
                  pl.BlockSpec((1, lh, p2), lambda i: (i, 0, 0)),
                  pl.BlockSpec((1, p2, lh), lambda i: (i, 0, 0)),
                  pl.BlockSpec((1, 8, p2), lambda i: (i, 0, 0))],
        out_specs=pl.BlockSpec((1, r, lh), lambda i: (i, 0, 0)),
        out_shape=jax.ShapeDtypeStruct((g, r, lh), BF16),
        scratch_shapes=[pltpu.VMEM((r, p2), F32), pltpu.VMEM((r, p2), F32), pltpu.VMEM((r, p2), F32)],
        compiler_params=_cparams(("parallel",), _VMEM_LIMIT),
        name="s5_scan",
    )(xg, tmat, w1, w2, vmat, am)


def _conv_kernel(z_ref, halo_ref, wdw_ref, bdw_ref, lng_ref, lnb_ref, wpw_ref, go_ref, o_ref, buf_ref, sh_ref, *, tt):
    first = pl.program_id(1) == 0
    zc = z_ref[0]
    zh = halo_ref[0]
    vh = zh[:, :CONV_WIDTH] * jax.nn.sigmoid(zh[:, CONV_WIDTH:])
    buf_ref[0:CONV_HALO, :] = vh * jnp.where(first, 0.0, 1.0)
    buf_ref[CONV_HALO:CONV_HALO + tt, :] = zc[:, :CONV_WIDTH] * jax.nn.sigmoid(zc[:, CONV_WIDTH:])
    for r in range(1, 8):
        sh_ref[r, 0:tt + CONV_HALO - 8, :] = buf_ref[pl.ds(r, tt + CONV_HALO - 8), :]
    acc = jnp.zeros((tt, CONV_WIDTH), F32) + bdw_ref[...]
    for k in range(CONV_K):
        off = CONV_HALO - (CONV_K - 1) + k
        r, a = off % 8, off - off % 8
        rows = buf_ref[pl.ds(a, tt), :] if r == 0 else sh_ref[r, pl.ds(a, tt), :]
        acc = acc + wdw_ref[k:k + 1, :] * rows
    mu = jnp.mean(acc, axis=-1, keepdims=True)
    var = jnp.mean(jnp.square(acc - mu), axis=-1, keepdims=True)
    y = (acc - mu) * lax.rsqrt(var + EPS) * lng_ref[...] + lnb_ref[...]
    y = jax.nn.silu(y)
    y = jnp.dot(y.astype(BF16), wpw_ref[...], preferred_element_type=F32)
    o_ref[0] = (_rms_rows(y) * go_ref[...]).astype(o_ref.dtype)


def _conv(ztok, w_dw, b_dw, ln_g, ln_b, w_pw, g_out, tt=512):
    b, s, _ = ztok.shape
    cw = CONV_WIDTH
    hb = tt // CONV_HALO
    kern = functools.partial(_conv_kernel, tt=tt)
    row = lambda v: v.reshape(1, cw)
    return pl.pallas_call(
        kern,
        grid=(b, s // tt),
        in_specs=[pl.BlockSpec((1, tt, 2 * cw), lambda i, j: (i, j, 0)),
                  pl.BlockSpec((1, CONV_HALO, 2 * cw), lambda i, j: (i, jnp.maximum(j * hb - 1, 0), 0)),
                  pl.BlockSpec((CONV_K + 1, cw), lambda i, j: (0, 0)),
                  pl.BlockSpec((1, cw), lambda i, j: (0, 0)),
                  pl.BlockSpec((1, cw), lambda i, j: (0, 0)),
                  pl.BlockSpec((1, cw), lambda i, j: (0, 0)),
                  pl.BlockSpec((cw, cw), lambda i, j: (0, 0)),
                  pl.BlockSpec((1, cw), lambda i, j: (0, 0))],
        out_specs=pl.BlockSpec((1, tt, cw), lambda i, j: (i, j, 0)),
        out_shape=jax.ShapeDtypeStruct((b, s, cw), BF16),
        scratch_shapes=[pltpu.VMEM((CONV_HALO + tt, cw), F32), pltpu.VMEM((8, CONV_HALO + tt, cw), F32)],
        compiler_params=_cparams(("parallel", "arbitrary")),
        name="conv_mixer",
    )(ztok, ztok, jnp.concatenate([w_dw, jnp.zeros((1, cw), F32)], 0), row(b_dw), row(ln_g), row(ln_b),
      w_pw.astype(BF16), row(g_out))


def _knorm_kernel(ks_ref, kw_ref, gs_ref, gw_ref, os_ref, ow_ref):
    for src, g_ref, dst in ((ks_ref, gs_ref, os_ref), (kw_ref, gw_ref, ow_ref)):
        x = src[0]
        for h in range(NSA_KV_HEADS):
            xh = x[:, h * HEAD_DIM:(h + 1) * HEAD_DIM]
            dst[0, h] = (_rms_rows(xh) * g_ref[...]).astype(dst.dtype)


def _knorm(ztok, col_s, col_w, gain_s, gain_w, tt=512):
    b, s, _ = ztok.shape
    kw = NSA_KV_HEADS * HEAD_DIM
    out = jax.ShapeDtypeStruct((b, NSA_KV_HEADS, s, HEAD_DIM), BF16)
    ospec = pl.BlockSpec((1, NSA_KV_HEADS, tt, HEAD_DIM), lambda i, j: (i, 0, j, 0))
    return pl.pallas_call(
        _knorm_kernel,
        grid=(b, s // tt),
        in_specs=[pl.BlockSpec((1, tt, kw), lambda i, j: (i, j, col_s // kw)),
                  pl.BlockSpec((1, tt, kw), lambda i, j: (i, j, col_w // kw)),
                  pl.BlockSpec((1, HEAD_DIM), lambda i, j: (0, 0)),
                  pl.BlockSpec((1, HEAD_DIM), lambda i, j: (0, 0))],
        out_specs=[ospec, ospec],
        out_shape=[out, out],
        compiler_params=_cparams(("parallel", "parallel")),
        name="nsa_key_norm",
    )(ztok, ztok, gain_s.reshape(1, HEAD_DIM), gain_w.reshape(1, HEAD_DIM))


def _compress_kernel(k_ref, v_ref, wka_ref, wkb_ref, ck_ref, w2k_ref, gk_ref,
                     wva_ref, wvb_ref, cv_ref, w2v_ref, ko_ref, vo_ref):
    hi = lax.Precision.HIGHEST
    n = k_ref.shape[1] // CMP_STRIDE
    kvw = k_ref.shape[2]
    nt = (((1,), (1,)), ((), ()))
    a, bm = jnp.zeros((n, kvw), F32), jnp.zeros((n, kvw), F32)
    at, bt = jnp.zeros((kvw, n), F32), jnp.zeros((kvw, n), F32)
    for l in range(CMP_STRIDE):
        kl = k_ref[0, pl.ds(l, n, stride=CMP_STRIDE), :]
        vl = v_ref[0, pl.ds(l, n, stride=CMP_STRIDE), :]
        a = a + jnp.dot(kl, wka_ref[l], precision=hi, preferred_element_type=F32)
        bm = bm + jnp.dot(kl, wkb_ref[l], precision=hi, preferred_element_type=F32)
        at = at + lax.dot_general(wva_ref[l], vl, nt, precision=hi, preferred_element_type=F32)
        bt = bt + lax.dot_general(wvb_ref[l], vl, nt, precision=hi, preferred_element_type=F32)
    pre = a + pltpu.roll(bm, n - 1, 0) + ck_ref[...]
    kc = jnp.dot(jax.nn.gelu(pre), w2k_ref[...], precision=hi, preferred_element_type=F32)
    for h in range(NSA_KV_HEADS):
        kh = kc[:, h * HEAD_DIM:(h + 1) * HEAD_DIM]
        ko_ref[0, h] = (_rms_rows(kh) * gk_ref[...]).astype(ko_ref.dtype)
    pre_t = at + pltpu.roll(bt, n - 1, 1) + cv_ref[...]
    vt = jnp.dot(w2v_ref[...], jax.nn.gelu(pre_t), precision=hi, preferred_element_type=F32)
    for h in range(NSA_KV_HEADS):
        vo_ref[0, h] = vt[h * HEAD_DIM:(h + 1) * HEAD_DIM, :].astype(vo_ref.dtype)


def _blockdiag2(w):
    z = jnp.zeros_like(w)
    return jnp.concatenate([jnp.concatenate([w, z], 1), jnp.concatenate([z, w], 1)], 0)


def _compress(ztok, col_k, col_v, pe, w1, w2, k_gain):
    b, s, _ = ztok.shape
    n = s // CMP_STRIDE
    hd, kvw = HEAD_DIM, NSA_KV_HEADS * HEAD_DIM
    hp = lax.Precision.HIGHEST

    def expand(w):
        wl = w.reshape(L_CMP, hd, hd)
        e = wl[:, None, :, None, :] * jnp.eye(NSA_KV_HEADS, dtype=F32)[None, :, None, :, None]
        e = e.reshape(L_CMP, kvw, kvw)
        return e[:CMP_STRIDE], e[CMP_STRIDE:]

    wka, wkb = expand(w1[0])
    wva, wvb = expand(w1[1])
    ck = jnp.tile(jnp.dot(pe[0].reshape(1, L_CMP * hd), w1[0], precision=hp), (1, NSA_KV_HEADS))
    cv = jnp.tile(jnp.dot(pe[1].reshape(1, L_CMP * hd), w1[1], precision=hp), (1, NSA_KV_HEADS)).T
    full = lambda shape: pl.BlockSpec(shape, lambda i: tuple(0 for _ in shape))
    return pl.pallas_call(
        _compress_kernel,
        grid=(b,),
        in_specs=[pl.BlockSpec((1, s, kvw), lambda i: (i, 0, col_k // kvw)),
                  pl.BlockSpec((1, s, kvw), lambda i: (i, 0, col_v // kvw)),
                  full((CMP_STRIDE, kvw, kvw)), full((CMP_STRIDE, kvw, kvw)), full((1, kvw)), full((kvw, kvw)),
                  full((1, hd)),
                  full((CMP_STRIDE, kvw, kvw)), full((CMP_STRIDE, kvw, kvw)), full((kvw, 1)), full((kvw, kvw))],
        out_specs=[pl.BlockSpec((1, NSA_KV_HEADS, n, hd), lambda i: (i, 0, 0, 0)),
                   pl.BlockSpec((1, NSA_KV_HEADS, hd, n), lambda i: (i, 0, 0, 0))],
        out_shape=[jax.ShapeDtypeStruct((b, NSA_KV_HEADS, n, hd), BF16),
                   jax.ShapeDtypeStruct((b, NSA_KV_HEADS, hd, n), BF16)],
        compiler_params=_cparams(("parallel",), _VMEM_LIMIT),
        name="nsa_compress",
    )(ztok, ztok, wka, wkb, ck, _blockdiag2(w2[0]), k_gain.reshape(1, hd),
      wva.transpose(0, 2, 1), wvb.transpose(0, 2, 1), cv, _blockdiag2(w2[1]).T)


def _t5_bias_by_dist(t5_table):
    n = np.arange(T5_MAX_DIST + 1)
    max_exact = T5_BUCKETS // 2
    nf = np.maximum(n, 1).astype(np.float32)
    large = max_exact + (np.log(nf / np.float32(max_exact)) / np.float32(math.log(T5_MAX_DIST / max_exact))
                         * np.float32(T5_BUCKETS - max_exact)).astype(np.int32)
    large = np.minimum(large, T5_BUCKETS - 1)
    bucket = np.where(n < max_exact, n, large)
    onehot = (bucket[:, None] == np.arange(T5_BUCKETS)[None, :]).astype(np.float32)
    return jnp.dot(onehot, t5_table, precision=lax.Precision.HIGHEST)


def _bias_tile(fdt, rows, stride, dist00, d_max=None):
    heads = fdt.shape[0]
    a0 = stride * (rows - 1)
    d_lo = dist00 - a0
    length = a0 + TQ
    d_hi = d_lo + length
    d_max = d_hi if d_max is None else d_max
    pieces = []
    for lo, hi, kind in ((d_lo, min(d_hi, 0), 'neg'), (max(d_lo, 0), min(d_hi, T5_MAX_DIST), 'tab'),
                         (max(d_lo, T5_MAX_DIST), min(d_hi, d_max), 'far'), (max(d_lo, d_max), d_hi, 'neg')):
        if hi > lo:
            pieces.append(fdt[:, lo:hi] if kind == 'tab'
                          else jnp.full((heads, hi - lo), NEG if kind == 'neg' else 0.0, F32))
    vec = jnp.concatenate(pieces, axis=1)
    c0 = -(-a0 // 128) * 128
    width = -(-(c0 + TQ) // 128) * 128
    vec = jnp.pad(vec, ((0, 0), (c0 - a0, width - (c0 - a0) - length)))

    def kern(v_ref, o_ref):
        x = jnp.broadcast_to(v_ref[0], (rows, width))
        o_ref[0] = pltpu.roll(x, 0, 1, stride=stride, stride_axis=0)[:, c0:c0 + TQ]

    return pl.pallas_call(
        kern,
        grid=(heads,),
        in_specs=[pl.BlockSpec((1, 1, width), lambda h: (h, 0, 0))],
        out_specs=pl.BlockSpec((1, rows, TQ), lambda h: (h, 0, 0)),
        out_shape=jax.ShapeDtypeStruct((heads, rows, TQ), F32),
        compiler_params=_cparams(("parallel",)),
        name="toeplitz_bias",
    )(vec.reshape(heads, 1, width))


def _nsa_bias_tiles(t5_table, seq):
    fd = _t5_bias_by_dist(t5_table).astype(F32)
    fdt = ((fd - fd[T5_MAX_DIST:]) * LOG2E).T
    n_cmp = seq // CMP_STRIDE
    qt = TQ // CMP_STRIDE
    r0 = n_cmp - qt
    band = _bias_tile(fdt, 2 * qt, CMP_STRIDE, CMP_STRIDE * qt - (L_CMP - 1))
    heads = fdt.shape[0]
    cmp_t = jnp.concatenate([jnp.zeros((heads, r0 - qt, TQ), F32), band,
                             jnp.full((heads, n_cmp - qt, TQ), NEG, F32)], axis=1)
    sel_t = _bias_tile(fdt, SEL_PAD + TQ, 1, SEL_PAD)
    win = _bias_tile(fdt, WINDOW + TQ, 1, WINDOW, d_max=WINDOW)
    rw = np.arange(WINDOW + TQ)[None, :, None]
    win_t = jnp.stack([jnp.where(rw >= WINDOW - q0, win, NEG) for q0 in (0, TQ, 2 * TQ)])
    split = lambda t: t.reshape(*t.shape[:-3], NSA_KV_HEADS, NSA_GROUP, *t.shape[-2:])

    def wide(t):
        t = jnp.swapaxes(split(t), -3, -2)
        return t.reshape(*t.shape[:-2], NSA_GROUP * TQ)

    return wide(cmp_t), wide(sel_t), wide(win_t)


def _q_head(qt_ref, g, qg_ref):
    q = qt_ref[0, g * HEAD_DIM:(g + 1) * HEAD_DIM, :]
    inv = lax.rsqrt(jnp.mean(q * q, axis=0, keepdims=True) + EPS)
    return (q * inv * qg_ref[...]).astype(BF16)


def _nsa_cmp_kernel(qt_ref, qg_ref, kc_ref, vct_ref, bias_ref, ov_ref, oc_ref, sel_ref, imp_ref, *, n_cmp, n_sel):
    qi = pl.program_id(1)
    qt = TQ // CMP_STRIDE
    gw = NSA_GROUP * HEAD_DIM
    kvs = range(NSA_KV_HEADS)
    row0 = pl.multiple_of((n_cmp - qt) - qi * qt, qt)
    qw = [jnp.concatenate([_q_head(qt_ref, kv * NSA_GROUP + g, qg_ref) for g in range(NSA_GROUP)], axis=1)
          for kv in kvs]

    def attend(n):
        for kv in kvs:
            s = (jnp.dot(kc_ref[0, kv, 0:n, :], qw[kv], preferred_element_type=F32)
                 + bias_ref[kv, pl.ds(row0, n), :])
            m = jnp.max(s, axis=0, keepdims=True)
            m = jnp.where(m < 0.5 * NEG, 0.0, m)
            p = jnp.exp2(s - m)
            p = p * (1.0 / jnp.maximum(jnp.sum(p, axis=0, keepdims=True), 1e-30))
            oc = jnp.dot(vct_ref[0, kv, :, 0:n], p.astype(BF16), preferred_element_type=F32)
            psum = jnp.zeros((n, TQ), F32)
            for g in range(NSA_GROUP):
                oc_ref[0, kv * gw + g * HEAD_DIM:kv * gw + (g + 1) * HEAD_DIM, :] = oc[:, g * TQ:(g + 1) * TQ]
                psum = psum + p[:, g * TQ:(g + 1) * TQ]
            hi = psum.astype(BF16)
            lo = (psum - hi.astype(F32)).astype(BF16)
            imp_ref[kv] = (jnp.dot(ov_ref[:, 0:n], hi, preferred_element_type=F32)
                           + jnp.dot(ov_ref[:, 0:n], lo, preferred_element_type=F32))

    chunk = min(n_cmp, 128)
    n_chunks = n_cmp // chunk
    need = lax.div((qi + 1) * qt + (chunk - 1), chunk)
    for c in range(1, n_chunks + 1):
        pl.when(need == c)(functools.partial(attend, c * chunk))

    blk = lax.broadcasted_iota(jnp.int32, (n_sel, TQ), 0)
    blk_t = lax.shift_right_logical(qi * TQ + lax.broadcasted_iota(jnp.int32, (n_sel, TQ), 1), L_SEL.bit_length() - 1)
    forced = (blk == 0) | (blk > blk_t - N_LOCAL)
    v0 = tuple(jnp.where(blk > blk_t, -jnp.inf, jnp.where(forced, SEL_FORCE, imp_ref[kv])) for kv in kvs)

    def pick(_, vs):
        out = []
        for v in vs:
            m = jnp.max(v, axis=0, keepdims=True)
            first = jnp.min(jnp.where(v == m, blk, n_sel), axis=0, keepdims=True)
            out.append(jnp.where(blk == first, -jnp.inf, v))
        return tuple(out)

    vs = lax.fori_loop(0, min(N_SELECT, n_sel), pick, v0)
    for kv in kvs:
        sel_ref[0, kv] = jnp.where((vs[kv] == -jnp.inf) & (v0[kv] > -jnp.inf), 0.0, NEG)


def _nsa_cmp(zt, qg, kc, vct, bias_c, ovt):
    b, _, s = zt.shape
    n_cmp, n_sel = s // CMP_STRIDE, s // L_SEL
    kvh = NSA_KV_HEADS
    kern = functools.partial(_nsa_cmp_kernel, n_cmp=n_cmp, n_sel=n_sel)
    return pl.pallas_call(
        kern,
        grid=(b, s // TQ),
        in_specs=[pl.BlockSpec((1, NSA_WIDTH, TQ), lambda i, j: (i, 0, j)),
                  pl.BlockSpec((HEAD_DIM, TQ), lambda i, j: (0, 0)),
                  pl.BlockSpec((1, kvh, n_cmp, HEAD_DIM), lambda i, j: (i, 0, 0, 0)),
                  pl.BlockSpec((1, kvh, HEAD_DIM, n_cmp), lambda i, j: (i, 0, 0, 0)),
                  pl.BlockSpec((kvh, bias_c.shape[1], NSA_GROUP * TQ), lambda i, j: (0, 0, 0)),
                  pl.BlockSpec((n_sel, n_cmp), lambda i, j: (0, 0))],
        out_specs=[pl.BlockSpec((1, NSA_WIDTH, TQ), lambda i, j: (i, 0, j)),
                   pl.BlockSpec((1, kvh, n_sel, TQ), lambda i, j: (i, 0, 0, j))],
        out_shape=[jax.ShapeDtypeStruct((b, NSA_WIDTH, s), F32),
                   jax.ShapeDtypeStruct((b, kvh, n_sel, s), F32)],
        scratch_shapes=[pltpu.VMEM((kvh, n_sel, TQ), F32)],
        compiler_params=_cparams(("parallel", "parallel"), _VMEM_LIMIT),
        name="nsa_compressed_select",
    )(zt, qg, kc, vct, bias_c, ovt)


def _nsa_main_kernel(qt_ref, qg_ref, ks_ref, vst_ref, kw_ref, vwt_ref, sel_ref, bs_ref, bw_ref, gate_ref,
                     oc_ref, o_ref, acc_ref, s_ref):
    qi = pl.program_id(1)
    q0 = pl.multiple_of(qi * TQ, TQ)
    near = SEL_PAD + TQ
    gw = NSA_GROUP * HEAD_DIM
    kvs = range(NSA_KV_HEADS)

    def expand_sel(kv, first_blk, n_blk):
        rows = [jnp.broadcast_to(sel_ref[0, kv, pl.ds(first_blk + r, 1), :], (L_SEL, TQ)) for r in range(n_blk)]
        rows = jnp.concatenate(rows, axis=0)
        return jnp.concatenate([rows] * NSA_GROUP, axis=1)

    qw = [jnp.concatenate([_q_head(qt_ref, kv * NSA_GROUP + g, qg_ref) for g in range(NSA_GROUP)], axis=1)
          for kv in kvs]
    m0 = []
    for kv in kvs:
        s = (jnp.dot(ks_ref[0, kv, pl.ds(q0, near), :], qw[kv], preferred_element_type=F32) + bs_ref[kv]
             + expand_sel(kv, qi * (TQ // L_SEL), near // L_SEL))
        m = jnp.max(s, axis=0, keepdims=True)
        p = jnp.exp2(s - m).astype(BF16)
        acc_ref[kv] = jnp.dot(vst_ref[0, kv, :, pl.ds(q0, near)], p, preferred_element_type=F32)
        m0.append(m)

    for kv in kvs:
        s = jnp.dot(kw_ref[0, kv, pl.ds(q0, WINDOW + TQ), :], qw[kv], preferred_element_type=F32) + bw_ref[0, kv]
        p = jnp.exp2(s - jnp.max(s, axis=0, keepdims=True)).astype(BF16)
        ow = jnp.dot(vwt_ref[0, kv, :, pl.ds(q0, WINDOW + TQ)], p, preferred_element_type=F32)
        ow = ow[:HEAD_DIM] * (1.0 / jnp.maximum(ow[HEAD_DIM:HEAD_DIM + 1], 1e-30))
        for g in range(NSA_GROUP):
            gates = jax.nn.sigmoid(gate_ref[0, kv, g * N_BRANCH:(g + 1) * N_BRANCH, :])
            rows = slice(kv * gw + g * HEAD_DIM, kv * gw + (g + 1) * HEAD_DIM)
            o_ref[0, rows, :] = gates[0:1] * oc_ref[0, rows, :] + gates[2:3] * ow[:, g * TQ:(g + 1) * TQ]

    def scores(c, slot):
        r0 = pl.multiple_of(c * TQ, TQ)
        mc = []
        for kv in kvs:
            s = (jnp.dot(ks_ref[0, kv, pl.ds(r0, TQ), :], qw[kv], preferred_element_type=F32)
                 + expand_sel(kv, c * (TQ // L_SEL), TQ // L_SEL))
            s_ref[slot, kv] = s
            mc.append(jnp.max(s, axis=0, keepdims=True))
        return tuple(mc)

    def consume(c, slot, m_old, mc):
        r0 = pl.multiple_of(c * TQ, TQ)
        m_out = []
        for kv in kvs:
            m_new = jnp.maximum(m_old[kv], mc[kv])
            alpha = jnp.exp2(m_old[kv] - m_new)
            p = jnp.exp2((s_ref[slot, kv] - m_new).astype(BF16))
            acc_ref[kv] = alpha * acc_ref[kv] + jnp.dot(vst_ref[0, kv, :, pl.ds(r0, TQ)], p,
                                                        preferred_element_type=F32)
            m_out.append(m_new)
        return tuple(m_out)

    first = SEL_PAD // TQ
    n_far = qi - first

    def pair(i, carry):
        m, mc = carry
        c = first + 2 * i
        mc1 = scores(c + 1, 1)
        m = consume(c, 0, m, mc)
        mc2 = scores(c + 2, 0)
        m = consume(c + 1, 1, m, mc1)
        return m, mc2

    m_far, mc_far = lax.fori_loop(0, n_far // 2, pair, (tuple(m0), scores(first, 0)))

    @pl.when((n_far > 0) & (n_far % 2 == 1))
    def _():
        consume(first + n_far - 1, 0, m_far, mc_far)

    for kv in kvs:
        os = acc_ref[kv]
        os = os[:HEAD_DIM] * (1.0 / jnp.maximum(os[HEAD_DIM:HEAD_DIM + 1], 1e-30))
        for g in range(NSA_GROUP):
            gate = jax.nn.sigmoid(gate_ref[0, kv, g * N_BRANCH + 1:g * N_BRANCH + 2, :])
            rows = slice(kv * gw + g * HEAD_DIM, kv * gw + (g + 1) * HEAD_DIM)
            o_ref[0, rows, :] = o_ref[0, rows, :] + gate * os[:, g * TQ:(g + 1) * TQ]


def _nsa_main(zt, qg, ks_p, vst_p, kw_p, vwt_p, sel_p, bias_s, bias_w, gates_t, oc_t):
    b, _, s = zt.shape
    kvh = NSA_KV_HEADS
    sp, wp = ks_p.shape[2], kw_p.shape[2]
    nb, vr = sel_p.shape[2], vst_p.shape[2]
    once = pl.Buffered(1)
    return pl.pallas_call(
        _nsa_main_kernel,
        grid=(b, s // TQ),
        in_specs=[pl.BlockSpec((1, NSA_WIDTH, TQ), lambda i, j: (i, 0, j)),
                  pl.BlockSpec((HEAD_DIM, TQ), lambda i, j: (0, 0)),
                  pl.BlockSpec((1, kvh, sp, HEAD_DIM), lambda i, j: (i, 0, 0, 0), pipeline_mode=once),
                  pl.BlockSpec((1, kvh, vr, sp), lambda i, j: (i, 0, 0, 0), pipeline_mode=once),
                  pl.BlockSpec((1, kvh, wp, HEAD_DIM), lambda i, j: (i, 0, 0, 0), pipeline_mode=once),
                  pl.BlockSpec((1, kvh, vr, wp), lambda i, j: (i, 0, 0, 0), pipeline_mode=once),
                  pl.BlockSpec((1, kvh, nb, TQ), lambda i, j: (i, 0, 0, j)),
                  pl.BlockSpec((kvh, SEL_PAD + TQ, NSA_GROUP * TQ), lambda i, j: (0, 0, 0), pipeline_mode=once),
                  pl.BlockSpec((1, kvh, WINDOW + TQ, NSA_GROUP * TQ), lambda i, j: (jnp.minimum(j, 2), 0, 0, 0)),
                  pl.BlockSpec((1, kvh, 16, TQ), lambda i, j: (i, 0, 0, j)),
                  pl.BlockSpec((1, NSA_WIDTH, TQ), lambda i, j: (i, 0, j))],
        out_specs=pl.BlockSpec((1, NSA_WIDTH, TQ), lambda i, j: (i, 0, j)),
        out_shape=jax.ShapeDtypeStruct((b, NSA_WIDTH, s), F32),
        scratch_shapes=[pltpu.VMEM((kvh, vr, NSA_GROUP * TQ), F32),
                        pltpu.VMEM((2, kvh, TQ, NSA_GROUP * TQ), F32)],
        compiler_params=_cparams(("parallel", "arbitrary"), _VMEM_LIMIT),
        name="nsa_selected_window",
    )(zt, qg, ks_p, vst_p, kw_p, vwt_p, sel_p, bias_s, bias_w, gates_t, oc_t)


def _mixout_kernel(gy_ref, yc_ref, yn_ref, h_ref, wglu_ref, go_ref, wo_ref, o_ref):
    sw = SSM_WIDTH
    half = sw // 2
    ag = (jnp.dot(gy_ref[0, 0].astype(BF16), wglu_ref[0:half, :], preferred_element_type=F32)
          + jnp.dot(gy_ref[1, 0].astype(BF16), wglu_ref[half:, :], preferred_element_type=F32))
    ys = ag[:, :sw] * jax.nn.sigmoid(ag[:, sw:])
    ys = (_rms_rows(ys) * go_ref[:, 0:sw]).astype(BF16)
    yn = (_rms_rows(yn_ref[0]) * go_ref[:, 2 * sw:]).astype(BF16)
    out = (jnp.dot(ys, wo_ref[0:sw, :], preferred_element_type=F32)
           + jnp.dot(yc_ref[0], wo_ref[sw:2 * sw, :], preferred_element_type=F32)
           + jnp.dot(yn, wo_ref[2 * sw:, :], preferred_element_type=F32))
    o_ref[0] = h_ref[0] + out


def _mixout(gy, yc, yn, h, w_glu, g_out, w_out, tm=1024):
    b, s, d = h.shape
    tok = lambda w: pl.BlockSpec((1, tm, w), lambda i, j: (i, j, 0))
    return pl.pallas_call(
        _mixout_kernel,
        grid=(b, s // tm),
        in_specs=[pl.BlockSpec((2, 1, tm, SSM_WIDTH // 2), lambda i, j: (0, i, j, 0)),
                  tok(CONV_WIDTH), tok(NSA_WIDTH), tok(d),
                  pl.BlockSpec((SSM_WIDTH, 2 * SSM_WIDTH), lambda i, j: (0, 0)),
                  pl.BlockSpec((1, d), lambda i, j: (0, 0)),
                  pl.BlockSpec((d, d), lambda i, j: (0, 0))],
        out_specs=tok(d),
        out_shape=jax.ShapeDtypeStruct((b, s, d), F32),
        compiler_params=_cparams(("parallel", "parallel")),
        name="mix_out",
    )(gy, yc, yn, h, w_glu.astype(BF16), g_out.reshape(1, d), w_out.astype(BF16))


def _memkv_kernel(mem_ref, g_ref, w_ref, kg_ref, k_ref, v_ref):
    mn = (_rms_rows(mem_ref[0]) * g_ref[...]).astype(BF16)
    kv = jnp.dot(mn, w_ref[...], preferred_element_type=F32)
    for h in range(X_HEADS):
        cols = slice(h * HEAD_DIM, (h + 1) * HEAD_DIM)
        k_ref[0, :, cols] = (_rms_rows(kv[:, cols]) * kg_ref[...]).astype(k_ref.dtype)
    v_ref[0] = kv[:, X_WIDTH:].astype(v_ref.dtype)


def _memkv(mem, gain, w_kv, k_gain):
    b, m, d = mem.shape
    out = jax.ShapeDtypeStruct((b, m, X_WIDTH), BF16)
    return pl.pallas_call(
        _memkv_kernel,
        grid=(b,),
        in_specs=[pl.BlockSpec((1, m, d), lambda i: (i, 0, 0)),
                  pl.BlockSpec((1, d), lambda i: (0, 0)),
                  pl.BlockSpec((d, 2 * X_WIDTH), lambda i: (0, 0)),
                  pl.BlockSpec((1, HEAD_DIM), lambda i: (0, 0))],
        out_specs=[pl.BlockSpec((1, m, X_WIDTH), lambda i: (i, 0, 0))] * 2,
        out_shape=[out, out],
        compiler_params=_cparams(("parallel",)),
        name="cross_mem_kv",
    )(mem, gain.reshape(1, d), w_kv.astype(BF16), k_gain.reshape(1, HEAD_DIM))


def _cross_kernel(h_ref, g_ref, wq_ref, qg_ref, k_ref, v_ref, wo_ref, o_ref):
    h = h_ref[0]
    hn = (_rms_rows(h) * g_ref[...]).astype(BF16)
    q = jnp.dot(hn, wq_ref[...], preferred_element_type=F32)
    out = h
    for hd in range(X_HEADS):
        cols = slice(hd * HEAD_DIM, (hd + 1) * HEAD_DIM)
        qh = (_rms_rows(q[:, cols]) * qg_ref[...]).astype(BF16)
        s = lax.dot_general(qh, k_ref[0, :, cols], (((1,), (1,)), ((), ())), preferred_element_type=F32)
        p = jnp.exp(s - jnp.max(s, axis=-1, keepdims=True))
        p = p * (1.0 / jnp.sum(p, axis=-1, keepdims=True))
        o = jnp.dot(p.astype(BF16), v_ref[0, :, cols], preferred_element_type=F32)
        out = out + jnp.dot(o.astype(BF16), wo_ref[cols, :], preferred_element_type=F32)
    o_ref[0] = out


def _cross(h, gain, w_q, q_gain, k, v, w_o, tm=1024):
    b, s, d = h.shape
    m = k.shape[1]
    return pl.pallas_call(
        _cross_kernel,
        grid=(b, s // tm),
        in_specs=[pl.BlockSpec((1, tm, d), lambda i, j: (i, j, 0)),
                  pl.BlockSpec((1, d), lambda i, j: (0, 0)),
                  pl.BlockSpec((d, X_WIDTH), lambda i, j: (0, 0)),
                  pl.BlockSpec((1, HEAD_DIM), lambda i, j: (0, 0)),
                  pl.BlockSpec((1, m, X_WIDTH), lambda i, j: (i, 0, 0)),
                  pl.BlockSpec((1, m, X_WIDTH), lambda i, j: (i, 0, 0)),
                  pl.BlockSpec((X_WIDTH, d), lambda i, j: (0, 0))],
        out_specs=pl.BlockSpec((1, tm, d), lambda i, j: (i, j, 0)),
        out_shape=jax.ShapeDtypeStruct((b, s, d), F32),
        compiler_params=_cparams(("parallel", "parallel")),
        name="cross_attention",
    )(h, gain.reshape(1, d), w_q.astype(BF16), (q_gain * HEAD_DIM ** -0.5).reshape(1, HEAD_DIM), k, v,
      w_o.astype(BF16))


def _ffn_kernel(h_ref, g_ref, wg_ref, wv_ref, wd_ref, o_ref, xn_ref, acc_ref):
    f = pl.program_id(1)

    @pl.when(f == 0)
    def _():
        xn_ref[...] = (_rms_rows(h_ref[...]) * g_ref[...]).astype(BF16)
        acc_ref[...] = jnp.zeros_like(acc_ref)

    x = xn_ref[...]
    gate = jnp.dot(x, wg_ref[...], preferred_element_type=F32)
    val = jnp.dot(x, wv_ref[...], preferred_element_type=F32)
    act = (jax.nn.silu(gate) * val).astype(BF16)
    acc_ref[...] += jnp.dot(act, wd_ref[...], preferred_element_type=F32)

    @pl.when(f == pl.num_programs(1) - 1)
    def _():
        o_ref[...] = h_ref[...] + acc_ref[...]


def _ffn(h2d, gain, w_up, w_down, tm=2048):
    t, d = h2d.shape
    nf = D_FF // FF_CHUNK
    wb = w_up.astype(BF16)
    return pl.pallas_call(
        _ffn_kernel,
        grid=(t // tm, nf),
        in_specs=[pl.BlockSpec((tm, d), lambda i, f: (i, 0)),
                  pl.BlockSpec((1, d), lambda i, f: (0, 0)),
                  pl.BlockSpec((d, FF_CHUNK), lambda i, f: (0, f)),
                  pl.BlockSpec((d, FF_CHUNK), lambda i, f: (0, f + nf)),
                  pl.BlockSpec((FF_CHUNK, d), lambda i, f: (f, 0))],
        out_specs=pl.BlockSpec((tm, d), lambda i, f: (i, 0)),
        out_shape=jax.ShapeDtypeStruct((t, d), F32),
        scratch_shapes=[pltpu.VMEM((tm, d), BF16), pltpu.VMEM((tm, d), F32)],
        compiler_params=_cparams(("parallel", "arbitrary"), _VMEM_LIMIT),
        name="ffn_swiglu",
    )(h2d, gain.reshape(1, d), wb, wb, w_down.astype(BF16))


def _router_kernel(h_ref, g_ref, wr_ref, xn_ref, gate_ref, asg_ref):
    xn = _rms_rows(h_ref[...]) * g_ref[...]
    xn_ref[...] = xn.astype(BF16)
    logits = jnp.dot(xn, wr_ref[...], precision=lax.Precision.HIGHEST, preferred_element_type=F32)
    lane = lax.broadcasted_iota(jnp.int32, logits.shape, 1)
    lg = jnp.where(lane < N_EXPERTS, logits, -jnp.inf)
    m1 = jnp.max(lg, axis=-1, keepdims=True)
    i1 = jnp.min(jnp.where(lg == m1, lane, 128), axis=-1, keepdims=True)
    lg2 = jnp.where(lane == i1, -jnp.inf, lg)
    m2 = jnp.max(lg2, axis=-1, keepdims=True)
    i2 = jnp.min(jnp.where(lg2 == m2, lane, 128), axis=-1, keepdims=True)
    e = jnp.exp(m2 - m1)
    den = 1.0 + e
    gate_ref[...] = jnp.where(lane == i1, 1.0 / den, jnp.where(lane == i2, e / den, 0.0))
    asg_ref[...] = ((lane == i1) | (lane == i2)).astype(jnp.int32)


def _router(h2d, gain, w_router, tm=512):
    t, d = h2d.shape
    wr = jnp.concatenate([w_router, jnp.zeros((d, 128 - N_EXPERTS), F32)], axis=1)
    return pl.pallas_call(
        _router_kernel,
        grid=(t // tm,),
        in_specs=[pl.BlockSpec((tm, d), lambda i: (i, 0)),
                  pl.BlockSpec((1, d), lambda i: (0, 0)),
                  pl.BlockSpec((d, 128), lambda i: (0, 0))],
        out_specs=[pl.BlockSpec((tm, d), lambda i: (i, 0)),
                   pl.BlockSpec((tm, 128), lambda i: (i, 0)),
                   pl.BlockSpec((tm, 128), lambda i: (i, 0))],
        out_shape=[jax.ShapeDtypeStruct((t, d), BF16), jax.ShapeDtypeStruct((t, 128), F32),
                   jax.ShapeDtypeStruct((t, 128), jnp.int32)],
        compiler_params=_cparams(("parallel",)),
        name="moe_router",
    )(h2d, gain.reshape(1, d), wr)


def _moe_windows(rb, lo, hi, active):
    lo_l = jnp.clip(lo - rb * MOE_TB, 0, MOE_TB)
    hi_l = jnp.clip(hi - rb * MOE_TB, 0, MOE_TB)
    shift = MOE_SUB.bit_length() - 1
    w0 = jnp.minimum(lax.shift_left(lax.shift_right_logical(lo_l, shift), shift), MOE_TB - MOE_WIN)
    has = active & (hi_l > lo_l)
    w1 = jnp.minimum(w0 + MOE_WIN, MOE_TB - MOE_WIN)
    return ((w0, 0, has), (w1, w0 + MOE_WIN, has & (hi_l > w0 + MOE_WIN)))


def _moe_gather_kernel(rb_ref, lo_ref, hi_ref, first_ref, tgt_ref, x_ref, o_ref):
    e, j, slot = pl.program_id(0), pl.program_id(1), pl.program_id(2)
    rb = rb_ref[e, j, slot]

    @pl.when(first_ref[e, j, slot] == 1)
    def _():
        o_ref[...] = jnp.zeros_like(o_ref)

    active = (slot == 0) | (rb != rb_ref[e, j, 0])
    tgt = tgt_ref[0]
    n_sub = MOE_TB // MOE_TS
    for sub in range(n_sub):
        toks = slice(sub * MOE_TS, (sub + 1) * MOE_TS)
        for start, cutoff, needed in _moe_windows(rb, lo_ref[e, j * n_sub + sub], hi_ref[e, j * n_sub + sub], active):

            @pl.when(needed)
            def _():
                local = start + lax.broadcasted_iota(jnp.int32, (MOE_WIN, MOE_TS), 0)
                rows = jnp.where(local >= cutoff, rb * MOE_TB + local, -2)
                onehot = jnp.where(tgt[:, toks] == rows, 1.0, 0.0).astype(BF16)
                part = jnp.dot(onehot, x_ref[toks, :], preferred_element_type=F32)
                sl = pl.ds(pl.multiple_of(start, MOE_SUB), MOE_WIN)
                o_ref[sl, :] = o_ref[sl, :] + part.astype(o_ref.dtype)


def _moe_ffn_kernel(exp_ref, nused_ref, x_ref, wg_ref, wv_ref, wd_ref, o_ref, acc_ref):
    r, f = pl.program_id(0), pl.program_id(1)
    used = r < nused_ref[0]

    @pl.when(f == 0)
    def _():
        acc_ref[...] = jnp.zeros_like(acc_ref)

    @pl.when(used)
    def _():
        x = x_ref[...]
        gate = jnp.dot(x, wg_ref[0], preferred_element_type=F32)
        val = jnp.dot(x, wv_ref[0], preferred_element_type=F32)
        act = (jax.nn.silu(gate) * val).astype(BF16)
        acc_ref[...] += jnp.dot(act, wd_ref[0], preferred_element_type=F32)

    @pl.when(f == pl.num_programs(1) - 1)
    def _():
        o_ref[...] = acc_ref[...].astype(o_ref.dtype)
    del exp_ref


def _moe_scatter_kernel(rb_ref, lo_ref, hi_ref, tgt_ref, gate_ref, y_ref, h_ref, o_ref):
    j, e, slot = pl.program_id(0), pl.program_id(1), pl.program_id(2)
    rb = rb_ref[e, j, slot]

    @pl.when((e == 0) & (slot == 0))
    def _():
        o_ref[...] = h_ref[...]

    active = (slot == 0) | (rb != rb_ref[e, j, 0])
    n_sub = MOE_TB // MOE_TS
    for sub in range(n_sub):
        toks = slice(sub * MOE_TS, (sub + 1) * MOE_TS)
        for start, cutoff, needed in _moe_windows(rb, lo_ref[e, j * n_sub + sub], hi_ref[e, j * n_sub + sub], active):

            @pl.when(needed)
            def _():
                mine = lax.broadcasted_iota(jnp.int32, (MOE_TS, N_EXPERTS), 1) == e
                tgt = jnp.sum(jnp.where(mine, tgt_ref[toks, :], 0), axis=1, keepdims=True)
                gate = jnp.sum(jnp.where(mine, gate_ref[toks, :], 0.0), axis=1, keepdims=True)
                local = start + lax.broadcasted_iota(jnp.int32, (MOE_TS, MOE_WIN), 1)
                rows = jnp.where(local >= cutoff, rb * MOE_TB + local, -2)
                onehot = jnp.where(tgt == rows, 1.0, 0.0).astype(BF16)
                y = y_ref[pl.ds(pl.multiple_of(start, MOE_SUB), MOE_WIN), :]
                o_ref[toks, :] = o_ref[toks, :] + gate * jnp.dot(onehot, y, preferred_element_type=F32)


def _moe(h2d, gain, w_router, w_up, w_down):
    t, d = h2d.shape
    tb = MOE_TB
    nj = t // tb
    n_rb = (t * TOP_K) // tb + N_EXPERTS
    xn, gates, asg = _router(h2d, gain, w_router)
    asg = asg[:, :N_EXPERTS]
    gates = gates[:, :N_EXPERTS]
    cs = jnp.cumsum(asg, axis=0)
    rank = cs - asg
    counts = cs[-1]
    padded = (counts + tb - 1) // tb * tb
    pad_end = jnp.cumsum(padded)
    start_p = pad_end - padded
    tgt = jnp.where(asg == 1, start_p[None, :] + rank, -1).astype(jnp.int32)
    ts = MOE_TS
    cb = jnp.concatenate([jnp.zeros((1, N_EXPERTS), jnp.int32), cs[ts - 1::ts]], axis=0)
    lo = (start_p[None, :] + cb[:-1]).T.astype(jnp.int32)
    hi = (start_p[None, :] + cb[1:]).T.astype(jnp.int32)
    rb0 = lo[:, ::tb // ts] // tb
    rb1 = jnp.maximum(rb0, (hi[:, tb // ts - 1::tb // ts] - 1) // tb)
    rb = jnp.stack([rb0, rb1], axis=-1).astype(jnp.int32)
    flat = rb.reshape(-1)
    first = jnp.concatenate([jnp.ones((1,), jnp.int32), (flat[1:] != flat[:-1]).astype(jnp.int32)])
    first = first.reshape(N_EXPERTS, nj, 2)
    n_used = (pad_end[-1] // tb).astype(jnp.int32).reshape(1)
    blk_exp = jnp.minimum(jnp.searchsorted(pad_end, jnp.arange(n_rb) * tb, side='right'),
                          N_EXPERTS - 1).astype(jnp.int32)

    xs = pl.pallas_call(
        _moe_gather_kernel,
        grid_spec=pltpu.PrefetchScalarGridSpec(
            num_scalar_prefetch=4,
            grid=(N_EXPERTS, nj, 2),
            in_specs=[pl.BlockSpec((1, 1, tb), lambda e, j, s, *_: (e, 0, j)),
                      pl.BlockSpec((tb, d), lambda e, j, s, *_: (j, 0))],
            out_specs=pl.BlockSpec((tb, d), lambda e, j, s, rb_ref, *_: (rb_ref[e, j, s], 0))),
        out_shape=jax.ShapeDtypeStruct((n_rb * tb, d), BF16),
        compiler_params=_cparams(("arbitrary", "arbitrary", "arbitrary"), _VMEM_LIMIT),
        name="moe_gather",
    )(rb, lo, hi, first, tgt.T.reshape(N_EXPERTS, 1, t), xn)

    nf = D_FF // FF_CHUNK
    wub = w_up.astype(BF16)
    ys = pl.pallas_call(
        _moe_ffn_kernel,
        grid_spec=pltpu.PrefetchScalarGridSpec(
            num_scalar_prefetch=2,
            grid=(n_rb, nf),
            in_specs=[pl.BlockSpec((tb, d), lambda r, f, *_: (r, 0)),
                      pl.BlockSpec((1, d, FF_CHUNK), lambda r, f, ex, nu: (ex[r], 0, f)),
                      pl.BlockSpec((1, d, FF_CHUNK), lambda r, f, ex, nu: (ex[r], 0, f + nf)),
                      pl.BlockSpec((1, FF_CHUNK, d), lambda r, f, ex, nu: (ex[r], f, 0))],
            out_specs=pl.BlockSpec((tb, d), lambda r, f, *_: (r, 0)),
            scratch_shapes=[pltpu.VMEM((tb, d), F32)]),
        out_shape=jax.ShapeDtypeStruct((n_rb * tb, d), BF16),
        compiler_params=_cparams(("arbitrary", "arbitrary"), _VMEM_LIMIT),
        name="moe_expert_ffn",
    )(blk_exp, n_used, xs, wub, wub, w_down.astype(BF16))

    return pl.pallas_call(
        _moe_scatter_kernel,
        grid_spec=pltpu.PrefetchScalarGridSpec(
            num_scalar_prefetch=3,
            grid=(nj, N_EXPERTS, 2),
            in_specs=[pl.BlockSpec((tb, N_EXPERTS), lambda j, e, s, *_: (j, 0)),
                      pl.BlockSpec((tb, N_EXPERTS), lambda j, e, s, *_: (j, 0)),
                      pl.BlockSpec((tb, d), lambda j, e, s, rb_ref, *_: (rb_ref[e, j, s], 0)),
                      pl.BlockSpec((tb, d), lambda j, e, s, *_: (j, 0))],
            out_specs=pl.BlockSpec((tb, d), lambda j, e, s, *_: (j, 0))),
        out_shape=jax.ShapeDtypeStruct((t, d), F32),
        compiler_params=_cparams(("arbitrary", "arbitrary", "arbitrary"), _VMEM_LIMIT),
        name="moe_scatter",
    )(rb, lo, hi, tgt, gates, ys, h2d)


_COL_CONV, _COL_SSM, _COL_KC, _COL_VC, _COL_KS, _COL_KW = 0, 512, 768, 896, 1024, 1152
_ROW_Q, _ROW_VS, _ROW_VW, _ROW_G = 0, 512, 640, 768


def _split_w_in(w_in):
    kvw = NSA_KV_HEADS * HEAD_DIM
    cuts = np.cumsum([0, SSM_WIDTH, 2 * CONV_WIDTH, NSA_WIDTH] + [kvw] * 6 + [N_BRANCH * NSA_HEADS])
    seg = lambda i: w_in[:, cuts[i]:cuts[i + 1]]
    ssm, conv, q, k_c, v_c, k_s, v_s, k_w, v_w, gate = (seg(i) for i in range(10))
    w_tok = jnp.concatenate([conv, ssm, k_c, v_c, k_s, k_w], axis=1).astype(BF16)
    gate = jnp.concatenate([gate, jnp.zeros((w_in.shape[0], 8), F32)], axis=1)
    w_t = jnp.concatenate([q, v_s, v_w, gate], axis=1).T.astype(BF16)
    return w_tok, w_t


def _layer_mixers(h, p, t5_tiles, s5_perm):
    b, s, d = h.shape
    w_tok, w_t = _split_w_in(p['w_in'])
    k_norm = p['nsa_k_norm']
    ztok, zt = _proj(h, p['norm_mix'], w_tok, w_t)
    ks, kw = _knorm(ztok, _COL_KS, _COL_KW, k_norm[1], k_norm[2])

    tables = _s5_tables(p['ssm_lambda_re'], p['ssm_lambda_im'], p['ssm_log_dt'], p['ssm_b_re'], p['ssm_b_im'],
                        p['ssm_c_re'], p['ssm_c_im'], p['ssm_d'])
    gy = _s5_unpack(_s5(_s5_pack(ztok, _COL_SSM, s5_perm[0]), tables, b), s5_perm[1], b)

    g_out = p['mix_out_norm']
    yc = _conv(ztok, p['conv_w_dw'], p['conv_b_dw'], p['conv_ln_g'], p['conv_ln_b'], p['conv_w_pw'],
               g_out[SSM_WIDTH:SSM_WIDTH + CONV_WIDTH])

    kvw = NSA_KV_HEADS * HEAD_DIM
    n_cmp = s // CMP_STRIDE
    kc, vct = _compress(ztok, _COL_KC, _COL_VC, p['nsa_cmp_pe'], p['nsa_cmp_w1'], p['nsa_cmp_w2'], k_norm[0])
    qg = jnp.broadcast_to((p['nsa_q_norm'] * (HEAD_DIM ** -0.5 * LOG2E))[:, None], (HEAD_DIM, TQ))
    bias_c, bias_s, bias_w = t5_tiles
    n_sel = s // L_SEL
    cs_ = np.arange(n_cmp) * CMP_STRIDE
    ss_ = np.arange(n_sel) * L_SEL
    ov = np.maximum(np.minimum(cs_[:, None] + L_CMP, ss_[None, :] + L_SEL) - np.maximum(cs_[:, None], ss_[None, :]), 0)
    ovt = jnp.asarray((ov.astype(np.float32) / L_CMP).T, BF16)
    oc_t, sel = _nsa_cmp(zt, qg, kc, vct, bias_c, ovt)

    front = lambda x, n, axis: jnp.pad(x, [(n, 0) if a == axis else (0, 0) for a in range(x.ndim)])
    ones_rows = jnp.concatenate([jnp.ones((b, NSA_KV_HEADS, 1, s), BF16),
                                 jnp.zeros((b, NSA_KV_HEADS, 15, s), BF16)], axis=2)
    heads_t = lambda rows: jnp.concatenate(
        [zt[:, rows:rows + kvw, :].astype(BF16).reshape(b, NSA_KV_HEADS, HEAD_DIM, s), ones_rows], axis=2)
    ks_p = front(ks, SEL_PAD, 2)
    kw_p = front(kw, WINDOW, 2)
    vst_p = front(heads_t(_ROW_VS), SEL_PAD, 3)
    vwt_p = front(heads_t(_ROW_VW), WINDOW, 3)
    sel_p = jnp.pad(sel, ((0, 0), (0, 0), (SEL_PAD // L_SEL, 0), (0, 0)), constant_values=NEG)
    gl = zt[:, _ROW_G:_ROW_G + N_BRANCH * NSA_HEADS, :].reshape(b, NSA_KV_HEADS, NSA_GROUP * N_BRANCH, s)
    gates_t = jnp.pad(gl, ((0, 0), (0, 0), (0, 16 - NSA_GROUP * N_BRANCH), (0, 0)))
    yn_t = _nsa_main(zt, qg, ks_p, vst_p, kw_p, vwt_p, sel_p, bias_s, bias_w, gates_t, oc_t)
    yn = yn_t.transpose(0, 2, 1)

    return _mixout(gy, yc, yn, h, p['ssm_w_glu'], g_out, p['w_out'])


def kernel(x, mem, norm_mix, w_in, ssm_lambda_re, ssm_lambda_im, ssm_log_dt, ssm_b_re, ssm_b_im, ssm_c_re, ssm_c_im, ssm_d, ssm_w_glu, conv_w_dw, conv_b_dw, conv_ln_g, conv_ln_b, conv_w_pw, nsa_q_norm, nsa_k_norm, nsa_cmp_pe, nsa_cmp_w1, nsa_cmp_w2, mix_out_norm, w_out, t5_table, norm_cross, norm_mem, x_w_q, x_w_kv, x_q_norm, x_k_norm, x_w_o, norm_ffn, ffn_w_up, ffn_w_down, moe_router, moe_w_up, moe_w_down):
    b, s, d = x.shape
    depth = w_in.shape[0]
    per_layer = dict(norm_mix=norm_mix, w_in=w_in, ssm_lambda_re=ssm_lambda_re, ssm_lambda_im=ssm_lambda_im,
                     ssm_log_dt=ssm_log_dt, ssm_b_re=ssm_b_re, ssm_b_im=ssm_b_im, ssm_c_re=ssm_c_re,
                     ssm_c_im=ssm_c_im, ssm_d=ssm_d, ssm_w_glu=ssm_w_glu, conv_w_dw=conv_w_dw,
                     conv_b_dw=conv_b_dw, conv_ln_g=conv_ln_g, conv_ln_b=conv_ln_b, conv_w_pw=conv_w_pw,
                     nsa_q_norm=nsa_q_norm, nsa_k_norm=nsa_k_norm, nsa_cmp_pe=nsa_cmp_pe, nsa_cmp_w1=nsa_cmp_w1,
                     nsa_cmp_w2=nsa_cmp_w2, mix_out_norm=mix_out_norm, w_out=w_out)
    t5_tiles = _nsa_bias_tiles(t5_table, s)
    s5_perm = _s5_perm_tables()
    h = x
    for layer in range(depth):
        p = {k: v[layer] for k, v in per_layer.items()}
        h = _layer_mixers(h, p, t5_tiles, s5_perm)
        mk, mv = _memkv(mem, norm_mem[layer], x_w_kv[layer], x_k_norm[layer])
        h = _cross(h, norm_cross[layer], x_w_q[layer], x_q_norm[layer], mk, mv, x_w_o[layer])
        h2d = h.reshape(b * s, d)
        if layer % 2 == 0:
            h2d = _ffn(h2d, norm_ffn[layer], ffn_w_up[layer // 2], ffn_w_down[layer // 2])
        else:
            h2d = _moe(h2d, norm_ffn[layer], moe_router[layer // 2], moe_w_up[layer // 2], moe_w_down[layer // 2])
        h = h2d.reshape(b, s, d)
    return h
```

```python
import functools
import math

import jax
import jax.numpy as jnp
import numpy as np
from jax import lax
from jax.experimental import pallas as pl
from jax.experimental.pallas import tpu as pltpu

F32 = jnp.float32
BF16 = jnp.bfloat16

D_MODEL = 1024
HEAD_DIM = 64
SSM_WIDTH = 256
SSM_GROUP = 16
SSM_GROUPS = 16
SSM_STATE = 64
SSM_CHUNK = 16
CONV_WIDTH = 256
CONV_K = 31
CONV_HALO = 32
NSA_WIDTH = 512
NSA_HEADS = 8
NSA_KV_HEADS = 2
NSA_GROUP = 4
N_BRANCH = 3
L_CMP = 32
CMP_STRIDE = 16
L_SEL = 64
N_SELECT = 16
N_LOCAL = 2
WINDOW = 512
SEL_FORCE = 1e6
T5_BUCKETS = 32
T5_MAX_DIST = 128
X_HEADS = 4
X_WIDTH = 256
D_FF = 2816
N_EXPERTS = 8
TOP_K = 2
EPS = 1e-6
NEG = -1e30
LOG2E = math.log2(math.e)

TQ = 256
SEL_PAD = 256
FF_CHUNK = 256
MOE_TB = 2048
MOE_TS = 256
MOE_SUB = 64
MOE_WIN = 192

_VMEM_LIMIT = 56 * 1024 * 1024


def _cparams(sem, vmem=None):
    return pltpu.CompilerParams(dimension_semantics=sem, vmem_limit_bytes=vmem)


def _rms_rows(x):
    return x * lax.rsqrt(jnp.mean(x * x, axis=-1, keepdims=True) + EPS)


def _proj_kernel(x_ref, g_ref, wtok_ref, wt_ref, ztok_ref, zt_ref):
    xn = (_rms_rows(x_ref[0]) * g_ref[...]).astype(BF16)
    ztok_ref[0] = jnp.dot(xn, wtok_ref[...], preferred_element_type=F32)
    zt_ref[0] = lax.dot_general(wt_ref[...], xn, (((1,), (1,)), ((), ())), preferred_element_type=F32)


def _proj(h, gain, w_tok, w_t, tm=1024):
    b, s, d = h.shape
    ntok, nt = w_tok.shape[1], w_t.shape[0]
    return pl.pallas_call(
        _proj_kernel,
        grid=(b, s // tm),
        in_specs=[pl.BlockSpec((1, tm, d), lambda i, j: (i, j, 0)),
                  pl.BlockSpec((1, d), lambda i, j: (0, 0)),
                  pl.BlockSpec((d, ntok), lambda i, j: (0, 0)),
                  pl.BlockSpec((nt, d), lambda i, j: (0, 0))],
        out_specs=[pl.BlockSpec((1, tm, ntok), lambda i, j: (i, j, 0)),
                   pl.BlockSpec((1, nt, tm), lambda i, j: (i, 0, j))],
        out_shape=[jax.ShapeDtypeStruct((b, s, ntok), F32), jax.ShapeDtypeStruct((b, nt, s), F32)],
        compiler_params=_cparams(("parallel", "parallel"), _VMEM_LIMIT),
        name="proj",
    )(h, gain.reshape(1, d), w_tok, w_t)


def _s5_tables(lam_re, lam_im, log_dt, b_re, b_im, c_re, c_im, d_skip):
    L, H, P = SSM_CHUNK, SSM_GROUP, SSM_STATE
    dt = jnp.exp(log_dt.astype(F32))[:, None]
    lr, li = lam_re.astype(F32), lam_im.astype(F32)
    mag = jnp.exp(lr * dt)
    ar, ai = mag * jnp.cos(li * dt), mag * jnp.sin(li * dt)
    den = lr * lr + li * li
    fr = ((ar - 1.0) * lr + ai * li) / den
    fi = (ai * lr - (ar - 1.0) * li) / den
    bbr = fr[..., None] * b_re - fi[..., None] * b_im
    bbi = fr[..., None] * b_im + fi[..., None] * b_re
    j = jnp.arange(L + 1, dtype=F32)[:, None, None]
    pmag = jnp.exp(lr[None] * dt[None] * j)
    pr, pi = pmag * jnp.cos(li[None] * dt[None] * j), pmag * jnp.sin(li[None] * dt[None] * j)
    cbr = c_re[:, :, :, None] * bbr[:, None, :, :] - c_im[:, :, :, None] * bbi[:, None, :, :]
    cbi = c_re[:, :, :, None] * bbi[:, None, :, :] + c_im[:, :, :, None] * bbr[:, None, :, :]
    hp = lax.Precision.HIGHEST
    kj = (jnp.einsum('jgp,ghpk->jghk', pr[:L], cbr, precision=hp)
          - jnp.einsum('jgp,ghpk->jghk', pi[:L], cbi, precision=hp))
    lag = np.arange(L)[None, :] - np.arange(L)[:, None]
    place = (lag[None] == np.arange(L)[:, None, None]).astype(np.float32)
    kt = jnp.einsum('jab,jghk->abghk', place, kj, precision=hp)
    kt = kt + (jnp.eye(L)[:, :, None, None, None] * (jnp.eye(H)[None, None, None] * d_skip[None, None, :, :, None]))
    tmat = kt.transpose(2, 0, 4, 1, 3).reshape(SSM_GROUPS, L * H, L * H)
    qr, qi = pr[:L][::-1], pi[:L][::-1]
    wre = qr[..., None] * bbr[None] - qi[..., None] * bbi[None]
    wim = qr[..., None] * bbi[None] + qi[..., None] * bbr[None]
    wre = wre.transpose(1, 0, 3, 2).reshape(SSM_GROUPS, L * H, P)
    wim = wim.transpose(1, 0, 3, 2).reshape(SSM_GROUPS, L * H, P)
    w1 = jnp.concatenate([wre, wim], axis=-1)
    w2 = jnp.concatenate([wim, wre], axis=-1)
    sr, si = pr[1:], pi[1:]
    vr = c_re[None] * sr[:, :, None, :] - c_im[None] * si[:, :, None, :]
    vi = c_re[None] * si[:, :, None, :] + c_im[None] * sr[:, :, None, :]
    vmat = jnp.concatenate([vr, -vi], axis=-1).transpose(1, 3, 0, 2).reshape(SSM_GROUPS, 2 * P, L * H)
    a_r, a_i = pr[L], pi[L]
    am = jnp.stack([jnp.concatenate([a_r, a_r], -1), jnp.concatenate([-a_i, a_i], -1),
                    jnp.concatenate([a_i, -a_i], -1)], axis=1)
    am = jnp.concatenate([am, jnp.zeros((SSM_GROUPS, 5, 2 * P), F32)], axis=1)
    return tmat.astype(BF16), w1.astype(BF16), w2.astype(BF16), vmat.astype(BF16), am


def _s5_perm_tables():
    L, G, H = SSM_CHUNK, SSM_GROUPS, SSM_GROUP
    i = jnp.arange(L * SSM_WIDTH)
    ti, gi, hi = i // SSM_WIDTH, (i // H) % G, i % H
    o = jnp.arange(L * H)
    pack = ((gi[None, :, None] == jnp.arange(G)[:, None, None]) & (ti[None, :, None] == (o // H)[None, None, :])
            & (hi[None, :, None] == (o % H)[None, None, :]))
    j = jnp.arange(G * L * H)
    gj, tj, hj = j // (L * H), (j // H) % L, j % H
    w = jnp.arange(SSM_WIDTH)
    unpack = ((tj[None, :, None] == jnp.arange(L)[:, None, None]) & (gj[None, :, None] == (w // H)[None, None, :])
              & (hj[None, :, None] == (w % H)[None, None, :]))
    return pack.astype(BF16), unpack.astype(BF16)


def _s5_pack_kernel(ulo_ref, uhi_ref, p_ref, x_ref, u2_ref, *, n_chunks):
    half = SSM_WIDTH // 2

    @pl.when(pl.program_id(1) == 0)
    def _():
        for t in range(SSM_CHUNK):
            for k, u_ref in enumerate((ulo_ref, uhi_ref)):
                u2_ref[:, t * SSM_WIDTH + k * half:t * SSM_WIDTH + (k + 1) * half] = (
                    u_ref[0, pl.ds(t, n_chunks, stride=SSM_CHUNK), :].astype(BF16))

    x_ref[0] = jnp.dot(u2_ref[...], p_ref[0], preferred_element_type=F32).astype(x_ref.dtype)


def _s5_pack(ztok, col, pack):
    b, s, _ = ztok.shape
    n_chunks = s // SSM_CHUNK
    lw, lh = SSM_CHUNK * SSM_WIDTH, SSM_CHUNK * SSM_GROUP
    half = SSM_WIDTH // 2
    return pl.pallas_call(
        functools.partial(_s5_pack_kernel, n_chunks=n_chunks),
        grid=(b, SSM_GROUPS),
        in_specs=[pl.BlockSpec((1, s, half), lambda i, g: (i, 0, col // half)),
                  pl.BlockSpec((1, s, half), lambda i, g: (i, 0, col // half + 1)),
                  pl.BlockSpec((1, lw, lh), lambda i, g: (g, 0, 0))],
        out_specs=pl.BlockSpec((1, n_chunks, lh), lambda i, g: (g, i, 0)),
        out_shape=jax.ShapeDtypeStruct((SSM_GROUPS, b * n_chunks, lh), BF16),
        scratch_shapes=[pltpu.VMEM((n_chunks, lw), BF16)],
        compiler_params=_cparams(("parallel", "arbitrary"), _VMEM_LIMIT),
        name="s5_pack",
    )(ztok, ztok, pack)


def _s5_unpack_kernel(g_ref, r_ref, o_ref, *, n_chunks):
    t = pl.program_id(1)
    rows = jnp.concatenate([g_ref[g] for g in range(SSM_GROUPS)], axis=1)
    y = jnp.dot(rows, r_ref[0], preferred_element_type=F32)
    half = SSM_WIDTH // 2
    for k in range(SSM_CHUNK):
        @pl.when(t == k)
        def _():
            for part in range(2):
                o_ref[part, 0, pl.ds(k, n_chunks, stride=SSM_CHUNK), :] = y[:, part * half:(part + 1) * half]


def _s5_unpack(gy, unpack, bsz):
    g, r, lh = gy.shape
    n_chunks = r // bsz
    half = SSM_WIDTH // 2
    return pl.pallas_call(
        functools.partial(_s5_unpack_kernel, n_chunks=n_chunks),
        grid=(bsz, SSM_CHUNK),
        in_specs=[pl.BlockSpec((g, n_chunks, lh), lambda i, t: (0, i, 0)),
                  pl.BlockSpec((1, g * lh, SSM_WIDTH), lambda i, t: (t, 0, 0))],
        out_specs=pl.BlockSpec((2, 1, n_chunks * SSM_CHUNK, half), lambda i, t: (0, i, 0, 0)),
        out_shape=jax.ShapeDtypeStruct((2, bsz, n_chunks * SSM_CHUNK, half), F32),
        compiler_params=_cparams(("parallel", "arbitrary"), _VMEM_LIMIT),
        name="s5_unpack",
    )(gy, unpack)


def _s5_kernel(x_ref, t_ref, w1_ref, w2_ref, v_ref, a_ref, o_ref, s1_ref, s2_ref, xin_ref, *, bsz, n_chunks):
    x = x_ref[0]
    s1_ref[...] = jnp.dot(x, w1_ref[0], preferred_element_type=F32)
    s2_ref[...] = jnp.dot(x, w2_ref[0], preferred_element_type=F32)
    a1, a2, a3 = a_ref[0, 0:1, :], a_ref[0, 1:2, :], a_ref[0, 2:3, :]

    def step(c, carry):
        ps, qs = carry
        new_p, new_q = [], []
        for bi in range(bsz):
            row = pl.ds(bi * n_chunks + c, 1)
            xin_ref[row, :] = ps[bi]
            new_p.append(ps[bi] * a1 + qs[bi] * a2 + s1_ref[row, :])
            new_q.append(qs[bi] * a1 + ps[bi] * a3 + s2_ref[row, :])
        return tuple(new_p), tuple(new_q)

    zero = tuple(jnp.zeros((1, 2 * SSM_STATE), F32) for _ in range(bsz))
    lax.fori_loop(0, n_chunks, step, (zero, zero))
    y = (jnp.dot(x, t_ref[0], preferred_element_type=F32)
         + jnp.dot(xin_ref[...].astype(BF16), v_ref[0], preferred_element_type=F32))
    o_ref[0] = jax.nn.gelu(y).astype(o_ref.dtype)


def _s5(xg, tables, bsz):
    tmat, w1, w2, vmat, am = tables
    g, r, lh = xg.shape
    p2 = 2 * SSM_STATE
    kern = functools.partial(_s5_kernel, bsz=bsz, n_chunks=r // bsz)
    return pl.pallas_call(
        kern,
        grid=(g,),
        in_specs=[pl.BlockSpec((1, r, lh), lambda i: (i, 0, 0)),
                  pl.BlockSpec((1, lh, lh), lambda i: (i, 0, 0)),
                  pl.BlockSpec((1, lh, p2), lambda i: (i, 0, 0)),
                  pl.BlockSpec((1, lh, p2), lambda i: (i, 0, 0)),
                  pl.BlockSpec((1, p2, lh), lambda i: (i, 0, 0)),
                  pl.BlockSpec((1, 8, p2), lambda i: (i, 0, 0))],
        out_specs=pl.BlockSpec((1, r, lh), lambda i: (i, 0, 0)),
        out_shape=jax.ShapeDtypeStruct((g, r, lh), BF16),
        scratch_shapes=[pltpu.VMEM((r, p2), F32), pltpu.VMEM((r, p2), F32), pltpu.VMEM((r, p2), F32)],
        compiler_params=_cparams(("parallel",), _VMEM_LIMIT),
        name="s5_scan",
    )(xg, tmat, w1, w2, vmat, am)


def _conv_kernel(z_ref, halo_ref, wdw_ref, bdw_ref, lng_ref, lnb_ref, wpw_ref, go_ref, o_ref, buf_ref, sh_ref, *, tt):
    first = pl.program_id(1) == 0
    zc = z_ref[0]
    zh = halo_ref[0]
    vh = zh[:, :CONV_WIDTH] * jax.nn.sigmoid(zh[:, CONV_WIDTH:])
    buf_ref[0:CONV_HALO, :] = vh * jnp.where(first, 0.0, 1.0)
    buf_ref[CONV_HALO:CONV_HALO + tt, :] = zc[:, :CONV_WIDTH] * jax.nn.sigmoid(zc[:, CONV_WIDTH:])
    for r in range(1, 8):
        sh_ref[r, 0:tt + CONV_HALO - 8, :] = buf_ref[pl.ds(r, tt + CONV_HALO - 8), :]
    acc = jnp.zeros((tt, CONV_WIDTH), F32) + bdw_ref[...]
    for k in range(CONV_K):
        off = CONV_HALO - (CONV_K - 1) + k
        r, a = off % 8, off - off % 8
        rows = buf_ref[pl.ds(a, tt), :] if r == 0 else sh_ref[r, pl.ds(a, tt), :]
        acc = acc + wdw_ref[k:k + 1, :] * rows
    mu = jnp.mean(acc, axis=-1, keepdims=True)
    var = jnp.mean(jnp.square(acc - mu), axis=-1, keepdims=True)
    y = (acc - mu) * lax.rsqrt(var + EPS) * lng_ref[...] + lnb_ref[...]
    y = jax.nn.silu(y)
    y = jnp.dot(y.astype(BF16), wpw_ref[...], preferred_element_type=F32)
    o_ref[0] = (_rms_rows(y) * go_ref[...]).astype(o_ref.dtype)


def _conv(ztok, w_dw, b_dw, ln_g, ln_b, w_pw, g_out, tt=512):
    b, s, _ = ztok.shape
    cw = CONV_WIDTH
    hb = tt // CONV_HALO
    kern = functools.partial(_conv_kernel, tt=tt)
    row = lambda v: v.reshape(1, cw)
    return pl.pallas_call(
        kern,
        grid=(b, s // tt),
        in_specs=[pl.BlockSpec((1, tt, 2 * cw), lambda i, j: (i, j, 0)),
                  pl.BlockSpec((1, CONV_HALO, 2 * cw), lambda i, j: (i, jnp.maximum(j * hb - 1, 0), 0)),
                  pl.BlockSpec((CONV_K + 1, cw), lambda i, j: (0, 0)),
                  pl.BlockSpec((1, cw), lambda i, j: (0, 0)),
                  pl.BlockSpec((1, cw), lambda i, j: (0, 0)),
                  pl.BlockSpec((1, cw), lambda i, j: (0, 0)),
                  pl.BlockSpec((cw, cw), lambda i, j: (0, 0)),
                  pl.BlockSpec((1, cw), lambda i, j: (0, 0))],
        out_specs=pl.BlockSpec((1, tt, cw), lambda i, j: (i, j, 0)),
        out_shape=jax.ShapeDtypeStruct((b, s, cw), BF16),
        scratch_shapes=[pltpu.VMEM((CONV_HALO + tt, cw), F32), pltpu.VMEM((8, CONV_HALO + tt, cw), F32)],
        compiler_params=_cparams(("parallel", "arbitrary")),
        name="conv_mixer",
    )(ztok, ztok, jnp.concatenate([w_dw, jnp.zeros((1, cw), F32)], 0), row(b_dw), row(ln_g), row(ln_b),
      w_pw.astype(BF16), row(g_out))


def _knorm_kernel(ks_ref, kw_ref, gs_ref, gw_ref, os_ref, ow_ref):
    for src, g_ref, dst in ((ks_ref, gs_ref, os_ref), (kw_ref, gw_ref, ow_ref)):
        x = src[0]
        for h in range(NSA_KV_HEADS):
            xh = x[:, h * HEAD_DIM:(h + 1) * HEAD_DIM]
            dst[0, h] = (_rms_rows(xh) * g_ref[...]).astype(dst.dtype)


def _knorm(ztok, col_s, col_w, gain_s, gain_w, tt=512):
    b, s, _ = ztok.shape
    kw = NSA_KV_HEADS * HEAD_DIM
    out = jax.ShapeDtypeStruct((b, NSA_KV_HEADS, s, HEAD_DIM), BF16)
    ospec = pl.BlockSpec((1, NSA_KV_HEADS, tt, HEAD_DIM), lambda i, j: (i, 0, j, 0))
    return pl.pallas_call(
        _knorm_kernel,
        grid=(b, s // tt),
        in_specs=[pl.BlockSpec((1, tt, kw), lambda i, j: (i, j, col_s // kw)),
                  pl.BlockSpec((1, tt, kw), lambda i, j: (i, j, col_w // kw)),
                  pl.BlockSpec((1, HEAD_DIM), lambda i, j: (0, 0)),
                  pl.BlockSpec((1, HEAD_DIM), lambda i, j: (0, 0))],
        out_specs=[ospec, ospec],
        out_shape=[out, out],
        compiler_params=_cparams(("parallel", "parallel")),
        name="nsa_key_norm",
    )(ztok, ztok, gain_s.reshape(1, HEAD_DIM), gain_w.reshape(1, HEAD_DIM))


def _compress_kernel(k_ref, v_ref, wka_ref, wkb_ref, ck_ref, w2k_ref, gk_ref,
                     wva_ref, wvb_ref, cv_ref, w2v_ref, ko_ref, vo_ref):
    hi = lax.Precision.HIGHEST
    n = k_ref.shape[1] // CMP_STRIDE
    kvw = k_ref.shape[2]
    nt = (((1,), (1,)), ((), ()))
    a, bm = jnp.zeros((n, kvw), F32), jnp.zeros((n, kvw), F32)
    at, bt = jnp.zeros((kvw, n), F32), jnp.zeros((kvw, n), F32)
    for l in range(CMP_STRIDE):
        kl = k_ref[0, pl.ds(l, n, stride=CMP_STRIDE), :]
        vl = v_ref[0, pl.ds(l, n, stride=CMP_STRIDE), :]
        a = a + jnp.dot(kl, wka_ref[l], precision=hi, preferred_element_type=F32)
        bm = bm + jnp.dot(kl, wkb_ref[l], precision=hi, preferred_element_type=F32)
        at = at + lax.dot_general(wva_ref[l], vl, nt, precision=hi, preferred_element_type=F32)
        bt = bt + lax.dot_general(wvb_ref[l], vl, nt, precision=hi, preferred_element_type=F32)
    pre = a + pltpu.roll(bm, n - 1, 0) + ck_ref[...]
    kc = jnp.dot(jax.nn.gelu(pre), w2k_ref[...], precision=hi, preferred_element_type=F32)
    for h in range(NSA_KV_HEADS):
        kh = kc[:, h * HEAD_DIM:(h + 1) * HEAD_DIM]
        ko_ref[0, h] = (_rms_rows(kh) * gk_ref[...]).astype(ko_ref.dtype)
    pre_t = at + pltpu.roll(bt, n - 1, 1) + cv_ref[...]
    vt = jnp.dot(w2v_ref[...], jax.nn.gelu(pre_t), precision=hi, preferred_element_type=F32)
    for h in range(NSA_KV_HEADS):
        vo_ref[0, h] = vt[h * HEAD_DIM:(h + 1) * HEAD_DIM, :].astype(vo_ref.dtype)


def _blockdiag2(w):
    z = jnp.zeros_like(w)
    return jnp.concatenate([jnp.concatenate([w, z], 1), jnp.concatenate([z, w], 1)], 0)


def _compress(ztok, col_k, col_v, pe, w1, w2, k_gain):
    b, s, _ = ztok.shape
    n = s // CMP_STRIDE
    hd, kvw = HEAD_DIM, NSA_KV_HEADS * HEAD_DIM
    hp = lax.Precision.HIGHEST

    def expand(w):
        wl = w.reshape(L_CMP, hd, hd)
        e = wl[:, None, :, None, :] * jnp.eye(NSA_KV_HEADS, dtype=F32)[None, :, None, :, None]
        e = e.reshape(L_CMP, kvw, kvw)
        return e[:CMP_STRIDE], e[CMP_STRIDE:]

    wka, wkb = expand(w1[0])
    wva, wvb = expand(w1[1])
    ck = jnp.tile(jnp.dot(pe[0].reshape(1, L_CMP * hd), w1[0], precision=hp), (1, NSA_KV_HEADS))
    cv = jnp.tile(jnp.dot(pe[1].reshape(1, L_CMP * hd), w1[1], precision=hp), (1, NSA_KV_HEADS)).T
    full = lambda shape: pl.BlockSpec(shape, lambda i: tuple(0 for _ in shape))
    return pl.pallas_call(
        _compress_kernel,
        grid=(b,),
        in_specs=[pl.BlockSpec((1, s, kvw), lambda i: (i, 0, col_k // kvw)),
                  pl.BlockSpec((1, s, kvw), lambda i: (i, 0, col_v // kvw)),
                  full((CMP_STRIDE, kvw, kvw)), full((CMP_STRIDE, kvw, kvw)), full((1, kvw)), full((kvw, kvw)),
                  full((1, hd)),
                  full((CMP_STRIDE, kvw, kvw)), full((CMP_STRIDE, kvw, kvw)), full((kvw, 1)), full((kvw, kvw))],
        out_specs=[pl.BlockSpec((1, NSA_KV_HEADS, n, hd), lambda i: (i, 0, 0, 0)),
                   pl.BlockSpec((1, NSA_KV_HEADS, hd, n), lambda i: (i, 0, 0, 0))],
        out_shape=[jax.ShapeDtypeStruct((b, NSA_KV_HEADS, n, hd), BF16),
                   jax.ShapeDtypeStruct((b, NSA_KV_HEADS, hd, n), BF16)],
        compiler_params=_cparams(("parallel",), _VMEM_LIMIT),
        name="nsa_compress",
    )(ztok, ztok, wka, wkb, ck, _blockdiag2(w2[0]), k_gain.reshape(1, hd),
      wva.transpose(0, 2, 1), wvb.transpose(0, 2, 1), cv, _blockdiag2(w2[1]).T)


def _t5_bias_by_dist(t5_table):
    n = np.arange(T5_MAX_DIST + 1)
    max_exact = T5_BUCKETS // 2
    nf = np.maximum(n, 1).astype(np.float32)
    large = max_exact + (np.log(nf / np.float32(max_exact)) / np.float32(math.log(T5_MAX_DIST / max_exact))
                         * np.float32(T5_BUCKETS - max_exact)).astype(np.int32)
    large = np.minimum(large, T5_BUCKETS - 1)
    bucket = np.where(n < max_exact, n, large)
    onehot = (bucket[:, None] == np.arange(T5_BUCKETS)[None, :]).astype(np.float32)
    return jnp.dot(onehot, t5_table, precision=lax.Precision.HIGHEST)


def _bias_tile(fdt, rows, stride, dist00, d_max=None):
    heads = fdt.shape[0]
    a0 = stride * (rows - 1)
    d_lo = dist00 - a0
    length = a0 + TQ
    d_hi = d_lo + length
    d_max = d_hi if d_max is None else d_max
    pieces = []
    for lo, hi, kind in ((d_lo, min(d_hi, 0), 'neg'), (max(d_lo, 0), min(d_hi, T5_MAX_DIST), 'tab'),
                         (max(d_lo, T5_MAX_DIST), min(d_hi, d_max), 'far'), (max(d_lo, d_max), d_hi, 'neg')):
        if hi > lo:
            pieces.append(fdt[:, lo:hi] if kind == 'tab'
                          else jnp.full((heads, hi - lo), NEG if kind == 'neg' else 0.0, F32))
    vec = jnp.concatenate(pieces, axis=1)
    c0 = -(-a0 // 128) * 128
    width = -(-(c0 + TQ) // 128) * 128
    vec = jnp.pad(vec, ((0, 0), (c0 - a0, width - (c0 - a0) - length)))

    def kern(v_ref, o_ref):
        x = jnp.broadcast_to(v_ref[0], (rows, width))
        o_ref[0] = pltpu.roll(x, 0, 1, stride=stride, stride_axis=0)[:, c0:c0 + TQ]

    return pl.pallas_call(
        kern,
        grid=(heads,),
        in_specs=[pl.BlockSpec((1, 1, width), lambda h: (h, 0, 0))],
        out_specs=pl.BlockSpec((1, rows, TQ), lambda h: (h, 0, 0)),
        out_shape=jax.ShapeDtypeStruct((heads, rows, TQ), F32),
        compiler_params=_cparams(("parallel",)),
        name="toeplitz_bias",
    )(vec.reshape(heads, 1, width))


def _nsa_bias_tiles(t5_table, seq):
    fd = _t5_bias_by_dist(t5_table).astype(F32)
    fdt = ((fd - fd[T5_MAX_DIST:]) * LOG2E).T
    n_cmp = seq // CMP_STRIDE
    qt = TQ // CMP_STRIDE
    r0 = n_cmp - qt
    band = _bias_tile(fdt, 2 * qt, CMP_STRIDE, CMP_STRIDE * qt - (L_CMP - 1))
    heads = fdt.shape[0]
    cmp_t = jnp.concatenate([jnp.zeros((heads, r0 - qt, TQ), F32), band,
                             jnp.full((heads, n_cmp - qt, TQ), NEG, F32)], axis=1)
    sel_t = _bias_tile(fdt, SEL_PAD + TQ, 1, SEL_PAD)
    win = _bias_tile(fdt, WINDOW + TQ, 1, WINDOW, d_max=WINDOW)
    rw = np.arange(WINDOW + TQ)[None, :, None]
    win_t = jnp.stack([jnp.where(rw >= WINDOW - q0, win, NEG) for q0 in (0, TQ, 2 * TQ)])
    split = lambda t: t.reshape(*t.shape[:-3], NSA_KV_HEADS, NSA_GROUP, *t.shape[-2:])

    def wide(t):
        t = jnp.swapaxes(split(t), -3, -2)
        return t.reshape(*t.shape[:-2], NSA_GROUP * TQ)

    return wide(cmp_t), wide(sel_t), wide(win_t)


def _q_head(qt_ref, g, qg_ref):
    q = qt_ref[0, g * HEAD_DIM:(g + 1) * HEAD_DIM, :]
    inv = lax.rsqrt(jnp.mean(q * q, axis=0, keepdims=True) + EPS)
    return (q * inv * qg_ref[...]).astype(BF16)


def _nsa_cmp_kernel(qt_ref, qg_ref, kc_ref, vct_ref, bias_ref, ov_ref, oc_ref, sel_ref, imp_ref, *, n_cmp, n_sel):
    qi = pl.program_id(1)
    qt = TQ // CMP_STRIDE
    gw = NSA_GROUP * HEAD_DIM
    kvs = range(NSA_KV_HEADS)
    row0 = pl.multiple_of((n_cmp - qt) - qi * qt, qt)
    qw = [jnp.concatenate([_q_head(qt_ref, kv * NSA_GROUP + g, qg_ref) for g in range(NSA_GROUP)], axis=1)
          for kv in kvs]

    def attend(n):
        for kv in kvs:
            s = (jnp.dot(kc_ref[0, kv, 0:n, :], qw[kv], preferred_element_type=F32)
                 + bias_ref[kv, pl.ds(row0, n), :])
            m = jnp.max(s, axis=0, keepdims=True)
            m = jnp.where(m < 0.5 * NEG, 0.0, m)
            p = jnp.exp2(s - m)
            p = p * (1.0 / jnp.maximum(jnp.sum(p, axis=0, keepdims=True), 1e-30))
            oc = jnp.dot(vct_ref[0, kv, :, 0:n], p.astype(BF16), preferred_element_type=F32)
            psum = jnp.zeros((n, TQ), F32)
            for g in range(NSA_GROUP):
                oc_ref[0, kv * gw + g * HEAD_DIM:kv * gw + (g + 1) * HEAD_DIM, :] = oc[:, g * TQ:(g + 1) * TQ]
                psum = psum + p[:, g * TQ:(g + 1) * TQ]
            hi = psum.astype(BF16)
            lo = (psum - hi.astype(F32)).astype(BF16)
            imp_ref[kv] = (jnp.dot(ov_ref[:, 0:n], hi, preferred_element_type=F32)
                           + jnp.dot(ov_ref[:, 0:n], lo, preferred_element_type=F32))

    chunk = min(n_cmp, 128)
    n_chunks = n_cmp // chunk
    need = lax.div((qi + 1) * qt + (chunk - 1), chunk)
    for c in range(1, n_chunks + 1):
        pl.when(need == c)(functools.partial(attend, c * chunk))

    blk = lax.broadcasted_iota(jnp.int32, (n_sel, TQ), 0)
    blk_t = lax.shift_right_logical(qi * TQ + lax.broadcasted_iota(jnp.int32, (n_sel, TQ), 1), L_SEL.bit_length() - 1)
    forced = (blk == 0) | (blk > blk_t - N_LOCAL)
    v0 = tuple(jnp.where(blk > blk_t, -jnp.inf, jnp.where(forced, SEL_FORCE, imp_ref[kv])) for kv in kvs)

    def pick(_, vs):
        out = []
        for v in vs:
            m = jnp.max(v, axis=0, keepdims=True)
            first = jnp.min(jnp.where(v == m, blk, n_sel), axis=0, keepdims=True)
            out.append(jnp.where(blk == first, -jnp.inf, v))
        return tuple(out)

    vs = lax.fori_loop(0, min(N_SELECT, n_sel), pick, v0)
    for kv in kvs:
        sel_ref[0, kv] = jnp.where((vs[kv] == -jnp.inf) & (v0[kv] > -jnp.inf), 0.0, NEG)


def _nsa_cmp(zt, qg, kc, vct, bias_c, ovt):
    b, _, s = zt.shape
    n_cmp, n_sel = s // CMP_STRIDE, s // L_SEL
    kvh = NSA_KV_HEADS
    kern = functools.partial(_nsa_cmp_kernel, n_cmp=n_cmp, n_sel=n_sel)
    return pl.pallas_call(
        kern,
        grid=(b, s // TQ),
        in_specs=[pl.BlockSpec((1, NSA_WIDTH, TQ), lambda i, j: (i, 0, j)),
                  pl.BlockSpec((HEAD_DIM, TQ), lambda i, j: (0, 0)),
                  pl.BlockSpec((1, kvh, n_cmp, HEAD_DIM), lambda i, j: (i, 0, 0, 0)),
                  pl.BlockSpec((1, kvh, HEAD_DIM, n_cmp), lambda i, j: (i, 0, 0, 0)),
                  pl.BlockSpec((kvh, bias_c.shape[1], NSA_GROUP * TQ), lambda i, j: (0, 0, 0)),
                  pl.BlockSpec((n_sel, n_cmp), lambda i, j: (0, 0))],
        out_specs=[pl.BlockSpec((1, NSA_WIDTH, TQ), lambda i, j: (i, 0, j)),
                   pl.BlockSpec((1, kvh, n_sel, TQ), lambda i, j: (i, 0, 0, j))],
        out_shape=[jax.ShapeDtypeStruct((b, NSA_WIDTH, s), F32),
                   jax.ShapeDtypeStruct((b, kvh, n_sel, s), F32)],
        scratch_shapes=[pltpu.VMEM((kvh, n_sel, TQ), F32)],
        compiler_params=_cparams(("parallel", "parallel"), _VMEM_LIMIT),
        name="nsa_compressed_select",
    )(zt, qg, kc, vct, bias_c, ovt)


def _nsa_main_kernel(qt_ref, qg_ref, ks_ref, vst_ref, kw_ref, vwt_ref, sel_ref, bs_ref, bw_ref, gate_ref,
                     oc_ref, o_ref, acc_ref, s_ref, ot_ref):
    qi = pl.program_id(1)
    q0 = pl.multiple_of(qi * TQ, TQ)
    near = SEL_PAD + TQ
    gw = NSA_GROUP * HEAD_DIM
    kvs = range(NSA_KV_HEADS)

    def expand_sel(kv, first_blk, n_blk):
        rows = [jnp.broadcast_to(sel_ref[0, kv, pl.ds(first_blk + r, 1), :], (L_SEL, TQ)) for r in range(n_blk)]
        rows = jnp.concatenate(rows, axis=0)
        return jnp.concatenate([rows] * NSA_GROUP, axis=1)

    qw = [jnp.concatenate([_q_head(qt_ref, kv * NSA_GROUP + g, qg_ref) for g in range(NSA_GROUP)], axis=1)
          for kv in kvs]
    m0 = []
    for kv in kvs:
        s = (jnp.dot(ks_ref[0, kv, pl.ds(q0, near), :], qw[kv], preferred_element_type=F32) + bs_ref[kv]
             + expand_sel(kv, qi * (TQ // L_SEL), near // L_SEL))
        m = jnp.max(s, axis=0, keepdims=True)
        p = jnp.exp2(s - m).astype(BF16)
        acc_ref[kv] = jnp.dot(vst_ref[0, kv, :, pl.ds(q0, near)], p, preferred_element_type=F32)
        m0.append(m)

    for kv in kvs:
        s = jnp.dot(kw_ref[0, kv, pl.ds(q0, WINDOW + TQ), :], qw[kv], preferred_element_type=F32) + bw_ref[0, kv]
        p = jnp.exp2(s - jnp.max(s, axis=0, keepdims=True)).astype(BF16)
        ow = jnp.dot(vwt_ref[0, kv, :, pl.ds(q0, WINDOW + TQ)], p, preferred_element_type=F32)
        ow = ow[:HEAD_DIM] * (1.0 / jnp.maximum(ow[HEAD_DIM:HEAD_DIM + 1], 1e-30))
        for g in range(NSA_GROUP):
            gates = jax.nn.sigmoid(gate_ref[0, kv, g * N_BRANCH:(g + 1) * N_BRANCH, :])
            rows = slice(kv * gw + g * HEAD_DIM, kv * gw + (g + 1) * HEAD_DIM)
            ot_ref[rows, :] = gates[0:1] * oc_ref[0, rows, :] + gates[2:3] * ow[:, g * TQ:(g + 1) * TQ]

    def scores(c, slot):
        r0 = pl.multiple_of(c * TQ, TQ)
        mc = []
        for kv in kvs:
            s = (jnp.dot(ks_ref[0, kv, pl.ds(r0, TQ), :], qw[kv], preferred_element_type=F32)
                 + expand_sel(kv, c * (TQ // L_SEL), TQ // L_SEL))
            s_ref[slot, kv] = s
            mc.append(jnp.max(s, axis=0, keepdims=True))
        return tuple(mc)

    def consume(c, slot, m_old, mc):
        r0 = pl.multiple_of(c * TQ, TQ)
        m_out = []
        for kv in kvs:
            m_new = jnp.maximum(m_old[kv], mc[kv])
            alpha = jnp.exp2(m_old[kv] - m_new)
            p = jnp.exp2((s_ref[slot, kv] - m_new).astype(BF16))
            acc_ref[kv] = alpha * acc_ref[kv] + jnp.dot(vst_ref[0, kv, :, pl.ds(r0, TQ)], p,
                                                        preferred_element_type=F32)
            m_out.append(m_new)
        return tuple(m_out)

    first = SEL_PAD // TQ
    n_far = qi - first

    def pair(i, carry):
        m, mc = carry
        c = first + 2 * i
        mc1 = scores(c + 1, 1)
        m = consume(c, 0, m, mc)
        mc2 = scores(c + 2, 0)
        m = consume(c + 1, 1, m, mc1)
        return m, mc2

    m_far, mc_far = lax.fori_loop(0, n_far // 2, pair, (tuple(m0), scores(first, 0)))

    @pl.when((n_far > 0) & (n_far % 2 == 1))
    def _():
        consume(first + n_far - 1, 0, m_far, mc_far)

    for kv in kvs:
        os = acc_ref[kv]
        os = os[:HEAD_DIM] * (1.0 / jnp.maximum(os[HEAD_DIM:HEAD_DIM + 1], 1e-30))
        for g in range(NSA_GROUP):
            gate = jax.nn.sigmoid(gate_ref[0, kv, g * N_BRANCH + 1:g * N_BRANCH + 2, :])
            rows = slice(kv * gw + g * HEAD_DIM, kv * gw + (g + 1) * HEAD_DIM)
            ot_ref[rows, :] = ot_ref[rows, :] + gate * os[:, g * TQ:(g + 1) * TQ]
    o_ref[0] = ot_ref[...].T


def _nsa_main(zt, qg, ks_p, vst_p, kw_p, vwt_p, sel_p, bias_s, bias_w, gates_t, oc_t):
    b, _, s = zt.shape
    kvh = NSA_KV_HEADS
    sp, wp = ks_p.shape[2], kw_p.shape[2]
    nb, vr = sel_p.shape[2], vst_p.shape[2]
    once = pl.Buffered(1)
    return pl.pallas_call(
        _nsa_main_kernel,
        grid=(b, s // TQ),
        in_specs=[pl.BlockSpec((1, NSA_WIDTH, TQ), lambda i, j: (i, 0, j)),
                  pl.BlockSpec((HEAD_DIM, TQ), lambda i, j: (0, 0)),
                  pl.BlockSpec((1, kvh, sp, HEAD_DIM), lambda i, j: (i, 0, 0, 0), pipeline_mode=once),
                  pl.BlockSpec((1, kvh, vr, sp), lambda i, j: (i, 0, 0, 0), pipeline_mode=once),
                  pl.BlockSpec((1, kvh, wp, HEAD_DIM), lambda i, j: (i, 0, 0, 0), pipeline_mode=once),
                  pl.BlockSpec((1, kvh, vr, wp), lambda i, j: (i, 0, 0, 0), pipeline_mode=once),
                  pl.BlockSpec((1, kvh, nb, TQ), lambda i, j: (i, 0, 0, j)),
                  pl.BlockSpec((kvh, SEL_PAD + TQ, NSA_GROUP * TQ), lambda i, j: (0, 0, 0), pipeline_mode=once),
                  pl.BlockSpec((1, kvh, WINDOW + TQ, NSA_GROUP * TQ), lambda i, j: (jnp.minimum(j, 2), 0, 0, 0)),
                  pl.BlockSpec((1, kvh, 16, TQ), lambda i, j: (i, 0, 0, j)),
                  pl.BlockSpec((1, NSA_WIDTH, TQ), lambda i, j: (i, 0, j))],
        out_specs=pl.BlockSpec((1, TQ, NSA_WIDTH), lambda i, j: (i, j, 0)),
        out_shape=jax.ShapeDtypeStruct((b, s, NSA_WIDTH), F32),
        scratch_shapes=[pltpu.VMEM((kvh, vr, NSA_GROUP * TQ), F32),
                        pltpu.VMEM((2, kvh, TQ, NSA_GROUP * TQ), F32),
                        pltpu.VMEM((NSA_WIDTH, TQ), F32)],
        compiler_params=_cparams(("parallel", "arbitrary"), _VMEM_LIMIT),
        name="nsa_selected_window",
    )(zt, qg, ks_p, vst_p, kw_p, vwt_p, sel_p, bias_s, bias_w, gates_t, oc_t)


def _mixout_kernel(gy_ref, yc_ref, yn_ref, h_ref, wglu_ref, go_ref, wo_ref, o_ref):
    sw = SSM_WIDTH
    half = sw // 2
    ag = (jnp.dot(gy_ref[0, 0].astype(BF16), wglu_ref[0:half, :], preferred_element_type=F32)
          + jnp.dot(gy_ref[1, 0].astype(BF16), wglu_ref[half:, :], preferred_element_type=F32))
    ys = ag[:, :sw] * jax.nn.sigmoid(ag[:, sw:])
    ys = (_rms_rows(ys) * go_ref[:, 0:sw]).astype(BF16)
    yn = (_rms_rows(yn_ref[0]) * go_ref[:, 2 * sw:]).astype(BF16)
    out = (jnp.dot(ys, wo_ref[0:sw, :], preferred_element_type=F32)
           + jnp.dot(yc_ref[0], wo_ref[sw:2 * sw, :], preferred_element_type=F32)
           + jnp.dot(yn, wo_ref[2 * sw:, :], preferred_element_type=F32))
    o_ref[0] = h_ref[0] + out


def _mixout(gy, yc, yn, h, w_glu, g_out, w_out, tm=1024):
    b, s, d = h.shape
    tok = lambda w: pl.BlockSpec((1, tm, w), lambda i, j: (i, j, 0))
    return pl.pallas_call(
        _mixout_kernel,
        grid=(b, s // tm),
        in_specs=[pl.BlockSpec((2, 1, tm, SSM_WIDTH // 2), lambda i, j: (0, i, j, 0)),
                  tok(CONV_WIDTH), tok(NSA_WIDTH), tok(d),
                  pl.BlockSpec((SSM_WIDTH, 2 * SSM_WIDTH), lambda i, j: (0, 0)),
                  pl.BlockSpec((1, d), lambda i, j: (0, 0)),
                  pl.BlockSpec((d, d), lambda i, j: (0, 0))],
        out_specs=tok(d),
        out_shape=jax.ShapeDtypeStruct((b, s, d), F32),
        compiler_params=_cparams(("parallel", "parallel")),
        name="mix_out",
    )(gy, yc, yn, h, w_glu.astype(BF16), g_out.reshape(1, d), w_out.astype(BF16))


def _memkv_kernel(mem_ref, g_ref, w_ref, kg_ref, k_ref, v_ref):
    mn = (_rms_rows(mem_ref[0]) * g_ref[...]).astype(BF16)
    kv = jnp.dot(mn, w_ref[...], preferred_element_type=F32)
    for h in range(X_HEADS):
        cols = slice(h * HEAD_DIM, (h + 1) * HEAD_DIM)
        k_ref[0, :, cols] = (_rms_rows(kv[:, cols]) * kg_ref[...]).astype(k_ref.dtype)
    v_ref[0] = kv[:, X_WIDTH:].astype(v_ref.dtype)


def _memkv(mem, gain, w_kv, k_gain):
    b, m, d = mem.shape
    out = jax.ShapeDtypeStruct((b, m, X_WIDTH), BF16)
    return pl.pallas_call(
        _memkv_kernel,
        grid=(b,),
        in_specs=[pl.BlockSpec((1, m, d), lambda i: (i, 0, 0)),
                  pl.BlockSpec((1, d), lambda i: (0, 0)),
                  pl.BlockSpec((d, 2 * X_WIDTH), lambda i: (0, 0)),
                  pl.BlockSpec((1, HEAD_DIM), lambda i: (0, 0))],
        out_specs=[pl.BlockSpec((1, m, X_WIDTH), lambda i: (i, 0, 0))] * 2,
        out_shape=[out, out],
        compiler_params=_cparams(("parallel",)),
        name="cross_mem_kv",
    )(mem, gain.reshape(1, d), w_kv.astype(BF16), k_gain.reshape(1, HEAD_DIM))


def _cross_kernel(h_ref, g_ref, wq_ref, qg_ref, k_ref, v_ref, wo_ref, o_ref):
    h = h_ref[0]
    hn = (_rms_rows(h) * g_ref[...]).astype(BF16)
    q = jnp.dot(hn, wq_ref[...], preferred_element_type=F32)
    out = h
    for hd in range(X_HEADS):
        cols = slice(hd * HEAD_DIM, (hd + 1) * HEAD_DIM)
        qh = (_rms_rows(q[:, cols]) * qg_ref[...]).astype(BF16)
        s = lax.dot_general(qh, k_ref[0, :, cols], (((1,), (1,)), ((), ())), preferred_element_type=F32)
        p = jnp.exp(s - jnp.max(s, axis=-1, keepdims=True))
        p = p * (1.0 / jnp.sum(p, axis=-1, keepdims=True))
        o = jnp.dot(p.astype(BF16), v_ref[0, :, cols], preferred_element_type=F32)
        out = out + jnp.dot(o.astype(BF16), wo_ref[cols, :], preferred_element_type=F32)
    o_ref[0] = out


def _cross(h, gain, w_q, q_gain, k, v, w_o, tm=1024):
    b, s, d = h.shape
    m = k.shape[1]
    return pl.pallas_call(
        _cross_kernel,
        grid=(b, s // tm),
        in_specs=[pl.BlockSpec((1, tm, d), lambda i, j: (i, j, 0)),
                  pl.BlockSpec((1, d), lambda i, j: (0, 0)),
                  pl.BlockSpec((d, X_WIDTH), lambda i, j: (0, 0)),
                  pl.BlockSpec((1, HEAD_DIM), lambda i, j: (0, 0)),
                  pl.BlockSpec((1, m, X_WIDTH), lambda i, j: (i, 0, 0)),
                  pl.BlockSpec((1, m, X_WIDTH), lambda i, j: (i, 0, 0)),
                  pl.BlockSpec((X_WIDTH, d), lambda i, j: (0, 0))],
        out_specs=pl.BlockSpec((1, tm, d), lambda i, j: (i, j, 0)),
        out_shape=jax.ShapeDtypeStruct((b, s, d), F32),
        compiler_params=_cparams(("parallel", "parallel")),
        name="cross_attention",
    )(h, gain.reshape(1, d), w_q.astype(BF16), (q_gain * HEAD_DIM ** -0.5).reshape(1, HEAD_DIM), k, v,
      w_o.astype(BF16))


def _ffn_kernel(h_ref, g_ref, wg_ref, wv_ref, wd_ref, o_ref, xn_ref, acc_ref):
    f = pl.program_id(1)

    @pl.when(f == 0)
    def _():
        xn_ref[...] = (_rms_rows(h_ref[...]) * g_ref[...]).astype(BF16)
        acc_ref[...] = jnp.zeros_like(acc_ref)

    x = xn_ref[...]
    gate = jnp.dot(x, wg_ref[...], preferred_element_type=F32)
    val = jnp.dot(x, wv_ref[...], preferred_element_type=F32)
    act = (jax.nn.silu(gate) * val).astype(BF16)
    acc_ref[...] += jnp.dot(act, wd_ref[...], preferred_element_type=F32)

    @pl.when(f == pl.num_programs(1) - 1)
    def _():
        o_ref[...] = h_ref[...] + acc_ref[...]


def _ffn(h2d, gain, w_up, w_down, tm=2048):
    t, d = h2d.shape
    nf = D_FF // FF_CHUNK
    wb = w_up.astype(BF16)
    return pl.pallas_call(
        _ffn_kernel,
        grid=(t // tm, nf),
        in_specs=[pl.BlockSpec((tm, d), lambda i, f: (i, 0)),
                  pl.BlockSpec((1, d), lambda i, f: (0, 0)),
                  pl.BlockSpec((d, FF_CHUNK), lambda i, f: (0, f)),
                  pl.BlockSpec((d, FF_CHUNK), lambda i, f: (0, f + nf)),
                  pl.BlockSpec((FF_CHUNK, d), lambda i, f: (f, 0))],
        out_specs=pl.BlockSpec((tm, d), lambda i, f: (i, 0)),
        out_shape=jax.ShapeDtypeStruct((t, d), F32),
        scratch_shapes=[pltpu.VMEM((tm, d), BF16), pltpu.VMEM((tm, d), F32)],
        compiler_params=_cparams(("parallel", "arbitrary"), _VMEM_LIMIT),
        name="ffn_swiglu",
    )(h2d, gain.reshape(1, d), wb, wb, w_down.astype(BF16))


def _router_kernel(h_ref, g_ref, wr_ref, xn_ref, gate_ref, asg_ref):
    xn = _rms_rows(h_ref[...]) * g_ref[...]
    xn_ref[...] = xn.astype(BF16)
    logits = jnp.dot(xn, wr_ref[...], precision=lax.Precision.HIGHEST, preferred_element_type=F32)
    lane = lax.broadcasted_iota(jnp.int32, logits.shape, 1)
    lg = jnp.where(lane < N_EXPERTS, logits, -jnp.inf)
    m1 = jnp.max(lg, axis=-1, keepdims=True)
    i1 = jnp.min(jnp.where(lg == m1, lane, 128), axis=-1, keepdims=True)
    lg2 = jnp.where(lane == i1, -jnp.inf, lg)
    m2 = jnp.max(lg2, axis=-1, keepdims=True)
    i2 = jnp.min(jnp.where(lg2 == m2, lane, 128), axis=-1, keepdims=True)
    e = jnp.exp(m2 - m1)
    den = 1.0 + e
    gate_ref[...] = jnp.where(lane == i1, 1.0 / den, jnp.where(lane == i2, e / den, 0.0))
    asg_ref[...] = ((lane == i1) | (lane == i2)).astype(jnp.int32)


def _router(h2d, gain, w_router, tm=512):
    t, d = h2d.shape
    wr = jnp.concatenate([w_router, jnp.zeros((d, 128 - N_EXPERTS), F32)], axis=1)
    return pl.pallas_call(
        _router_kernel,
        grid=(t // tm,),
        in_specs=[pl.BlockSpec((tm, d), lambda i: (i, 0)),
                  pl.BlockSpec((1, d), lambda i: (0, 0)),
                  pl.BlockSpec((d, 128), lambda i: (0, 0))],
        out_specs=[pl.BlockSpec((tm, d), lambda i: (i, 0)),
                   pl.BlockSpec((tm, 128), lambda i: (i, 0)),
                   pl.BlockSpec((tm, 128), lambda i: (i, 0))],
        out_shape=[jax.ShapeDtypeStruct((t, d), BF16), jax.ShapeDtypeStruct((t, 128), F32),
                   jax.ShapeDtypeStruct((t, 128), jnp.int32)],
        compiler_params=_cparams(("parallel",)),
        name="moe_router",
    )(h2d, gain.reshape(1, d), wr)


def _moe_windows(rb, lo, hi, active):
    lo_l = jnp.clip(lo - rb * MOE_TB, 0, MOE_TB)
    hi_l = jnp.clip(hi - rb * MOE_TB, 0, MOE_TB)
    shift = MOE_SUB.bit_length() - 1
    w0 = jnp.minimum(lax.shift_left(lax.shift_right_logical(lo_l, shift), shift), MOE_TB - MOE_WIN)
    has = active & (hi_l > lo_l)
    w1 = jnp.minimum(w0 + MOE_WIN, MOE_TB - MOE_WIN)
    return ((w0, 0, has), (w1, w0 + MOE_WIN, has & (hi_l > w0 + MOE_WIN)))


def _moe_gather_kernel(rb_ref, lo_ref, hi_ref, first_ref, tgt_ref, x_ref, o_ref):
    e, j, slot = pl.program_id(0), pl.program_id(1), pl.program_id(2)
    rb = rb_ref[e, j, slot]

    @pl.when(first_ref[e, j, slot] == 1)
    def _():
        o_ref[...] = jnp.zeros_like(o_ref)

    active = (slot == 0) | (rb != rb_ref[e, j, 0])
    tgt = tgt_ref[0]
    n_sub = MOE_TB // MOE_TS
    for sub in range(n_sub):
        toks = slice(sub * MOE_TS, (sub + 1) * MOE_TS)
        for start, cutoff, needed in _moe_windows(rb, lo_ref[e, j * n_sub + sub], hi_ref[e, j * n_sub + sub], active):

            @pl.when(needed)
            def _():
                local = start + lax.broadcasted_iota(jnp.int32, (MOE_WIN, MOE_TS), 0)
                rows = jnp.where(local >= cutoff, rb * MOE_TB + local, -2)
                onehot = jnp.where(tgt[:, toks] == rows, 1.0, 0.0).astype(BF16)
                part = jnp.dot(onehot, x_ref[toks, :], preferred_element_type=F32)
                sl = pl.ds(pl.multiple_of(start, MOE_SUB), MOE_WIN)
                o_ref[sl, :] = o_ref[sl, :] + part.astype(o_ref.dtype)


def _moe_ffn_kernel(exp_ref, nused_ref, x_ref, wg_ref, wv_ref, wd_ref, o_ref, acc_ref):
    r, f = pl.program_id(0), pl.program_id(1)
    used = r < nused_ref[0]

    @pl.when(f == 0)
    def _():
        acc_ref[...] = jnp.zeros_like(acc_ref)

    @pl.when(used)
    def _():
        x = x_ref[...]
        gate = jnp.dot(x, wg_ref[0], preferred_element_type=F32)
        val = jnp.dot(x, wv_ref[0], preferred_element_type=F32)
        act = (jax.nn.silu(gate) * val).astype(BF16)
        acc_ref[...] += jnp.dot(act, wd_ref[0], preferred_element_type=F32)

    @pl.when(f == pl.num_programs(1) - 1)
    def _():
        o_ref[...] = acc_ref[...].astype(o_ref.dtype)
    del exp_ref


def _moe_scatter_kernel(rb_ref, lo_ref, hi_ref, tgt_ref, gate_ref, y_ref, h_ref, o_ref):
    j, e, slot = pl.program_id(0), pl.program_id(1), pl.program_id(2)
    rb = rb_ref[e, j, slot]

    @pl.when((e == 0) & (slot == 0))
    def _():
        o_ref[...] = h_ref[...]

    active = (slot == 0) | (rb != rb_ref[e, j, 0])
    n_sub = MOE_TB // MOE_TS
    for sub in range(n_sub):
        toks = slice(sub * MOE_TS, (sub + 1) * MOE_TS)
        for start, cutoff, needed in _moe_windows(rb, lo_ref[e, j * n_sub + sub], hi_ref[e, j * n_sub + sub], active):

            @pl.when(needed)
            def _():
                mine = lax.broadcasted_iota(jnp.int32, (MOE_TS, N_EXPERTS), 1) == e
                tgt = jnp.sum(jnp.where(mine, tgt_ref[toks, :], 0), axis=1, keepdims=True)
                gate = jnp.sum(jnp.where(mine, gate_ref[toks, :], 0.0), axis=1, keepdims=True)
                local = start + lax.broadcasted_iota(jnp.int32, (MOE_TS, MOE_WIN), 1)
                rows = jnp.where(local >= cutoff, rb * MOE_TB + local, -2)
                onehot = jnp.where(tgt == rows, 1.0, 0.0).astype(BF16)
                y = y_ref[pl.ds(pl.multiple_of(start, MOE_SUB), MOE_WIN), :]
                o_ref[toks, :] = o_ref[toks, :] + gate * jnp.dot(onehot, y, preferred_element_type=F32)


def _moe(h2d, gain, w_router, w_up, w_down):
    t, d = h2d.shape
    tb = MOE_TB
    nj = t // tb
    n_rb = (t * TOP_K) // tb + N_EXPERTS
    xn, gates, asg = _router(h2d, gain, w_router)
    asg = asg[:, :N_EXPERTS]
    gates = gates[:, :N_EXPERTS]
    cs = jnp.cumsum(asg, axis=0)
    rank = cs - asg
    counts = cs[-1]
    padded = (counts + tb - 1) // tb * tb
    pad_end = jnp.cumsum(padded)
    start_p = pad_end - padded
    tgt = jnp.where(asg == 1, start_p[None, :] + rank, -1).astype(jnp.int32)
    ts = MOE_TS
    cb = jnp.concatenate([jnp.zeros((1, N_EXPERTS), jnp.int32), cs[ts - 1::ts]], axis=0)
    lo = (start_p[None, :] + cb[:-1]).T.astype(jnp.int32)
    hi = (start_p[None, :] + cb[1:]).T.astype(jnp.int32)
    rb0 = lo[:, ::tb // ts] // tb
    rb1 = jnp.maximum(rb0, (hi[:, tb // ts - 1::tb // ts] - 1) // tb)
    rb = jnp.stack([rb0, rb1], axis=-1).astype(jnp.int32)
    flat = rb.reshape(-1)
    first = jnp.concatenate([jnp.ones((1,), jnp.int32), (flat[1:] != flat[:-1]).astype(jnp.int32)])
    first = first.reshape(N_EXPERTS, nj, 2)
    n_used = (pad_end[-1] // tb).astype(jnp.int32).reshape(1)
    blk_exp = jnp.minimum(jnp.searchsorted(pad_end, jnp.arange(n_rb) * tb, side='right'),
                          N_EXPERTS - 1).astype(jnp.int32)

    xs = pl.pallas_call(
        _moe_gather_kernel,
        grid_spec=pltpu.PrefetchScalarGridSpec(
            num_scalar_prefetch=4,
            grid=(N_EXPERTS, nj, 2),
            in_specs=[pl.BlockSpec((1, 1, tb), lambda e, j, s, *_: (e, 0, j)),
                      pl.BlockSpec((tb, d), lambda e, j, s, *_: (j, 0))],
            out_specs=pl.BlockSpec((tb, d), lambda e, j, s, rb_ref, *_: (rb_ref[e, j, s], 0))),
        out_shape=jax.ShapeDtypeStruct((n_rb * tb, d), BF16),
        compiler_params=_cparams(("arbitrary", "arbitrary", "arbitrary"), _VMEM_LIMIT),
        name="moe_gather",
    )(rb, lo, hi, first, tgt.T.reshape(N_EXPERTS, 1, t), xn)

    nf = D_FF // FF_CHUNK
    wub = w_up.astype(BF16)
    ys = pl.pallas_call(
        _moe_ffn_kernel,
        grid_spec=pltpu.PrefetchScalarGridSpec(
            num_scalar_prefetch=2,
            grid=(n_rb, nf),
            in_specs=[pl.BlockSpec((tb, d), lambda r, f, *_: (r, 0)),
                      pl.BlockSpec((1, d, FF_CHUNK), lambda r, f, ex, nu: (ex[r], 0, f)),
                      pl.BlockSpec((1, d, FF_CHUNK), lambda r, f, ex, nu: (ex[r], 0, f + nf)),
                      pl.BlockSpec((1, FF_CHUNK, d), lambda r, f, ex, nu: (ex[r], f, 0))],
            out_specs=pl.BlockSpec((tb, d), lambda r, f, *_: (r, 0)),
            scratch_shapes=[pltpu.VMEM((tb, d), F32)]),
        out_shape=jax.ShapeDtypeStruct((n_rb * tb, d), BF16),
        compiler_params=_cparams(("arbitrary", "arbitrary"), _VMEM_LIMIT),
        name="moe_expert_ffn",
    )(blk_exp, n_used, xs, wub, wub, w_down.astype(BF16))

    return pl.pallas_call(
        _moe_scatter_kernel,
        grid_spec=pltpu.PrefetchScalarGridSpec(
            num_scalar_prefetch=3,
            grid=(nj, N_EXPERTS, 2),
            in_specs=[pl.BlockSpec((tb, N_EXPERTS), lambda j, e, s, *_: (j, 0)),
                      pl.BlockSpec((tb, N_EXPERTS), lambda j, e, s, *_: (j, 0)),
                      pl.BlockSpec((tb, d), lambda j, e, s, rb_ref, *_: (rb_ref[e, j, s], 0)),
                      pl.BlockSpec((tb, d), lambda j, e, s, *_: (j, 0))],
            out_specs=pl.BlockSpec((tb, d), lambda j, e, s, *_: (j, 0))),
        out_shape=jax.ShapeDtypeStruct((t, d), F32),
        compiler_params=_cparams(("arbitrary", "arbitrary", "arbitrary"), _VMEM_LIMIT),
        name="moe_scatter",
    )(rb, lo, hi, tgt, gates, ys, h2d)


_COL_CONV, _COL_SSM, _COL_KC, _COL_VC, _COL_KS, _COL_KW = 0, 512, 768, 896, 1024, 1152
_ROW_Q, _ROW_VS, _ROW_VW, _ROW_G = 0, 512, 640, 768


def _split_w_in(w_in):
    kvw = NSA_KV_HEADS * HEAD_DIM
    cuts = np.cumsum([0, SSM_WIDTH, 2 * CONV_WIDTH, NSA_WIDTH] + [kvw] * 6 + [N_BRANCH * NSA_HEADS])
    seg = lambda i: w_in[:, cuts[i]:cuts[i + 1]]
    ssm, conv, q, k_c, v_c, k_s, v_s, k_w, v_w, gate = (seg(i) for i in range(10))
    w_tok = jnp.concatenate([conv, ssm, k_c, v_c, k_s, k_w], axis=1).astype(BF16)
    gate = jnp.concatenate([gate, jnp.zeros((w_in.shape[0], 8), F32)], axis=1)
    w_t = jnp.concatenate([q, v_s, v_w, gate], axis=1).T.astype(BF16)
    return w_tok, w_t


def _layer_mixers(h, p, t5_tiles, s5_perm):
    b, s, d = h.shape
    w_tok, w_t = _split_w_in(p['w_in'])
    k_norm = p['nsa_k_norm']
    ztok, zt = _proj(h, p['norm_mix'], w_tok, w_t)
    ks, kw = _knorm(ztok, _COL_KS, _COL_KW, k_norm[1], k_norm[2])

    tables = _s5_tables(p['ssm_lambda_re'], p['ssm_lambda_im'], p['ssm_log_dt'], p['ssm_b_re'], p['ssm_b_im'],
                        p['ssm_c_re'], p['ssm_c_im'], p['ssm_d'])
    gy = _s5_unpack(_s5(_s5_pack(ztok, _COL_SSM, s5_perm[0]), tables, b), s5_perm[1], b)

    g_out = p['mix_out_norm']
    yc = _conv(ztok, p['conv_w_dw'], p['conv_b_dw'], p['conv_ln_g'], p['conv_ln_b'], p['conv_w_pw'],
               g_out[SSM_WIDTH:SSM_WIDTH + CONV_WIDTH])

    kvw = NSA_KV_HEADS * HEAD_DIM
    n_cmp = s // CMP_STRIDE
    kc, vct = _compress(ztok, _COL_KC, _COL_VC, p['nsa_cmp_pe'], p['nsa_cmp_w1'], p['nsa_cmp_w2'], k_norm[0])
    qg = jnp.broadcast_to((p['nsa_q_norm'] * (HEAD_DIM ** -0.5 * LOG2E))[:, None], (HEAD_DIM, TQ))
    bias_c, bias_s, bias_w = t5_tiles
    n_sel = s // L_SEL
    cs_ = np.arange(n_cmp) * CMP_STRIDE
    ss_ = np.arange(n_sel) * L_SEL
    ov = np.maximum(np.minimum(cs_[:, None] + L_CMP, ss_[None, :] + L_SEL) - np.maximum(cs_[:, None], ss_[None, :]), 0)
    ovt = jnp.asarray((ov.astype(np.float32) / L_CMP).T, BF16)
    oc_t, sel = _nsa_cmp(zt, qg, kc, vct, bias_c, ovt)

    front = lambda x, n, axis: jnp.pad(x, [(n, 0) if a == axis else (0, 0) for a in range(x.ndim)])
    ones_rows = jnp.concatenate([jnp.ones((b, NSA_KV_HEADS, 1, s), BF16),
                                 jnp.zeros((b, NSA_KV_HEADS, 15, s), BF16)], axis=2)
    heads_t = lambda rows: jnp.concatenate(
        [zt[:, rows:rows + kvw, :].astype(BF16).reshape(b, NSA_KV_HEADS, HEAD_DIM, s), ones_rows], axis=2)
    ks_p = front(ks, SEL_PAD, 2)
    kw_p = front(kw, WINDOW, 2)
    vst_p = front(heads_t(_ROW_VS), SEL_PAD, 3)
    vwt_p = front(heads_t(_ROW_VW), WINDOW, 3)
    sel_p = jnp.pad(sel, ((0, 0), (0, 0), (SEL_PAD // L_SEL, 0), (0, 0)), constant_values=NEG)
    gl = zt[:, _ROW_G:_ROW_G + N_BRANCH * NSA_HEADS, :].reshape(b, NSA_KV_HEADS, NSA_GROUP * N_BRANCH, s)
    gates_t = jnp.pad(gl, ((0, 0), (0, 0), (0, 16 - NSA_GROUP * N_BRANCH), (0, 0)))
    yn_t = _nsa_main(zt, qg, ks_p, vst_p, kw_p, vwt_p, sel_p, bias_s, bias_w, gates_t, oc_t)
    yn = yn_t

    return _mixout(gy, yc, yn, h, p['ssm_w_glu'], g_out, p['w_out'])


def kernel(x, mem, norm_mix, w_in, ssm_lambda_re, ssm_lambda_im, ssm_log_dt, ssm_b_re, ssm_b_im, ssm_c_re, ssm_c_im, ssm_d, ssm_w_glu, conv_w_dw, conv_b_dw, conv_ln_g, conv_ln_b, conv_w_pw, nsa_q_norm, nsa_k_norm, nsa_cmp_pe, nsa_cmp_w1, nsa_cmp_w2, mix_out_norm, w_out, t5_table, norm_cross, norm_mem, x_w_q, x_w_kv, x_q_norm, x_k_norm, x_w_o, norm_ffn, ffn_w_up, ffn_w_down, moe_router, moe_w_up, moe_w_down):
    b, s, d = x.shape
    depth = w_in.shape[0]
    per_layer = dict(norm_mix=norm_mix, w_in=w_in, ssm_lambda_re=ssm_lambda_re, ssm_lambda_im=ssm_lambda_im,
                     ssm_log_dt=ssm_log_dt, ssm_b_re=ssm_b_re, ssm_b_im=ssm_b_im, ssm_c_re=ssm_c_re,
                     ssm_c_im=ssm_c_im, ssm_d=ssm_d, ssm_w_glu=ssm_w_glu, conv_w_dw=conv_w_dw,
                     conv_b_dw=conv_b_dw, conv_ln_g=conv_ln_g, conv_ln_b=conv_ln_b, conv_w_pw=conv_w_pw,
                     nsa_q_norm=nsa_q_norm, nsa_k_norm=nsa_k_norm, nsa_cmp_pe=nsa_cmp_pe, nsa_cmp_w1=nsa_cmp_w1,
                     nsa_cmp_w2=nsa_cmp_w2, mix_out_norm=mix_out_norm, w_out=w_out)
    t5_tiles = _nsa_bias_tiles(t5_table, s)
    s5_perm = _s5_perm_tables()
    h = x
    for layer in range(depth):
        p = {k: v[layer] for k, v in per_layer.items()}
        h = _layer_mixers(h, p, t5_tiles, s5_perm)
        mk, mv = _memkv(mem, norm_mem[layer], x_w_kv[layer], x_k_norm[layer])
        h = _cross(h, norm_cross[layer], x_w_q[layer], x_q_norm[layer], mk, mv, x_w_o[layer])
        h2d = h.reshape(b * s, d)
        if layer % 2 == 0:
            h2d = _ffn(h2d, norm_ffn[layer], ffn_w_up[layer // 2], ffn_w_down[layer // 2])
        else:
            h2d = _moe(h2d, norm_ffn[layer], moe_router[layer // 2], moe_w_up[layer // 2], moe_w_down[layer // 2])
        h = h2d.reshape(b, s, d)
    return h
```

```python
import functools
import math

import jax
import jax.numpy as jnp
import numpy as np
from jax import lax
from jax.experimental import pallas as pl
from jax.experimental.pallas import tpu as pltpu

F32 = jnp.float32
BF16 = jnp.bfloat16

D_MODEL = 1024
HEAD_DIM = 64
SSM_WIDTH = 256
SSM_GROUP = 16
SSM_GROUPS = 16
SSM_STATE = 64
SSM_CHUNK = 16
CONV_WIDTH = 256
CONV_K = 31
CONV_HALO = 32
NSA_WIDTH = 512
NSA_HEADS = 8
NSA_KV_HEADS = 2
NSA_GROUP = 4
N_BRANCH = 3
L_CMP = 32
CMP_STRIDE = 16
L_SEL = 64
N_SELECT = 16
N_LOCAL = 2
WINDOW = 512
SEL_FORCE = 1e6
T5_BUCKETS = 32
T5_MAX_DIST = 128
X_HEADS = 4
X_WIDTH = 256
D_FF = 2816
N_EXPERTS = 8
TOP_K = 2
EPS = 1e-6
NEG = -1e30
LOG2E = math.log2(math.e)

TQ = 256
SEL_PAD = 256
FF_CHUNK = 256
MOE_TB = 2048
MOE_TS = 256
MOE_SUB = 64
MOE_WIN = 192

_VMEM_LIMIT = 56 * 1024 * 1024


def _cparams(sem, vmem=None):
    return pltpu.CompilerParams(dimension_semantics=sem, vmem_limit_bytes=vmem)


def _rms_rows(x):
    return x * lax.rsqrt(jnp.mean(x * x, axis=-1, keepdims=True) + EPS)


def _proj_kernel(x_ref, g_ref, wtok_ref, wt_ref, ztok_ref, zt_ref):
    xn = (_rms_rows(x_ref[0]) * g_ref[...]).astype(BF16)
    ztok_ref[0] = jnp.dot(xn, wtok_ref[...], preferred_element_type=F32)
    zt_ref[0] = lax.dot_general(wt_ref[...], xn, (((1,), (1,)), ((), ())), preferred_element_type=F32)


def _proj(h, gain, w_tok, w_t, tm=1024):
    b, s, d = h.shape
    ntok, nt = w_tok.shape[1], w_t.shape[0]
    return pl.pallas_call(
        _proj_kernel,
        grid=(b, s // tm),
        in_specs=[pl.BlockSpec((1, tm, d), lambda i, j: (i, j, 0)),
                  pl.BlockSpec((1, d), lambda i, j: (0, 0)),
                  pl.BlockSpec((d, ntok), lambda i, j: (0, 0)),
                  pl.BlockSpec((nt, d), lambda i, j: (0, 0))],
        out_specs=[pl.BlockSpec((1, tm, ntok), lambda i, j: (i, j, 0)),
                   pl.BlockSpec((1, nt, tm), lambda i, j: (i, 0, j))],
        out_shape=[jax.ShapeDtypeStruct((b, s, ntok), F32), jax.ShapeDtypeStruct((b, nt, s), F32)],
        compiler_params=_cparams(("parallel", "parallel"), _VMEM_LIMIT),
        name="proj",
    )(h, gain.reshape(1, d), w_tok, w_t)


def _s5_tables(lam_re, lam_im, log_dt, b_re, b_im, c_re, c_im, d_skip):
    L, H, P = SSM_CHUNK, SSM_GROUP, SSM_STATE
    dt = jnp.exp(log_dt.astype(F32))[:, None]
    lr, li = lam_re.astype(F32), lam_im.astype(F32)
    mag = jnp.exp(lr * dt)
    ar, ai = mag * jnp.cos(li * dt), mag * jnp.sin(li * dt)
    den = lr * lr + li * li
    fr = ((ar - 1.0) * lr + ai * li) / den
    fi = (ai * lr - (ar - 1.0) * li) / den
    bbr = fr[..., None] * b_re - fi[..., None] * b_im
    bbi = fr[..., None] * b_im + fi[..., None] * b_re
    j = jnp.arange(L + 1, dtype=F32)[:, None, None]
    pmag = jnp.exp(lr[None] * dt[None] * j)
    pr, pi = pmag * jnp.cos(li[None] * dt[None] * j), pmag * jnp.sin(li[None] * dt[None] * j)
    cbr = c_re[:, :, :, None] * bbr[:, None, :, :] - c_im[:, :, :, None] * bbi[:, None, :, :]
    cbi = c_re[:, :, :, None] * bbi[:, None, :, :] + c_im[:, :, :, None] * bbr[:, None, :, :]
    hp = lax.Precision.HIGHEST
    kj = (jnp.einsum('jgp,ghpk->jghk', pr[:L], cbr, precision=hp)
          - jnp.einsum('jgp,ghpk->jghk', pi[:L], cbi, precision=hp))
    lag = np.arange(L)[None, :] - np.arange(L)[:, None]
    place = (lag[None] == np.arange(L)[:, None, None]).astype(np.float32)
    kt = jnp.einsum('jab,jghk->abghk', place, kj, precision=hp)
    kt = kt + (jnp.eye(L)[:, :, None, None, None] * (jnp.eye(H)[None, None, None] * d_skip[None, None, :, :, None]))
    tmat = kt.transpose(2, 0, 4, 1, 3).reshape(SSM_GROUPS, L * H, L * H)
    qr, qi = pr[:L][::-1], pi[:L][::-1]
    wre = qr[..., None] * bbr[None] - qi[..., None] * bbi[None]
    wim = qr[..., None] * bbi[None] + qi[..., None] * bbr[None]
    wre = wre.transpose(1, 0, 3, 2).reshape(SSM_GROUPS, L * H, P)
    wim = wim.transpose(1, 0, 3, 2).reshape(SSM_GROUPS, L * H, P)
    w1 = jnp.concatenate([wre, wim], axis=-1)
    w2 = jnp.concatenate([wim, wre], axis=-1)
    sr, si = pr[1:], pi[1:]
    vr = c_re[None] * sr[:, :, None, :] - c_im[None] * si[:, :, None, :]
    vi = c_re[None] * si[:, :, None, :] + c_im[None] * sr[:, :, None, :]
    vmat = jnp.concatenate([vr, -vi], axis=-1).transpose(1, 3, 0, 2).reshape(SSM_GROUPS, 2 * P, L * H)
    a_r, a_i = pr[L], pi[L]
    am = jnp.stack([jnp.concatenate([a_r, a_r], -1), jnp.concatenate([-a_i, a_i], -1),
                    jnp.concatenate([a_i, -a_i], -1)], axis=1)
    am = jnp.concatenate([am, jnp.zeros((SSM_GROUPS, 5, 2 * P), F32)], axis=1)
    return tmat.astype(BF16), w1.astype(BF16), w2.astype(BF16), vmat.astype(BF16), am


def _s5_perm_tables():
    L, G, H = SSM_CHUNK, SSM_GROUPS, SSM_GROUP
    i = jnp.arange(L * SSM_WIDTH)
    ti, gi, hi = i // SSM_WIDTH, (i // H) % G, i % H
    o = jnp.arange(L * H)
    pack = ((gi[None, :, None] == jnp.arange(G)[:, None, None]) & (ti[None, :, None] == (o // H)[None, None, :])
            & (hi[None, :, None] == (o % H)[None, None, :]))
    j = jnp.arange(G * L * H)
    gj, tj, hj = j // (L * H), (j // H) % L, j % H
    w = jnp.arange(SSM_WIDTH)
    unpack = ((tj[None, :, None] == jnp.arange(L)[:, None, None]) & (gj[None, :, None] == (w // H)[None, None, :])
              & (hj[None, :, None] == (w % H)[None, None, :]))
    return pack.astype(BF16), unpack.astype(BF16)


def _s5_pack_kernel(ulo_ref, uhi_ref, p_ref, x_ref, u2_ref, *, n_chunks):
    half = SSM_WIDTH // 2

    @pl.when(pl.program_id(1) == 0)
    def _():
        for t in range(SSM_CHUNK):
            for k, u_ref in enumerate((ulo_ref, uhi_ref)):
                u2_ref[:, t * SSM_WIDTH + k * half:t * SSM_WIDTH + (k + 1) * half] = (
                    u_ref[0, pl.ds(t, n_chunks, stride=SSM_CHUNK), :].astype(BF16))

    x_ref[0] = jnp.dot(u2_ref[...], p_ref[0], preferred_element_type=F32).astype(x_ref.dtype)


def _s5_pack(ztok, col, pack):
    b, s, _ = ztok.shape
    n_chunks = s // SSM_CHUNK
    lw, lh = SSM_CHUNK * SSM_WIDTH, SSM_CHUNK * SSM_GROUP
    half = SSM_WIDTH // 2
    return pl.pallas_call(
        functools.partial(_s5_pack_kernel, n_chunks=n_chunks),
        grid=(b, SSM_GROUPS),
        in_specs=[pl.BlockSpec((1, s, half), lambda i, g: (i, 0, col // half)),
                  pl.BlockSpec((1, s, half), lambda i, g: (i, 0, col // half + 1)),
                  pl.BlockSpec((1, lw, lh), lambda i, g: (g, 0, 0))],
        out_specs=pl.BlockSpec((1, n_chunks, lh), lambda i, g: (g, i, 0)),
        out_shape=jax.ShapeDtypeStruct((SSM_GROUPS, b * n_chunks, lh), BF16),
        scratch_shapes=[pltpu.VMEM((n_chunks, lw), BF16)],
        compiler_params=_cparams(("parallel", "arbitrary"), _VMEM_LIMIT),
        name="s5_pack",
    )(ztok, ztok, pack)


def _s5_unpack_kernel(g_ref, r_ref, o_ref, *, n_chunks):
    t = pl.program_id(1)
    rows = jnp.concatenate([g_ref[g] for g in range(SSM_GROUPS)], axis=1)
    y = jnp.dot(rows, r_ref[0], preferred_element_type=F32)
    half = SSM_WIDTH // 2
    for k in range(SSM_CHUNK):
        @pl.when(t == k)
        def _():
            for part in range(2):
                o_ref[part, 0, pl.ds(k, n_chunks, stride=SSM_CHUNK), :] = y[:, part * half:(part + 1) * half]


def _s5_unpack(gy, unpack, bsz):
    g, r, lh = gy.shape
    n_chunks = r // bsz
    half = SSM_WIDTH // 2
    return pl.pallas_call(
        functools.partial(_s5_unpack_kernel, n_chunks=n_chunks),
        grid=(bsz, SSM_CHUNK),
        in_specs=[pl.BlockSpec((g, n_chunks, lh), lambda i, t: (0, i, 0)),
                  pl.BlockSpec((1, g * lh, SSM_WIDTH), lambda i, t: (t, 0, 0))],
        out_specs=pl.BlockSpec((2, 1, n_chunks * SSM_CHUNK, half), lambda i, t: (0, i, 0, 0)),
        out_shape=jax.ShapeDtypeStruct((2, bsz, n_chunks * SSM_CHUNK, half), F32),
        compiler_params=_cparams(("parallel", "arbitrary"), _VMEM_LIMIT),
        name="s5_unpack",
    )(gy, unpack)


def _s5_kernel(x_ref, t_ref, w1_ref, w2_ref, v_ref, a_ref, o_ref, s1_ref, s2_ref, xin_ref, *, bsz, n_chunks):
    x = x_ref[0]
    s1_ref[...] = jnp.dot(x, w1_ref[0], preferred_element_type=F32)
    s2_ref[...] = jnp.dot(x, w2_ref[0], preferred_element_type=F32)
    a1, a2, a3 = a_ref[0, 0:1, :], a_ref[0, 1:2, :], a_ref[0, 2:3, :]

    def step(c, carry):
        ps, qs = carry
        new_p, new_q = [], []
        for bi in range(bsz):
            row = pl.ds(bi * n_chunks + c, 1)
            xin_ref[row, :] = ps[bi]
            new_p.append(ps[bi] * a1 + qs[bi] * a2 + s1_ref[row, :])
            new_q.append(qs[bi] * a1 + ps[bi] * a3 + s2_ref[row, :])
        return tuple(new_p), tuple(new_q)

    zero = tuple(jnp.zeros((1, 2 * SSM_STATE), F32) for _ in range(bsz))
    lax.fori_loop(0, n_chunks, step, (zero, zero))
    y = (jnp.dot(x, t_ref[0], preferred_element_type=F32)
         + jnp.dot(xin_ref[...].astype(BF16), v_ref[0], preferred_element_type=F32))
    o_ref[0] = jax.nn.gelu(y).astype(o_ref.dtype)


def _s5(xg, tables, bsz):
    tmat, w1, w2, vmat, am = tables
    g, r, lh = xg.shape
    p2 = 2 * SSM_STATE
    kern = functools.partial(_s5_kernel, bsz=bsz, n_chunks=r // bsz)
    return pl.pallas_call(
        kern,
        grid=(g,),
        in_specs=[pl.BlockSpec((1, r, lh), lambda i: (i, 0, 0)),
                  pl.BlockSpec((1, lh, lh), lambda i: (i, 0, 0)),
                  pl.BlockSpec((1, lh, p2), lambda i: (i, 0, 0)),
                  pl.BlockSpec((1, lh, p2), lambda i: (i, 0, 0)),
                  pl.BlockSpec((1, p2, lh), lambda i: (i, 0, 0)),
                  pl.BlockSpec((1, 8, p2), lambda i: (i, 0, 0))],
        out_specs=pl.BlockSpec((1, r, lh), lambda i: (i, 0, 0)),
        out_shape=jax.ShapeDtypeStruct((g, r, lh), BF16),
        scratch_shapes=[pltpu.VMEM((r, p2), F32), pltpu.VMEM((r, p2), F32), pltpu.VMEM((r, p2), F32)],
        compiler_params=_cparams(("parallel",), _VMEM_LIMIT),
        name="s5_scan",
    )(xg, tmat, w1, w2, vmat, am)


def _conv_kernel(z_ref, halo_ref, wdw_ref, bdw_ref, lng_ref, lnb_ref, wpw_ref, go_ref, o_ref, buf_ref, sh_ref, *, tt):
    first = pl.program_id(1) == 0
    zc = z_ref[0]
    zh = halo_ref[0]
    vh = zh[:, :CONV_WIDTH] * jax.nn.sigmoid(zh[:, CONV_WIDTH:])
    buf_ref[0:CONV_HALO, :] = vh * jnp.where(first, 0.0, 1.0)
    buf_ref[CONV_HALO:CONV_HALO + tt, :] = zc[:, :CONV_WIDTH] * jax.nn.sigmoid(zc[:, CONV_WIDTH:])
    for r in range(1, 8):
        sh_ref[r, 0:tt + CONV_HALO - 8, :] = buf_ref[pl.ds(r, tt + CONV_HALO - 8), :]
    acc = jnp.zeros((tt, CONV_WIDTH), F32) + bdw_ref[...]
    for k in range(CONV_K):
        off = CONV_HALO - (CONV_K - 1) + k
        r, a = off % 8, off - off % 8
        rows = buf_ref[pl.ds(a, tt), :] if r == 0 else sh_ref[r, pl.ds(a, tt), :]
        acc = acc + wdw_ref[k:k + 1, :] * rows
    mu = jnp.mean(acc, axis=-1, keepdims=True)
    var = jnp.mean(jnp.square(acc - mu), axis=-1, keepdims=True)
    y = (acc - mu) * lax.rsqrt(var + EPS) * lng_ref[...] + lnb_ref[...]
    y = jax.nn.silu(y)
    y = jnp.dot(y.astype(BF16), wpw_ref[...], preferred_element_type=F32)
    o_ref[0] = (_rms_rows(y) * go_ref[...]).astype(o_ref.dtype)


def _conv(ztok, w_dw, b_dw, ln_g, ln_b, w_pw, g_out, tt=512):
    b, s, _ = ztok.shape
    cw = CONV_WIDTH
    hb = tt // CONV_HALO
    kern = functools.partial(_conv_kernel, tt=tt)
    row = lambda v: v.reshape(1, cw)
    return pl.pallas_call(
        kern,
        grid=(b, s // tt),
        in_specs=[pl.BlockSpec((1, tt, 2 * cw), lambda i, j: (i, j, 0)),
                  pl.BlockSpec((1, CONV_HALO, 2 * cw), lambda i, j: (i, jnp.maximum(j * hb - 1, 0), 0)),
                  pl.BlockSpec((CONV_K + 1, cw), lambda i, j: (0, 0)),
                  pl.BlockSpec((1, cw), lambda i, j: (0, 0)),
                  pl.BlockSpec((1, cw), lambda i, j: (0, 0)),
                  pl.BlockSpec((1, cw), lambda i, j: (0, 0)),
                  pl.BlockSpec((cw, cw), lambda i, j: (0, 0)),
                  pl.BlockSpec((1, cw), lambda i, j: (0, 0))],
        out_specs=pl.BlockSpec((1, tt, cw), lambda i, j: (i, j, 0)),
        out_shape=jax.ShapeDtypeStruct((b, s, cw), BF16),
        scratch_shapes=[pltpu.VMEM((CONV_HALO + tt, cw), F32), pltpu.VMEM((8, CONV_HALO + tt, cw), F32)],
        compiler_params=_cparams(("parallel", "arbitrary")),
        name="conv_mixer",
    )(ztok, ztok, jnp.concatenate([w_dw, jnp.zeros((1, cw), F32)], 0), row(b_dw), row(ln_g), row(ln_b),
      w_pw.astype(BF16), row(g_out))


def _knorm_kernel(ks_ref, kw_ref, gs_ref, gw_ref, os_ref, ow_ref):
    for src, g_ref, dst in ((ks_ref, gs_ref, os_ref), (kw_ref, gw_ref, ow_ref)):
        x = src[0]
        for h in range(NSA_KV_HEADS):
            xh = x[:, h * HEAD_DIM:(h + 1) * HEAD_DIM]
            dst[0, h] = (_rms_rows(xh) * g_ref[...]).astype(dst.dtype)


def _knorm(ztok, col_s, col_w, gain_s, gain_w, tt=512):
    b, s, _ = ztok.shape
    kw = NSA_KV_HEADS * HEAD_DIM
    out = jax.ShapeDtypeStruct((b, NSA_KV_HEADS, s, HEAD_DIM), BF16)
    ospec = pl.BlockSpec((1, NSA_KV_HEADS, tt, HEAD_DIM), lambda i, j: (i, 0, j, 0))
    return pl.pallas_call(
        _knorm_kernel,
        grid=(b, s // tt),
        in_specs=[pl.BlockSpec((1, tt, kw), lambda i, j: (i, j, col_s // kw)),
                  pl.BlockSpec((1, tt, kw), lambda i, j: (i, j, col_w // kw)),
                  pl.BlockSpec((1, HEAD_DIM), lambda i, j: (0, 0)),
                  pl.BlockSpec((1, HEAD_DIM), lambda i, j: (0, 0))],
        out_specs=[ospec, ospec],
        out_shape=[out, out],
        compiler_params=_cparams(("parallel", "parallel")),
        name="nsa_key_norm",
    )(ztok, ztok, gain_s.reshape(1, HEAD_DIM), gain_w.reshape(1, HEAD_DIM))


def _compress_kernel(k_ref, v_ref, wka_ref, wkb_ref, ck_ref, w2k_ref, gk_ref,
                     wva_ref, wvb_ref, cv_ref, w2v_ref, ko_ref, vo_ref):
    hi = lax.Precision.HIGHEST
    n = k_ref.shape[1] // CMP_STRIDE
    kvw = k_ref.shape[2]
    nt = (((1,), (1,)), ((), ()))
    a, bm = jnp.zeros((n, kvw), F32), jnp.zeros((n, kvw), F32)
    at, bt = jnp.zeros((kvw, n), F32), jnp.zeros((kvw, n), F32)
    for l in range(CMP_STRIDE):
        kl = k_ref[0, pl.ds(l, n, stride=CMP_STRIDE), :]
        vl = v_ref[0, pl.ds(l, n, stride=CMP_STRIDE), :]
        a = a + jnp.dot(kl, wka_ref[l], precision=hi, preferred_element_type=F32)
        bm = bm + jnp.dot(kl, wkb_ref[l], precision=hi, preferred_element_type=F32)
        at = at + lax.dot_general(wva_ref[l], vl, nt, precision=hi, preferred_element_type=F32)
        bt = bt + lax.dot_general(wvb_ref[l], vl, nt, precision=hi, preferred_element_type=F32)
    pre = a + pltpu.roll(bm, n - 1, 0) + ck_ref[...]
    kc = jnp.dot(jax.nn.gelu(pre), w2k_ref[...], precision=hi, preferred_element_type=F32)
    for h in range(NSA_KV_HEADS):
        kh = kc[:, h * HEAD_DIM:(h + 1) * HEAD_DIM]
        ko_ref[0, h] = (_rms_rows(kh) * gk_ref[...]).astype(ko_ref.dtype)
    pre_t = at + pltpu.roll(bt, n - 1, 1) + cv_ref[...]
    vt = jnp.dot(w2v_ref[...], jax.nn.gelu(pre_t), precision=hi, preferred_element_type=F32)
    for h in range(NSA_KV_HEADS):
        vo_ref[0, h] = vt[h * HEAD_DIM:(h + 1) * HEAD_DIM, :].astype(vo_ref.dtype)


def _blockdiag2(w):
    z = jnp.zeros_like(w)
    return jnp.concatenate([jnp.concatenate([w, z], 1), jnp.concatenate([z, w], 1)], 0)


def _compress(ztok, col_k, col_v, pe, w1, w2, k_gain):
    b, s, _ = ztok.shape
    n = s // CMP_STRIDE
    hd, kvw = HEAD_DIM, NSA_KV_HEADS * HEAD_DIM
    hp = lax.Precision.HIGHEST

    def expand(w):
        wl = w.reshape(L_CMP, hd, hd)
        e = wl[:, None, :, None, :] * jnp.eye(NSA_KV_HEADS, dtype=F32)[None, :, None, :, None]
        e = e.reshape(L_CMP, kvw, kvw)
        return e[:CMP_STRIDE], e[CMP_STRIDE:]

    wka, wkb = expand(w1[0])
    wva, wvb = expand(w1[1])
    ck = jnp.tile(jnp.dot(pe[0].reshape(1, L_CMP * hd), w1[0], precision=hp), (1, NSA_KV_HEADS))
    cv = jnp.tile(jnp.dot(pe[1].reshape(1, L_CMP * hd), w1[1], precision=hp), (1, NSA_KV_HEADS)).T
    full = lambda shape: pl.BlockSpec(shape, lambda i: tuple(0 for _ in shape))
    return pl.pallas_call(
        _compress_kernel,
        grid=(b,),
        in_specs=[pl.BlockSpec((1, s, kvw), lambda i: (i, 0, col_k // kvw)),
                  pl.BlockSpec((1, s, kvw), lambda i: (i, 0, col_v // kvw)),
                  full((CMP_STRIDE, kvw, kvw)), full((CMP_STRIDE, kvw, kvw)), full((1, kvw)), full((kvw, kvw)),
                  full((1, hd)),
                  full((CMP_STRIDE, kvw, kvw)), full((CMP_STRIDE, kvw, kvw)), full((kvw, 1)), full((kvw, kvw))],
        out_specs=[pl.BlockSpec((1, NSA_KV_HEADS, n, hd), lambda i: (i, 0, 0, 0)),
                   pl.BlockSpec((1, NSA_KV_HEADS, hd, n), lambda i: (i, 0, 0, 0))],
        out_shape=[jax.ShapeDtypeStruct((b, NSA_KV_HEADS, n, hd), BF16),
                   jax.ShapeDtypeStruct((b, NSA_KV_HEADS, hd, n), BF16)],
        compiler_params=_cparams(("parallel",), _VMEM_LIMIT),
        name="nsa_compress",
    )(ztok, ztok, wka, wkb, ck, _blockdiag2(w2[0]), k_gain.reshape(1, hd),
      wva.transpose(0, 2, 1), wvb.transpose(0, 2, 1), cv, _blockdiag2(w2[1]).T)


def _t5_bias_by_dist(t5_table):
    n = np.arange(T5_MAX_DIST + 1)
    max_exact = T5_BUCKETS // 2
    nf = np.maximum(n, 1).astype(np.float32)
    large = max_exact + (np.log(nf / np.float32(max_exact)) / np.float32(math.log(T5_MAX_DIST / max_exact))
                         * np.float32(T5_BUCKETS - max_exact)).astype(np.int32)
    large = np.minimum(large, T5_BUCKETS - 1)
    bucket = np.where(n < max_exact, n, large)
    onehot = (bucket[:, None] == np.arange(T5_BUCKETS)[None, :]).astype(np.float32)
    return jnp.dot(onehot, t5_table, precision=lax.Precision.HIGHEST)


def _bias_tile(fdt, rows, stride, dist00, d_max=None):
    heads = fdt.shape[0]
    a0 = stride * (rows - 1)
    d_lo = dist00 - a0
    length = a0 + TQ
    d_hi = d_lo + length
    d_max = d_hi if d_max is None else d_max
    pieces = []
    for lo, hi, kind in ((d_lo, min(d_hi, 0), 'neg'), (max(d_lo, 0), min(d_hi, T5_MAX_DIST), 'tab'),
                         (max(d_lo, T5_MAX_DIST), min(d_hi, d_max), 'far'), (max(d_lo, d_max), d_hi, 'neg')):
        if hi > lo:
            pieces.append(fdt[:, lo:hi] if kind == 'tab'
                          else jnp.full((heads, hi - lo), NEG if kind == 'neg' else 0.0, F32))
    vec = jnp.concatenate(pieces, axis=1)
    c0 = -(-a0 // 128) * 128
    width = -(-(c0 + TQ) // 128) * 128
    vec = jnp.pad(vec, ((0, 0), (c0 - a0, width - (c0 - a0) - length)))

    def kern(v_ref, o_ref):
        x = jnp.broadcast_to(v_ref[0], (rows, width))
        o_ref[0] = pltpu.roll(x, 0, 1, stride=stride, stride_axis=0)[:, c0:c0 + TQ]

    return pl.pallas_call(
        kern,
        grid=(heads,),
        in_specs=[pl.BlockSpec((1, 1, width), lambda h: (h, 0, 0))],
        out_specs=pl.BlockSpec((1, rows, TQ), lambda h: (h, 0, 0)),
        out_shape=jax.ShapeDtypeStruct((heads, rows, TQ), F32),
        compiler_params=_cparams(("parallel",)),
        name="toeplitz_bias",
    )(vec.reshape(heads, 1, width))


def _nsa_bias_tiles(t5_table, seq):
    fd = _t5_bias_by_dist(t5_table).astype(F32)
    fdt = ((fd - fd[T5_MAX_DIST:]) * LOG2E).T
    n_cmp = seq // CMP_STRIDE
    qt = TQ // CMP_STRIDE
    r0 = n_cmp - qt
    band = _bias_tile(fdt, 2 * qt, CMP_STRIDE, CMP_STRIDE * qt - (L_CMP - 1))
    heads = fdt.shape[0]
    cmp_t = jnp.concatenate([jnp.zeros((heads, r0 - qt, TQ), F32), band,
                             jnp.full((heads, n_cmp - qt, TQ), NEG, F32)], axis=1)
    sel_t = _bias_tile(fdt, SEL_PAD + TQ, 1, SEL_PAD)
    win = _bias_tile(fdt, WINDOW + TQ, 1, WINDOW, d_max=WINDOW)
    rw = np.arange(WINDOW + TQ)[None, :, None]
    win_t = jnp.stack([jnp.where(rw >= WINDOW - q0, win, NEG) for q0 in (0, TQ, 2 * TQ)])
    split = lambda t: t.reshape(*t.shape[:-3], NSA_KV_HEADS, NSA_GROUP, *t.shape[-2:])

    def wide(t):
        t = jnp.swapaxes(split(t), -3, -2)
        return t.reshape(*t.shape[:-2], NSA_GROUP * TQ)

    return wide(cmp_t), wide(sel_t), wide(win_t)


def _q_head(qt_ref, g, qg_ref):
    q = qt_ref[0, g * HEAD_DIM:(g + 1) * HEAD_DIM, :]
    inv = lax.rsqrt(jnp.mean(q * q, axis=0, keepdims=True) + EPS)
    return (q * inv * qg_ref[...]).astype(BF16)


def _nsa_cmp_kernel(qt_ref, qg_ref, kc_ref, vct_ref, bias_ref, ov_ref, oc_ref, sel_ref, imp_ref, *, n_cmp, n_sel):
    qi = pl.program_id(1)
    qt = TQ // CMP_STRIDE
    gw = NSA_GROUP * HEAD_DIM
    kvs = range(NSA_KV_HEADS)
    row0 = pl.multiple_of((n_cmp - qt) - qi * qt, qt)
    qw = [jnp.concatenate([_q_head(qt_ref, kv * NSA_GROUP + g, qg_ref) for g in range(NSA_GROUP)], axis=1)
          for kv in kvs]

    def attend(n):
        for kv in kvs:
            s = (jnp.dot(kc_ref[0, kv, 0:n, :], qw[kv], preferred_element_type=F32)
                 + bias_ref[kv, pl.ds(row0, n), :])
            m = jnp.max(s, axis=0, keepdims=True)
            m = jnp.where(m < 0.5 * NEG, 0.0, m)
            p = jnp.exp2(s - m)
            p = p * (1.0 / jnp.maximum(jnp.sum(p, axis=0, keepdims=True), 1e-30))
            oc = jnp.dot(vct_ref[0, kv, :, 0:n], p.astype(BF16), preferred_element_type=F32)
            psum = jnp.zeros((n, TQ), F32)
            for g in range(NSA_GROUP):
                oc_ref[0, kv * gw + g * HEAD_DIM:kv * gw + (g + 1) * HEAD_DIM, :] = oc[:, g * TQ:(g + 1) * TQ]
                psum = psum + p[:, g * TQ:(g + 1) * TQ]
            hi = psum.astype(BF16)
            lo = (psum - hi.astype(F32)).astype(BF16)
            imp_ref[kv] = (jnp.dot(ov_ref[:, 0:n], hi, preferred_element_type=F32)
                           + jnp.dot(ov_ref[:, 0:n], lo, preferred_element_type=F32))

    chunk = min(n_cmp, 128)
    n_chunks = n_cmp // chunk
    need = lax.div((qi + 1) * qt + (chunk - 1), chunk)
    for c in range(1, n_chunks + 1):
        pl.when(need == c)(functools.partial(attend, c * chunk))

    blk = lax.broadcasted_iota(jnp.int32, (n_sel, TQ), 0)
    blk_t = lax.shift_right_logical(qi * TQ + lax.broadcasted_iota(jnp.int32, (n_sel, TQ), 1), L_SEL.bit_length() - 1)
    forced = (blk == 0) | (blk > blk_t - N_LOCAL)
    v0 = tuple(jnp.where(blk > blk_t, -jnp.inf, jnp.where(forced, SEL_FORCE, imp_ref[kv])) for kv in kvs)

    def pick(_, vs):
        out = []
        for v in vs:
            m = jnp.max(v, axis=0, keepdims=True)
            first = jnp.min(jnp.where(v == m, blk, n_sel), axis=0, keepdims=True)
            out.append(jnp.where(blk == first, -jnp.inf, v))
        return tuple(out)

    vs = lax.fori_loop(0, min(N_SELECT, n_sel), pick, v0)
    for kv in kvs:
        sel_ref[0, kv] = jnp.where((vs[kv] == -jnp.inf) & (v0[kv] > -jnp.inf), 0.0, NEG)


def _nsa_cmp(zt, qg, kc, vct, bias_c, ovt):
    b, _, s = zt.shape
    n_cmp, n_sel = s // CMP_STRIDE, s // L_SEL
    kvh = NSA_KV_HEADS
    kern = functools.partial(_nsa_cmp_kernel, n_cmp=n_cmp, n_sel=n_sel)
    return pl.pallas_call(
        kern,
        grid=(b, s // TQ),
        in_specs=[pl.BlockSpec((1, NSA_WIDTH, TQ), lambda i, j: (i, 0, j)),
                  pl.BlockSpec((HEAD_DIM, TQ), lambda i, j: (0, 0)),
                  pl.BlockSpec((1, kvh, n_cmp, HEAD_DIM), lambda i, j: (i, 0, 0, 0)),
                  pl.BlockSpec((1, kvh, HEAD_DIM, n_cmp), lambda i, j: (i, 0, 0, 0)),
                  pl.BlockSpec((kvh, bias_c.shape[1], NSA_GROUP * TQ), lambda i, j: (0, 0, 0)),
                  pl.BlockSpec((n_sel, n_cmp), lambda i, j: (0, 0))],
        out_specs=[pl.BlockSpec((1, NSA_WIDTH, TQ), lambda i, j: (i, 0, j)),
                   pl.BlockSpec((1, kvh, n_sel, TQ), lambda i, j: (i, 0, 0, j))],
        out_shape=[jax.ShapeDtypeStruct((b, NSA_WIDTH, s), F32),
                   jax.ShapeDtypeStruct((b, kvh, n_sel, s), F32)],
        scratch_shapes=[pltpu.VMEM((kvh, n_sel, TQ), F32)],
        compiler_params=_cparams(("parallel", "parallel"), _VMEM_LIMIT),
        name="nsa_compressed_select",
    )(zt, qg, kc, vct, bias_c, ovt)


def _nsa_main_kernel(qt_ref, qg_ref, ks_ref, vst_ref, kw_ref, vwt_ref, sel_ref, bs_ref, bw_ref, gate_ref,
                     oc_ref, o_ref, acc_ref, s_ref, ot_ref):
    qi = pl.program_id(1)
    q0 = pl.multiple_of(qi * TQ, TQ)
    near = SEL_PAD + TQ
    gw = NSA_GROUP * HEAD_DIM
    kvs = range(NSA_KV_HEADS)

    def expand_sel(kv, first_blk, n_blk):
        rows = [jnp.broadcast_to(sel_ref[0, kv, pl.ds(first_blk + r, 1), :], (L_SEL, TQ)) for r in range(n_blk)]
        rows = jnp.concatenate(rows, axis=0)
        return jnp.concatenate([rows] * NSA_GROUP, axis=1)

    qw = [jnp.concatenate([_q_head(qt_ref, kv * NSA_GROUP + g, qg_ref) for g in range(NSA_GROUP)], axis=1)
          for kv in kvs]
    m0 = []
    for kv in kvs:
        s = (jnp.dot(ks_ref[0, kv, pl.ds(q0, near), :], qw[kv], preferred_element_type=F32) + bs_ref[kv]
             + expand_sel(kv, qi * (TQ // L_SEL), near // L_SEL))
        m = jnp.max(s, axis=0, keepdims=True)
        p = jnp.exp2(s - m).astype(BF16)
        acc_ref[kv] = jnp.dot(vst_ref[0, kv, :, pl.ds(q0, near)], p, preferred_element_type=F32)
        m0.append(m)

    for kv in kvs:
        s = jnp.dot(kw_ref[0, kv, pl.ds(q0, WINDOW + TQ), :], qw[kv], preferred_element_type=F32) + bw_ref[0, kv]
        p = jnp.exp2(s - jnp.max(s, axis=0, keepdims=True)).astype(BF16)
        ow = jnp.dot(vwt_ref[0, kv, :, pl.ds(q0, WINDOW + TQ)], p, preferred_element_type=F32)
        ow = ow[:HEAD_DIM] * (1.0 / jnp.maximum(ow[HEAD_DIM:HEAD_DIM + 1], 1e-30))
        for g in range(NSA_GROUP):
            gates = jax.nn.sigmoid(gate_ref[0, kv, g * N_BRANCH:(g + 1) * N_BRANCH, :])
            rows = slice(kv * gw + g * HEAD_DIM, kv * gw + (g + 1) * HEAD_DIM)
            ot_ref[rows, :] = gates[0:1] * oc_ref[0, rows, :] + gates[2:3] * ow[:, g * TQ:(g + 1) * TQ]

    def scores(c, slot):
        r0 = pl.multiple_of(c * TQ, TQ)
        mc = []
        for kv in kvs:
            s = (jnp.dot(ks_ref[0, kv, pl.ds(r0, TQ), :], qw[kv], preferred_element_type=F32)
                 + expand_sel(kv, c * (TQ // L_SEL), TQ // L_SEL))
            s_ref[slot, kv] = s
            mc.append(jnp.max(s, axis=0, keepdims=True))
        return tuple(mc)

    def consume(c, slot, m_old, mc):
        r0 = pl.multiple_of(c * TQ, TQ)
        m_out = []
        for kv in kvs:
            m_new = jnp.maximum(m_old[kv], mc[kv])
            alpha = jnp.exp2(m_old[kv] - m_new)
            p = jnp.exp2((s_ref[slot, kv] - m_new).astype(BF16))
            acc_ref[kv] = alpha * acc_ref[kv] + jnp.dot(vst_ref[0, kv, :, pl.ds(r0, TQ)], p,
                                                        preferred_element_type=F32)
            m_out.append(m_new)
        return tuple(m_out)

    first = SEL_PAD // TQ
    n_far = qi - first

    def quad(i, carry):
        m, mc = carry
        c = first + 4 * i
        mc1 = scores(c + 1, 1)
        m = consume(c, 0, m, mc)
        mc2 = scores(c + 2, 0)
        m = consume(c + 1, 1, m, mc1)
        mc3 = scores(c + 3, 1)
        m = consume(c + 2, 0, m, mc2)
        mc4 = scores(c + 4, 0)
        m = consume(c + 3, 1, m, mc3)
        return m, mc4

    m_far, mc_far = lax.fori_loop(0, n_far // 4, quad, (tuple(m0), scores(first, 0)))
    rem = n_far % 4
    c_rem = first + n_far - rem

    @pl.when((n_far > 0) & (rem == 1))
    def _():
        consume(c_rem, 0, m_far, mc_far)

    @pl.when((n_far > 0) & (rem == 2))
    def _():
        mc1 = scores(c_rem + 1, 1)
        m = consume(c_rem, 0, m_far, mc_far)
        consume(c_rem + 1, 1, m, mc1)

    @pl.when((n_far > 0) & (rem == 3))
    def _():
        mc1 = scores(c_rem + 1, 1)
        m = consume(c_rem, 0, m_far, mc_far)
        mc2 = scores(c_rem + 2, 0)
        m = consume(c_rem + 1, 1, m, mc1)
        consume(c_rem + 2, 0, m, mc2)

    for kv in kvs:
        os = acc_ref[kv]
        os = os[:HEAD_DIM] * (1.0 / jnp.maximum(os[HEAD_DIM:HEAD_DIM + 1], 1e-30))
        for g in range(NSA_GROUP):
            gate = jax.nn.sigmoid(gate_ref[0, kv, g * N_BRANCH + 1:g * N_BRANCH + 2, :])
            rows = slice(kv * gw + g * HEAD_DIM, kv * gw + (g + 1) * HEAD_DIM)
            ot_ref[rows, :] = ot_ref[rows, :] + gate * os[:, g * TQ:(g + 1) * TQ]
    o_ref[0] = ot_ref[...].T


def _nsa_main(zt, qg, ks_p, vst_p, kw_p, vwt_p, sel_p, bias_s, bias_w, gates_t, oc_t):
    b, _, s = zt.shape
    kvh = NSA_KV_HEADS
    sp, wp = ks_p.shape[2], kw_p.shape[2]
    nb, vr = sel_p.shape[2], vst_p.shape[2]
    once = pl.Buffered(1)
    return pl.pallas_call(
        _nsa_main_kernel,
        grid=(b, s // TQ),
        in_specs=[pl.BlockSpec((1, NSA_WIDTH, TQ), lambda i, j: (i, 0, j)),
                  pl.BlockSpec((HEAD_DIM, TQ), lambda i, j: (0, 0)),
                  pl.BlockSpec((1, kvh, sp, HEAD_DIM), lambda i, j: (i, 0, 0, 0), pipeline_mode=once),
                  pl.BlockSpec((1, kvh, vr, sp), lambda i, j: (i, 0, 0, 0), pipeline_mode=once),
                  pl.BlockSpec((1, kvh, wp, HEAD_DIM), lambda i, j: (i, 0, 0, 0), pipeline_mode=once),
                  pl.BlockSpec((1, kvh, vr, wp), lambda i, j: (i, 0, 0, 0), pipeline_mode=once),
                  pl.BlockSpec((1, kvh, nb, TQ), lambda i, j: (i, 0, 0, j)),
                  pl.BlockSpec((kvh, SEL_PAD + TQ, NSA_GROUP * TQ), lambda i, j: (0, 0, 0), pipeline_mode=once),
                  pl.BlockSpec((1, kvh, WINDOW + TQ, NSA_GROUP * TQ), lambda i, j: (jnp.minimum(j, 2), 0, 0, 0)),
                  pl.BlockSpec((1, kvh, 16, TQ), lambda i, j: (i, 0, 0, j)),
                  pl.BlockSpec((1, NSA_WIDTH, TQ), lambda i, j: (i, 0, j))],
        out_specs=pl.BlockSpec((1, TQ, NSA_WIDTH), lambda i, j: (i, j, 0)),
        out_shape=jax.ShapeDtypeStruct((b, s, NSA_WIDTH), F32),
        scratch_shapes=[pltpu.VMEM((kvh, vr, NSA_GROUP * TQ), F32),
                        pltpu.VMEM((2, kvh, TQ, NSA_GROUP * TQ), F32),
                        pltpu.VMEM((NSA_WIDTH, TQ), F32)],
        compiler_params=_cparams(("parallel", "arbitrary"), _VMEM_LIMIT),
        name="nsa_selected_window",
    )(zt, qg, ks_p, vst_p, kw_p, vwt_p, sel_p, bias_s, bias_w, gates_t, oc_t)


def _mixout_kernel(gy_ref, yc_ref, yn_ref, h_ref, wglu_ref, go_ref, wo_ref, o_ref):
    sw = SSM_WIDTH
    half = sw // 2
    ag = (jnp.dot(gy_ref[0, 0].astype(BF16), wglu_ref[0:half, :], preferred_element_type=F32)
          + jnp.dot(gy_ref[1, 0].astype(BF16), wglu_ref[half:, :], preferred_element_type=F32))
    ys = ag[:, :sw] * jax.nn.sigmoid(ag[:, sw:])
    ys = (_rms_rows(ys) * go_ref[:, 0:sw]).astype(BF16)
    yn = (_rms_rows(yn_ref[0]) * go_ref[:, 2 * sw:]).astype(BF16)
    out = (jnp.dot(ys, wo_ref[0:sw, :], preferred_element_type=F32)
           + jnp.dot(yc_ref[0], wo_ref[sw:2 * sw, :], preferred_element_type=F32)
           + jnp.dot(yn, wo_ref[2 * sw:, :], preferred_element_type=F32))
    o_ref[0] = h_ref[0] + out


def _mixout(gy, yc, yn, h, w_glu, g_out, w_out, tm=1024):
    b, s, d = h.shape
    tok = lambda w: pl.BlockSpec((1, tm, w), lambda i, j: (i, j, 0))
    return pl.pallas_call(
        _mixout_kernel,
        grid=(b, s // tm),
        in_specs=[pl.BlockSpec((2, 1, tm, SSM_WIDTH // 2), lambda i, j: (0, i, j, 0)),
                  tok(CONV_WIDTH), tok(NSA_WIDTH), tok(d),
                  pl.BlockSpec((SSM_WIDTH, 2 * SSM_WIDTH), lambda i, j: (0, 0)),
                  pl.BlockSpec((1, d), lambda i, j: (0, 0)),
                  pl.BlockSpec((d, d), lambda i, j: (0, 0))],
        out_specs=tok(d),
        out_shape=jax.ShapeDtypeStruct((b, s, d), F32),
        compiler_params=_cparams(("parallel", "parallel")),
        name="mix_out",
    )(gy, yc, yn, h, w_glu.astype(BF16), g_out.reshape(1, d), w_out.astype(BF16))


def _memkv_kernel(mem_ref, g_ref, w_ref, kg_ref, k_ref, v_ref):
    mn = (_rms_rows(mem_ref[0]) * g_ref[...]).astype(BF16)
    kv = jnp.dot(mn, w_ref[...], preferred_element_type=F32)
    for h in range(X_HEADS):
        cols = slice(h * HEAD_DIM, (h + 1) * HEAD_DIM)
        k_ref[0, :, cols] = (_rms_rows(kv[:, cols]) * kg_ref[...]).astype(k_ref.dtype)
    v_ref[0] = kv[:, X_WIDTH:].astype(v_ref.dtype)


def _memkv(mem, gain, w_kv, k_gain):
    b, m, d = mem.shape
    out = jax.ShapeDtypeStruct((b, m, X_WIDTH), BF16)
    return pl.pallas_call(
        _memkv_kernel,
        grid=(b,),
        in_specs=[pl.BlockSpec((1, m, d), lambda i: (i, 0, 0)),
                  pl.BlockSpec((1, d), lambda i: (0, 0)),
                  pl.BlockSpec((d, 2 * X_WIDTH), lambda i: (0, 0)),
                  pl.BlockSpec((1, HEAD_DIM), lambda i: (0, 0))],
        out_specs=[pl.BlockSpec((1, m, X_WIDTH), lambda i: (i, 0, 0))] * 2,
        out_shape=[out, out],
        compiler_params=_cparams(("parallel",)),
        name="cross_mem_kv",
    )(mem, gain.reshape(1, d), w_kv.astype(BF16), k_gain.reshape(1, HEAD_DIM))


def _cross_kernel(h_ref, g_ref, wq_ref, qg_ref, k_ref, v_ref, wo_ref, o_ref):
    h = h_ref[0]
    hn = (_rms_rows(h) * g_ref[...]).astype(BF16)
    q = jnp.dot(hn, wq_ref[...], preferred_element_type=F32)
    out = h
    for hd in range(X_HEADS):
        cols = slice(hd * HEAD_DIM, (hd + 1) * HEAD_DIM)
        qh = (_rms_rows(q[:, cols]) * qg_ref[...]).astype(BF16)
        s = lax.dot_general(qh, k_ref[0, :, cols], (((1,), (1,)), ((), ())), preferred_element_type=F32)
        p = jnp.exp(s - jnp.max(s, axis=-1, keepdims=True))
        p = p * (1.0 / jnp.sum(p, axis=-1, keepdims=True))
        o = jnp.dot(p.astype(BF16), v_ref[0, :, cols], preferred_element_type=F32)
        out = out + jnp.dot(o.astype(BF16), wo_ref[cols, :], preferred_element_type=F32)
    o_ref[0] = out


def _cross(h, gain, w_q, q_gain, k, v, w_o, tm=1024):
    b, s, d = h.shape
    m = k.shape[1]
    return pl.pallas_call(
        _cross_kernel,
        grid=(b, s // tm),
        in_specs=[pl.BlockSpec((1, tm, d), lambda i, j: (i, j, 0)),
                  pl.BlockSpec((1, d), lambda i, j: (0, 0)),
                  pl.BlockSpec((d, X_WIDTH), lambda i, j: (0, 0)),
                  pl.BlockSpec((1, HEAD_DIM), lambda i, j: (0, 0)),
                  pl.BlockSpec((1, m, X_WIDTH), lambda i, j: (i, 0, 0)),
                  pl.BlockSpec((1, m, X_WIDTH), lambda i, j: (i, 0, 0)),
                  pl.BlockSpec((X_WIDTH, d), lambda i, j: (0, 0))],
        out_specs=pl.BlockSpec((1, tm, d), lambda i, j: (i, j, 0)),
        out_shape=jax.ShapeDtypeStruct((b, s, d), F32),
        compiler_params=_cparams(("parallel", "parallel")),
        name="cross_attention",
    )(h, gain.reshape(1, d), w_q.astype(BF16), (q_gain * HEAD_DIM ** -0.5).reshape(1, HEAD_DIM), k, v,
      w_o.astype(BF16))


def _ffn_kernel(h_ref, g_ref, wg_ref, wv_ref, wd_ref, o_ref, xn_ref, acc_ref):
    f = pl.program_id(1)

    @pl.when(f == 0)
    def _():
        xn_ref[...] = (_rms_rows(h_ref[...]) * g_ref[...]).astype(BF16)
        acc_ref[...] = jnp.zeros_like(acc_ref)

    x = xn_ref[...]
    gate = jnp.dot(x, wg_ref[...], preferred_element_type=F32)
    val = jnp.dot(x, wv_ref[...], preferred_element_type=F32)
    act = (jax.nn.silu(gate) * val).astype(BF16)
    acc_ref[...] += jnp.dot(act, wd_ref[...], preferred_element_type=F32)

    @pl.when(f == pl.num_programs(1) - 1)
    def _():
        o_ref[...] = h_ref[...] + acc_ref[...]


def _ffn(h2d, gain, w_up, w_down, tm=2048):
    t, d = h2d.shape
    nf = D_FF // FF_CHUNK
    wb = w_up.astype(BF16)
    return pl.pallas_call(
        _ffn_kernel,
        grid=(t // tm, nf),
        in_specs=[pl.BlockSpec((tm, d), lambda i, f: (i, 0)),
                  pl.BlockSpec((1, d), lambda i, f: (0, 0)),
                  pl.BlockSpec((d, FF_CHUNK), lambda i, f: (0, f)),
                  pl.BlockSpec((d, FF_CHUNK), lambda i, f: (0, f + nf)),
                  pl.BlockSpec((FF_CHUNK, d), lambda i, f: (f, 0))],
        out_specs=pl.BlockSpec((tm, d), lambda i, f: (i, 0)),
        out_shape=jax.ShapeDtypeStruct((t, d), F32),
        scratch_shapes=[pltpu.VMEM((tm, d), BF16), pltpu.VMEM((tm, d), F32)],
        compiler_params=_cparams(("parallel", "arbitrary"), _VMEM_LIMIT),
        name="ffn_swiglu",
    )(h2d, gain.reshape(1, d), wb, wb, w_down.astype(BF16))


def _router_kernel(h_ref, g_ref, wr_ref, xn_ref, gate_ref, asg_ref):
    xn = _rms_rows(h_ref[...]) * g_ref[...]
    xn_ref[...] = xn.astype(BF16)
    logits = jnp.dot(xn, wr_ref[...], precision=lax.Precision.HIGHEST, preferred_element_type=F32)
    lane = lax.broadcasted_iota(jnp.int32, logits.shape, 1)
    lg = jnp.where(lane < N_EXPERTS, logits, -jnp.inf)
    m1 = jnp.max(lg, axis=-1, keepdims=True)
    i1 = jnp.min(jnp.where(lg == m1, lane, 128), axis=-1, keepdims=True)
    lg2 = jnp.where(lane == i1, -jnp.inf, lg)
    m2 = jnp.max(lg2, axis=-1, keepdims=True)
    i2 = jnp.min(jnp.where(lg2 == m2, lane, 128), axis=-1, keepdims=True)
    e = jnp.exp(m2 - m1)
    den = 1.0 + e
    gate_ref[...] = jnp.where(lane == i1, 1.0 / den, jnp.where(lane == i2, e / den, 0.0))
    asg_ref[...] = ((lane == i1) | (lane == i2)).astype(jnp.int32)


def _router(h2d, gain, w_router, tm=512):
    t, d = h2d.shape
    wr = jnp.concatenate([w_router, jnp.zeros((d, 128 - N_EXPERTS), F32)], axis=1)
    return pl.pallas_call(
        _router_kernel,
        grid=(t // tm,),
        in_specs=[pl.BlockSpec((tm, d), lambda i: (i, 0)),
                  pl.BlockSpec((1, d), lambda i: (0, 0)),
                  pl.BlockSpec((d, 128), lambda i: (0, 0))],
        out_specs=[pl.BlockSpec((tm, d), lambda i: (i, 0)),
                   pl.BlockSpec((tm, 128), lambda i: (i, 0)),
                   pl.BlockSpec((tm, 128), lambda i: (i, 0))],
        out_shape=[jax.ShapeDtypeStruct((t, d), BF16), jax.ShapeDtypeStruct((t, 128), F32),
                   jax.ShapeDtypeStruct((t, 128), jnp.int32)],
        compiler_params=_cparams(("parallel",)),
        name="moe_router",
    )(h2d, gain.reshape(1, d), wr)


def _moe_windows(rb, lo, hi, active):
    lo_l = jnp.clip(lo - rb * MOE_TB, 0, MOE_TB)
    hi_l = jnp.clip(hi - rb * MOE_TB, 0, MOE_TB)
    shift = MOE_SUB.bit_length() - 1
    w0 = jnp.minimum(lax.shift_left(lax.shift_right_logical(lo_l, shift), shift), MOE_TB - MOE_WIN)
    has = active & (hi_l > lo_l)
    w1 = jnp.minimum(w0 + MOE_WIN, MOE_TB - MOE_WIN)
    return ((w0, 0, has), (w1, w0 + MOE_WIN, has & (hi_l > w0 + MOE_WIN)))


def _moe_gather_kernel(rb_ref, lo_ref, hi_ref, first_ref, tgt_ref, x_ref, o_ref):
    e, j, slot = pl.program_id(0), pl.program_id(1), pl.program_id(2)
    rb = rb_ref[e, j, slot]

    @pl.when(first_ref[e, j, slot] == 1)
    def _():
        o_ref[...] = jnp.zeros_like(o_ref)

    active = (slot == 0) | (rb != rb_ref[e, j, 0])
    tgt = tgt_ref[0]
    n_sub = MOE_TB // MOE_TS
    for sub in range(n_sub):
        toks = slice(sub * MOE_TS, (sub + 1) * MOE_TS)
        for start, cutoff, needed in _moe_windows(rb, lo_ref[e, j * n_sub + sub], hi_ref[e, j * n_sub + sub], active):

            @pl.when(needed)
            def _():
                local = start + lax.broadcasted_iota(jnp.int32, (MOE_WIN, MOE_TS), 0)
                rows = jnp.where(local >= cutoff, rb * MOE_TB + local, -2)
                onehot = jnp.where(tgt[:, toks] == rows, 1.0, 0.0).astype(BF16)
                part = jnp.dot(onehot, x_ref[toks, :], preferred_element_type=F32)
                sl = pl.ds(pl.multiple_of(start, MOE_SUB), MOE_WIN)
                o_ref[sl, :] = o_ref[sl, :] + part.astype(o_ref.dtype)


def _moe_ffn_kernel(exp_ref, nused_ref, x_ref, wg_ref, wv_ref, wd_ref, o_ref, acc_ref):
    r, f = pl.program_id(0), pl.program_id(1)
    used = r < nused_ref[0]

    @pl.when(f == 0)
    def _():
        acc_ref[...] = jnp.zeros_like(acc_ref)

    @pl.when(used)
    def _():
        x = x_ref[...]
        gate = jnp.dot(x, wg_ref[0], preferred_element_type=F32)
        val = jnp.dot(x, wv_ref[0], preferred_element_type=F32)
        act = (jax.nn.silu(gate) * val).astype(BF16)
        acc_ref[...] += jnp.dot(act, wd_ref[0], preferred_element_type=F32)

    @pl.when(f == pl.num_programs(1) - 1)
    def _():
        o_ref[...] = acc_ref[...].astype(o_ref.dtype)
    del exp_ref


def _moe_scatter_kernel(rb_ref, lo_ref, hi_ref, tgt_ref, gate_ref, y_ref, h_ref, o_ref):
    j, e, slot = pl.program_id(0), pl.program_id(1), pl.program_id(2)
    rb = rb_ref[e, j, slot]

    @pl.when((e == 0) & (slot == 0))
    def _():
        o_ref[...] = h_ref[...]

    active = (slot == 0) | (rb != rb_ref[e, j, 0])
    n_sub = MOE_TB // MOE_TS
    for sub in range(n_sub):
        toks = slice(sub * MOE_TS, (sub + 1) * MOE_TS)
        for start, cutoff, needed in _moe_windows(rb, lo_ref[e, j * n_sub + sub], hi_ref[e, j * n_sub + sub], active):

            @pl.when(needed)
            def _():
                mine = lax.broadcasted_iota(jnp.int32, (MOE_TS, N_EXPERTS), 1) == e
                tgt = jnp.sum(jnp.where(mine, tgt_ref[toks, :], 0), axis=1, keepdims=True)
                gate = jnp.sum(jnp.where(mine, gate_ref[toks, :], 0.0), axis=1, keepdims=True)
                local = start + lax.broadcasted_iota(jnp.int32, (MOE_TS, MOE_WIN), 1)
                rows = jnp.where(local >= cutoff, rb * MOE_TB + local, -2)
                onehot = jnp.where(tgt == rows, 1.0, 0.0).astype(BF16)
                y = y_ref[pl.ds(pl.multiple_of(start, MOE_SUB), MOE_WIN), :]
                o_ref[toks, :] = o_ref[toks, :] + gate * jnp.dot(onehot, y, preferred_element_type=F32)


def _moe(h2d, gain, w_router, w_up, w_down):
    t, d = h2d.shape
    tb = MOE_TB
    nj = t // tb
    n_rb = (t * TOP_K) // tb + N_EXPERTS
    xn, gates, asg = _router(h2d, gain, w_router)
    asg = asg[:, :N_EXPERTS]
    gates = gates[:, :N_EXPERTS]
    cs = jnp.cumsum(asg, axis=0)
    rank = cs - asg
    counts = cs[-1]
    padded = (counts + tb - 1) // tb * tb
    pad_end = jnp.cumsum(padded)
    start_p = pad_end - padded
    tgt = jnp.where(asg == 1, start_p[None, :] + rank, -1).astype(jnp.int32)
    ts = MOE_TS
    cb = jnp.concatenate([jnp.zeros((1, N_EXPERTS), jnp.int32), cs[ts - 1::ts]], axis=0)
    lo = (start_p[None, :] + cb[:-1]).T.astype(jnp.int32)
    hi = (start_p[None, :] + cb[1:]).T.astype(jnp.int32)
    rb0 = lo[:, ::tb // ts] // tb
    rb1 = jnp.maximum(rb0, (hi[:, tb // ts - 1::tb // ts] - 1) // tb)
    rb = jnp.stack([rb0, rb1], axis=-1).astype(jnp.int32)
    flat = rb.reshape(-1)
    first = jnp.concatenate([jnp.ones((1,), jnp.int32), (flat[1:] != flat[:-1]).astype(jnp.int32)])
    first = first.reshape(N_EXPERTS, nj, 2)
    n_used = (pad_end[-1] // tb).astype(jnp.int32).reshape(1)
    blk_exp = jnp.minimum(jnp.searchsorted(pad_end, jnp.arange(n_rb) * tb, side='right'),
                          N_EXPERTS - 1).astype(jnp.int32)

    xs = pl.pallas_call(
        _moe_gather_kernel,
        grid_spec=pltpu.PrefetchScalarGridSpec(
            num_scalar_prefetch=4,
            grid=(N_EXPERTS, nj, 2),
            in_specs=[pl.BlockSpec((1, 1, tb), lambda e, j, s, *_: (e, 0, j)),
                      pl.BlockSpec((tb, d), lambda e, j, s, *_: (j, 0))],
            out_specs=pl.BlockSpec((tb, d), lambda e, j, s, rb_ref, *_: (rb_ref[e, j, s], 0))),
        out_shape=jax.ShapeDtypeStruct((n_rb * tb, d), BF16),
        compiler_params=_cparams(("arbitrary", "arbitrary", "arbitrary"), _VMEM_LIMIT),
        name="moe_gather",
    )(rb, lo, hi, first, tgt.T.reshape(N_EXPERTS, 1, t), xn)

    nf = D_FF // FF_CHUNK
    wub = w_up.astype(BF16)
    ys = pl.pallas_call(
        _moe_ffn_kernel,
        grid_spec=pltpu.PrefetchScalarGridSpec(
            num_scalar_prefetch=2,
            grid=(n_rb, nf),
            in_specs=[pl.BlockSpec((tb, d), lambda r, f, *_: (r, 0)),
                      pl.BlockSpec((1, d, FF_CHUNK), lambda r, f, ex, nu: (ex[r], 0, f)),
                      pl.BlockSpec((1, d, FF_CHUNK), lambda r, f, ex, nu: (ex[r], 0, f + nf)),
                      pl.BlockSpec((1, FF_CHUNK, d), lambda r, f, ex, nu: (ex[r], f, 0))],
            out_specs=pl.BlockSpec((tb, d), lambda r, f, *_: (r, 0)),
            scratch_shapes=[pltpu.VMEM((tb, d), F32)]),
        out_shape=jax.ShapeDtypeStruct((n_rb * tb, d), BF16),
        compiler_params=_cparams(("arbitrary", "arbitrary"), _VMEM_LIMIT),
        name="moe_expert_ffn",
    )(blk_exp, n_used, xs, wub, wub, w_down.astype(BF16))

    return pl.pallas_call(
        _moe_scatter_kernel,
        grid_spec=pltpu.PrefetchScalarGridSpec(
            num_scalar_prefetch=3,
            grid=(nj, N_EXPERTS, 2),
            in_specs=[pl.BlockSpec((tb, N_EXPERTS), lambda j, e, s, *_: (j, 0)),
                      pl.BlockSpec((tb, N_EXPERTS), lambda j, e, s, *_: (j, 0)),
                      pl.BlockSpec((tb, d), lambda j, e, s, rb_ref, *_: (rb_ref[e, j, s], 0)),
                      pl.BlockSpec((tb, d), lambda j, e, s, *_: (j, 0))],
            out_specs=pl.BlockSpec((tb, d), lambda j, e, s, *_: (j, 0))),
        out_shape=jax.ShapeDtypeStruct((t, d), F32),
        compiler_params=_cparams(("arbitrary", "arbitrary", "arbitrary"), _VMEM_LIMIT),
        name="moe_scatter",
    )(rb, lo, hi, tgt, gates, ys, h2d)


_COL_CONV, _COL_SSM, _COL_KC, _COL_VC, _COL_KS, _COL_KW = 0, 512, 768, 896, 1024, 1152
_ROW_Q, _ROW_VS, _ROW_VW, _ROW_G = 0, 512, 640, 768


def _split_w_in(w_in):
    kvw = NSA_KV_HEADS * HEAD_DIM
    cuts = np.cumsum([0, SSM_WIDTH, 2 * CONV_WIDTH, NSA_WIDTH] + [kvw] * 6 + [N_BRANCH * NSA_HEADS])
    seg = lambda i: w_in[:, cuts[i]:cuts[i + 1]]
    ssm, conv, q, k_c, v_c, k_s, v_s, k_w, v_w, gate = (seg(i) for i in range(10))
    w_tok = jnp.concatenate([conv, ssm, k_c, v_c, k_s, k_w], axis=1).astype(BF16)
    gate = jnp.concatenate([gate, jnp.zeros((w_in.shape[0], 8), F32)], axis=1)
    w_t = jnp.concatenate([q, v_s, v_w, gate], axis=1).T.astype(BF16)
    return w_tok, w_t


def _layer_mixers(h, p, t5_tiles, s5_perm):
    b, s, d = h.shape
    w_tok, w_t = _split_w_in(p['w_in'])
    k_norm = p['nsa_k_norm']
    ztok, zt = _proj(h, p['norm_mix'], w_tok, w_t)
    ks, kw = _knorm(ztok, _COL_KS, _COL_KW, k_norm[1], k_norm[2])

    tables = _s5_tables(p['ssm_lambda_re'], p['ssm_lambda_im'], p['ssm_log_dt'], p['ssm_b_re'], p['ssm_b_im'],
                        p['ssm_c_re'], p['ssm_c_im'], p['ssm_d'])
    gy = _s5_unpack(_s5(_s5_pack(ztok, _COL_SSM, s5_perm[0]), tables, b), s5_perm[1], b)

    g_out = p['mix_out_norm']
    yc = _conv(ztok, p['conv_w_dw'], p['conv_b_dw'], p['conv_ln_g'], p['conv_ln_b'], p['conv_w_pw'],
               g_out[SSM_WIDTH:SSM_WIDTH + CONV_WIDTH])

    kvw = NSA_KV_HEADS * HEAD_DIM
    n_cmp = s // CMP_STRIDE
    kc, vct = _compress(ztok, _COL_KC, _COL_VC, p['nsa_cmp_pe'], p['nsa_cmp_w1'], p['nsa_cmp_w2'], k_norm[0])
    qg = jnp.broadcast_to((p['nsa_q_norm'] * (HEAD_DIM ** -0.5 * LOG2E))[:, None], (HEAD_DIM, TQ))
    bias_c, bias_s, bias_w = t5_tiles
    n_sel = s // L_SEL
    cs_ = np.arange(n_cmp) * CMP_STRIDE
    ss_ = np.arange(n_sel) * L_SEL
    ov = np.maximum(np.minimum(cs_[:, None] + L_CMP, ss_[None, :] + L_SEL) - np.maximum(cs_[:, None], ss_[None, :]), 0)
    ovt = jnp.asarray((ov.astype(np.float32) / L_CMP).T, BF16)
    oc_t, sel = _nsa_cmp(zt, qg, kc, vct, bias_c, ovt)

    front = lambda x, n, axis: jnp.pad(x, [(n, 0) if a == axis else (0, 0) for a in range(x.ndim)])
    ones_rows = jnp.concatenate([jnp.ones((b, NSA_KV_HEADS, 1, s), BF16),
                                 jnp.zeros((b, NSA_KV_HEADS, 15, s), BF16)], axis=2)
    heads_t = lambda rows: jnp.concatenate(
        [zt[:, rows:rows + kvw, :].astype(BF16).reshape(b, NSA_KV_HEADS, HEAD_DIM, s), ones_rows], axis=2)
    ks_p = front(ks, SEL_PAD, 2)
    kw_p = front(kw, WINDOW, 2)
    vst_p = front(heads_t(_ROW_VS), SEL_PAD, 3)
    vwt_p = front(heads_t(_ROW_VW), WINDOW, 3)
    sel_p = jnp.pad(sel, ((0, 0), (0, 0), (SEL_PAD // L_SEL, 0), (0, 0)), constant_values=NEG)
    gl = zt[:, _ROW_G:_ROW_G + N_BRANCH * NSA_HEADS, :].reshape(b, NSA_KV_HEADS, NSA_GROUP * N_BRANCH, s)
    gates_t = jnp.pad(gl, ((0, 0), (0, 0), (0, 16 - NSA_GROUP * N_BRANCH), (0, 0)))
    yn_t = _nsa_main(zt, qg, ks_p, vst_p, kw_p, vwt_p, sel_p, bias_s, bias_w, gates_t, oc_t)
    yn = yn_t

    return _mixout(gy, yc, yn, h, p['ssm_w_glu'], g_out, p['w_out'])


def kernel(x, mem, norm_mix, w_in, ssm_lambda_re, ssm_lambda_im, ssm_log_dt, ssm_b_re, ssm_b_im, ssm_c_re, ssm_c_im, ssm_d, ssm_w_glu, conv_w_dw, conv_b_dw, conv_ln_g, conv_ln_b, conv_w_pw, nsa_q_norm, nsa_k_norm, nsa_cmp_pe, nsa_cmp_w1, nsa_cmp_w2, mix_out_norm, w_out, t5_table, norm_cross, norm_mem, x_w_q, x_w_kv, x_q_norm, x_k_norm, x_w_o, norm_ffn, ffn_w_up, ffn_w_down, moe_router, moe_w_up, moe_w_down):
    b, s, d = x.shape
    depth = w_in.shape[0]
    per_layer = dict(norm_mix=norm_mix, w_in=w_in, ssm_lambda_re=ssm_lambda_re, ssm_lambda_im=ssm_lambda_im,
                     ssm_log_dt=ssm_log_dt, ssm_b_re=ssm_b_re, ssm_b_im=ssm_b_im, ssm_c_re=ssm_c_re,
                     ssm_c_im=ssm_c_im, ssm_d=ssm_d, ssm_w_glu=ssm_w_glu, conv_w_dw=conv_w_dw,
                     conv_b_dw=conv_b_dw, conv_ln_g=conv_ln_g, conv_ln_b=conv_ln_b, conv_w_pw=conv_w_pw,
                     nsa_q_norm=nsa_q_norm, nsa_k_norm=nsa_k_norm, nsa_cmp_pe=nsa_cmp_pe, nsa_cmp_w1=nsa_cmp_w1,
                     nsa_cmp_w2=nsa_cmp_w2, mix_out_norm=mix_out_norm, w_out=w_out)
    t5_tiles = _nsa_bias_tiles(t5_table, s)
    s5_perm = _s5_perm_tables()
    h = x
    for layer in range(depth):
        p = {k: v[layer] for k, v in per_layer.items()}
        h = _layer_mixers(h, p, t5_tiles, s5_perm)
        mk, mv = _memkv(mem, norm_mem[layer], x_w_kv[layer], x_k_norm[layer])
        h = _cross(h, norm_cross[layer], x_w_q[layer], x_q_norm[layer], mk, mv, x_w_o[layer])
        h2d = h.reshape(b * s, d)
        if layer % 2 == 0:
            h2d = _ffn(h2d, norm_ffn[layer], ffn_w_up[layer // 2], ffn_w_down[layer // 2])
        else:
            h2d = _moe(h2d, norm_ffn[layer], moe_router[layer // 2], moe_w_up[layer // 2], moe_w_down[layer // 2])
        h = h2d.reshape(b, s, d)
    return h
```

```python
import functools
import math

import jax
import jax.numpy as jnp
import numpy as np
from jax import lax
from jax.experimental import pallas as pl
from jax.experimental.pallas import tpu as pltpu

F32 = jnp.float32
BF16 = jnp.bfloat16

D_MODEL = 1024
HEAD_DIM = 64
SSM_WIDTH = 256
SSM_GROUP = 16
SSM_GROUPS = 16
SSM_STATE = 64
SSM_CHUNK = 16
CONV_WIDTH = 256
CONV_K = 31
CONV_HALO = 32
NSA_WIDTH = 512
NSA_HEADS = 8
NSA_KV_HEADS = 2
NSA_GROUP = 4
N_BRANCH = 3
L_CMP = 32
CMP_STRIDE = 16
L_SEL = 64
N_SELECT = 16
N_LOCAL = 2
WINDOW = 512
SEL_FORCE = 1e6
T5_BUCKETS = 32
T5_MAX_DIST = 128
X_HEADS = 4
X_WIDTH = 256
D_FF = 2816
N_EXPERTS = 8
TOP_K = 2
EPS = 1e-6
NEG = -1e30
LOG2E = math.log2(math.e)

TQ = 256
SEL_PAD = 256
FF_CHUNK = 256
MOE_TB = 2048
MOE_TS = 256
MOE_SUB = 64
MOE_WIN = 192

_VMEM_LIMIT = 56 * 1024 * 1024


def _cparams(sem, vmem=None):
    return pltpu.CompilerParams(dimension_semantics=sem, vmem_limit_bytes=vmem)


def _rms_rows(x):
    return x * lax.rsqrt(jnp.mean(x * x, axis=-1, keepdims=True) + EPS)


def _proj_kernel(x_ref, g_ref, wtok_ref, wt_ref, ztok_ref, zt_ref):
    xn = (_rms_rows(x_ref[0]) * g_ref[...]).astype(BF16)
    ztok_ref[0] = jnp.dot(xn, wtok_ref[...], preferred_element_type=F32)
    zt_ref[0] = lax.dot_general(wt_ref[...], xn, (((1,), (1,)), ((), ())), preferred_element_type=F32)


def _proj(h, gain, w_tok, w_t, tm=1024):
    b, s, d = h.shape
    ntok, nt = w_tok.shape[1], w_t.shape[0]
    return pl.pallas_call(
        _proj_kernel,
        grid=(b, s // tm),
        in_specs=[pl.BlockSpec((1, tm, d), lambda i, j: (i, j, 0)),
                  pl.BlockSpec((1, d), lambda i, j: (0, 0)),
                  pl.BlockSpec((d, ntok), lambda i, j: (0, 0)),
                  pl.BlockSpec((nt, d), lambda i, j: (0, 0))],
        out_specs=[pl.BlockSpec((1, tm, ntok), lambda i, j: (i, j, 0)),
                   pl.BlockSpec((1, nt, tm), lambda i, j: (i, 0, j))],
        out_shape=[jax.ShapeDtypeStruct((b, s, ntok), F32), jax.ShapeDtypeStruct((b, nt, s), F32)],
        compiler_params=_cparams(("parallel", "parallel"), _VMEM_LIMIT),
        name="proj",
    )(h, gain.reshape(1, d), w_tok, w_t)


def _s5_tables(lam_re, lam_im, log_dt, b_re, b_im, c_re, c_im, d_skip):
    L, H, P = SSM_CHUNK, SSM_GROUP, SSM_STATE
    dt = jnp.exp(log_dt.astype(F32))[:, None]
    lr, li = lam_re.astype(F32), lam_im.astype(F32)
    mag = jnp.exp(lr * dt)
    ar, ai = mag * jnp.cos(li * dt), mag * jnp.sin(li * dt)
    den = lr * lr + li * li
    fr = ((ar - 1.0) * lr + ai * li) / den
    fi = (ai * lr - (ar - 1.0) * li) / den
    bbr = fr[..., None] * b_re - fi[..., None] * b_im
    bbi = fr[..., None] * b_im + fi[..., None] * b_re
    j = jnp.arange(L + 1, dtype=F32)[:, None, None]
    pmag = jnp.exp(lr[None] * dt[None] * j)
    pr, pi = pmag * jnp.cos(li[None] * dt[None] * j), pmag * jnp.sin(li[None] * dt[None] * j)
    cbr = c_re[:, :, :, None] * bbr[:, None, :, :] - c_im[:, :, :, None] * bbi[:, None, :, :]
    cbi = c_re[:, :, :, None] * bbi[:, None, :, :] + c_im[:, :, :, None] * bbr[:, None, :, :]
    hp = lax.Precision.HIGHEST
    kj = (jnp.einsum('jgp,ghpk->jghk', pr[:L], cbr, precision=hp)
          - jnp.einsum('jgp,ghpk->jghk', pi[:L], cbi, precision=hp))
    lag = np.arange(L)[None, :] - np.arange(L)[:, None]
    place = (lag[None] == np.arange(L)[:, None, None]).astype(np.float32)
    kt = jnp.einsum('jab,jghk->abghk', place, kj, precision=hp)
    kt = kt + (jnp.eye(L)[:, :, None, None, None] * (jnp.eye(H)[None, None, None] * d_skip[None, None, :, :, None]))
    tmat = kt.transpose(2, 0, 4, 1, 3).reshape(SSM_GROUPS, L * H, L * H)
    qr, qi = pr[:L][::-1], pi[:L][::-1]
    wre = qr[..., None] * bbr[None] - qi[..., None] * bbi[None]
    wim = qr[..., None] * bbi[None] + qi[..., None] * bbr[None]
    wre = wre.transpose(1, 0, 3, 2).reshape(SSM_GROUPS, L * H, P)
    wim = wim.transpose(1, 0, 3, 2).reshape(SSM_GROUPS, L * H, P)
    w1 = jnp.concatenate([wre, wim], axis=-1)
    w2 = jnp.concatenate([wim, wre], axis=-1)
    sr, si = pr[1:], pi[1:]
    vr = c_re[None] * sr[:, :, None, :] - c_im[None] * si[:, :, None, :]
    vi = c_re[None] * si[:, :, None, :] + c_im[None] * sr[:, :, None, :]
    vmat = jnp.concatenate([vr, -vi], axis=-1).transpose(1, 3, 0, 2).reshape(SSM_GROUPS, 2 * P, L * H)
    a_r, a_i = pr[L], pi[L]
    am = jnp.stack([jnp.concatenate([a_r, a_r], -1), jnp.concatenate([-a_i, a_i], -1),
                    jnp.concatenate([a_i, -a_i], -1)], axis=1)
    am = jnp.concatenate([am, jnp.zeros((SSM_GROUPS, 5, 2 * P), F32)], axis=1)
    return tmat.astype(BF16), w1.astype(BF16), w2.astype(BF16), vmat.astype(BF16), am


def _s5_perm_tables():
    L, G, H = SSM_CHUNK, SSM_GROUPS, SSM_GROUP
    i = jnp.arange(L * SSM_WIDTH)
    ti, gi, hi = i // SSM_WIDTH, (i // H) % G, i % H
    o = jnp.arange(L * H)
    pack = ((gi[None, :, None] == jnp.arange(G)[:, None, None]) & (ti[None, :, None] == (o // H)[None, None, :])
            & (hi[None, :, None] == (o % H)[None, None, :]))
    j = jnp.arange(G * L * H)
    gj, tj, hj = j // (L * H), (j // H) % L, j % H
    w = jnp.arange(SSM_WIDTH)
    unpack = ((tj[None, :, None] == jnp.arange(L)[:, None, None]) & (gj[None, :, None] == (w // H)[None, None, :])
              & (hj[None, :, None] == (w % H)[None, None, :]))
    return pack.astype(BF16), unpack.astype(BF16)


def _s5_pack_kernel(ulo_ref, uhi_ref, p_ref, x_ref, u2_ref, *, n_chunks):
    half = SSM_WIDTH // 2

    @pl.when(pl.program_id(1) == 0)
    def _():
        for t in range(SSM_CHUNK):
            for k, u_ref in enumerate((ulo_ref, uhi_ref)):
                u2_ref[:, t * SSM_WIDTH + k * half:t * SSM_WIDTH + (k + 1) * half] = (
                    u_ref[0, pl.ds(t, n_chunks, stride=SSM_CHUNK), :].astype(BF16))

    x_ref[0] = jnp.dot(u2_ref[...], p_ref[0], preferred_element_type=F32).astype(x_ref.dtype)


def _s5_pack(ztok, col, pack):
    b, s, _ = ztok.shape
    n_chunks = s // SSM_CHUNK
    lw, lh = SSM_CHUNK * SSM_WIDTH, SSM_CHUNK * SSM_GROUP
    half = SSM_WIDTH // 2
    return pl.pallas_call(
        functools.partial(_s5_pack_kernel, n_chunks=n_chunks),
        grid=(b, SSM_GROUPS),
        in_specs=[pl.BlockSpec((1, s, half), lambda i, g: (i, 0, col // half)),
                  pl.BlockSpec((1, s, half), lambda i, g: (i, 0, col // half + 1)),
                  pl.BlockSpec((1, lw, lh), lambda i, g: (g, 0, 0))],
        out_specs=pl.BlockSpec((1, n_chunks, lh), lambda i, g: (g, i, 0)),
        out_shape=jax.ShapeDtypeStruct((SSM_GROUPS, b * n_chunks, lh), BF16),
        scratch_shapes=[pltpu.VMEM((n_chunks, lw), BF16)],
        compiler_params=_cparams(("parallel", "arbitrary"), _VMEM_LIMIT),
        name="s5_pack",
    )(ztok, ztok, pack)


def _s5_unpack_kernel(g_ref, r_ref, o_ref, *, n_chunks):
    t = pl.program_id(1)
    rows = jnp.concatenate([g_ref[g] for g in range(SSM_GROUPS)], axis=1)
    y = jnp.dot(rows, r_ref[0], preferred_element_type=F32)
    half = SSM_WIDTH // 2
    for k in range(SSM_CHUNK):
        @pl.when(t == k)
        def _():
            for part in range(2):
                o_ref[part, 0, pl.ds(k, n_chunks, stride=SSM_CHUNK), :] = y[:, part * half:(part + 1) * half]


def _s5_unpack(gy, unpack, bsz):
    g, r, lh = gy.shape
    n_chunks = r // bsz
    half = SSM_WIDTH // 2
    return pl.pallas_call(
        functools.partial(_s5_unpack_kernel, n_chunks=n_chunks),
        grid=(bsz, SSM_CHUNK),
        in_specs=[pl.BlockSpec((g, n_chunks, lh), lambda i, t: (0, i, 0)),
                  pl.BlockSpec((1, g * lh, SSM_WIDTH), lambda i, t: (t, 0, 0))],
        out_specs=pl.BlockSpec((2, 1, n_chunks * SSM_CHUNK, half), lambda i, t: (0, i, 0, 0)),
        out_shape=jax.ShapeDtypeStruct((2, bsz, n_chunks * SSM_CHUNK, half), F32),
        compiler_params=_cparams(("parallel", "arbitrary"), _VMEM_LIMIT),
        name="s5_unpack",
    )(gy, unpack)


def _s5_kernel(x_ref, t_ref, w1_ref, w2_ref, v_ref, a_ref, o_ref, s1_ref, s2_ref, xin_ref, *, bsz, n_chunks):
    x = x_ref[0]
    s1_ref[...] = jnp.dot(x, w1_ref[0], preferred_element_type=F32)
    s2_ref[...] = jnp.dot(x, w2_ref[0], preferred_element_type=F32)
    a1, a2, a3 = a_ref[0, 0:1, :], a_ref[0, 1:2, :], a_ref[0, 2:3, :]

    def step(c, carry):
        ps, qs = carry
        new_p, new_q = [], []
        for bi in range(bsz):
            row = pl.ds(bi * n_chunks + c, 1)
            xin_ref[row, :] = ps[bi]
            new_p.append(ps[bi] * a1 + qs[bi] * a2 + s1_ref[row, :])
            new_q.append(qs[bi] * a1 + ps[bi] * a3 + s2_ref[row, :])
        return tuple(new_p), tuple(new_q)

    zero = tuple(jnp.zeros((1, 2 * SSM_STATE), F32) for _ in range(bsz))
    lax.fori_loop(0, n_chunks, step, (zero, zero))
    y = (jnp.dot(x, t_ref[0], preferred_element_type=F32)
         + jnp.dot(xin_ref[...].astype(BF16), v_ref[0], preferred_element_type=F32))
    o_ref[0] = jax.nn.gelu(y).astype(o_ref.dtype)


def _s5(xg, tables, bsz):
    tmat, w1, w2, vmat, am = tables
    g, r, lh = xg.shape
    p2 = 2 * SSM_STATE
    kern = functools.partial(_s5_kernel, bsz=bsz, n_chunks=r // bsz)
    return pl.pallas_call(
        kern,
        grid=(g,),
        in_specs=[pl.BlockSpec((1, r, lh), lambda i: (i, 0, 0)),
                  pl.BlockSpec((1, lh, lh), lambda i: (i, 0, 0)),
                  pl.BlockSpec((1, lh, p2), lambda i: (i, 0, 0)),
                  pl.BlockSpec((1, lh, p2), lambda i: (i, 0, 0)),
                  pl.BlockSpec((1, p2, lh), lambda i: (i, 0, 0)),
                  pl.BlockSpec((1, 8, p2), lambda i: (i, 0, 0))],
        out_specs=pl.BlockSpec((1, r, lh), lambda i: (i, 0, 0)),
        out_shape=jax.ShapeDtypeStruct((g, r, lh), BF16),
        scratch_shapes=[pltpu.VMEM((r, p2), F32), pltpu.VMEM((r, p2), F32), pltpu.VMEM((r, p2), F32)],
        compiler_params=_cparams(("parallel",), _VMEM_LIMIT),
        name="s5_scan",
    )(xg, tmat, w1, w2, vmat, am)


def _conv_kernel(z_ref, halo_ref, wdw_ref, bdw_ref, lng_ref, lnb_ref, wpw_ref, go_ref, o_ref, buf_ref, sh_ref, *, tt):
    first = pl.program_id(1) == 0
    zc = z_ref[0]
    zh = halo_ref[0]
    vh = zh[:, :CONV_WIDTH] * jax.nn.sigmoid(zh[:, CONV_WIDTH:])
    buf_ref[0:CONV_HALO, :] = vh * jnp.where(first, 0.0, 1.0)
    buf_ref[CONV_HALO:CONV_HALO + tt, :] = zc[:, :CONV_WIDTH] * jax.nn.sigmoid(zc[:, CONV_WIDTH:])
    for r in range(1, 8):
        sh_ref[r, 0:tt + CONV_HALO - 8, :] = buf_ref[pl.ds(r, tt + CONV_HALO - 8), :]
    acc = jnp.zeros((tt, CONV_WIDTH), F32) + bdw_ref[...]
    for k in range(CONV_K):
        off = CONV_HALO - (CONV_K - 1) + k
        r, a = off % 8, off - off % 8
        rows = buf_ref[pl.ds(a, tt), :] if r == 0 else sh_ref[r, pl.ds(a, tt), :]
        acc = acc + wdw_ref[k:k + 1, :] * rows
    mu = jnp.mean(acc, axis=-1, keepdims=True)
    var = jnp.mean(jnp.square(acc - mu), axis=-1, keepdims=True)
    y = (acc - mu) * lax.rsqrt(var + EPS) * lng_ref[...] + lnb_ref[...]
    y = jax.nn.silu(y)
    y = jnp.dot(y.astype(BF16), wpw_ref[...], preferred_element_type=F32)
    o_ref[0] = (_rms_rows(y) * go_ref[...]).astype(o_ref.dtype)


def _conv(ztok, w_dw, b_dw, ln_g, ln_b, w_pw, g_out, tt=512):
    b, s, _ = ztok.shape
    cw = CONV_WIDTH
    hb = tt // CONV_HALO
    kern = functools.partial(_conv_kernel, tt=tt)
    row = lambda v: v.reshape(1, cw)
    return pl.pallas_call(
        kern,
        grid=(b, s // tt),
        in_specs=[pl.BlockSpec((1, tt, 2 * cw), lambda i, j: (i, j, 0)),
                  pl.BlockSpec((1, CONV_HALO, 2 * cw), lambda i, j: (i, jnp.maximum(j * hb - 1, 0), 0)),
                  pl.BlockSpec((CONV_K + 1, cw), lambda i, j: (0, 0)),
                  pl.BlockSpec((1, cw), lambda i, j: (0, 0)),
                  pl.BlockSpec((1, cw), lambda i, j: (0, 0)),
                  pl.BlockSpec((1, cw), lambda i, j: (0, 0)),
                  pl.BlockSpec((cw, cw), lambda i, j: (0, 0)),
                  pl.BlockSpec((1, cw), lambda i, j: (0, 0))],
        out_specs=pl.BlockSpec((1, tt, cw), lambda i, j: (i, j, 0)),
        out_shape=jax.ShapeDtypeStruct((b, s, cw), BF16),
        scratch_shapes=[pltpu.VMEM((CONV_HALO + tt, cw), F32), pltpu.VMEM((8, CONV_HALO + tt, cw), F32)],
        compiler_params=_cparams(("parallel", "arbitrary")),
        name="conv_mixer",
    )(ztok, ztok, jnp.concatenate([w_dw, jnp.zeros((1, cw), F32)], 0), row(b_dw), row(ln_g), row(ln_b),
      w_pw.astype(BF16), row(g_out))


def _knorm_kernel(ks_ref, kw_ref, gs_ref, gw_ref, os_ref, ow_ref):
    for src, g_ref, dst in ((ks_ref, gs_ref, os_ref), (kw_ref, gw_ref, ow_ref)):
        x = src[0]
        for h in range(NSA_KV_HEADS):
            xh = x[:, h * HEAD_DIM:(h + 1) * HEAD_DIM]
            dst[0, h] = (_rms_rows(xh) * g_ref[...]).astype(dst.dtype)


def _knorm(ztok, col_s, col_w, gain_s, gain_w, tt=512):
    b, s, _ = ztok.shape
    kw = NSA_KV_HEADS * HEAD_DIM
    out = jax.ShapeDtypeStruct((b, NSA_KV_HEADS, s, HEAD_DIM), BF16)
    ospec = pl.BlockSpec((1, NSA_KV_HEADS, tt, HEAD_DIM), lambda i, j: (i, 0, j, 0))
    return pl.pallas_call(
        _knorm_kernel,
        grid=(b, s // tt),
        in_specs=[pl.BlockSpec((1, tt, kw), lambda i, j: (i, j, col_s // kw)),
                  pl.BlockSpec((1, tt, kw), lambda i, j: (i, j, col_w // kw)),
                  pl.BlockSpec((1, HEAD_DIM), lambda i, j: (0, 0)),
                  pl.BlockSpec((1, HEAD_DIM), lambda i, j: (0, 0))],
        out_specs=[ospec, ospec],
        out_shape=[out, out],
        compiler_params=_cparams(("parallel", "parallel")),
        name="nsa_key_norm",
    )(ztok, ztok, gain_s.reshape(1, HEAD_DIM), gain_w.reshape(1, HEAD_DIM))


def _compress_kernel(k_ref, v_ref, wka_ref, wkb_ref, ck_ref, w2k_ref, gk_ref,
                     wva_ref, wvb_ref, cv_ref, w2v_ref, ko_ref, vo_ref):
    hi = lax.Precision.HIGHEST
    n = k_ref.shape[1] // CMP_STRIDE
    kvw = k_ref.shape[2]
    nt = (((1,), (1,)), ((), ()))
    a, bm = jnp.zeros((n, kvw), F32), jnp.zeros((n, kvw), F32)
    at, bt = jnp.zeros((kvw, n), F32), jnp.zeros((kvw, n), F32)
    for l in range(CMP_STRIDE):
        kl = k_ref[0, pl.ds(l, n, stride=CMP_STRIDE), :]
        vl = v_ref[0, pl.ds(l, n, stride=CMP_STRIDE), :]
        a = a + jnp.dot(kl, wka_ref[l], precision=hi, preferred_element_type=F32)
        bm = bm + jnp.dot(kl, wkb_ref[l], precision=hi, preferred_element_type=F32)
        at = at + lax.dot_general(wva_ref[l], vl, nt, precision=hi, preferred_element_type=F32)
        bt = bt + lax.dot_general(wvb_ref[l], vl, nt, precision=hi, preferred_element_type=F32)
    pre = a + pltpu.roll(bm, n - 1, 0) + ck_ref[...]
    kc = jnp.dot(jax.nn.gelu(pre), w2k_ref[...], precision=hi, preferred_element_type=F32)
    for h in range(NSA_KV_HEADS):
        kh = kc[:, h * HEAD_DIM:(h + 1) * HEAD_DIM]
        ko_ref[0, h] = (_rms_rows(kh) * gk_ref[...]).astype(ko_ref.dtype)
    pre_t = at + pltpu.roll(bt, n - 1, 1) + cv_ref[...]
    vt = jnp.dot(w2v_ref[...], jax.nn.gelu(pre_t), precision=hi, preferred_element_type=F32)
    for h in range(NSA_KV_HEADS):
        vo_ref[0, h] = vt[h * HEAD_DIM:(h + 1) * HEAD_DIM, :].astype(vo_ref.dtype)


def _blockdiag2(w):
    z = jnp.zeros_like(w)
    return jnp.concatenate([jnp.concatenate([w, z], 1), jnp.concatenate([z, w], 1)], 0)


def _compress(ztok, col_k, col_v, pe, w1, w2, k_gain):
    b, s, _ = ztok.shape
    n = s // CMP_STRIDE
    hd, kvw = HEAD_DIM, NSA_KV_HEADS * HEAD_DIM
    hp = lax.Precision.HIGHEST

    def expand(w):
        wl = w.reshape(L_CMP, hd, hd)
        e = wl[:, None, :, None, :] * jnp.eye(NSA_KV_HEADS, dtype=F32)[None, :, None, :, None]
        e = e.reshape(L_CMP, kvw, kvw)
        return e[:CMP_STRIDE], e[CMP_STRIDE:]

    wka, wkb = expand(w1[0])
    wva, wvb = expand(w1[1])
    ck = jnp.tile(jnp.dot(pe[0].reshape(1, L_CMP * hd), w1[0], precision=hp), (1, NSA_KV_HEADS))
    cv = jnp.tile(jnp.dot(pe[1].reshape(1, L_CMP * hd), w1[1], precision=hp), (1, NSA_KV_HEADS)).T
    full = lambda shape: pl.BlockSpec(shape, lambda i: tuple(0 for _ in shape))
    return pl.pallas_call(
        _compress_kernel,
        grid=(b,),
        in_specs=[pl.BlockSpec((1, s, kvw), lambda i: (i, 0, col_k // kvw)),
                  pl.BlockSpec((1, s, kvw), lambda i: (i, 0, col_v // kvw)),
                  full((CMP_STRIDE, kvw, kvw)), full((CMP_STRIDE, kvw, kvw)), full((1, kvw)), full((kvw, kvw)),
                  full((1, hd)),
                  full((CMP_STRIDE, kvw, kvw)), full((CMP_STRIDE, kvw, kvw)), full((kvw, 1)), full((kvw, kvw))],
        out_specs=[pl.BlockSpec((1, NSA_KV_HEADS, n, hd), lambda i: (i, 0, 0, 0)),
                   pl.BlockSpec((1, NSA_KV_HEADS, hd, n), lambda i: (i, 0, 0, 0))],
        out_shape=[jax.ShapeDtypeStruct((b, NSA_KV_HEADS, n, hd), BF16),
                   jax.ShapeDtypeStruct((b, NSA_KV_HEADS, hd, n), BF16)],
        compiler_params=_cparams(("parallel",), _VMEM_LIMIT),
        name="nsa_compress",
    )(ztok, ztok, wka, wkb, ck, _blockdiag2(w2[0]), k_gain.reshape(1, hd),
      wva.transpose(0, 2, 1), wvb.transpose(0, 2, 1), cv, _blockdiag2(w2[1]).T)


def _t5_bias_by_dist(t5_table):
    n = np.arange(T5_MAX_DIST + 1)
    max_exact = T5_BUCKETS // 2
    nf = np.maximum(n, 1).astype(np.float32)
    large = max_exact + (np.log(nf / np.float32(max_exact)) / np.float32(math.log(T5_MAX_DIST / max_exact))
                         * np.float32(T5_BUCKETS - max_exact)).astype(np.int32)
    large = np.minimum(large, T5_BUCKETS - 1)
    bucket = np.where(n < max_exact, n, large)
    onehot = (bucket[:, None] == np.arange(T5_BUCKETS)[None, :]).astype(np.float32)
    return jnp.dot(onehot, t5_table, precision=lax.Precision.HIGHEST)


def _bias_tile(fdt, rows, stride, dist00, d_max=None):
    heads = fdt.shape[0]
    a0 = stride * (rows - 1)
    d_lo = dist00 - a0
    length = a0 + TQ
    d_hi = d_lo + length
    d_max = d_hi if d_max is None else d_max
    pieces = []
    for lo, hi, kind in ((d_lo, min(d_hi, 0), 'neg'), (max(d_lo, 0), min(d_hi, T5_MAX_DIST), 'tab'),
                         (max(d_lo, T5_MAX_DIST), min(d_hi, d_max), 'far'), (max(d_lo, d_max), d_hi, 'neg')):
        if hi > lo:
            pieces.append(fdt[:, lo:hi] if kind == 'tab'
                          else jnp.full((heads, hi - lo), NEG if kind == 'neg' else 0.0, F32))
    vec = jnp.concatenate(pieces, axis=1)
    c0 = -(-a0 // 128) * 128
    width = -(-(c0 + TQ) // 128) * 128
    vec = jnp.pad(vec, ((0, 0), (c0 - a0, width - (c0 - a0) - length)))

    def kern(v_ref, o_ref):
        x = jnp.broadcast_to(v_ref[0], (rows, width))
        o_ref[0] = pltpu.roll(x, 0, 1, stride=stride, stride_axis=0)[:, c0:c0 + TQ]

    return pl.pallas_call(
        kern,
        grid=(heads,),
        in_specs=[pl.BlockSpec((1, 1, width), lambda h: (h, 0, 0))],
        out_specs=pl.BlockSpec((1, rows, TQ), lambda h: (h, 0, 0)),
        out_shape=jax.ShapeDtypeStruct((heads, rows, TQ), F32),
        compiler_params=_cparams(("parallel",)),
        name="toeplitz_bias",
    )(vec.reshape(heads, 1, width))


def _nsa_bias_tiles(t5_table, seq):
    fd = _t5_bias_by_dist(t5_table).astype(F32)
    fdt = ((fd - fd[T5_MAX_DIST:]) * LOG2E).T
    n_cmp = seq // CMP_STRIDE
    qt = TQ // CMP_STRIDE
    r0 = n_cmp - qt
    band = _bias_tile(fdt, 2 * qt, CMP_STRIDE, CMP_STRIDE * qt - (L_CMP - 1))
    heads = fdt.shape[0]
    cmp_t = jnp.concatenate([jnp.zeros((heads, r0 - qt, TQ), F32), band,
                             jnp.full((heads, n_cmp - qt, TQ), NEG, F32)], axis=1)
    sel_t = _bias_tile(fdt, SEL_PAD + TQ, 1, SEL_PAD)
    win = _bias_tile(fdt, WINDOW + TQ, 1, WINDOW, d_max=WINDOW)
    rw = np.arange(WINDOW + TQ)[None, :, None]
    win_t = jnp.stack([jnp.where(rw >= WINDOW - q0, win, NEG) for q0 in (0, TQ, 2 * TQ)])
    split = lambda t: t.reshape(*t.shape[:-3], NSA_KV_HEADS, NSA_GROUP, *t.shape[-2:])

    def wide(t):
        t = jnp.swapaxes(split(t), -3, -2)
        return t.reshape(*t.shape[:-2], NSA_GROUP * TQ)

    return wide(cmp_t), wide(sel_t), wide(win_t)


def _q_head(qt_ref, g, qg_ref):
    q = qt_ref[0, g * HEAD_DIM:(g + 1) * HEAD_DIM, :]
    inv = lax.rsqrt(jnp.mean(q * q, axis=0, keepdims=True) + EPS)
    return (q * inv * qg_ref[...]).astype(BF16)


def _nsa_cmp_kernel(qt_ref, qg_ref, kc_ref, vct_ref, bias_ref, ov_ref, oc_ref, sel_ref, imp_ref, *, n_cmp, n_sel):
    qi = pl.program_id(1)
    qt = TQ // CMP_STRIDE
    gw = NSA_GROUP * HEAD_DIM
    kvs = range(NSA_KV_HEADS)
    row0 = pl.multiple_of((n_cmp - qt) - qi * qt, qt)
    qw = [jnp.concatenate([_q_head(qt_ref, kv * NSA_GROUP + g, qg_ref) for g in range(NSA_GROUP)], axis=1)
          for kv in kvs]

    def attend(n):
        for kv in kvs:
            s = (jnp.dot(kc_ref[0, kv, 0:n, :], qw[kv], preferred_element_type=F32)
                 + bias_ref[kv, pl.ds(row0, n), :])
            m = jnp.max(s, axis=0, keepdims=True)
            m = jnp.where(m < 0.5 * NEG, 0.0, m)
            p = jnp.exp2(s - m)
            p = p * (1.0 / jnp.maximum(jnp.sum(p, axis=0, keepdims=True), 1e-30))
            oc = jnp.dot(vct_ref[0, kv, :, 0:n], p.astype(BF16), preferred_element_type=F32)
            psum = jnp.zeros((n, TQ), F32)
            for g in range(NSA_GROUP):
                oc_ref[0, kv * gw + g * HEAD_DIM:kv * gw + (g + 1) * HEAD_DIM, :] = oc[:, g * TQ:(g + 1) * TQ]
                psum = psum + p[:, g * TQ:(g + 1) * TQ]
            hi = psum.astype(BF16)
            lo = (psum - hi.astype(F32)).astype(BF16)
            imp_ref[kv] = (jnp.dot(ov_ref[:, 0:n], hi, preferred_element_type=F32)
                           + jnp.dot(ov_ref[:, 0:n], lo, preferred_element_type=F32))

    chunk = min(n_cmp, 128)
    n_chunks = n_cmp // chunk
    need = lax.div((qi + 1) * qt + (chunk - 1), chunk)
    for c in range(1, n_chunks + 1):
        pl.when(need == c)(functools.partial(attend, c * chunk))

    blk = lax.broadcasted_iota(jnp.int32, (n_sel, TQ), 0)
    blk_t = lax.shift_right_logical(qi * TQ + lax.broadcasted_iota(jnp.int32, (n_sel, TQ), 1), L_SEL.bit_length() - 1)
    forced = (blk == 0) | (blk > blk_t - N_LOCAL)
    v0 = tuple(jnp.where(blk > blk_t, -jnp.inf, jnp.where(forced, SEL_FORCE, imp_ref[kv])) for kv in kvs)

    def pick(_, vs):
        out = []
        for v in vs:
            m = jnp.max(v, axis=0, keepdims=True)
            first = jnp.min(jnp.where(v == m, blk, n_sel), axis=0, keepdims=True)
            out.append(jnp.where(blk == first, -jnp.inf, v))
        return tuple(out)

    vs = lax.fori_loop(0, min(N_SELECT, n_sel), pick, v0)
    for kv in kvs:
        sel_ref[0, kv] = jnp.where((vs[kv] == -jnp.inf) & (v0[kv] > -jnp.inf), 0.0, NEG)


def _nsa_cmp(zt, qg, kc, vct, bias_c, ovt):
    b, _, s = zt.shape
    n_cmp, n_sel = s // CMP_STRIDE, s // L_SEL
    kvh = NSA_KV_HEADS
    kern = functools.partial(_nsa_cmp_kernel, n_cmp=n_cmp, n_sel=n_sel)
    return pl.pallas_call(
        kern,
        grid=(b, s // TQ),
        in_specs=[pl.BlockSpec((1, NSA_WIDTH, TQ), lambda i, j: (i, 0, j)),
                  pl.BlockSpec((HEAD_DIM, TQ), lambda i, j: (0, 0)),
                  pl.BlockSpec((1, kvh, n_cmp, HEAD_DIM), lambda i, j: (i, 0, 0, 0)),
                  pl.BlockSpec((1, kvh, HEAD_DIM, n_cmp), lambda i, j: (i, 0, 0, 0)),
                  pl.BlockSpec((kvh, bias_c.shape[1], NSA_GROUP * TQ), lambda i, j: (0, 0, 0)),
                  pl.BlockSpec((n_sel, n_cmp), lambda i, j: (0, 0))],
        out_specs=[pl.BlockSpec((1, NSA_WIDTH, TQ), lambda i, j: (i, 0, j)),
                   pl.BlockSpec((1, kvh, n_sel, TQ), lambda i, j: (i, 0, 0, j))],
        out_shape=[jax.ShapeDtypeStruct((b, NSA_WIDTH, s), F32),
                   jax.ShapeDtypeStruct((b, kvh, n_sel, s), F32)],
        scratch_shapes=[pltpu.VMEM((kvh, n_sel, TQ), F32)],
        compiler_params=_cparams(("parallel", "parallel"), _VMEM_LIMIT),
        name="nsa_compressed_select",
    )(zt, qg, kc, vct, bias_c, ovt)


def _nsa_main_kernel(qt_ref, qg_ref, ks_ref, vst_ref, kw_ref, vwt_ref, sel_ref, bs_ref, bw_ref, gate_ref,
                     oc_ref, o_ref, acc_ref, s_ref, ot_ref):
    qi = pl.program_id(1)
    q0 = pl.multiple_of(qi * TQ, TQ)
    near = SEL_PAD + TQ
    gw = NSA_GROUP * HEAD_DIM
    kvs = range(NSA_KV_HEADS)

    def expand_sel(kv, first_blk, n_blk):
        rows = [jnp.broadcast_to(sel_ref[0, kv, pl.ds(first_blk + r, 1), :], (L_SEL, TQ)) for r in range(n_blk)]
        rows = jnp.concatenate(rows, axis=0)
        return jnp.concatenate([rows] * NSA_GROUP, axis=1)

    qw = [jnp.concatenate([_q_head(qt_ref, kv * NSA_GROUP + g, qg_ref) for g in range(NSA_GROUP)], axis=1)
          for kv in kvs]
    m0 = []
    for kv in kvs:
        s = (jnp.dot(ks_ref[0, kv, pl.ds(q0, near), :], qw[kv], preferred_element_type=F32) + bs_ref[kv]
             + expand_sel(kv, qi * (TQ // L_SEL), near // L_SEL))
        m = jnp.max(s, axis=0, keepdims=True)
        p = jnp.exp2(s - m).astype(BF16)
        acc_ref[kv] = jnp.dot(vst_ref[0, kv, :, pl.ds(q0, near)], p, preferred_element_type=F32)
        m0.append(m)

    for kv in kvs:
        s = jnp.dot(kw_ref[0, kv, pl.ds(q0, WINDOW + TQ), :], qw[kv], preferred_element_type=F32) + bw_ref[0, kv]
        p = jnp.exp2(s - jnp.max(s, axis=0, keepdims=True)).astype(BF16)
        ow = jnp.dot(vwt_ref[0, kv, :, pl.ds(q0, WINDOW + TQ)], p, preferred_element_type=F32)
        ow = ow[:HEAD_DIM] * (1.0 / jnp.maximum(ow[HEAD_DIM:HEAD_DIM + 1], 1e-30))
        for g in range(NSA_GROUP):
            gates = jax.nn.sigmoid(gate_ref[0, kv, g * N_BRANCH:(g + 1) * N_BRANCH, :])
            rows = slice(kv * gw + g * HEAD_DIM, kv * gw + (g + 1) * HEAD_DIM)
            ot_ref[rows, :] = gates[0:1] * oc_ref[0, rows, :] + gates[2:3] * ow[:, g * TQ:(g + 1) * TQ]

    def scores(c, slot):
        r0 = pl.multiple_of(c * TQ, TQ)
        mc = []
        for kv in kvs:
            s = (jnp.dot(ks_ref[0, kv, pl.ds(r0, TQ), :], qw[kv], preferred_element_type=F32)
                 + expand_sel(kv, c * (TQ // L_SEL), TQ // L_SEL))
            s_ref[slot, kv] = s
            mc.append(jnp.max(s, axis=0, keepdims=True))
        return tuple(mc)

    def consume(c, slot, m_old, mc):
        r0 = pl.multiple_of(c * TQ, TQ)
        m_out = []
        for kv in kvs:
            m_new = jnp.maximum(m_old[kv], mc[kv])
            alpha = jnp.exp2(m_old[kv] - m_new)
            p = jnp.exp2((s_ref[slot, kv] - m_new).astype(BF16))
            acc_ref[kv] = alpha * acc_ref[kv] + jnp.dot(vst_ref[0, kv, :, pl.ds(r0, TQ)], p,
                                                        preferred_element_type=F32)
            m_out.append(m_new)
        return tuple(m_out)

    first = SEL_PAD // TQ
    n_far = qi - first

    def trips(unroll, base):
        def trip(i, carry):
            m, mc = carry
            c = base + unroll * i
            for u in range(unroll):
                mc_next = scores(c + u + 1, (u + 1) % 2)
                m = consume(c + u, u % 2, m, mc)
                mc = mc_next
            return m, mc
        return trip

    n_pos = jnp.maximum(n_far, 0)
    n8 = lax.shift_right_logical(n_pos, 3)
    carry = lax.fori_loop(0, n8, trips(8, first), (tuple(m0), scores(first, 0)))
    left = n_pos - 8 * n8
    base2 = first + 8 * n8
    n2 = lax.shift_right_logical(left, 1)
    m_far, mc_far = lax.fori_loop(0, n2, trips(2, base2), carry)

    @pl.when(left - 2 * n2 == 1)
    def _():
        consume(base2 + 2 * n2, 0, m_far, mc_far)

    for kv in kvs:
        os = acc_ref[kv]
        os = os[:HEAD_DIM] * (1.0 / jnp.maximum(os[HEAD_DIM:HEAD_DIM + 1], 1e-30))
        for g in range(NSA_GROUP):
            gate = jax.nn.sigmoid(gate_ref[0, kv, g * N_BRANCH + 1:g * N_BRANCH + 2, :])
            rows = slice(kv * gw + g * HEAD_DIM, kv * gw + (g + 1) * HEAD_DIM)
            ot_ref[rows, :] = ot_ref[rows, :] + gate * os[:, g * TQ:(g + 1) * TQ]
    o_ref[0] = ot_ref[...].T


def _nsa_main(zt, qg, ks_p, vst_p, kw_p, vwt_p, sel_p, bias_s, bias_w, gates_t, oc_t):
    b, _, s = zt.shape
    kvh = NSA_KV_HEADS
    sp, wp = ks_p.shape[2], kw_p.shape[2]
    nb, vr = sel_p.shape[2], vst_p.shape[2]
    once = pl.Buffered(1)
    return pl.pallas_call(
        _nsa_main_kernel,
        grid=(b, s // TQ),
        in_specs=[pl.BlockSpec((1, NSA_WIDTH, TQ), lambda i, j: (i, 0, j)),
                  pl.BlockSpec((HEAD_DIM, TQ), lambda i, j: (0, 0)),
                  pl.BlockSpec((1, kvh, sp, HEAD_DIM), lambda i, j: (i, 0, 0, 0), pipeline_mode=once),
                  pl.BlockSpec((1, kvh, vr, sp), lambda i, j: (i, 0, 0, 0), pipeline_mode=once),
                  pl.BlockSpec((1, kvh, wp, HEAD_DIM), lambda i, j: (i, 0, 0, 0), pipeline_mode=once),
                  pl.BlockSpec((1, kvh, vr, wp), lambda i, j: (i, 0, 0, 0), pipeline_mode=once),
                  pl.BlockSpec((1, kvh, nb, TQ), lambda i, j: (i, 0, 0, j)),
                  pl.BlockSpec((kvh, SEL_PAD + TQ, NSA_GROUP * TQ), lambda i, j: (0, 0, 0), pipeline_mode=once),
                  pl.BlockSpec((1, kvh, WINDOW + TQ, NSA_GROUP * TQ), lambda i, j: (jnp.minimum(j, 2), 0, 0, 0)),
                  pl.BlockSpec((1, kvh, 16, TQ), lambda i, j: (i, 0, 0, j)),
                  pl.BlockSpec((1, NSA_WIDTH, TQ), lambda i, j: (i, 0, j))],
        out_specs=pl.BlockSpec((1, TQ, NSA_WIDTH), lambda i, j: (i, j, 0)),
        out_shape=jax.ShapeDtypeStruct((b, s, NSA_WIDTH), F32),
        scratch_shapes=[pltpu.VMEM((kvh, vr, NSA_GROUP * TQ), F32),
                        pltpu.VMEM((2, kvh, TQ, NSA_GROUP * TQ), F32),
                        pltpu.VMEM((NSA_WIDTH, TQ), F32)],
        compiler_params=_cparams(("parallel", "arbitrary"), _VMEM_LIMIT),
        name="nsa_selected_window",
    )(zt, qg, ks_p, vst_p, kw_p, vwt_p, sel_p, bias_s, bias_w, gates_t, oc_t)


def _mixout_kernel(gy_ref, yc_ref, yn_ref, h_ref, wglu_ref, go_ref, wo_ref, o_ref):
    sw = SSM_WIDTH
    half = sw // 2
    ag = (jnp.dot(gy_ref[0, 0].astype(BF16), wglu_ref[0:half, :], preferred_element_type=F32)
          + jnp.dot(gy_ref[1, 0].astype(BF16), wglu_ref[half:, :], preferred_element_type=F32))
    ys = ag[:, :sw] * jax.nn.sigmoid(ag[:, sw:])
    ys = (_rms_rows(ys) * go_ref[:, 0:sw]).astype(BF16)
    yn = (_rms_rows(yn_ref[0]) * go_ref[:, 2 * sw:]).astype(BF16)
    out = (jnp.dot(ys, wo_ref[0:sw, :], preferred_element_type=F32)
           + jnp.dot(yc_ref[0], wo_ref[sw:2 * sw, :], preferred_element_type=F32)
           + jnp.dot(yn, wo_ref[2 * sw:, :], preferred_element_type=F32))
    o_ref[0] = h_ref[0] + out


def _mixout(gy, yc, yn, h, w_glu, g_out, w_out, tm=1024):
    b, s, d = h.shape
    tok = lambda w: pl.BlockSpec((1, tm, w), lambda i, j: (i, j, 0))
    return pl.pallas_call(
        _mixout_kernel,
        grid=(b, s // tm),
        in_specs=[pl.BlockSpec((2, 1, tm, SSM_WIDTH // 2), lambda i, j: (0, i, j, 0)),
                  tok(CONV_WIDTH), tok(NSA_WIDTH), tok(d),
                  pl.BlockSpec((SSM_WIDTH, 2 * SSM_WIDTH), lambda i, j: (0, 0)),
                  pl.BlockSpec((1, d), lambda i, j: (0, 0)),
                  pl.BlockSpec((d, d), lambda i, j: (0, 0))],
        out_specs=tok(d),
        out_shape=jax.ShapeDtypeStruct((b, s, d), F32),
        compiler_params=_cparams(("parallel", "parallel")),
        name="mix_out",
    )(gy, yc, yn, h, w_glu.astype(BF16), g_out.reshape(1, d), w_out.astype(BF16))


def _memkv_kernel(mem_ref, g_ref, w_ref, kg_ref, k_ref, v_ref):
    mn = (_rms_rows(mem_ref[0]) * g_ref[...]).astype(BF16)
    kv = jnp.dot(mn, w_ref[...], preferred_element_type=F32)
    for h in range(X_HEADS):
        cols = slice(h * HEAD_DIM, (h + 1) * HEAD_DIM)
        k_ref[0, :, cols] = (_rms_rows(kv[:, cols]) * kg_ref[...]).astype(k_ref.dtype)
    v_ref[0] = kv[:, X_WIDTH:].astype(v_ref.dtype)


def _memkv(mem, gain, w_kv, k_gain):
    b, m, d = mem.shape
    out = jax.ShapeDtypeStruct((b, m, X_WIDTH), BF16)
    return pl.pallas_call(
        _memkv_kernel,
        grid=(b,),
        in_specs=[pl.BlockSpec((1, m, d), lambda i: (i, 0, 0)),
                  pl.BlockSpec((1, d), lambda i: (0, 0)),
                  pl.BlockSpec((d, 2 * X_WIDTH), lambda i: (0, 0)),
                  pl.BlockSpec((1, HEAD_DIM), lambda i: (0, 0))],
        out_specs=[pl.BlockSpec((1, m, X_WIDTH), lambda i: (i, 0, 0))] * 2,
        out_shape=[out, out],
        compiler_params=_cparams(("parallel",)),
        name="cross_mem_kv",
    )(mem, gain.reshape(1, d), w_kv.astype(BF16), k_gain.reshape(1, HEAD_DIM))


def _cross_kernel(h_ref, g_ref, wq_ref, qg_ref, k_ref, v_ref, wo_ref, o_ref):
    h = h_ref[0]
    hn = (_rms_rows(h) * g_ref[...]).astype(BF16)
    q = jnp.dot(hn, wq_ref[...], preferred_element_type=F32)
    out = h
    for hd in range(X_HEADS):
        cols = slice(hd * HEAD_DIM, (hd + 1) * HEAD_DIM)
        qh = (_rms_rows(q[:, cols]) * qg_ref[...]).astype(BF16)
        s = lax.dot_general(qh, k_ref[0, :, cols], (((1,), (1,)), ((), ())), preferred_element_type=F32)
        p = jnp.exp(s - jnp.max(s, axis=-1, keepdims=True))
        p = p * (1.0 / jnp.sum(p, axis=-1, keepdims=True))
        o = jnp.dot(p.astype(BF16), v_ref[0, :, cols], preferred_element_type=F32)
        out = out + jnp.dot(o.astype(BF16), wo_ref[cols, :], preferred_element_type=F32)
    o_ref[0] = out


def _cross(h, gain, w_q, q_gain, k, v, w_o, tm=1024):
    b, s, d = h.shape
    m = k.shape[1]
    return pl.pallas_call(
        _cross_kernel,
        grid=(b, s // tm),
        in_specs=[pl.BlockSpec((1, tm, d), lambda i, j: (i, j, 0)),
                  pl.BlockSpec((1, d), lambda i, j: (0, 0)),
                  pl.BlockSpec((d, X_WIDTH), lambda i, j: (0, 0)),
                  pl.BlockSpec((1, HEAD_DIM), lambda i, j: (0, 0)),
                  pl.BlockSpec((1, m, X_WIDTH), lambda i, j: (i, 0, 0)),
                  pl.BlockSpec((1, m, X_WIDTH), lambda i, j: (i, 0, 0)),
                  pl.BlockSpec((X_WIDTH, d), lambda i, j: (0, 0))],
        out_specs=pl.BlockSpec((1, tm, d), lambda i, j: (i, j, 0)),
        out_shape=jax.ShapeDtypeStruct((b, s, d), F32),
        compiler_params=_cparams(("parallel", "parallel")),
        name="cross_attention",
    )(h, gain.reshape(1, d), w_q.astype(BF16), (q_gain * HEAD_DIM ** -0.5).reshape(1, HEAD_DIM), k, v,
      w_o.astype(BF16))


def _ffn_kernel(h_ref, g_ref, wg_ref, wv_ref, wd_ref, o_ref, xn_ref, acc_ref):
    f = pl.program_id(1)

    @pl.when(f == 0)
    def _():
        xn_ref[...] = (_rms_rows(h_ref[...]) * g_ref[...]).astype(BF16)
        acc_ref[...] = jnp.zeros_like(acc_ref)

    x = xn_ref[...]
    gate = jnp.dot(x, wg_ref[...], preferred_element_type=F32)
    val = jnp.dot(x, wv_ref[...], preferred_element_type=F32)
    act = (jax.nn.silu(gate) * val).astype(BF16)
    acc_ref[...] += jnp.dot(act, wd_ref[...], preferred_element_type=F32)

    @pl.when(f == pl.num_programs(1) - 1)
    def _():
        o_ref[...] = h_ref[...] + acc_ref[...]


def _ffn(h2d, gain, w_up, w_down, tm=2048):
    t, d = h2d.shape
    nf = D_FF // FF_CHUNK
    wb = w_up.astype(BF16)
    return pl.pallas_call(
        _ffn_kernel,
        grid=(t // tm, nf),
        in_specs=[pl.BlockSpec((tm, d), lambda i, f: (i, 0)),
                  pl.BlockSpec((1, d), lambda i, f: (0, 0)),
                  pl.BlockSpec((d, FF_CHUNK), lambda i, f: (0, f)),
                  pl.BlockSpec((d, FF_CHUNK), lambda i, f: (0, f + nf)),
                  pl.BlockSpec((FF_CHUNK, d), lambda i, f: (f, 0))],
        out_specs=pl.BlockSpec((tm, d), lambda i, f: (i, 0)),
        out_shape=jax.ShapeDtypeStruct((t, d), F32),
        scratch_shapes=[pltpu.VMEM((tm, d), BF16), pltpu.VMEM((tm, d), F32)],
        compiler_params=_cparams(("parallel", "arbitrary"), _VMEM_LIMIT),
        name="ffn_swiglu",
    )(h2d, gain.reshape(1, d), wb, wb, w_down.astype(BF16))


def _router_kernel(h_ref, g_ref, wr_ref, xn_ref, gate_ref, asg_ref):
    xn = _rms_rows(h_ref[...]) * g_ref[...]
    xn_ref[...] = xn.astype(BF16)
    logits = jnp.dot(xn, wr_ref[...], precision=lax.Precision.HIGHEST, preferred_element_type=F32)
    lane = lax.broadcasted_iota(jnp.int32, logits.shape, 1)
    lg = jnp.where(lane < N_EXPERTS, logits, -jnp.inf)
    m1 = jnp.max(lg, axis=-1, keepdims=True)
    i1 = jnp.min(jnp.where(lg == m1, lane, 128), axis=-1, keepdims=True)
    lg2 = jnp.where(lane == i1, -jnp.inf, lg)
    m2 = jnp.max(lg2, axis=-1, keepdims=True)
    i2 = jnp.min(jnp.where(lg2 == m2, lane, 128), axis=-1, keepdims=True)
    e = jnp.exp(m2 - m1)
    den = 1.0 + e
    gate_ref[...] = jnp.where(lane == i1, 1.0 / den, jnp.where(lane == i2, e / den, 0.0))
    asg_ref[...] = ((lane == i1) | (lane == i2)).astype(jnp.int32)


def _router(h2d, gain, w_router, tm=512):
    t, d = h2d.shape
    wr = jnp.concatenate([w_router, jnp.zeros((d, 128 - N_EXPERTS), F32)], axis=1)
    return pl.pallas_call(
        _router_kernel,
        grid=(t // tm,),
        in_specs=[pl.BlockSpec((tm, d), lambda i: (i, 0)),
                  pl.BlockSpec((1, d), lambda i: (0, 0)),
                  pl.BlockSpec((d, 128), lambda i: (0, 0))],
        out_specs=[pl.BlockSpec((tm, d), lambda i: (i, 0)),
                   pl.BlockSpec((tm, 128), lambda i: (i, 0)),
                   pl.BlockSpec((tm, 128), lambda i: (i, 0))],
        out_shape=[jax.ShapeDtypeStruct((t, d), BF16), jax.ShapeDtypeStruct((t, 128), F32),
                   jax.ShapeDtypeStruct((t, 128), jnp.int32)],
        compiler_params=_cparams(("parallel",)),
        name="moe_router",
    )(h2d, gain.reshape(1, d), wr)


def _moe_windows(rb, lo, hi, active):
    lo_l = jnp.clip(lo - rb * MOE_TB, 0, MOE_TB)
    hi_l = jnp.clip(hi - rb * MOE_TB, 0, MOE_TB)
    shift = MOE_SUB.bit_length() - 1
    w0 = jnp.minimum(lax.shift_left(lax.shift_right_logical(lo_l, shift), shift), MOE_TB - MOE_WIN)
    has = active & (hi_l > lo_l)
    w1 = jnp.minimum(w0 + MOE_WIN, MOE_TB - MOE_WIN)
    return ((w0, 0, has), (w1, w0 + MOE_WIN, has & (hi_l > w0 + MOE_WIN)))


def _moe_gather_kernel(rb_ref, lo_ref, hi_ref, first_ref, tgt_ref, x_ref, o_ref):
    e, j, slot = pl.program_id(0), pl.program_id(1), pl.program_id(2)
    rb = rb_ref[e, j, slot]

    @pl.when(first_ref[e, j, slot] == 1)
    def _():
        o_ref[...] = jnp.zeros_like(o_ref)

    active = (slot == 0) | (rb != rb_ref[e, j, 0])
    tgt = tgt_ref[0]
    n_sub = MOE_TB // MOE_TS
    for sub in range(n_sub):
        toks = slice(sub * MOE_TS, (sub + 1) * MOE_TS)
        for start, cutoff, needed in _moe_windows(rb, lo_ref[e, j * n_sub + sub], hi_ref[e, j * n_sub + sub], active):

            @pl.when(needed)
            def _():
                local = start + lax.broadcasted_iota(jnp.int32, (MOE_WIN, MOE_TS), 0)
                rows = jnp.where(local >= cutoff, rb * MOE_TB + local, -2)
                onehot = jnp.where(tgt[:, toks] == rows, 1.0, 0.0).astype(BF16)
                part = jnp.dot(onehot, x_ref[toks, :], preferred_element_type=F32)
                sl = pl.ds(pl.multiple_of(start, MOE_SUB), MOE_WIN)
                o_ref[sl, :] = o_ref[sl, :] + part.astype(o_ref.dtype)


def _moe_ffn_kernel(exp_ref, nused_ref, x_ref, wg_ref, wv_ref, wd_ref, o_ref, acc_ref):
    r, f = pl.program_id(0), pl.program_id(1)
    used = r < nused_ref[0]

    @pl.when(f == 0)
    def _():
        acc_ref[...] = jnp.zeros_like(acc_ref)

    @pl.when(used)
    def _():
        x = x_ref[...]
        gate = jnp.dot(x, wg_ref[0], preferred_element_type=F32)
        val = jnp.dot(x, wv_ref[0], preferred_element_type=F32)
        act = (jax.nn.silu(gate) * val).astype(BF16)
        acc_ref[...] += jnp.dot(act, wd_ref[0], preferred_element_type=F32)

    @pl.when(f == pl.num_programs(1) - 1)
    def _():
        o_ref[...] = acc_ref[...].astype(o_ref.dtype)
    del exp_ref


def _moe_scatter_kernel(rb_ref, lo_ref, hi_ref, tgt_ref, gate_ref, y_ref, h_ref, o_ref):
    j, e, slot = pl.program_id(0), pl.program_id(1), pl.program_id(2)
    rb = rb_ref[e, j, slot]

    @pl.when((e == 0) & (slot == 0))
    def _():
        o_ref[...] = h_ref[...]

    active = (slot == 0) | (rb != rb_ref[e, j, 0])
    n_sub = MOE_TB // MOE_TS
    for sub in range(n_sub):
        toks = slice(sub * MOE_TS, (sub + 1) * MOE_TS)
        for start, cutoff, needed in _moe_windows(rb, lo_ref[e, j * n_sub + sub], hi_ref[e, j * n_sub + sub], active):

            @pl.when(needed)
            def _():
                mine = lax.broadcasted_iota(jnp.int32, (MOE_TS, N_EXPERTS), 1) == e
                tgt = jnp.sum(jnp.where(mine, tgt_ref[toks, :], 0), axis=1, keepdims=True)
                gate = jnp.sum(jnp.where(mine, gate_ref[toks, :], 0.0), axis=1, keepdims=True)
                local = start + lax.broadcasted_iota(jnp.int32, (MOE_TS, MOE_WIN), 1)
                rows = jnp.where(local >= cutoff, rb * MOE_TB + local, -2)
                onehot = jnp.where(tgt == rows, 1.0, 0.0).astype(BF16)
                y = y_ref[pl.ds(pl.multiple_of(start, MOE_SUB), MOE_WIN), :]
                o_ref[toks, :] = o_ref[toks, :] + gate * jnp.dot(onehot, y, preferred_element_type=F32)


def _moe(h2d, gain, w_router, w_up, w_down):
    t, d = h2d.shape
    tb = MOE_TB
    nj = t // tb
    n_rb = (t * TOP_K) // tb + N_EXPERTS
    xn, gates, asg = _router(h2d, gain, w_router)
    asg = asg[:, :N_EXPERTS]
    gates = gates[:, :N_EXPERTS]
    cs = jnp.cumsum(asg, axis=0)
    rank = cs - asg
    counts = cs[-1]
    padded = (counts + tb - 1) // tb * tb
    pad_end = jnp.cumsum(padded)
    start_p = pad_end - padded
    tgt = jnp.where(asg == 1, start_p[None, :] + rank, -1).astype(jnp.int32)
    ts = MOE_TS
    cb = jnp.concatenate([jnp.zeros((1, N_EXPERTS), jnp.int32), cs[ts - 1::ts]], axis=0)
    lo = (start_p[None, :] + cb[:-1]).T.astype(jnp.int32)
    hi = (start_p[None, :] + cb[1:]).T.astype(jnp.int32)
    rb0 = lo[:, ::tb // ts] // tb
    rb1 = jnp.maximum(rb0, (hi[:, tb // ts - 1::tb // ts] - 1) // tb)
    rb = jnp.stack([rb0, rb1], axis=-1).astype(jnp.int32)
    flat = rb.reshape(-1)
    first = jnp.concatenate([jnp.ones((1,), jnp.int32), (flat[1:] != flat[:-1]).astype(jnp.int32)])
    first = first.reshape(N_EXPERTS, nj, 2)
    n_used = (pad_end[-1] // tb).astype(jnp.int32).reshape(1)
    blk_exp = jnp.minimum(jnp.searchsorted(pad_end, jnp.arange(n_rb) * tb, side='right'),
                          N_EXPERTS - 1).astype(jnp.int32)

    xs = pl.pallas_call(
        _moe_gather_kernel,
        grid_spec=pltpu.PrefetchScalarGridSpec(
            num_scalar_prefetch=4,
            grid=(N_EXPERTS, nj, 2),
            in_specs=[pl.BlockSpec((1, 1, tb), lambda e, j, s, *_: (e, 0, j)),
                      pl.BlockSpec((tb, d), lambda e, j, s, *_: (j, 0))],
            out_specs=pl.BlockSpec((tb, d), lambda e, j, s, rb_ref, *_: (rb_ref[e, j, s], 0))),
        out_shape=jax.ShapeDtypeStruct((n_rb * tb, d), BF16),
        compiler_params=_cparams(("arbitrary", "arbitrary", "arbitrary"), _VMEM_LIMIT),
        name="moe_gather",
    )(rb, lo, hi, first, tgt.T.reshape(N_EXPERTS, 1, t), xn)

    nf = D_FF // FF_CHUNK
    wub = w_up.astype(BF16)
    ys = pl.pallas_call(
        _moe_ffn_kernel,
        grid_spec=pltpu.PrefetchScalarGridSpec(
            num_scalar_prefetch=2,
            grid=(n_rb, nf),
            in_specs=[pl.BlockSpec((tb, d), lambda r, f, *_: (r, 0)),
                      pl.BlockSpec((1, d, FF_CHUNK), lambda r, f, ex, nu: (ex[r], 0, f)),
                      pl.BlockSpec((1, d, FF_CHUNK), lambda r, f, ex, nu: (ex[r], 0, f + nf)),
                      pl.BlockSpec((1, FF_CHUNK, d), lambda r, f, ex, nu: (ex[r], f, 0))],
            out_specs=pl.BlockSpec((tb, d), lambda r, f, *_: (r, 0)),
            scratch_shapes=[pltpu.VMEM((tb, d), F32)]),
        out_shape=jax.ShapeDtypeStruct((n_rb * tb, d), BF16),
        compiler_params=_cparams(("arbitrary", "arbitrary"), _VMEM_LIMIT),
        name="moe_expert_ffn",
    )(blk_exp, n_used, xs, wub, wub, w_down.astype(BF16))

    return pl.pallas_call(
        _moe_scatter_kernel,
        grid_spec=pltpu.PrefetchScalarGridSpec(
            num_scalar_prefetch=3,
            grid=(nj, N_EXPERTS, 2),
            in_specs=[pl.BlockSpec((tb, N_EXPERTS), lambda j, e, s, *_: (j, 0)),
                      pl.BlockSpec((tb, N_EXPERTS), lambda j, e, s, *_: (j, 0)),
                      pl.BlockSpec((tb, d), lambda j, e, s, rb_ref, *_: (rb_ref[e, j, s], 0)),
                      pl.BlockSpec((tb, d), lambda j, e, s, *_: (j, 0))],
            out_specs=pl.BlockSpec((tb, d), lambda j, e, s, *_: (j, 0))),
        out_shape=jax.ShapeDtypeStruct((t, d), F32),
        compiler_params=_cparams(("arbitrary", "arbitrary", "arbitrary"), _VMEM_LIMIT),
        name="moe_scatter",
    )(rb, lo, hi, tgt, gates, ys, h2d)


_COL_CONV, _COL_SSM, _COL_KC, _COL_VC, _COL_KS, _COL_KW = 0, 512, 768, 896, 1024, 1152
_ROW_Q, _ROW_VS, _ROW_VW, _ROW_G = 0, 512, 640, 768


def _split_w_in(w_in):
    kvw = NSA_KV_HEADS * HEAD_DIM
    cuts = np.cumsum([0, SSM_WIDTH, 2 * CONV_WIDTH, NSA_WIDTH] + [kvw] * 6 + [N_BRANCH * NSA_HEADS])
    seg = lambda i: w_in[:, cuts[i]:cuts[i + 1]]
    ssm, conv, q, k_c, v_c, k_s, v_s, k_w, v_w, gate = (seg(i) for i in range(10))
    w_tok = jnp.concatenate([conv, ssm, k_c, v_c, k_s, k_w], axis=1).astype(BF16)
    gate = jnp.concatenate([gate, jnp.zeros((w_in.shape[0], 8), F32)], axis=1)
    w_t = jnp.concatenate([q, v_s, v_w, gate], axis=1).T.astype(BF16)
    return w_tok, w_t


def _layer_mixers(h, p, t5_tiles, s5_perm):
    b, s, d = h.shape
    w_tok, w_t = _split_w_in(p['w_in'])
    k_norm = p['nsa_k_norm']
    ztok, zt = _proj(h, p['norm_mix'], w_tok, w_t)
    ks, kw = _knorm(ztok, _COL_KS, _COL_KW, k_norm[1], k_norm[2])

    tables = _s5_tables(p['ssm_lambda_re'], p['ssm_lambda_im'], p['ssm_log_dt'], p['ssm_b_re'], p['ssm_b_im'],
                        p['ssm_c_re'], p['ssm_c_im'], p['ssm_d'])
    gy = _s5_unpack(_s5(_s5_pack(ztok, _COL_SSM, s5_perm[0]), tables, b), s5_perm[1], b)

    g_out = p['mix_out_norm']
    yc = _conv(ztok, p['conv_w_dw'], p['conv_b_dw'], p['conv_ln_g'], p['conv_ln_b'], p['conv_w_pw'],
               g_out[SSM_WIDTH:SSM_WIDTH + CONV_WIDTH])

    kvw = NSA_KV_HEADS * HEAD_DIM
    n_cmp = s // CMP_STRIDE
    kc, vct = _compress(ztok, _COL_KC, _COL_VC, p['nsa_cmp_pe'], p['nsa_cmp_w1'], p['nsa_cmp_w2'], k_norm[0])
    qg = jnp.broadcast_to((p['nsa_q_norm'] * (HEAD_DIM ** -0.5 * LOG2E))[:, None], (HEAD_DIM, TQ))
    bias_c, bias_s, bias_w = t5_tiles
    n_sel = s // L_SEL
    cs_ = np.arange(n_cmp) * CMP_STRIDE
    ss_ = np.arange(n_sel) * L_SEL
    ov = np.maximum(np.minimum(cs_[:, None] + L_CMP, ss_[None, :] + L_SEL) - np.maximum(cs_[:, None], ss_[None, :]), 0)
    ovt = jnp.asarray((ov.astype(np.float32) / L_CMP).T, BF16)
    oc_t, sel = _nsa_cmp(zt, qg, kc, vct, bias_c, ovt)

    front = lambda x, n, axis: jnp.pad(x, [(n, 0) if a == axis else (0, 0) for a in range(x.ndim)])
    ones_rows = jnp.concatenate([jnp.ones((b, NSA_KV_HEADS, 1, s), BF16),
                                 jnp.zeros((b, NSA_KV_HEADS, 15, s), BF16)], axis=2)
    heads_t = lambda rows: jnp.concatenate(
        [zt[:, rows:rows + kvw, :].astype(BF16).reshape(b, NSA_KV_HEADS, HEAD_DIM, s), ones_rows], axis=2)
    ks_p = front(ks, SEL_PAD, 2)
    kw_p = front(kw, WINDOW, 2)
    vst_p = front(heads_t(_ROW_VS), SEL_PAD, 3)
    vwt_p = front(heads_t(_ROW_VW), WINDOW, 3)
    sel_p = jnp.pad(sel, ((0, 0), (0, 0), (SEL_PAD // L_SEL, 0), (0, 0)), constant_values=NEG)
    gl = zt[:, _ROW_G:_ROW_G + N_BRANCH * NSA_HEADS, :].reshape(b, NSA_KV_HEADS, NSA_GROUP * N_BRANCH, s)
    gates_t = jnp.pad(gl, ((0, 0), (0, 0), (0, 16 - NSA_GROUP * N_BRANCH), (0, 0)))
    yn_t = _nsa_main(zt, qg, ks_p, vst_p, kw_p, vwt_p, sel_p, bias_s, bias_w, gates_t, oc_t)
    yn = yn_t

    return _mixout(gy, yc, yn, h, p['ssm_w_glu'], g_out, p['w_out'])


def kernel(x, mem, norm_mix, w_in, ssm_lambda_re, ssm_lambda_im, ssm_log_dt, ssm_b_re, ssm_b_im, ssm_c_re, ssm_c_im, ssm_d, ssm_w_glu, conv_w_dw, conv_b_dw, conv_ln_g, conv_ln_b, conv_w_pw, nsa_q_norm, nsa_k_norm, nsa_cmp_pe, nsa_cmp_w1, nsa_cmp_w2, mix_out_norm, w_out, t5_table, norm_cross, norm_mem, x_w_q, x_w_kv, x_q_norm, x_k_norm, x_w_o, norm_ffn, ffn_w_up, ffn_w_down, moe_router, moe_w_up, moe_w_down):
    b, s, d = x.shape
    depth = w_in.shape[0]
    per_layer = dict(norm_mix=norm_mix, w_in=w_in, ssm_lambda_re=ssm_lambda_re, ssm_lambda_im=ssm_lambda_im,
                     ssm_log_dt=ssm_log_dt, ssm_b_re=ssm_b_re, ssm_b_im=ssm_b_im, ssm_c_re=ssm_c_re,
                     ssm_c_im=ssm_c_im, ssm_d=ssm_d, ssm_w_glu=ssm_w_glu, conv_w_dw=conv_w_dw,
                     conv_b_dw=conv_b_dw, conv_ln_g=conv_ln_g, conv_ln_b=conv_ln_b, conv_w_pw=conv_w_pw,
                     nsa_q_norm=nsa_q_norm, nsa_k_norm=nsa_k_norm, nsa_cmp_pe=nsa_cmp_pe, nsa_cmp_w1=nsa_cmp_w1,
                     nsa_cmp_w2=nsa_cmp_w2, mix_out_norm=mix_out_norm, w_out=w_out)
    t5_tiles = _nsa_bias_tiles(t5_table, s)
    s5_perm = _s5_perm_tables()
    h = x
    for layer in range(depth):
        p = {k: v[layer] for k, v in per_layer.items()}
        h = _layer_mixers(h, p, t5_tiles, s5_perm)
        mk, mv = _memkv(mem, norm_mem[layer], x_w_kv[layer], x_k_norm[layer])
        h = _cross(h, norm_cross[layer], x_w_q[layer], x_q_norm[layer], mk, mv, x_w_o[layer])
        h2d = h.reshape(b * s, d)
        if layer % 2 == 0:
            h2d = _ffn(h2d, norm_ffn[layer], ffn_w_up[layer // 2], ffn_w_down[layer // 2])
        else:
            h2d = _moe(h2d, norm_ffn[layer], moe_router[layer // 2], moe_w_up[layer // 2], moe_w_down[layer // 2])
        h = h2d.reshape(b, s, d)
    return h
```
